```python
import math
import jax, jax.numpy as jnp
from jax import lax
import numpy as np

D_MODEL = 1024
BATCH = 8
SEQ = 4096
DEPTH = 1

GRID_W = 64
CTX_LEN = 256
D_MIX = D_MODEL
D_S5 = D_MIX // 2
S5_GROUP = 16
S5_GROUPS = D_S5 // S5_GROUP
S5_STATE = 64
D_FOURIER = D_MIX - D_S5
FOURIER_GROUPS = 4
FOURIER_GROUP = D_FOURIER // FOURIER_GROUPS
N_EXPERT_GROUPS = 4
EXPERTS_PER_GROUP = 8
N_EXPERTS = N_EXPERT_GROUPS * EXPERTS_PER_GROUP
TOP_K_INNER = 2
D_EXPERT = D_MODEL // 4
EPS = 1e-6
DT_MIN = 1e-3
DT_MAX = 1e-1

kernel_name = "hymba_s5_fnet_hmoe_prefix_block"


def rms_norm(x, g):
    xf = x.astype(jnp.float32)
    y = xf * lax.rsqrt(jnp.mean(xf * xf, axis=-1, keepdims=True) + EPS)
    return (y * g.astype(jnp.float32)).astype(x.dtype)


def adaln(cond, w, b):
    m = jax.nn.silu(cond) @ w + b
    return [p[:, None, :] for p in jnp.split(m, 6, axis=-1)]


def modulate(h, shift, scale):
    return h * (1.0 + scale) + shift


def to_lbgh(u):
    b, l, _ = u.shape
    return u.reshape(b, l, S5_GROUPS, S5_GROUP).transpose(1, 0, 2, 3)


def s5_discretize(lam_re, lam_im, b_re, b_im, log_step):
    f32 = jnp.float32
    dt = jnp.exp(log_step.astype(f32))[:, None]
    lr, li = lam_re.astype(f32), lam_im.astype(f32)
    mag = jnp.exp(lr * dt)
    abar_re, abar_im = mag * jnp.cos(li * dt), mag * jnp.sin(li * dt)
    den = lr * lr + li * li
    nr, ni = abar_re - 1.0, abar_im
    q_re = (nr * lr + ni * li) / den
    q_im = (ni * lr - nr * li) / den
    br, bi = b_re.astype(f32), b_im.astype(f32)
    bbar_re = q_re[..., None] * br - q_im[..., None] * bi
    bbar_im = q_re[..., None] * bi + q_im[..., None] * br
    return abar_re, abar_im, bbar_re, bbar_im


def _linear_recurrence_combine(e1, e2):
    a1r, a1i, b1r, b1i = e1
    a2r, a2i, b2r, b2i = e2
    return (a2r * a1r - a2i * a1i,
            a2r * a1i + a2i * a1r,
            a2r * b1r - a2i * b1i + b2r,
            a2r * b1i + a2i * b1r + b2i)


def s5_scan(u, disc, h0_re, h0_im, reverse):
    abar_re, abar_im, bbar_re, bbar_im = disc
    bu_re = jnp.einsum('lbgh,gph->lbgp', u, bbar_re)
    bu_im = jnp.einsum('lbgh,gph->lbgp', u, bbar_im)
    first = -1 if reverse else 0
    bu_re = bu_re.at[first].add(abar_re * h0_re - abar_im * h0_im)
    bu_im = bu_im.at[first].add(abar_re * h0_im + abar_im * h0_re)
    n = u.shape[0]
    a_re = jnp.broadcast_to(abar_re, (n, 1) + abar_re.shape)
    a_im = jnp.broadcast_to(abar_im, (n, 1) + abar_im.shape)
    _, _, h_re, h_im = lax.associative_scan(
        _linear_recurrence_combine, (a_re, a_im, bu_re, bu_im), reverse=reverse, axis=0)
    return h_re, h_im


def s5_bidirectional(u, lp, h0, need_output):
    y = None
    finals = []
    for d, reverse in enumerate((False, True)):
        disc = s5_discretize(lp["lam_re"][d], lp["lam_im"][d], lp["b_re"][d],
                             lp["b_im"][d], lp["log_step"][d])
        h_re, h_im = s5_scan(u, disc, h0[d][0], h0[d][1], reverse)
        last = 0 if reverse else -1
        finals.append((h_re[last], h_im[last]))
        if need_output:
            yd = (jnp.einsum('lbgp,ghp->lbgh', h_re, lp["c_re"][d].astype(jnp.float32))
                  - jnp.einsum('lbgp,ghp->lbgh', h_im, lp["c_im"][d].astype(jnp.float32)))
            y = yd if y is None else y + yd
    return y, finals


def fourier_mixer(v, w_four):
    b, l, _ = v.shape
    vg = v.reshape(b, l, FOURIER_GROUPS, FOURIER_GROUP).astype(jnp.float32)
    f = jnp.fft.fft2(vg, axes=(1, 3), norm="ortho").real.astype(v.dtype)
    return jnp.einsum('blgc,gcd->blgd', f, w_four).reshape(b, l, D_FOURIER)


def mixer_inputs(h_stream, shift, scale, lp):
    h = modulate(rms_norm(h_stream, lp["norm1_g"]), shift, scale)
    z = h @ lp["w_in"]
    return to_lbgh(z[..., :D_S5]), z[..., D_S5:]


def mixer_output(y_s5, u, v, lp):
    b, l = v.shape[0], v.shape[1]
    y = y_s5 + lp["s5_d"].reshape(S5_GROUPS, S5_GROUP) * u
    y = y.transpose(1, 0, 2, 3).reshape(b, l, D_S5).astype(v.dtype)
    y = jax.nn.gelu(y)
    y = y * jax.nn.sigmoid(y @ lp["s5_w_glu"])
    f = fourier_mixer(v, lp["fourier_w"])
    merged = jnp.concatenate([rms_norm(y, lp["mix_norm_s5_g"]),
                              rms_norm(f, lp["mix_norm_f_g"])], axis=-1)
    return merged @ lp["w_out"]


def hmoe(h, lp):
    def per_sample(t):
        f32 = jnp.float32
        g_logits = (t @ lp["w_group"] + lp["b_group"]).astype(f32)
        g_prob = jax.nn.softmax(g_logits, axis=-1)
        g_w, g_idx = lax.top_k(g_prob, 1)
        e_logits = (jnp.einsum('td,dge->tge', t, lp["w_router"]) + lp["b_router"]).astype(f32)
        sel = jax.nn.one_hot(g_idx[:, 0], N_EXPERT_GROUPS, dtype=f32)
        e_sel = jnp.sum(e_logits * sel[:, :, None], axis=1)
        top_v, top_i = lax.top_k(e_sel, TOP_K_INNER)
        weights = g_w * jax.nn.softmax(top_v, axis=-1)
        expert_id = g_idx * EXPERTS_PER_GROUP + top_i
        gates = jnp.sum(jax.nn.one_hot(expert_id, N_EXPERTS, dtype=f32) * weights[..., None], axis=1)
        hg = jnp.einsum('td,edf->tef', t, lp["w_gate"])
        hu = jnp.einsum('td,edf->tef', t, lp["w_up"])
        a = jax.nn.silu(hg) * hu * gates[..., None].astype(t.dtype)
        return jnp.einsum('tef,efd->td', a, lp["w_down"])
    return lax.map(per_sample, h)


def setup_inputs(seed: int = 0) -> dict:
    key = jax.random.key(seed)
    ks = jax.random.split(key, 30)
    f32 = jnp.float32

    def nrm(k, shape, s):
        return s * jax.random.normal(k, shape, f32)

    G, P, H = S5_GROUPS, S5_STATE, S5_GROUP
    lam_im0 = math.pi * jnp.arange(P, dtype=f32)
    return {
        "x": nrm(ks[0], (BATCH, SEQ, D_MODEL), 1.0),
        "c": nrm(ks[1], (BATCH, D_MODEL), 1.0),
        "ctx": nrm(ks[2], (BATCH, CTX_LEN, D_MODEL), 1.0),
        "c_ctx": nrm(ks[3], (D_MODEL,), 1.0),
        "w_ada": nrm(ks[4], (DEPTH, D_MODEL, 6 * D_MODEL), 0.5 * D_MODEL ** -0.5),
        "b_ada": nrm(ks[5], (DEPTH, 6 * D_MODEL), 0.02),
        "norm1_g": 1.0 + nrm(ks[6], (DEPTH, D_MODEL), 0.02),
        "norm2_g": 1.0 + nrm(ks[7], (DEPTH, D_MODEL), 0.02),
        "w_in": nrm(ks[8], (DEPTH, D_MODEL, D_MIX), D_MODEL ** -0.5),
        "s5_lam_re": -0.5 + nrm(ks[9], (DEPTH, 2, G, P), 0.01),
        "s5_lam_im": lam_im0 + nrm(ks[10], (DEPTH, 2, G, P), 0.01),
        "s5_b_re": nrm(ks[11], (DEPTH, 2, G, P, H), (2.0 * H) ** -0.5),
        "s5_b_im": nrm(ks[12], (DEPTH, 2, G, P, H), (2.0 * H) ** -0.5),
        "s5_c_re": nrm(ks[13], (DEPTH, 2, G, H, P), (2.0 * P) ** -0.5),
        "s5_c_im": nrm(ks[14], (DEPTH, 2, G, H, P), (2.0 * P) ** -0.5),
        "s5_log_step": jax.random.uniform(ks[15], (DEPTH, 2, G), f32,
                                          math.log(DT_MIN), math.log(DT_MAX)),
        "s5_d": nrm(ks[16], (DEPTH, D_S5), 1.0),
        "s5_w_glu": nrm(ks[17], (DEPTH, D_S5, D_S5), D_S5 ** -0.5),
        "fourier_w": nrm(ks[18], (DEPTH, FOURIER_GROUPS, FOURIER_GROUP, FOURIER_GROUP),
                          FOURIER_GROUP ** -0.5),
        "mix_norm_s5_g": 1.0 + nrm(ks[19], (DEPTH, D_S5), 0.02),
        "mix_norm_f_g": 1.0 + nrm(ks[20], (DEPTH, D_FOURIER), 0.02),
        "w_out": nrm(ks[21], (DEPTH, D_MIX, D_MODEL), D_MIX ** -0.5),
        "moe_w_group": nrm(ks[22], (DEPTH, D_MODEL, N_EXPERT_GROUPS), D_MODEL ** -0.5),
        "moe_b_group": nrm(ks[23], (DEPTH, N_EXPERT_GROUPS), 0.01),
        "moe_w_router": nrm(ks[24], (DEPTH, D_MODEL, N_EXPERT_GROUPS, EXPERTS_PER_GROUP),
                             D_MODEL ** -0.5),
        "moe_b_router": nrm(ks[25], (DEPTH, N_EXPERT_GROUPS, EXPERTS_PER_GROUP), 0.01),
        "moe_w_gate": nrm(ks[26], (DEPTH, N_EXPERTS, D_MODEL, D_EXPERT), D_MODEL ** -0.5),
        "moe_w_up": nrm(ks[27], (DEPTH, N_EXPERTS, D_MODEL, D_EXPERT), D_MODEL ** -0.5),
        "moe_w_down": nrm(ks[28], (DEPTH, N_EXPERTS, D_EXPERT, D_MODEL), D_EXPERT ** -0.5),
        "final_norm_g": 1.0 + nrm(ks[29], (D_MODEL,), 0.02),
    }


def reference(x, c, ctx, c_ctx, w_ada, b_ada, norm1_g, norm2_g, w_in,
              s5_lam_re, s5_lam_im, s5_b_re, s5_b_im, s5_c_re, s5_c_im, s5_log_step,
              s5_d, s5_w_glu, fourier_w, mix_norm_s5_g, mix_norm_f_g, w_out,
              moe_w_group, moe_b_group, moe_w_router, moe_b_router,
              moe_w_gate, moe_w_up, moe_w_down, final_norm_g):
    n_lat = x.shape[1]
    rows = n_lat // GRID_W
    assert rows * GRID_W == n_lat
    batch = x.shape[0]
    ctx_h = ctx
    for layer in range(DEPTH):
        last = layer == DEPTH - 1
        lp = {
            "norm1_g": norm1_g[layer], "norm2_g": norm2_g[layer], "w_in": w_in[layer],
            "lam_re": s5_lam_re[layer], "lam_im": s5_lam_im[layer],
            "b_re": s5_b_re[layer], "b_im": s5_b_im[layer],
            "c_re": s5_c_re[layer], "c_im": s5_c_im[layer],
            "log_step": s5_log_step[layer], "s5_d": s5_d[layer], "s5_w_glu": s5_w_glu[layer],
            "fourier_w": fourier_w[layer], "mix_norm_s5_g": mix_norm_s5_g[layer],
            "mix_norm_f_g": mix_norm_f_g[layer], "w_out": w_out[layer],
            "w_group": moe_w_group[layer], "b_group": moe_b_group[layer],
            "w_router": moe_w_router[layer], "b_router": moe_b_router[layer],
            "w_gate": moe_w_gate[layer], "w_up": moe_w_up[layer], "w_down": moe_w_down[layer],
        }
        sh1, sc1, g1, sh2, sc2, g2 = adaln(c, w_ada[layer], b_ada[layer])
        csh1, csc1, cg1, csh2, csc2, cg2 = adaln(c_ctx[None, :], w_ada[layer], b_ada[layer])

        u_c, v_c = mixer_inputs(ctx_h, csh1, csc1, lp)
        zero = jnp.zeros((batch, S5_GROUPS, S5_STATE), jnp.float32)
        y_c, ctx_finals = s5_bidirectional(u_c, lp, ((zero, zero), (zero, zero)),
                                           need_output=not last)

        u_x, v_x = mixer_inputs(x, sh1, sc1, lp)
        y_x, _ = s5_bidirectional(u_x, lp, ctx_finals, need_output=True)
        x = x + g1 * mixer_output(y_x, u_x, v_x, lp)
        x = x + g2 * hmoe(modulate(rms_norm(x, lp["norm2_g"]), sh2, sc2), lp)

        if not last:
            ctx_h = ctx_h + cg1 * mixer_output(y_c, u_c, v_c, lp)
            ctx_h = ctx_h + cg2 * hmoe(modulate(rms_norm(ctx_h, lp["norm2_g"]), csh2, csc2), lp)
    return rms_norm(x, final_norm_g)
```

```python
import functools
import math

import numpy as np
import jax
import jax.numpy as jnp
from jax import lax
from jax.experimental import pallas as pl
from jax.experimental.pallas import tpu as pltpu

EPS = 1e-6
S5_GROUP = 16
S5_STATE = 64
S5_CHUNK = 16
FOURIER_GROUPS = 4
N_EXPERT_GROUPS = 4
EXPERTS_PER_GROUP = 8
FFT_R = 64
FFT_BLK = 8
MOE_TM = 256
ROUTE_LANES = 128
VMEM_LIMIT = 56 * 1024 * 1024

_HI = lax.Precision.HIGHEST
_BF = jnp.bfloat16
_F32 = jnp.float32


def _cparams(sem):
    return pltpu.CompilerParams(dimension_semantics=sem, vmem_limit_bytes=VMEM_LIMIT)


def _dot(a, b):
    return jnp.dot(a, b, preferred_element_type=_F32)


def _rms(x, g):
    return x * lax.rsqrt(jnp.mean(x * x, axis=-1, keepdims=True) + EPS) * g


def _adaln_body(c_ref, w_ref, b_ref, o_ref):
    c = c_ref[...]
    a = c * jax.nn.sigmoid(c)
    o_ref[...] = jnp.dot(a, w_ref[...], preferred_element_type=_F32, precision=_HI) + b_ref[...]


def _adaln(cond, w, b):
    m, d = cond.shape
    n = w.shape[1]
    tn = 768
    return pl.pallas_call(
        _adaln_body,
        grid=(n // tn,),
        in_specs=[pl.BlockSpec((m, d), lambda j: (0, 0)),
                  pl.BlockSpec((d, tn), lambda j: (0, j)),
                  pl.BlockSpec((1, tn), lambda j: (0, j))],
        out_specs=pl.BlockSpec((m, tn), lambda j: (0, j)),
        out_shape=jax.ShapeDtypeStruct((m, n), _F32),
        compiler_params=_cparams(("arbitrary",)),
        name="adaln",
    )(cond, w, b.reshape(1, n))


def _inproj_body(x_ref, sh_ref, sc_ref, g_ref, w_ref, m1_ref, zs_ref, gc_ref):
    r, blk, d = x_ref.shape
    x = x_ref[...].reshape(r * blk, d)
    h = _rms(x, g_ref[...]) * (1.0 + sc_ref[...]) + sh_ref[...]
    z = _dot(h.astype(_BF), w_ref[...])
    ds5 = zs_ref.shape[-1]
    zs_ref[...] = z[:, :ds5].reshape(r, blk, ds5)
    v = z[:, ds5:].astype(_BF)
    g1 = _dot(m1_ref[...], v)
    gc_ref[...] = g1.astype(_BF).reshape(r, 2 * blk, v.shape[-1])


def _inproj(x, sh, sc, g, w_bf, m1):
    b, l, d = x.shape
    r, blk = FFT_R, FFT_BLK
    nj = r // blk
    dmix = w_bf.shape[1]
    ds5 = dmix // 2
    df = dmix - ds5
    x4 = x.reshape(b, r, r, d)
    zs, gc = pl.pallas_call(
        _inproj_body,
        grid=(b, nj),
        in_specs=[pl.BlockSpec((None, r, blk, d), lambda i, j: (i, 0, j, 0)),
                  pl.BlockSpec((None, 1, d), lambda i, j: (i, 0, 0)),
                  pl.BlockSpec((None, 1, d), lambda i, j: (i, 0, 0)),
                  pl.BlockSpec((1, d), lambda i, j: (0, 0)),
                  pl.BlockSpec((d, dmix), lambda i, j: (0, 0)),
                  pl.BlockSpec(m1.shape, lambda i, j: (0, 0))],
        out_specs=[pl.BlockSpec((None, r, blk, ds5), lambda i, j: (i, 0, j, 0)),
                   pl.BlockSpec((None, r, 2 * blk, df), lambda i, j: (i, 0, j, 0))],
        out_shape=[jax.ShapeDtypeStruct((b, r, r, ds5), _F32),
                   jax.ShapeDtypeStruct((b, r, 2 * r, df), _BF)],
        compiler_params=_cparams(("parallel", "arbitrary")),
        name="inproj",
    )(x4, sh, sc, g, w_bf, m1)
    return zs.reshape(b, l, ds5), gc


def _inproj_ctx_body(x_ref, sh_ref, sc_ref, g_ref, w_ref, zs_ref):
    h = _rms(x_ref[...], g_ref[...]) * (1.0 + sc_ref[...]) + sh_ref[...]
    zs_ref[...] = _dot(h.astype(_BF), w_ref[...])


def _inproj_ctx(ctx, sh, sc, g, w_s5_bf):
    b, lc, d = ctx.shape
    ds5 = w_s5_bf.shape[1]
    return pl.pallas_call(
        _inproj_ctx_body,
        grid=(b,),
        in_specs=[pl.BlockSpec((None, lc, d), lambda i: (i, 0, 0)),
                  pl.BlockSpec((1, d), lambda i: (0, 0)),
                  pl.BlockSpec((1, d), lambda i: (0, 0)),
                  pl.BlockSpec((1, d), lambda i: (0, 0)),
                  pl.BlockSpec((d, ds5), lambda i: (0, 0))],
        out_specs=pl.BlockSpec((None, lc, ds5), lambda i: (i, 0, 0)),
        out_shape=jax.ShapeDtypeStruct((b, lc, ds5), _F32),
        compiler_params=_cparams(("arbitrary",)),
        name="inproj_ctx",
    )(ctx, sh, sc, g, w_s5_bf)


@functools.lru_cache(maxsize=None)
def _fft_stage1_matrix():
    r, blk = FFT_R, FFT_BLK
    k1 = np.arange(r)[:, None]
    l1 = np.arange(r)[None, :]
    ang = 2.0 * np.pi * ((k1 * l1) % r) / r
    f = np.stack([np.cos(ang), -np.sin(ang)], axis=1)
    m = np.einsum("kal,pq->kaplq", f, np.eye(blk))
    return m.reshape(r * 2 * blk, r * blk).astype(jnp.bfloat16)


@functools.lru_cache(maxsize=None)
def _fft_stage2_matrices():
    r, blk = FFT_R, FFT_BLK
    n = r * r
    nt = r // blk
    l2 = np.arange(r)
    k2 = np.arange(r)
    out = np.zeros((nt, 2, r, blk, blk, r // blk, 2, blk), np.float32)
    for i in range(nt):
        for kl in range(blk):
            k1 = blk * i + kl
            ang = 2.0 * np.pi * (((k2[:, None] * l2[None, :] * r) + l2[None, :] * k1) % n) / n
            tr, ti = np.cos(ang), -np.sin(ang)
            tr = tr.reshape(r, r // blk, blk)
            ti = ti.reshape(r, r // blk, blk)
            out[i, 0, :, kl, kl, :, 0, :] = tr
            out[i, 0, :, kl, kl, :, 1, :] = -ti
            out[i, 1, :, kl, kl, :, 0, :] = ti
            out[i, 1, :, kl, kl, :, 1, :] = tr
    return out.reshape(nt, 2 * r * blk, blk * 2 * r).astype(jnp.bfloat16)


def _fft2_body(gc_ref, m2_ref, cw_ref, sw_ref, g_ref, o_ref):
    kb, rr, df = gc_ref.shape
    gc = gc_ref[...].reshape(kb * rr, df)
    x = _dot(m2_ref[...], gc)
    half = x.shape[0] // 2
    xr = x[:half].astype(_BF)
    xi = x[half:].astype(_BF)
    ng = cw_ref.shape[0]
    fg = df // ng
    parts = []
    for g in range(ng):
        sl = slice(g * fg, (g + 1) * fg)
        parts.append(_dot(xr[:, sl], cw_ref[g]) + _dot(xi[:, sl], sw_ref[g]))
    f = jnp.concatenate(parts, axis=-1)
    fn = _rms(f, g_ref[...])
    o_ref[...] = fn.reshape(o_ref.shape)


def _fft2(gc, m2, cw, sw, gf):
    b, r, rr, df = gc.shape
    blk = FFT_BLK
    nt = r // blk
    out = pl.pallas_call(
        _fft2_body,
        grid=(b, nt),
        in_specs=[pl.BlockSpec((None, blk, rr, df), lambda i, j: (i, j, 0, 0)),
                  pl.BlockSpec((None,) + m2.shape[1:], lambda i, j: (j, 0, 0)),
                  pl.BlockSpec(cw.shape, lambda i, j: (0, 0, 0)),
                  pl.BlockSpec(sw.shape, lambda i, j: (0, 0, 0)),
                  pl.BlockSpec((1, df), lambda i, j: (0, 0))],
        out_specs=pl.BlockSpec((None, r, blk, df), lambda i, j: (i, 0, j, 0)),
        out_shape=jax.ShapeDtypeStruct((b, r, r, df), _F32),
        compiler_params=_cparams(("parallel", "arbitrary")),
        name="fft2",
    )(gc, m2, cw, sw, gf)
    return out.reshape(b, r * r, df)


def _s5_body(u_ref, ws_ref, wy_ref, at_ref, y_ref, s_scr, h_scr, *, n_ctx, n_lat, nb):
    p = S5_STATE
    u = u_ref[...]
    s_scr[...] = _dot(u, ws_ref[...])
    a_re = at_ref[0:1, :]
    a_im = at_ref[1:2, :]
    lane = lax.broadcasted_iota(jnp.int32, (nb, 2 * p), 1)
    is_f = lane < p
    n_all = n_ctx + n_lat

    def load(step, in_ctx):
        cf = step
        cr = (n_ctx - 1 - step) if in_ctx else (n_all + n_ctx - 1 - step)
        rf = pl.multiple_of(cf * nb, nb)
        rr = pl.multiple_of(cr * nb, nb)
        sf = s_scr[pl.ds(rf, nb), :]
        sr = s_scr[pl.ds(rr, nb), :]
        s_re = jnp.where(is_f, sf[:, :2 * p], sr[:, :2 * p])
        s_im = jnp.where(is_f, sf[:, 2 * p:], sr[:, 2 * p:])
        return rf, rr, s_re, s_im

    def advance(h_re, h_im, s_re, s_im):
        return (a_re * h_re - a_im * h_im + s_re, a_re * h_im + a_im * h_re + s_im)

    def ctx_step(step, carry):
        h_re, h_im = carry
        _, _, s_re, s_im = load(step, True)
        return advance(h_re, h_im, s_re, s_im)

    def lat_step(step, carry):
        h_re, h_im = carry
        rf, rr, s_re, s_im = load(step, False)
        of = pl.multiple_of(rf - n_ctx * nb, nb)
        orr = pl.multiple_of(rr - n_ctx * nb, nb)
        h_scr[pl.ds(of, nb), 0:p] = h_re[:, 0:p]
        h_scr[pl.ds(orr, nb), p:2 * p] = h_re[:, p:2 * p]
        h_scr[pl.ds(of, nb), 2 * p:3 * p] = h_im[:, 0:p]
        h_scr[pl.ds(orr, nb), 3 * p:4 * p] = h_im[:, p:2 * p]
        return advance(h_re, h_im, s_re, s_im)

    zero = jnp.zeros((nb, 2 * p), _F32)
    carry = lax.fori_loop(0, n_ctx, ctx_step, (zero, zero))
    lax.fori_loop(n_ctx, n_all, lat_step, carry)

    u_lat = u[n_ctx * nb:, :]
    lhs = jnp.concatenate([u_lat, h_scr[...].astype(_BF)], axis=-1)
    y_ref[...] = _dot(lhs, wy_ref[...])


def _s5(u_g, ws, wy, at, n_ctx, n_lat, nb):
    ng, rows, kk = u_g.shape
    rows_lat = n_lat * nb
    body = functools.partial(_s5_body, n_ctx=n_ctx, n_lat=n_lat, nb=nb)
    return pl.pallas_call(
        body,
        grid=(ng,),
        in_specs=[pl.BlockSpec((None, rows, kk), lambda g: (g, 0, 0)),
                  pl.BlockSpec((None,) + ws.shape[1:], lambda g: (g, 0, 0)),
                  pl.BlockSpec((None,) + wy.shape[1:], lambda g: (g, 0, 0)),
                  pl.BlockSpec((None,) + at.shape[1:], lambda g: (g, 0, 0))],
        out_specs=pl.BlockSpec((None, rows_lat, kk), lambda g: (g, 0, 0)),
        out_shape=jax.ShapeDtypeStruct((ng, rows_lat, kk), _F32),
        scratch_shapes=[pltpu.VMEM((rows, 4 * S5_STATE), _F32),
                        pltpu.VMEM((rows_lat, 4 * S5_STATE), _F32)],
        compiler_params=_cparams(("parallel",)),
        name="s5_scan",
    )(u_g, ws, wy, at)


def _s5_weights(lam_re, lam_im, b_re, b_im, c_re, c_im, log_step):
    t = S5_CHUNK
    ng, p = lam_re.shape[1], lam_re.shape[2]
    hh = b_re.shape[-1]
    dt = jnp.exp(log_step.astype(_F32))[..., None]
    lr, li = lam_re.astype(_F32), lam_im.astype(_F32)
    mag = jnp.exp(lr * dt)
    ab_re, ab_im = mag * jnp.cos(li * dt), mag * jnp.sin(li * dt)
    den = lr * lr + li * li
    nr, ni = ab_re - 1.0, ab_im
    q_re = (nr * lr + ni * li) / den
    q_im = (ni * lr - nr * li) / den
    br, bi = b_re.astype(_F32), b_im.astype(_F32)
    bb_re = q_re[..., None] * br - q_im[..., None] * bi
    bb_im = q_re[..., None] * bi + q_im[..., None] * br
    k = jnp.arange(t + 1, dtype=_F32)[:, None, None, None]
    pm = jnp.exp(k * (lr * dt)[None])
    ap_re = pm * jnp.cos(k * (li * dt)[None])
    ap_im = pm * jnp.sin(k * (li * dt)[None])
    cr, ci = c_re.astype(_F32), c_im.astype(_F32)

    def ein(spec, *ops):
        return jnp.einsum(spec, *ops, precision=_HI)

    ca_re = cr[None] * ap_re[:, :, :, None, :] - ci[None] * ap_im[:, :, :, None, :]
    ca_im = cr[None] * ap_im[:, :, :, None, :] + ci[None] * ap_re[:, :, :, None, :]
    kk = ein("kdghp,dgpj->kdghj", ca_re, bb_re) - ein("kdghp,dgpj->kdghj", ca_im, bb_im)
    tl = jnp.arange(t)
    lag_f = tl[None, :] - tl[:, None]
    kf = jnp.where((lag_f >= 0)[:, :, None, None, None], kk[jnp.clip(lag_f, 0, t), 0], 0.0)
    kr = jnp.where((lag_f <= 0)[:, :, None, None, None], kk[jnp.clip(-lag_f, 0, t), 1], 0.0)
    msum = (kf + kr).transpose(2, 0, 4, 1, 3).reshape(ng, t * hh, t * hh)

    pw_f = (t - 1 - tl)
    pw_r = tl

    def state_w(pw, d):
        ar, ai = ap_re[pw, d], ap_im[pw, d]
        wr = ar[..., None] * bb_re[d][None] - ai[..., None] * bb_im[d][None]
        wi = ar[..., None] * bb_im[d][None] + ai[..., None] * bb_re[d][None]
        wr = wr.transpose(1, 0, 3, 2).reshape(ng, t * hh, p)
        wi = wi.transpose(1, 0, 3, 2).reshape(ng, t * hh, p)
        return wr, wi

    wf_re, wf_im = state_w(pw_f, 0)
    wr_re, wr_im = state_w(pw_r, 1)
    ws = jnp.concatenate([wf_re, wr_re, wf_im, wr_im], axis=-1)

    def out_w(pw, d):
        xr = ca_re[pw, d]
        xi = ca_im[pw, d]
        vr = xr.transpose(1, 3, 0, 2).reshape(ng, p, t * hh)
        vi = (-xi).transpose(1, 3, 0, 2).reshape(ng, p, t * hh)
        return vr, vi

    vf_re, vf_im = out_w(tl + 1, 0)
    vr_re, vr_im = out_w(t - tl, 1)
    wy = jnp.concatenate([msum, vf_re, vr_re, vf_im, vr_im], axis=1)

    at = jnp.stack([jnp.concatenate([ap_re[t, 0], ap_re[t, 1]], axis=-1),
                    jnp.concatenate([ap_im[t, 0], ap_im[t, 1]], axis=-1)], axis=1)
    at = jnp.pad(at, ((0, 0), (0, 6), (0, 0)))
    return ws.astype(_BF), wy.astype(_BF), at


def _post_body(x_ref, u_ref, y_ref, fn_ref, d_ref, wglu_ref, gs_ref, wo_ref, g1_ref,
               n2_ref, sh_ref, sc_ref, wr_ref, br_ref, x1_ref, t_ref, rt_ref):
    ds5 = u_ref.shape[-1]
    y = y_ref[...] + d_ref[...] * u_ref[...]
    y = jax.nn.gelu(y, approximate=True)
    y = y * jax.nn.sigmoid(_dot(y.astype(_BF), wglu_ref[...]))
    yn = _rms(y, gs_ref[...]).astype(_BF)
    mix = _dot(yn, wo_ref[0:ds5, :]) + _dot(fn_ref[...].astype(_BF), wo_ref[ds5:, :])
    x1 = x_ref[...] + g1_ref[...] * mix
    x1_ref[...] = x1
    t = _rms(x1, n2_ref[...]) * (1.0 + sc_ref[...]) + sh_ref[...]
    t_ref[...] = t.astype(_BF)

    ng, epg = N_EXPERT_GROUPS, EXPERTS_PER_GROUP
    logits = jnp.dot(t, wr_ref[...], preferred_element_type=_F32, precision=_HI) + br_ref[...]
    lane = lax.broadcasted_iota(jnp.int32, logits.shape, 1)
    neg = jnp.float32(-jnp.inf)
    big = jnp.int32(1 << 20)
    gl = jnp.where(lane < ng, logits, neg)
    gmax = jnp.max(gl, axis=-1, keepdims=True)
    gidx = jnp.min(jnp.where(gl == gmax, lane, big), axis=-1, keepdims=True)
    gw = 1.0 / jnp.sum(jnp.where(lane < ng, jnp.exp(logits - gmax), 0.0), axis=-1, keepdims=True)
    lo = ng + gidx * epg
    el = jnp.where((lane >= lo) & (lane < lo + epg), logits, neg)
    v0 = jnp.max(el, axis=-1, keepdims=True)
    i0 = jnp.min(jnp.where(el == v0, lane, big), axis=-1, keepdims=True)
    el1 = jnp.where(lane == i0, neg, el)
    v1 = jnp.max(el1, axis=-1, keepdims=True)
    i1 = jnp.min(jnp.where(el1 == v1, lane, big), axis=-1, keepdims=True)
    p0 = 1.0 / (1.0 + jnp.exp(v1 - v0))
    w0 = gw * p0
    w1 = gw * (1.0 - p0)
    e0 = (i0 - ng).astype(_F32)
    e1 = (i1 - ng).astype(_F32)
    rt = jnp.where(lane == 0, e0, jnp.where(lane == 1, e1, jnp.where(lane == 2, w0,
         jnp.where(lane == 3, w1, 0.0))))
    rt_ref[...] = rt


def _post(x, u, y, fn, s5_d, wglu_bf, gs, wo_bf, g1, n2, sh2, sc2, wr, br, tm=512):
    b, l, d = x.shape
    ds5 = u.shape[-1]
    tok = lambda w: pl.BlockSpec((None, tm, w), lambda i, j: (i, j, 0))
    per_b = pl.BlockSpec((None, 1, d), lambda i, j: (i, 0, 0))
    full = lambda a: pl.BlockSpec(a.shape, lambda i, j: (0,) * a.ndim)
    return pl.pallas_call(
        _post_body,
        grid=(b, l // tm),
        in_specs=[tok(d), tok(ds5), tok(ds5), tok(fn.shape[-1]), full(s5_d), full(wglu_bf),
                  full(gs), full(wo_bf), per_b, full(n2), per_b, per_b, full(wr), full(br)],
        out_specs=[tok(d), tok(d), tok(ROUTE_LANES)],
        out_shape=[jax.ShapeDtypeStruct((b, l, d), _F32),
                   jax.ShapeDtypeStruct((b, l, d), _BF),
                   jax.ShapeDtypeStruct((b, l, ROUTE_LANES), _F32)],
        compiler_params=_cparams(("parallel", "arbitrary")),
        name="post_mixer",
    )(x, u, y, fn, s5_d, wglu_bf, gs, wo_bf, g1, n2, sh2, sc2, wr, br)


def _moe_body(te_ref, xs_ref, gate_ref, wg_ref, wu_ref, wd_ref, o_ref):
    del te_ref
    xs = xs_ref[...]
    hg = _dot(xs, wg_ref[...])
    hu = _dot(xs, wu_ref[...])
    a = hg * jax.nn.sigmoid(hg) * hu * gate_ref[...]
    o_ref[...] = _dot(a.astype(_BF), wd_ref[...]).astype(o_ref.dtype)


def _moe(tile_expert, xs, gate, wg_bf, wu_bf, wd_bf):
    nr, d = xs.shape
    tm = MOE_TM
    de = wg_bf.shape[-1]
    grid_spec = pltpu.PrefetchScalarGridSpec(
        num_scalar_prefetch=1,
        grid=(nr // tm,),
        in_specs=[pl.BlockSpec((tm, d), lambda i, te: (i, 0)),
                  pl.BlockSpec((tm, 1), lambda i, te: (i, 0)),
                  pl.BlockSpec((None, d, de), lambda i, te: (te[i], 0, 0)),
                  pl.BlockSpec((None, d, de), lambda i, te: (te[i], 0, 0)),
                  pl.BlockSpec((None, de, d), lambda i, te: (te[i], 0, 0))],
        out_specs=pl.BlockSpec((tm, d), lambda i, te: (i, 0)),
    )
    return pl.pallas_call(
        _moe_body,
        grid_spec=grid_spec,
        out_shape=jax.ShapeDtypeStruct((nr, d), _BF),
        compiler_params=_cparams(("arbitrary",)),
        name="moe_experts",
    )(tile_expert, xs, gate, wg_bf, wu_bf, wd_bf)


def _final_body(x1_ref, y0_ref, y1_ref, g2_ref, gf_ref, o_ref):
    m = y0_ref[...].astype(_F32) + y1_ref[...].astype(_F32)
    o_ref[...] = _rms(x1_ref[...] + g2_ref[...] * m, gf_ref[...])


def _final(x1, y0, y1, g2, gf, tm=512):
    b, l, d = x1.shape
    tok = pl.BlockSpec((None, tm, d), lambda i, j: (i, j, 0))
    return pl.pallas_call(
        _final_body,
        grid=(b, l // tm),
        in_specs=[tok, tok, tok, pl.BlockSpec((None, 1, d), lambda i, j: (i, 0, 0)),
                  pl.BlockSpec((1, d), lambda i, j: (0, 0))],
        out_specs=tok,
        out_shape=jax.ShapeDtypeStruct((b, l, d), _F32),
        compiler_params=_cparams(("parallel", "arbitrary")),
        name="final_norm",
    )(x1, y0, y1, g2, gf)


def _route_plan(eid, gates, n_experts, tm, nr):
    n_pairs = eid.shape[0]
    counts = jnp.zeros((n_experts,), jnp.int32).at[eid].add(1)
    padded = ((counts + tm - 1) // tm) * tm
    pad_end = jnp.cumsum(padded)
    pad_start = pad_end - padded
    raw_start = jnp.cumsum(counts) - counts
    order = jnp.argsort(eid, stable=True)
    e_sorted = eid[order]
    pos_sorted = pad_start[e_sorted] + (jnp.arange(n_pairs, dtype=jnp.int32) - raw_start[e_sorted])
    pair_pos = jnp.zeros((n_pairs,), jnp.int32).at[order].set(pos_sorted)
    row_token = jnp.zeros((nr,), jnp.int32).at[pos_sorted].set((order // 2).astype(jnp.int32))
    row_gate = jnp.zeros((nr,), _F32).at[pos_sorted].set(gates[order])
    tile_start = jnp.arange(nr // tm, dtype=jnp.int32) * tm
    tile_expert = jnp.minimum(jnp.searchsorted(pad_end, tile_start, side="right"),
                              n_experts - 1).astype(jnp.int32)
    return pair_pos, row_token, row_gate, tile_expert


def kernel(x, c, ctx, c_ctx, w_ada, b_ada, norm1_g, norm2_g, w_in, s5_lam_re, s5_lam_im, s5_b_re, s5_b_im, s5_c_re, s5_c_im, s5_log_step, s5_d, s5_w_glu, fourier_w, mix_norm_s5_g, mix_norm_f_g, w_out, moe_w_group, moe_b_group, moe_w_router, moe_b_router, moe_w_gate, moe_w_up, moe_w_down, final_norm_g):
    b, l, d = x.shape
    lc = ctx.shape[1]
    depth = w_ada.shape[0]
    assert depth == 1 and l == FFT_R * FFT_R and lc % S5_CHUNK == 0 and b % 8 == 0
    ds5 = s5_d.shape[-1]
    df = w_in.shape[-1] - ds5
    n_exp = moe_w_gate.shape[1]
    row = lambda a: a.reshape(1, -1)

    cond = jnp.concatenate([c, c_ctx[None, :], jnp.zeros((7, d), _F32)], axis=0)
    mod = _adaln(cond, w_ada[0], b_ada[0])
    sh1, sc1, g1, sh2, sc2, g2 = [mod[:b, i * d:(i + 1) * d].reshape(b, 1, d) for i in range(6)]
    csh1 = mod[b:b + 1, 0:d]
    csc1 = mod[b:b + 1, d:2 * d]

    w_in_bf = w_in[0].astype(_BF)
    m1 = jnp.asarray(_fft_stage1_matrix())
    m2 = jnp.asarray(_fft_stage2_matrices())
    zs, gc = _inproj(x, sh1, sc1, row(norm1_g[0]), w_in_bf, m1)
    zs_ctx = _inproj_ctx(ctx, csh1, csc1, row(norm1_g[0]), w_in_bf[:, :ds5])

    fg = df // FOURIER_GROUPS
    cc = np.arange(fg)
    ang = 2.0 * np.pi * ((cc[:, None] * cc[None, :]) % fg) / fg
    scale = 1.0 / math.sqrt(l * fg)
    cw = jnp.einsum("cm,gmd->gcd", jnp.asarray(np.cos(ang) * scale, _F32), fourier_w[0], precision=_HI)
    sw = jnp.einsum("cm,gmd->gcd", jnp.asarray(np.sin(ang) * scale, _F32), fourier_w[0], precision=_HI)
    fn = _fft2(gc, m2, cw.astype(_BF), sw.astype(_BF), row(mix_norm_f_g[0]))

    t = S5_CHUNK
    n_ctx, n_lat = lc // t, l // t
    n_grp = ds5 // S5_GROUP
    u_all = jnp.concatenate([zs_ctx, zs], axis=1).astype(_BF)
    u_g = u_all.reshape(b, n_ctx + n_lat, t, n_grp, S5_GROUP).transpose(3, 1, 0, 2, 4)
    u_g = u_g.reshape(n_grp, (n_ctx + n_lat) * b, t * S5_GROUP)
    ws, wy, at = _s5_weights(s5_lam_re[0], s5_lam_im[0], s5_b_re[0], s5_b_im[0],
                             s5_c_re[0], s5_c_im[0], s5_log_step[0])
    y_g = _s5(u_g, ws, wy, at, n_ctx, n_lat, b)
    y_tok = y_g.reshape(n_grp, n_lat, b, t, S5_GROUP).transpose(2, 1, 3, 0, 4).reshape(b, l, ds5)

    n_rt = N_EXPERT_GROUPS * (1 + EXPERTS_PER_GROUP)
    w_rt = jnp.concatenate([moe_w_group[0], moe_w_router[0].reshape(d, -1)], axis=-1)
    w_rt = jnp.pad(w_rt, ((0, 0), (0, ROUTE_LANES - n_rt)))
    b_rt = jnp.pad(jnp.concatenate([moe_b_group[0], moe_b_router[0].reshape(-1)]), (0, ROUTE_LANES - n_rt))
    x1, tmod, rt = _post(x, zs, y_tok, fn, row(s5_d[0]), s5_w_glu[0].astype(_BF), row(mix_norm_s5_g[0]),
                         w_out[0].astype(_BF), g1, row(norm2_g[0]), sh2, sc2, w_rt, row(b_rt))

    tm = MOE_TM
    n_tok = b * l
    nr = 2 * n_tok + n_exp * tm
    rt2 = rt.reshape(n_tok, ROUTE_LANES)
    eid = rt2[:, 0:2].astype(jnp.int32).reshape(-1)
    gates = rt2[:, 2:4].reshape(-1)
    pair_pos, row_token, row_gate, tile_expert = _route_plan(eid, gates, n_exp, tm, nr)
    xs = jnp.take(tmod.reshape(n_tok, d), row_token, axis=0)
    ys = _moe(tile_expert, xs, row_gate.reshape(nr, 1), moe_w_gate[0].astype(_BF),
              moe_w_up[0].astype(_BF), moe_w_down[0].astype(_BF))
    pp = pair_pos.reshape(n_tok, 2)
    y0 = jnp.take(ys, pp[:, 0], axis=0).reshape(b, l, d)
    y1 = jnp.take(ys, pp[:, 1], axis=0).reshape(b, l, d)
    return _final(x1, y0, y1, g2, row(final_norm_g))
```

```python
import functools
import math

import numpy as np
import jax
import jax.numpy as jnp
from jax import lax
from jax.experimental import pallas as pl
from jax.experimental.pallas import tpu as pltpu

EPS = 1e-6
S5_GROUP = 16
S5_STATE = 64
S5_CHUNK = 16
FOURIER_GROUPS = 4
N_EXPERT_GROUPS = 4
EXPERTS_PER_GROUP = 8
FFT_R = 64
FFT_BLK = 8
MOE_TM = 256
ROUTE_LANES = 128
VMEM_LIMIT = 56 * 1024 * 1024

_HI = lax.Precision.HIGHEST
_BF = jnp.bfloat16
_F32 = jnp.float32


def _cparams(sem):
    return pltpu.CompilerParams(dimension_semantics=sem, vmem_limit_bytes=VMEM_LIMIT)


def _dot(a, b):
    return jnp.dot(a, b, preferred_element_type=_F32)


def _rms(x, g):
    return x * lax.rsqrt(jnp.mean(x * x, axis=-1, keepdims=True) + EPS) * g


def _adaln_body(c_ref, w_ref, b_ref, o_ref):
    c = c_ref[...]
    a = c * jax.nn.sigmoid(c)
    o_ref[...] = jnp.dot(a, w_ref[...], preferred_element_type=_F32, precision=_HI) + b_ref[...]


def _adaln(cond, w, b):
    m, d = cond.shape
    n = w.shape[1]
    tn = 768
    return pl.pallas_call(
        _adaln_body,
        grid=(n // tn,),
        in_specs=[pl.BlockSpec((m, d), lambda j: (0, 0)),
                  pl.BlockSpec((d, tn), lambda j: (0, j)),
                  pl.BlockSpec((1, tn), lambda j: (0, j))],
        out_specs=pl.BlockSpec((m, tn), lambda j: (0, j)),
        out_shape=jax.ShapeDtypeStruct((m, n), _F32),
        compiler_params=_cparams(("arbitrary",)),
        name="adaln",
    )(cond, w, b.reshape(1, n))


def _inproj_body(x_ref, sh_ref, sc_ref, g_ref, w_ref, m1_ref, zs_ref, gc_ref):
    r, blk, d = x_ref.shape
    x = x_ref[...].reshape(r * blk, d)
    h = _rms(x, g_ref[...]) * (1.0 + sc_ref[...]) + sh_ref[...]
    z = _dot(h.astype(_BF), w_ref[...])
    ds5 = zs_ref.shape[-1]
    zs_ref[...] = z[:, :ds5].reshape(r, blk, ds5)
    v = z[:, ds5:].astype(_BF)
    g1 = _dot(m1_ref[...], v)
    gc_ref[...] = g1.astype(_BF).reshape(r, 2 * blk, v.shape[-1])


def _inproj(x, sh, sc, g, w_bf, m1):
    b, l, d = x.shape
    r, blk = FFT_R, FFT_BLK
    nj = r // blk
    dmix = w_bf.shape[1]
    ds5 = dmix // 2
    df = dmix - ds5
    x4 = x.reshape(b, r, r, d)
    zs, gc = pl.pallas_call(
        _inproj_body,
        grid=(b, nj),
        in_specs=[pl.BlockSpec((None, r, blk, d), lambda i, j: (i, 0, j, 0)),
                  pl.BlockSpec((None, 1, d), lambda i, j: (i, 0, 0)),
                  pl.BlockSpec((None, 1, d), lambda i, j: (i, 0, 0)),
                  pl.BlockSpec((1, d), lambda i, j: (0, 0)),
                  pl.BlockSpec((d, dmix), lambda i, j: (0, 0)),
                  pl.BlockSpec(m1.shape, lambda i, j: (0, 0))],
        out_specs=[pl.BlockSpec((None, r, blk, ds5), lambda i, j: (i, 0, j, 0)),
                   pl.BlockSpec((None, r, 2 * blk, df), lambda i, j: (i, 0, j, 0))],
        out_shape=[jax.ShapeDtypeStruct((b, r, r, ds5), _F32),
                   jax.ShapeDtypeStruct((b, r, 2 * r, df), _BF)],
        compiler_params=_cparams(("parallel", "arbitrary")),
        name="inproj",
    )(x4, sh, sc, g, w_bf, m1)
    return zs.reshape(b, l, ds5), gc


def _seg_transpose(arrs):
    n = len(arrs)
    seg = lax.broadcasted_iota(jnp.int32, arrs[0].shape, 1) // S5_GROUP
    d = n // 2
    while d >= 1:
        keep = (seg & d) == 0
        new = list(arrs)
        for i in range(n):
            if i & d == 0:
                a, b = arrs[i], arrs[i + d]
                new[i] = jnp.where(keep, a, pltpu.roll(b, S5_GROUP * d, 1))
                new[i + d] = jnp.where(keep, pltpu.roll(a, 128 - S5_GROUP * d, 1), b)
        arrs = new
        d //= 2
    return arrs


@functools.lru_cache(maxsize=None)
def _chunk_row_perm():
    t = S5_CHUNK
    p = np.zeros((t * t, t * t), np.float32)
    for c in range(t):
        for tl in range(t):
            p[tl * t + c, c * t + tl] = 1.0
    return p.astype(jnp.bfloat16)


def _pack_chunks(z_bf, p_ref, u_ref):
    t = S5_CHUNK
    nsub = z_bf.shape[0] // (t * t)
    pieces = [_dot(p_ref[...], z_bf[s * t * t:(s + 1) * t * t, :]) for s in range(nsub)]
    a = []
    for tl in range(t):
        rows = [pc[tl * t:(tl + 1) * t, :] for pc in pieces]
        a.append(rows[0] if nsub == 1 else jnp.concatenate(rows, axis=0))
    lanes = 128
    gpb = lanes // S5_GROUP
    for j in range(z_bf.shape[1] // lanes):
        for hi in range(t // gpb):
            outs = _seg_transpose([a[gpb * hi + tlo][:, lanes * j:lanes * (j + 1)] for tlo in range(gpb)])
            for glo in range(gpb):
                u_ref[gpb * j + glo, :, lanes * hi:lanes * (hi + 1)] = outs[glo].astype(u_ref.dtype)


def _inproj_ctx_body(x_ref, sh_ref, sc_ref, g_ref, w_ref, p_ref, u_ref):
    h = _rms(x_ref[...], g_ref[...]) * (1.0 + sc_ref[...]) + sh_ref[...]
    z = _dot(h.astype(_BF), w_ref[...])
    _pack_chunks(z.astype(_BF), p_ref, u_ref)


def _inproj_ctx(ctx, sh, sc, g, w_s5_bf, perm):
    b, lc, d = ctx.shape
    ds5 = w_s5_bf.shape[1]
    n_grp = ds5 // S5_GROUP
    kk = S5_CHUNK * S5_GROUP
    return pl.pallas_call(
        _inproj_ctx_body,
        grid=(b,),
        in_specs=[pl.BlockSpec((None, lc, d), lambda i: (i, 0, 0)),
                  pl.BlockSpec((1, d), lambda i: (0, 0)),
                  pl.BlockSpec((1, d), lambda i: (0, 0)),
                  pl.BlockSpec((1, d), lambda i: (0, 0)),
                  pl.BlockSpec((d, ds5), lambda i: (0, 0)),
                  pl.BlockSpec(perm.shape, lambda i: (0, 0))],
        out_specs=pl.BlockSpec((n_grp, lc // S5_CHUNK, kk), lambda i: (0, 0, i)),
        out_shape=jax.ShapeDtypeStruct((n_grp, lc // S5_CHUNK, b * kk), _BF),
        compiler_params=_cparams(("arbitrary",)),
        name="inproj_ctx",
    )(ctx, sh, sc, g, w_s5_bf, perm)


def _s5_pack_body(z_ref, p_ref, u_ref):
    _pack_chunks(z_ref[...].astype(_BF), p_ref, u_ref)


def _s5_pack(zs, perm, tok=2048):
    b, l, ds5 = zs.shape
    n_grp = ds5 // S5_GROUP
    kk = S5_CHUNK * S5_GROUP
    return pl.pallas_call(
        _s5_pack_body,
        grid=(b, l // tok),
        in_specs=[pl.BlockSpec((None, tok, ds5), lambda i, j: (i, j, 0)),
                  pl.BlockSpec(perm.shape, lambda i, j: (0, 0))],
        out_specs=pl.BlockSpec((n_grp, tok // S5_CHUNK, kk), lambda i, j: (0, j, i)),
        out_shape=jax.ShapeDtypeStruct((n_grp, l // S5_CHUNK, b * kk), _BF),
        compiler_params=_cparams(("parallel", "arbitrary")),
        name="s5_pack",
    )(zs, perm)


@functools.lru_cache(maxsize=None)
def _fft_stage1_matrix():
    r, blk = FFT_R, FFT_BLK
    k1 = np.arange(r)[:, None]
    l1 = np.arange(r)[None, :]
    ang = 2.0 * np.pi * ((k1 * l1) % r) / r
    f = np.stack([np.cos(ang), -np.sin(ang)], axis=1)
    m = np.einsum("kal,pq->kaplq", f, np.eye(blk))
    return m.reshape(r * 2 * blk, r * blk).astype(np.float32)


@functools.lru_cache(maxsize=None)
def _fft_stage2_matrices():
    r, blk = FFT_R, FFT_BLK
    n = r * r
    nt = r // blk
    l2 = np.arange(r)
    k2 = np.arange(r)
    out = np.zeros((nt, 2, r, blk, blk, r // blk, 2, blk), np.float32)
    for i in range(nt):
        for kl in range(blk):
            k1 = blk * i + kl
            ang = 2.0 * np.pi * (((k2[:, None] * l2[None, :] * r) + l2[None, :] * k1) % n) / n
            tr, ti = np.cos(ang), -np.sin(ang)
            tr = tr.reshape(r, r // blk, blk)
            ti = ti.reshape(r, r // blk, blk)
            out[i, 0, :, kl, kl, :, 0, :] = tr
            out[i, 0, :, kl, kl, :, 1, :] = -ti
            out[i, 1, :, kl, kl, :, 0, :] = ti
            out[i, 1, :, kl, kl, :, 1, :] = tr
    return out.reshape(nt, 2 * r * blk, blk * 2 * r)


def _fft2_body(gc_ref, m2_ref, cw_ref, sw_ref, g_ref, o_ref):
    kb, rr, df = gc_ref.shape
    gc = gc_ref[...].reshape(kb * rr, df)
    x = _dot(m2_ref[...], gc)
    half = x.shape[0] // 2
    xr = x[:half].astype(_BF)
    xi = x[half:].astype(_BF)
    ng = cw_ref.shape[0]
    fg = df // ng
    parts = []
    for g in range(ng):
        sl = slice(g * fg, (g + 1) * fg)
        parts.append(_dot(xr[:, sl], cw_ref[g]) + _dot(xi[:, sl], sw_ref[g]))
    f = jnp.concatenate(parts, axis=-1)
    fn = _rms(f, g_ref[...])
    o_ref[...] = fn.reshape(o_ref.shape)


def _fft2(gc, m2, cw, sw, gf):
    b, r, rr, df = gc.shape
    blk = FFT_BLK
    nt = r // blk
    out = pl.pallas_call(
        _fft2_body,
        grid=(b, nt),
        in_specs=[pl.BlockSpec((None, blk, rr, df), lambda i, j: (i, j, 0, 0)),
                  pl.BlockSpec((None,) + m2.shape[1:], lambda i, j: (j, 0, 0)),
                  pl.BlockSpec(cw.shape, lambda i, j: (0, 0, 0)),
                  pl.BlockSpec(sw.shape, lambda i, j: (0, 0, 0)),
                  pl.BlockSpec((1, df), lambda i, j: (0, 0))],
        out_specs=pl.BlockSpec((None, r, blk, df), lambda i, j: (i, 0, j, 0)),
        out_shape=jax.ShapeDtypeStruct((b, r, r, df), _F32),
        compiler_params=_cparams(("parallel", "arbitrary")),
        name="fft2",
    )(gc, m2, cw, sw, gf)
    return out.reshape(b, r * r, df)


def _s5_body(uc_ref, ul_ref, ws_ref, wy_ref, at_ref, y_ref,
             s_re, s_im, ha_re, hb_re, ha_im, hb_im, *, nb):
    p = S5_STATE
    kk = ws_ref.shape[0]
    n_ctx, n_lat = uc_ref.shape[0], ul_ref.shape[0]
    ws = ws_ref[...]
    for b in range(nb):
        sl = slice(b * kk, (b + 1) * kk)
        sc = _dot(uc_ref[:, sl], ws)
        sl_ = _dot(ul_ref[:, sl], ws)
        s_re[pl.ds(b, n_ctx, stride=nb), :] = sc[:, :2 * p]
        s_im[pl.ds(b, n_ctx, stride=nb), :] = sc[:, 2 * p:]
        s_re[pl.ds(n_ctx * nb + b, n_lat, stride=nb), :] = sl_[:, :2 * p]
        s_im[pl.ds(n_ctx * nb + b, n_lat, stride=nb), :] = sl_[:, 2 * p:]

    a_re = at_ref[0:1, :]
    a_im = at_ref[1:2, :]
    is_f = lax.broadcasted_iota(jnp.int32, (nb, 2 * p), 1) < p
    n_all = n_ctx + n_lat

    def load(cf, cr):
        rf = pl.multiple_of(cf * nb, nb)
        rr = pl.multiple_of(cr * nb, nb)
        return (jnp.where(is_f, s_re[pl.ds(rf, nb), :], s_re[pl.ds(rr, nb), :]),
                jnp.where(is_f, s_im[pl.ds(rf, nb), :], s_im[pl.ds(rr, nb), :]))

    def advance(h_re, h_im, x_re, x_im):
        return (a_re * h_re - a_im * h_im + x_re, a_re * h_im + a_im * h_re + x_im)

    def ctx_step(step, carry):
        x_re, x_im = load(step, n_ctx - 1 - step)
        return advance(*carry, x_re, x_im)

    def lat_step(step, carry):
        h_re, h_im = carry
        rf = pl.multiple_of(step * nb, nb)
        rr = pl.multiple_of((n_lat - 1 - step) * nb, nb)
        ha_re[pl.ds(rf, nb), :] = h_re
        hb_re[pl.ds(rr, nb), :] = h_re
        ha_im[pl.ds(rf, nb), :] = h_im
        hb_im[pl.ds(rr, nb), :] = h_im
        x_re, x_im = load(n_ctx + step, n_all - 1 - step)
        return advance(h_re, h_im, x_re, x_im)

    zero = jnp.zeros((nb, 2 * p), _F32)
    carry = lax.fori_loop(0, n_ctx, ctx_step, (zero, zero))
    lax.fori_loop(0, n_lat, lat_step, carry)

    wy = wy_ref[...]
    for b in range(nb):
        rows = pl.ds(b, n_lat, stride=nb)
        hin = jnp.concatenate([ha_re[rows, :], hb_re[rows, :], ha_im[rows, :], hb_im[rows, :]], axis=-1)
        lhs = jnp.concatenate([ul_ref[:, b * kk:(b + 1) * kk], hin.astype(_BF)], axis=-1)
        y_ref[:, b * kk:(b + 1) * kk] = _dot(lhs, wy)


def _s5(u_ctx, u_lat, ws, wy, at, nb):
    ng, n_ctx, w = u_ctx.shape
    n_lat = u_lat.shape[1]
    body = functools.partial(_s5_body, nb=nb)
    hs = pltpu.VMEM((n_lat * nb, 2 * S5_STATE), _F32)
    ss = pltpu.VMEM(((n_ctx + n_lat) * nb, 2 * S5_STATE), _F32)
    return pl.pallas_call(
        body,
        grid=(ng,),
        in_specs=[pl.BlockSpec((None, n_ctx, w), lambda g: (g, 0, 0)),
                  pl.BlockSpec((None, n_lat, w), lambda g: (g, 0, 0)),
                  pl.BlockSpec((None,) + ws.shape[1:], lambda g: (g, 0, 0)),
                  pl.BlockSpec((None,) + wy.shape[1:], lambda g: (g, 0, 0)),
                  pl.BlockSpec((None,) + at.shape[1:], lambda g: (g, 0, 0))],
        out_specs=pl.BlockSpec((None, n_lat, w), lambda g: (g, 0, 0)),
        out_shape=jax.ShapeDtypeStruct((ng, n_lat, w), _F32),
        scratch_shapes=[ss, ss, hs, hs, hs, hs],
        compiler_params=_cparams(("parallel",)),
        name="s5_scan",
    )(u_ctx, u_lat, ws, wy, at)


def _s5_weights(lam_re, lam_im, b_re, b_im, c_re, c_im, log_step):
    t = S5_CHUNK
    ng, p = lam_re.shape[1], lam_re.shape[2]
    hh = b_re.shape[-1]
    dt = jnp.exp(log_step.astype(_F32))[..., None]
    lr, li = lam_re.astype(_F32), lam_im.astype(_F32)
    mag = jnp.exp(lr * dt)
    ab_re, ab_im = mag * jnp.cos(li * dt), mag * jnp.sin(li * dt)
    den = lr * lr + li * li
    nr, ni = ab_re - 1.0, ab_im
    q_re = (nr * lr + ni * li) / den
    q_im = (ni * lr - nr * li) / den
    br, bi = b_re.astype(_F32), b_im.astype(_F32)
    bb_re = q_re[..., None] * br - q_im[..., None] * bi
    bb_im = q_re[..., None] * bi + q_im[..., None] * br
    k = jnp.arange(t + 1, dtype=_F32)[:, None, None, None]
    pm = jnp.exp(k * (lr * dt)[None])
    ap_re = pm * jnp.cos(k * (li * dt)[None])
    ap_im = pm * jnp.sin(k * (li * dt)[None])
    cr, ci = c_re.astype(_F32), c_im.astype(_F32)

    def ein(spec, *ops):
        return jnp.einsum(spec, *ops, precision=_HI)

    ca_re = cr[None] * ap_re[:, :, :, None, :] - ci[None] * ap_im[:, :, :, None, :]
    ca_im = cr[None] * ap_im[:, :, :, None, :] + ci[None] * ap_re[:, :, :, None, :]
    kk = ein("kdghp,dgpj->kdghj", ca_re, bb_re) - ein("kdghp,dgpj->kdghj", ca_im, bb_im)
    tl = jnp.arange(t)
    lag_f = tl[None, :] - tl[:, None]
    kf = jnp.where((lag_f >= 0)[:, :, None, None, None], kk[jnp.clip(lag_f, 0, t), 0], 0.0)
    kr = jnp.where((lag_f <= 0)[:, :, None, None, None], kk[jnp.clip(-lag_f, 0, t), 1], 0.0)
    msum = (kf + kr).transpose(2, 0, 4, 1, 3).reshape(ng, t * hh, t * hh)

    pw_f = (t - 1 - tl)
    pw_r = tl

    def state_w(pw, d):
        ar, ai = ap_re[pw, d], ap_im[pw, d]
        wr = ar[..., None] * bb_re[d][None] - ai[..., None] * bb_im[d][None]
        wi = ar[..., None] * bb_im[d][None] + ai[..., None] * bb_re[d][None]
        wr = wr.transpose(1, 0, 3, 2).reshape(ng, t * hh, p)
        wi = wi.transpose(1, 0, 3, 2).reshape(ng, t * hh, p)
        return wr, wi

    wf_re, wf_im = state_w(pw_f, 0)
    wr_re, wr_im = state_w(pw_r, 1)
    ws = jnp.concatenate([wf_re, wr_re, wf_im, wr_im], axis=-1)

    def out_w(pw, d):
        xr = ca_re[pw, d]
        xi = ca_im[pw, d]
        vr = xr.transpose(1, 3, 0, 2).reshape(ng, p, t * hh)
        vi = (-xi).transpose(1, 3, 0, 2).reshape(ng, p, t * hh)
        return vr, vi

    vf_re, vf_im = out_w(tl + 1, 0)
    vr_re, vr_im = out_w(t - tl, 1)
    zz = jnp.zeros_like(vf_re)
    wy = jnp.concatenate([msum, vf_re, zz, zz, vr_re, vf_im, zz, zz, vr_im], axis=1)

    at = jnp.stack([jnp.concatenate([ap_re[t, 0], ap_re[t, 1]], axis=-1),
                    jnp.concatenate([ap_im[t, 0], ap_im[t, 1]], axis=-1)], axis=1)
    at = jnp.pad(at, ((0, 0), (0, 6), (0, 0)))
    return ws.astype(_BF), wy.astype(_BF), at


def _post_body(x_ref, u_ref, yg_ref, fn_ref, d_ref, wglu_ref, gs_ref, wo_ref, g1_ref,
               n2_ref, sh_ref, sc_ref, wr_ref, br_ref, tri_ref, x1_ref, t_ref, rt_ref, cnt_ref,
               y_scr, cnt_scr):
    ds5 = u_ref.shape[-1]
    first = (pl.program_id(0) == 0) & (pl.program_id(1) == 0)

    @pl.when(first)
    def _():
        cnt_scr[...] = jnp.zeros_like(cnt_scr)

    t_chunk = S5_CHUNK
    n_chunk = yg_ref.shape[1]
    lanes = 128
    gpb = lanes // S5_GROUP
    for j in range(ds5 // lanes):
        for hi in range(t_chunk // gpb):
            outs = _seg_transpose([yg_ref[gpb * j + glo, :, lanes * hi:lanes * (hi + 1)] for glo in range(gpb)])
            for tlo in range(gpb):
                y_scr[j, pl.ds(gpb * hi + tlo, n_chunk, stride=t_chunk), :] = outs[tlo]
    y_s5 = jnp.concatenate([y_scr[j] for j in range(ds5 // lanes)], axis=-1)

    y = y_s5 + d_ref[...] * u_ref[...]
    y = jax.nn.gelu(y, approximate=True)
    y = y * jax.nn.sigmoid(_dot(y.astype(_BF), wglu_ref[...]))
    yn = _rms(y, gs_ref[...]).astype(_BF)
    mix = _dot(yn, wo_ref[0:ds5, :]) + _dot(fn_ref[...].astype(_BF), wo_ref[ds5:, :])
    x1 = x_ref[...] + g1_ref[...] * mix
    x1_ref[...] = x1
    t = _rms(x1, n2_ref[...]) * (1.0 + sc_ref[...]) + sh_ref[...]
    t_ref[...] = t.astype(_BF)

    ng, epg = N_EXPERT_GROUPS, EXPERTS_PER_GROUP
    logits = jnp.dot(t, wr_ref[...], preferred_element_type=_F32, precision=_HI) + br_ref[...]
    lane = lax.broadcasted_iota(jnp.int32, logits.shape, 1)
    neg = jnp.float32(-jnp.inf)
    big = jnp.int32(1 << 20)
    gl = jnp.where(lane < ng, logits, neg)
    gmax = jnp.max(gl, axis=-1, keepdims=True)
    gidx = jnp.min(jnp.where(gl == gmax, lane, big), axis=-1, keepdims=True)
    gw = 1.0 / jnp.sum(jnp.where(lane < ng, jnp.exp(logits - gmax), 0.0), axis=-1, keepdims=True)
    lo = ng + gidx * epg
    el = jnp.where((lane >= lo) & (lane < lo + epg), logits, neg)
    v0 = jnp.max(el, axis=-1, keepdims=True)
    i0 = jnp.min(jnp.where(el == v0, lane, big), axis=-1, keepdims=True)
    el1 = jnp.where(lane == i0, neg, el)
    v1 = jnp.max(el1, axis=-1, keepdims=True)
    i1 = jnp.min(jnp.where(el1 == v1, lane, big), axis=-1, keepdims=True)
    p0 = 1.0 / (1.0 + jnp.exp(v1 - v0))
    w0 = gw * p0
    w1 = gw * (1.0 - p0)
    e0 = i0 - ng
    e1 = i1 - ng

    oh0 = lane == e0
    oh1 = lane == e1
    oh = (oh0 | oh1).astype(_F32)
    prefix = _dot(tri_ref[...], oh.astype(_BF)) + cnt_scr[...]
    r0 = jnp.sum(jnp.where(oh0, prefix, 0.0), axis=-1, keepdims=True)
    r1 = jnp.sum(jnp.where(oh1, prefix, 0.0), axis=-1, keepdims=True)
    cnt = cnt_scr[...] + jnp.sum(oh, axis=0, keepdims=True)
    cnt_scr[...] = cnt
    cnt_ref[...] = cnt

    vals = (e0.astype(_F32), e1.astype(_F32), w0, w1, r0, r1)
    rt = jnp.zeros(logits.shape, _F32)
    for k, v in enumerate(vals):
        rt = jnp.where(lane == k, v, rt)
    rt_ref[...] = rt


@functools.lru_cache(maxsize=None)
def _strict_lower_ones(n):
    return np.tril(np.ones((n, n), np.float32), -1).astype(jnp.bfloat16)


def _post(x, u, y_g, fn, s5_d, wglu_bf, gs, wo_bf, g1, n2, sh2, sc2, wr, br, tm=512):
    b, l, d = x.shape
    ds5 = u.shape[-1]
    n_grp, _, w = y_g.shape
    kk = w // b
    tri = jnp.asarray(_strict_lower_ones(tm))
    tok = lambda w: pl.BlockSpec((None, tm, w), lambda i, j: (i, j, 0))
    per_b = pl.BlockSpec((None, 1, d), lambda i, j: (i, 0, 0))
    full = lambda a: pl.BlockSpec(a.shape, lambda i, j: (0,) * a.ndim)
    ygs = pl.BlockSpec((n_grp, tm // S5_CHUNK, kk), lambda i, j: (0, j, i))
    return pl.pallas_call(
        _post_body,
        grid=(b, l // tm),
        in_specs=[tok(d), tok(ds5), ygs, tok(fn.shape[-1]), full(s5_d), full(wglu_bf),
                  full(gs), full(wo_bf), per_b, full(n2), per_b, per_b, full(wr), full(br), full(tri)],
        out_specs=[tok(d), tok(d), tok(ROUTE_LANES), pl.BlockSpec((1, ROUTE_LANES), lambda i, j: (0, 0))],
        out_shape=[jax.ShapeDtypeStruct((b, l, d), _F32),
                   jax.ShapeDtypeStruct((b, l, d), _BF),
                   jax.ShapeDtypeStruct((b, l, ROUTE_LANES), _F32),
                   jax.ShapeDtypeStruct((1, ROUTE_LANES), _F32)],
        scratch_shapes=[pltpu.VMEM((ds5 // 128, tm, 128), _F32),
                        pltpu.VMEM((1, ROUTE_LANES), _F32)],
        compiler_params=_cparams(("arbitrary", "arbitrary")),
        name="post_mixer",
    )(x, u, y_g, fn, s5_d, wglu_bf, gs, wo_bf, g1, n2, sh2, sc2, wr, br, tri)


def _moe_body(te_ref, xs_ref, wg_ref, wu_ref, wd_ref, o_ref, wg_bf, wu_bf, wd_bf):
    i = pl.program_id(0)
    new_expert = (i == 0) | (te_ref[i] != te_ref[jnp.maximum(i - 1, 0)])

    @pl.when(new_expert)
    def _():
        wg_bf[...] = wg_ref[...].astype(_BF)
        wu_bf[...] = wu_ref[...].astype(_BF)
        wd_bf[...] = wd_ref[...].astype(_BF)

    xs = xs_ref[...]
    hg = _dot(xs, wg_bf[...])
    hu = _dot(xs, wu_bf[...])
    a = hg * jax.nn.sigmoid(hg) * hu
    o_ref[...] = _dot(a.astype(_BF), wd_bf[...]).astype(o_ref.dtype)


def _moe(tile_expert, xs, wg, wu, wd):
    nr, d = xs.shape
    tm = MOE_TM
    de = wg.shape[-1]
    grid_spec = pltpu.PrefetchScalarGridSpec(
        num_scalar_prefetch=1,
        grid=(nr // tm,),
        in_specs=[pl.BlockSpec((tm, d), lambda i, te: (i, 0)),
                  pl.BlockSpec((None, d, de), lambda i, te: (te[i], 0, 0)),
                  pl.BlockSpec((None, d, de), lambda i, te: (te[i], 0, 0)),
                  pl.BlockSpec((None, de, d), lambda i, te: (te[i], 0, 0))],
        out_specs=pl.BlockSpec((tm, d), lambda i, te: (i, 0)),
        scratch_shapes=[pltpu.VMEM((d, de), _BF), pltpu.VMEM((d, de), _BF), pltpu.VMEM((de, d), _BF)],
    )
    return pl.pallas_call(
        _moe_body,
        grid_spec=grid_spec,
        out_shape=jax.ShapeDtypeStruct((nr, d), _BF),
        compiler_params=_cparams(("arbitrary",)),
        name="moe_experts",
    )(tile_expert, xs, wg, wu, wd)


def _final_body(x1_ref, y0_ref, y1_ref, rt_ref, g2_ref, gf_ref, o_ref):
    w0 = rt_ref[:, 2:3]
    w1 = rt_ref[:, 3:4]
    m = w0 * y0_ref[...].astype(_F32) + w1 * y1_ref[...].astype(_F32)
    o_ref[...] = _rms(x1_ref[...] + g2_ref[...] * m, gf_ref[...])


def _final(x1, y0, y1, rt, g2, gf, tm=512):
    b, l, d = x1.shape
    tok = pl.BlockSpec((None, tm, d), lambda i, j: (i, j, 0))
    return pl.pallas_call(
        _final_body,
        grid=(b, l // tm),
        in_specs=[tok, tok, tok, pl.BlockSpec((None, tm, ROUTE_LANES), lambda i, j: (i, j, 0)),
                  pl.BlockSpec((None, 1, d), lambda i, j: (i, 0, 0)),
                  pl.BlockSpec((1, d), lambda i, j: (0, 0))],
        out_specs=tok,
        out_shape=jax.ShapeDtypeStruct((b, l, d), _F32),
        compiler_params=_cparams(("parallel", "arbitrary")),
        name="final_norm",
    )(x1, y0, y1, rt, g2, gf)


def _route_plan(eid, rank, counts, tm, nr):
    n_experts = counts.shape[0]
    padded = ((counts + tm - 1) // tm) * tm
    pad_end = jnp.cumsum(padded)
    pad_start = pad_end - padded
    onehot = eid[..., None] == jnp.arange(n_experts, dtype=jnp.int32)
    pos = jnp.sum(jnp.where(onehot, pad_start, 0), axis=-1) + rank
    tile_start = jnp.arange(nr // tm, dtype=jnp.int32) * tm
    tile_expert = jnp.sum(tile_start[:, None] >= pad_end[None, :], axis=-1)
    tile_expert = jnp.minimum(tile_expert, n_experts - 1).astype(jnp.int32)
    return pos.astype(jnp.int32), tile_expert


def kernel(x, c, ctx, c_ctx, w_ada, b_ada, norm1_g, norm2_g, w_in, s5_lam_re, s5_lam_im, s5_b_re, s5_b_im, s5_c_re, s5_c_im, s5_log_step, s5_d, s5_w_glu, fourier_w, mix_norm_s5_g, mix_norm_f_g, w_out, moe_w_group, moe_b_group, moe_w_router, moe_b_router, moe_w_gate, moe_w_up, moe_w_down, final_norm_g):
    b, l, d = x.shape
    lc = ctx.shape[1]
    depth = w_ada.shape[0]
    assert depth == 1 and l == FFT_R * FFT_R and lc % S5_CHUNK == 0 and b % 8 == 0
    ds5 = s5_d.shape[-1]
    df = w_in.shape[-1] - ds5
    n_exp = moe_w_gate.shape[1]
    row = lambda a: a.reshape(1, -1)

    cond = jnp.concatenate([c, c_ctx[None, :], jnp.zeros((7, d), _F32)], axis=0)
    mod = _adaln(cond, w_ada[0], b_ada[0])
    sh1, sc1, g1, sh2, sc2, g2 = [mod[:b, i * d:(i + 1) * d].reshape(b, 1, d) for i in range(6)]
    csh1 = mod[b:b + 1, 0:d]
    csc1 = mod[b:b + 1, d:2 * d]

    w_in_bf = w_in[0].astype(_BF)
    m1 = jnp.asarray(_fft_stage1_matrix()).astype(_BF)
    m2 = jnp.asarray(_fft_stage2_matrices()).astype(_BF)
    perm = jnp.asarray(_chunk_row_perm())
    zs, gc = _inproj(x, sh1, sc1, row(norm1_g[0]), w_in_bf, m1)
    u_ctx = _inproj_ctx(ctx, csh1, csc1, row(norm1_g[0]), w_in_bf[:, :ds5], perm)
    u_lat = _s5_pack(zs, perm)

    fg = df // FOURIER_GROUPS
    cc = np.arange(fg)
    ang = 2.0 * np.pi * ((cc[:, None] * cc[None, :]) % fg) / fg
    scale = 1.0 / math.sqrt(l * fg)
    cw = jnp.einsum("cm,gmd->gcd", jnp.asarray(np.cos(ang) * scale, _F32), fourier_w[0], precision=_HI)
    sw = jnp.einsum("cm,gmd->gcd", jnp.asarray(np.sin(ang) * scale, _F32), fourier_w[0], precision=_HI)
    fn = _fft2(gc, m2, cw.astype(_BF), sw.astype(_BF), row(mix_norm_f_g[0]))

    ws, wy, at = _s5_weights(s5_lam_re[0], s5_lam_im[0], s5_b_re[0], s5_b_im[0],
                             s5_c_re[0], s5_c_im[0], s5_log_step[0])
    y_g = _s5(u_ctx, u_lat, ws, wy, at, b)

    n_rt = N_EXPERT_GROUPS * (1 + EXPERTS_PER_GROUP)
    w_rt = jnp.concatenate([moe_w_group[0], moe_w_router[0].reshape(d, -1)], axis=-1)
    w_rt = jnp.pad(w_rt, ((0, 0), (0, ROUTE_LANES - n_rt)))
    b_rt = jnp.pad(jnp.concatenate([moe_b_group[0], moe_b_router[0].reshape(-1)]), (0, ROUTE_LANES - n_rt))
    x1, tmod, rt, cnt = _post(x, zs, y_g, fn, row(s5_d[0]), s5_w_glu[0].astype(_BF), row(mix_norm_s5_g[0]),
                              w_out[0].astype(_BF), g1, row(norm2_g[0]), sh2, sc2, w_rt, row(b_rt))

    tm = MOE_TM
    n_tok = b * l
    nr = 2 * n_tok + n_exp * tm
    rt2 = rt.reshape(n_tok, ROUTE_LANES)
    eid = rt2[:, 0:2].astype(jnp.int32)
    rank = rt2[:, 4:6].astype(jnp.int32)
    pos, tile_expert = _route_plan(eid, rank, cnt[0, :n_exp].astype(jnp.int32), tm, nr)
    t_rows = tmod.reshape(n_tok, d)
    xs = jnp.zeros((nr, d), _BF)
    xs = xs.at[pos[:, 0]].set(t_rows, unique_indices=True, mode="drop")
    xs = xs.at[pos[:, 1]].set(t_rows, unique_indices=True, mode="drop")
    ys = _moe(tile_expert, xs, moe_w_gate[0], moe_w_up[0], moe_w_down[0])
    y0 = jnp.take(ys, pos[:, 0], axis=0).reshape(b, l, d)
    y1 = jnp.take(ys, pos[:, 1], axis=0).reshape(b, l, d)
    return _final(x1, y0, y1, rt, g2, row(final_norm_g))
```

```python
import functools
import math

import numpy as np
import jax
import jax.numpy as jnp
from jax import lax
from jax.experimental import pallas as pl
from jax.experimental.pallas import tpu as pltpu

EPS = 1e-6
S5_GROUP = 16
S5_STATE = 64
S5_CHUNK = 16
FOURIER_GROUPS = 4
N_EXPERT_GROUPS = 4
EXPERTS_PER_GROUP = 8
FFT_R = 64
FFT_BLK = 8
MOE_TM = 256
ROUTE_LANES = 128
VMEM_LIMIT = 56 * 1024 * 1024

_HI = lax.Precision.HIGHEST
_BF = jnp.bfloat16
_F32 = jnp.float32


def _cparams(sem):
    return pltpu.CompilerParams(dimension_semantics=sem, vmem_limit_bytes=VMEM_LIMIT)


def _dot(a, b):
    return jnp.dot(a, b, preferred_element_type=_F32)


def _rms(x, g):
    return x * lax.rsqrt(jnp.mean(x * x, axis=-1, keepdims=True) + EPS) * g


def _adaln_body(c_ref, w_ref, b_ref, o_ref):
    c = c_ref[...]
    a = c * jax.nn.sigmoid(c)
    o_ref[...] = jnp.dot(a, w_ref[...], preferred_element_type=_F32, precision=_HI) + b_ref[...]


def _adaln(cond, w, b):
    m, d = cond.shape
    n = w.shape[1]
    tn = 768
    return pl.pallas_call(
        _adaln_body,
        grid=(n // tn,),
        in_specs=[pl.BlockSpec((m, d), lambda j: (0, 0)),
                  pl.BlockSpec((d, tn), lambda j: (0, j)),
                  pl.BlockSpec((1, tn), lambda j: (0, j))],
        out_specs=pl.BlockSpec((m, tn), lambda j: (0, j)),
        out_shape=jax.ShapeDtypeStruct((m, n), _F32),
        compiler_params=_cparams(("arbitrary",)),
        name="adaln",
    )(cond, w, b.reshape(1, n))


def _inproj_body(x_ref, sh_ref, sc_ref, g_ref, w_ref, m1_ref, zs_ref, gc_ref):
    r, blk, d = x_ref.shape
    x = x_ref[...].reshape(r * blk, d)
    h = _rms(x, g_ref[...]) * (1.0 + sc_ref[...]) + sh_ref[...]
    z = _dot(h.astype(_BF), w_ref[...])
    ds5 = zs_ref.shape[-1]
    zs_ref[...] = z[:, :ds5].reshape(r, blk, ds5)
    v = z[:, ds5:].astype(_BF)
    g1 = _dot(m1_ref[...], v)
    gc_ref[...] = g1.astype(_BF).reshape(r, 2 * blk, v.shape[-1])


def _inproj(x, sh, sc, g, w_bf, m1):
    b, l, d = x.shape
    r, blk = FFT_R, FFT_BLK
    nj = r // blk
    dmix = w_bf.shape[1]
    ds5 = dmix // 2
    df = dmix - ds5
    x4 = x.reshape(b, r, r, d)
    zs, gc = pl.pallas_call(
        _inproj_body,
        grid=(b, nj),
        in_specs=[pl.BlockSpec((None, r, blk, d), lambda i, j: (i, 0, j, 0)),
                  pl.BlockSpec((None, 1, d), lambda i, j: (i, 0, 0)),
                  pl.BlockSpec((None, 1, d), lambda i, j: (i, 0, 0)),
                  pl.BlockSpec((1, d), lambda i, j: (0, 0)),
                  pl.BlockSpec((d, dmix), lambda i, j: (0, 0)),
                  pl.BlockSpec(m1.shape, lambda i, j: (0, 0))],
        out_specs=[pl.BlockSpec((None, r, blk, ds5), lambda i, j: (i, 0, j, 0)),
                   pl.BlockSpec((None, r, 2 * blk, df), lambda i, j: (i, 0, j, 0))],
        out_shape=[jax.ShapeDtypeStruct((b, r, r, ds5), _F32),
                   jax.ShapeDtypeStruct((b, r, 2 * r, df), _BF)],
        compiler_params=_cparams(("parallel", "arbitrary")),
        name="inproj",
    )(x4, sh, sc, g, w_bf, m1)
    return zs.reshape(b, l, ds5), gc


def _seg_transpose(arrs):
    n = len(arrs)
    seg = lax.broadcasted_iota(jnp.int32, arrs[0].shape, 1) // S5_GROUP
    d = n // 2
    while d >= 1:
        keep = (seg & d) == 0
        new = list(arrs)
        for i in range(n):
            if i & d == 0:
                a, b = arrs[i], arrs[i + d]
                new[i] = jnp.where(keep, a, pltpu.roll(b, S5_GROUP * d, 1))
                new[i + d] = jnp.where(keep, pltpu.roll(a, 128 - S5_GROUP * d, 1), b)
        arrs = new
        d //= 2
    return arrs


@functools.lru_cache(maxsize=None)
def _chunk_row_perm():
    t = S5_CHUNK
    p = np.zeros((t * t, t * t), np.float32)
    for c in range(t):
        for tl in range(t):
            p[tl * t + c, c * t + tl] = 1.0
    return p.astype(jnp.bfloat16)


def _pack_chunks(z_bf, p_ref, u_ref):
    t = S5_CHUNK
    nsub = z_bf.shape[0] // (t * t)
    pieces = [_dot(p_ref[...], z_bf[s * t * t:(s + 1) * t * t, :]) for s in range(nsub)]
    a = []
    for tl in range(t):
        rows = [pc[tl * t:(tl + 1) * t, :] for pc in pieces]
        a.append(rows[0] if nsub == 1 else jnp.concatenate(rows, axis=0))
    lanes = 128
    gpb = lanes // S5_GROUP
    for j in range(z_bf.shape[1] // lanes):
        for hi in range(t // gpb):
            outs = _seg_transpose([a[gpb * hi + tlo][:, lanes * j:lanes * (j + 1)] for tlo in range(gpb)])
            for glo in range(gpb):
                u_ref[gpb * j + glo, :, lanes * hi:lanes * (hi + 1)] = outs[glo].astype(u_ref.dtype)


def _inproj_ctx_body(x_ref, sh_ref, sc_ref, g_ref, w_ref, p_ref, u_ref):
    h = _rms(x_ref[...], g_ref[...]) * (1.0 + sc_ref[...]) + sh_ref[...]
    z = _dot(h.astype(_BF), w_ref[...])
    _pack_chunks(z.astype(_BF), p_ref, u_ref)


def _inproj_ctx(ctx, sh, sc, g, w_s5_bf, perm):
    b, lc, d = ctx.shape
    ds5 = w_s5_bf.shape[1]
    n_grp = ds5 // S5_GROUP
    kk = S5_CHUNK * S5_GROUP
    return pl.pallas_call(
        _inproj_ctx_body,
        grid=(b,),
        in_specs=[pl.BlockSpec((None, lc, d), lambda i: (i, 0, 0)),
                  pl.BlockSpec((1, d), lambda i: (0, 0)),
                  pl.BlockSpec((1, d), lambda i: (0, 0)),
                  pl.BlockSpec((1, d), lambda i: (0, 0)),
                  pl.BlockSpec((d, ds5), lambda i: (0, 0)),
                  pl.BlockSpec(perm.shape, lambda i: (0, 0))],
        out_specs=pl.BlockSpec((n_grp, lc // S5_CHUNK, kk), lambda i: (0, 0, i)),
        out_shape=jax.ShapeDtypeStruct((n_grp, lc // S5_CHUNK, b * kk), _BF),
        compiler_params=_cparams(("arbitrary",)),
        name="inproj_ctx",
    )(ctx, sh, sc, g, w_s5_bf, perm)


def _s5_pack_body(z_ref, p_ref, u_ref):
    _pack_chunks(z_ref[...].astype(_BF), p_ref, u_ref)


def _s5_pack(zs, perm, tok=2048):
    b, l, ds5 = zs.shape
    n_grp = ds5 // S5_GROUP
    kk = S5_CHUNK * S5_GROUP
    return pl.pallas_call(
        _s5_pack_body,
        grid=(b, l // tok),
        in_specs=[pl.BlockSpec((None, tok, ds5), lambda i, j: (i, j, 0)),
                  pl.BlockSpec(perm.shape, lambda i, j: (0, 0))],
        out_specs=pl.BlockSpec((n_grp, tok // S5_CHUNK, kk), lambda i, j: (0, j, i)),
        out_shape=jax.ShapeDtypeStruct((n_grp, l // S5_CHUNK, b * kk), _BF),
        compiler_params=_cparams(("parallel", "arbitrary")),
        name="s5_pack",
    )(zs, perm)


@functools.lru_cache(maxsize=None)
def _fft_stage1_matrix():
    r, blk = FFT_R, FFT_BLK
    k1 = np.arange(r)[:, None]
    l1 = np.arange(r)[None, :]
    ang = 2.0 * np.pi * ((k1 * l1) % r) / r
    f = np.stack([np.cos(ang), -np.sin(ang)], axis=1)
    m = np.einsum("kal,pq->kaplq", f, np.eye(blk))
    return m.reshape(r * 2 * blk, r * blk).astype(np.float32)


@functools.lru_cache(maxsize=None)
def _fft_stage2_matrices():
    r, blk = FFT_R, FFT_BLK
    n = r * r
    nt = r // blk
    l2 = np.arange(r)
    k2 = np.arange(r)
    out = np.zeros((nt, 2, r, blk, blk, r // blk, 2, blk), np.float32)
    for i in range(nt):
        for kl in range(blk):
            k1 = blk * i + kl
            ang = 2.0 * np.pi * (((k2[:, None] * l2[None, :] * r) + l2[None, :] * k1) % n) / n
            tr, ti = np.cos(ang), -np.sin(ang)
            tr = tr.reshape(r, r // blk, blk)
            ti = ti.reshape(r, r // blk, blk)
            out[i, 0, :, kl, kl, :, 0, :] = tr
            out[i, 0, :, kl, kl, :, 1, :] = -ti
            out[i, 1, :, kl, kl, :, 0, :] = ti
            out[i, 1, :, kl, kl, :, 1, :] = tr
    return out.reshape(nt, 2 * r * blk, blk * 2 * r)


def _fft2_body(gc_ref, m2_ref, cw_ref, sw_ref, g_ref, o_ref):
    kb, rr, df = gc_ref.shape
    gc = gc_ref[...].reshape(kb * rr, df)
    x = _dot(m2_ref[...], gc)
    half = x.shape[0] // 2
    xr = x[:half].astype(_BF)
    xi = x[half:].astype(_BF)
    ng = cw_ref.shape[0]
    fg = df // ng
    parts = []
    for g in range(ng):
        sl = slice(g * fg, (g + 1) * fg)
        parts.append(_dot(xr[:, sl], cw_ref[g]) + _dot(xi[:, sl], sw_ref[g]))
    f = jnp.concatenate(parts, axis=-1)
    fn = _rms(f, g_ref[...])
    o_ref[...] = fn.reshape(o_ref.shape)


def _fft2(gc, m2, cw, sw, gf):
    b, r, rr, df = gc.shape
    blk = FFT_BLK
    nt = r // blk
    out = pl.pallas_call(
        _fft2_body,
        grid=(b, nt),
        in_specs=[pl.BlockSpec((None, blk, rr, df), lambda i, j: (i, j, 0, 0)),
                  pl.BlockSpec((None,) + m2.shape[1:], lambda i, j: (j, 0, 0)),
                  pl.BlockSpec(cw.shape, lambda i, j: (0, 0, 0)),
                  pl.BlockSpec(sw.shape, lambda i, j: (0, 0, 0)),
                  pl.BlockSpec((1, df), lambda i, j: (0, 0))],
        out_specs=pl.BlockSpec((None, r, blk, df), lambda i, j: (i, 0, j, 0)),
        out_shape=jax.ShapeDtypeStruct((b, r, r, df), _F32),
        compiler_params=_cparams(("parallel", "arbitrary")),
        name="fft2",
    )(gc, m2, cw, sw, gf)
    return out.reshape(b, r * r, df)


def _s5_body(uc_ref, ul_ref, ws_ref, wy_ref, at_ref, y_ref,
             s_re, s_im, ha_re, hb_re, ha_im, hb_im, *, nb):
    p = S5_STATE
    kk = ws_ref.shape[0]
    n_ctx, n_lat = uc_ref.shape[0], ul_ref.shape[0]
    ws = ws_ref[...]
    for b in range(nb):
        sl = slice(b * kk, (b + 1) * kk)
        sc = _dot(uc_ref[:, sl], ws)
        sl_ = _dot(ul_ref[:, sl], ws)
        s_re[pl.ds(b, n_ctx, stride=nb), :] = sc[:, :2 * p]
        s_im[pl.ds(b, n_ctx, stride=nb), :] = sc[:, 2 * p:]
        s_re[pl.ds(n_ctx * nb + b, n_lat, stride=nb), :] = sl_[:, :2 * p]
        s_im[pl.ds(n_ctx * nb + b, n_lat, stride=nb), :] = sl_[:, 2 * p:]

    a_re = at_ref[0:1, :]
    a_im = at_ref[1:2, :]
    is_f = lax.broadcasted_iota(jnp.int32, (nb, 2 * p), 1) < p
    n_all = n_ctx + n_lat

    def load(cf, cr):
        rf = pl.multiple_of(cf * nb, nb)
        rr = pl.multiple_of(cr * nb, nb)
        return (jnp.where(is_f, s_re[pl.ds(rf, nb), :], s_re[pl.ds(rr, nb), :]),
                jnp.where(is_f, s_im[pl.ds(rf, nb), :], s_im[pl.ds(rr, nb), :]))

    def advance(h_re, h_im, x_re, x_im):
        return (a_re * h_re - a_im * h_im + x_re, a_re * h_im + a_im * h_re + x_im)

    def ctx_step(step, carry):
        x_re, x_im = load(step, n_ctx - 1 - step)
        return advance(*carry, x_re, x_im)

    def lat_step(step, carry):
        h_re, h_im = carry
        rf = pl.multiple_of(step * nb, nb)
        rr = pl.multiple_of((n_lat - 1 - step) * nb, nb)
        ha_re[pl.ds(rf, nb), :] = h_re
        hb_re[pl.ds(rr, nb), :] = h_re
        ha_im[pl.ds(rf, nb), :] = h_im
        hb_im[pl.ds(rr, nb), :] = h_im
        x_re, x_im = load(n_ctx + step, n_all - 1 - step)
        return advance(h_re, h_im, x_re, x_im)

    zero = jnp.zeros((nb, 2 * p), _F32)
    carry = lax.fori_loop(0, n_ctx, ctx_step, (zero, zero))
    lax.fori_loop(0, n_lat, lat_step, carry)

    wy = wy_ref[...]
    for b in range(nb):
        rows = pl.ds(b, n_lat, stride=nb)
        hin = jnp.concatenate([ha_re[rows, :], hb_re[rows, :], ha_im[rows, :], hb_im[rows, :]], axis=-1)
        lhs = jnp.concatenate([ul_ref[:, b * kk:(b + 1) * kk], hin.astype(_BF)], axis=-1)
        y_ref[:, b * kk:(b + 1) * kk] = _dot(lhs, wy)


def _s5(u_ctx, u_lat, ws, wy, at, nb):
    ng, n_ctx, w = u_ctx.shape
    n_lat = u_lat.shape[1]
    body = functools.partial(_s5_body, nb=nb)
    hs = pltpu.VMEM((n_lat * nb, 2 * S5_STATE), _F32)
    ss = pltpu.VMEM(((n_ctx + n_lat) * nb, 2 * S5_STATE), _F32)
    return pl.pallas_call(
        body,
        grid=(ng,),
        in_specs=[pl.BlockSpec((None, n_ctx, w), lambda g: (g, 0, 0)),
                  pl.BlockSpec((None, n_lat, w), lambda g: (g, 0, 0)),
                  pl.BlockSpec((None,) + ws.shape[1:], lambda g: (g, 0, 0)),
                  pl.BlockSpec((None,) + wy.shape[1:], lambda g: (g, 0, 0)),
                  pl.BlockSpec((None,) + at.shape[1:], lambda g: (g, 0, 0))],
        out_specs=pl.BlockSpec((None, n_lat, w), lambda g: (g, 0, 0)),
        out_shape=jax.ShapeDtypeStruct((ng, n_lat, w), _F32),
        scratch_shapes=[ss, ss, hs, hs, hs, hs],
        compiler_params=_cparams(("parallel",)),
        name="s5_scan",
    )(u_ctx, u_lat, ws, wy, at)


def _s5_weights(lam_re, lam_im, b_re, b_im, c_re, c_im, log_step):
    t = S5_CHUNK
    ng, p = lam_re.shape[1], lam_re.shape[2]
    hh = b_re.shape[-1]
    dt = jnp.exp(log_step.astype(_F32))[..., None]
    lr, li = lam_re.astype(_F32), lam_im.astype(_F32)
    mag = jnp.exp(lr * dt)
    ab_re, ab_im = mag * jnp.cos(li * dt), mag * jnp.sin(li * dt)
    den = lr * lr + li * li
    nr, ni = ab_re - 1.0, ab_im
    q_re = (nr * lr + ni * li) / den
    q_im = (ni * lr - nr * li) / den
    br, bi = b_re.astype(_F32), b_im.astype(_F32)
    bb_re = q_re[..., None] * br - q_im[..., None] * bi
    bb_im = q_re[..., None] * bi + q_im[..., None] * br
    k = jnp.arange(t + 1, dtype=_F32)[:, None, None, None]
    pm = jnp.exp(k * (lr * dt)[None])
    ap_re = pm * jnp.cos(k * (li * dt)[None])
    ap_im = pm * jnp.sin(k * (li * dt)[None])
    cr, ci = c_re.astype(_F32), c_im.astype(_F32)

    def ein(spec, *ops):
        return jnp.einsum(spec, *ops, precision=_HI)

    ca_re = cr[None] * ap_re[:, :, :, None, :] - ci[None] * ap_im[:, :, :, None, :]
    ca_im = cr[None] * ap_im[:, :, :, None, :] + ci[None] * ap_re[:, :, :, None, :]
    kk = ein("kdghp,dgpj->kdghj", ca_re, bb_re) - ein("kdghp,dgpj->kdghj", ca_im, bb_im)
    tl = jnp.arange(t)
    lag_f = tl[None, :] - tl[:, None]
    kf = jnp.where((lag_f >= 0)[:, :, None, None, None], kk[jnp.clip(lag_f, 0, t), 0], 0.0)
    kr = jnp.where((lag_f <= 0)[:, :, None, None, None], kk[jnp.clip(-lag_f, 0, t), 1], 0.0)
    msum = (kf + kr).transpose(2, 0, 4, 1, 3).reshape(ng, t * hh, t * hh)

    pw_f = (t - 1 - tl)
    pw_r = tl

    def state_w(pw, d):
        ar, ai = ap_re[pw, d], ap_im[pw, d]
        wr = ar[..., None] * bb_re[d][None] - ai[..., None] * bb_im[d][None]
        wi = ar[..., None] * bb_im[d][None] + ai[..., None] * bb_re[d][None]
        wr = wr.transpose(1, 0, 3, 2).reshape(ng, t * hh, p)
        wi = wi.transpose(1, 0, 3, 2).reshape(ng, t * hh, p)
        return wr, wi

    wf_re, wf_im = state_w(pw_f, 0)
    wr_re, wr_im = state_w(pw_r, 1)
    ws = jnp.concatenate([wf_re, wr_re, wf_im, wr_im], axis=-1)

    def out_w(pw, d):
        xr = ca_re[pw, d]
        xi = ca_im[pw, d]
        vr = xr.transpose(1, 3, 0, 2).reshape(ng, p, t * hh)
        vi = (-xi).transpose(1, 3, 0, 2).reshape(ng, p, t * hh)
        return vr, vi

    vf_re, vf_im = out_w(tl + 1, 0)
    vr_re, vr_im = out_w(t - tl, 1)
    zz = jnp.zeros_like(vf_re)
    wy = jnp.concatenate([msum, vf_re, zz, zz, vr_re, vf_im, zz, zz, vr_im], axis=1)

    at = jnp.stack([jnp.concatenate([ap_re[t, 0], ap_re[t, 1]], axis=-1),
                    jnp.concatenate([ap_im[t, 0], ap_im[t, 1]], axis=-1)], axis=1)
    at = jnp.pad(at, ((0, 0), (0, 6), (0, 0)))
    return ws.astype(_BF), wy.astype(_BF), at


def _post_body(x_ref, u_ref, yg_ref, fn_ref, d_ref, wglu_ref, gs_ref, wo_ref, g1_ref,
               n2_ref, sh_ref, sc_ref, wr_ref, br_ref, tri_ref, x1_ref, t_ref, rt_ref, cnt_ref,
               y_scr, cnt_scr):
    ds5 = u_ref.shape[-1]
    first = (pl.program_id(0) == 0) & (pl.program_id(1) == 0)

    @pl.when(first)
    def _():
        cnt_scr[...] = jnp.zeros_like(cnt_scr)

    t_chunk = S5_CHUNK
    n_chunk = yg_ref.shape[1]
    lanes = 128
    gpb = lanes // S5_GROUP
    for j in range(ds5 // lanes):
        for hi in range(t_chunk // gpb):
            outs = _seg_transpose([yg_ref[gpb * j + glo, :, lanes * hi:lanes * (hi + 1)] for glo in range(gpb)])
            for tlo in range(gpb):
                y_scr[j, pl.ds(gpb * hi + tlo, n_chunk, stride=t_chunk), :] = outs[tlo]
    y_s5 = jnp.concatenate([y_scr[j] for j in range(ds5 // lanes)], axis=-1)

    y = y_s5 + d_ref[...] * u_ref[...]
    y = jax.nn.gelu(y, approximate=True)
    y = y * jax.nn.sigmoid(_dot(y.astype(_BF), wglu_ref[...]))
    yn = _rms(y, gs_ref[...]).astype(_BF)
    mix = _dot(yn, wo_ref[0:ds5, :]) + _dot(fn_ref[...].astype(_BF), wo_ref[ds5:, :])
    x1 = x_ref[...] + g1_ref[...] * mix
    x1_ref[...] = x1
    t = _rms(x1, n2_ref[...]) * (1.0 + sc_ref[...]) + sh_ref[...]
    t_ref[...] = t.astype(_BF)

    ng, epg = N_EXPERT_GROUPS, EXPERTS_PER_GROUP
    logits = jnp.dot(t, wr_ref[...], preferred_element_type=_F32, precision=_HI) + br_ref[...]
    lane = lax.broadcasted_iota(jnp.int32, logits.shape, 1)
    neg = jnp.float32(-jnp.inf)
    big = jnp.int32(1 << 20)
    gl = jnp.where(lane < ng, logits, neg)
    gmax = jnp.max(gl, axis=-1, keepdims=True)
    gidx = jnp.min(jnp.where(gl == gmax, lane, big), axis=-1, keepdims=True)
    gw = 1.0 / jnp.sum(jnp.where(lane < ng, jnp.exp(logits - gmax), 0.0), axis=-1, keepdims=True)
    lo = ng + gidx * epg
    el = jnp.where((lane >= lo) & (lane < lo + epg), logits, neg)
    v0 = jnp.max(el, axis=-1, keepdims=True)
    i0 = jnp.min(jnp.where(el == v0, lane, big), axis=-1, keepdims=True)
    el1 = jnp.where(lane == i0, neg, el)
    v1 = jnp.max(el1, axis=-1, keepdims=True)
    i1 = jnp.min(jnp.where(el1 == v1, lane, big), axis=-1, keepdims=True)
    p0 = 1.0 / (1.0 + jnp.exp(v1 - v0))
    w0 = gw * p0
    w1 = gw * (1.0 - p0)
    e0 = i0 - ng
    e1 = i1 - ng

    oh0 = lane == e0
    oh1 = lane == e1
    oh = (oh0 | oh1).astype(_F32)
    prefix = _dot(tri_ref[...], oh.astype(_BF)) + cnt_scr[...]
    r0 = jnp.sum(jnp.where(oh0, prefix, 0.0), axis=-1, keepdims=True)
    r1 = jnp.sum(jnp.where(oh1, prefix, 0.0), axis=-1, keepdims=True)
    cnt = cnt_scr[...] + jnp.sum(oh, axis=0, keepdims=True)
    cnt_scr[...] = cnt
    cnt_ref[...] = cnt

    vals = (e0.astype(_F32), e1.astype(_F32), w0, w1, r0, r1)
    rt = jnp.zeros(logits.shape, _F32)
    for k, v in enumerate(vals):
        rt = jnp.where(lane == k, v, rt)
    rt_ref[...] = rt


@functools.lru_cache(maxsize=None)
def _strict_lower_ones(n):
    return np.tril(np.ones((n, n), np.float32), -1).astype(jnp.bfloat16)


def _post(x, u, y_g, fn, s5_d, wglu_bf, gs, wo_bf, g1, n2, sh2, sc2, wr, br, tm=512):
    b, l, d = x.shape
    ds5 = u.shape[-1]
    n_grp, _, w = y_g.shape
    kk = w // b
    tri = jnp.asarray(_strict_lower_ones(tm))
    tok = lambda w: pl.BlockSpec((None, tm, w), lambda i, j: (i, j, 0))
    per_b = pl.BlockSpec((None, 1, d), lambda i, j: (i, 0, 0))
    full = lambda a: pl.BlockSpec(a.shape, lambda i, j: (0,) * a.ndim)
    ygs = pl.BlockSpec((n_grp, tm // S5_CHUNK, kk), lambda i, j: (0, j, i))
    return pl.pallas_call(
        _post_body,
        grid=(b, l // tm),
        in_specs=[tok(d), tok(ds5), ygs, tok(fn.shape[-1]), full(s5_d), full(wglu_bf),
                  full(gs), full(wo_bf), per_b, full(n2), per_b, per_b, full(wr), full(br), full(tri)],
        out_specs=[tok(d), tok(d), tok(ROUTE_LANES), pl.BlockSpec((1, ROUTE_LANES), lambda i, j: (0, 0))],
        out_shape=[jax.ShapeDtypeStruct((b, l, d), _F32),
                   jax.ShapeDtypeStruct((b, l, d), _BF),
                   jax.ShapeDtypeStruct((b, l, ROUTE_LANES), _F32),
                   jax.ShapeDtypeStruct((1, ROUTE_LANES), _F32)],
        scratch_shapes=[pltpu.VMEM((ds5 // 128, tm, 128), _F32),
                        pltpu.VMEM((1, ROUTE_LANES), _F32)],
        compiler_params=_cparams(("arbitrary", "arbitrary")),
        name="post_mixer",
    )(x, u, y_g, fn, s5_d, wglu_bf, gs, wo_bf, g1, n2, sh2, sc2, wr, br, tri)


def _moe_body(te_ref, xs_ref, wg_ref, wu_ref, wd_ref, o_ref, wg_bf, wu_bf, wd_bf):
    i = pl.program_id(0)
    new_expert = (i == 0) | (te_ref[i] != te_ref[jnp.maximum(i - 1, 0)])

    @pl.when(new_expert)
    def _():
        wg_bf[...] = wg_ref[...].astype(_BF)
        wu_bf[...] = wu_ref[...].astype(_BF)
        wd_bf[...] = wd_ref[...].astype(_BF)

    xs = xs_ref[...]
    hg = _dot(xs, wg_bf[...])
    hu = _dot(xs, wu_bf[...])
    a = hg * jax.nn.sigmoid(hg) * hu
    o_ref[...] = _dot(a.astype(_BF), wd_bf[...]).astype(o_ref.dtype)


def _moe(tile_expert, xs, wg, wu, wd):
    nr, d = xs.shape
    tm = MOE_TM
    de = wg.shape[-1]
    grid_spec = pltpu.PrefetchScalarGridSpec(
        num_scalar_prefetch=1,
        grid=(nr // tm,),
        in_specs=[pl.BlockSpec((tm, d), lambda i, te: (i, 0)),
                  pl.BlockSpec((None, d, de), lambda i, te: (te[i], 0, 0)),
                  pl.BlockSpec((None, d, de), lambda i, te: (te[i], 0, 0)),
                  pl.BlockSpec((None, de, d), lambda i, te: (te[i], 0, 0))],
        out_specs=pl.BlockSpec((tm, d), lambda i, te: (i, 0)),
        scratch_shapes=[pltpu.VMEM((d, de), _BF), pltpu.VMEM((d, de), _BF), pltpu.VMEM((de, d), _BF)],
    )
    return pl.pallas_call(
        _moe_body,
        grid_spec=grid_spec,
        out_shape=jax.ShapeDtypeStruct((nr, d), _BF),
        compiler_params=_cparams(("arbitrary",)),
        name="moe_experts",
    )(tile_expert, xs, wg, wu, wd)


def _final_body(x1_ref, y0_ref, y1_ref, rt_ref, g2_ref, gf_ref, o_ref):
    w0 = rt_ref[:, 2:3]
    w1 = rt_ref[:, 3:4]
    m = w0 * y0_ref[...].astype(_F32) + w1 * y1_ref[...].astype(_F32)
    o_ref[...] = _rms(x1_ref[...] + g2_ref[...] * m, gf_ref[...])


def _final(x1, y0, y1, rt, g2, gf, tm=512):
    b, l, d = x1.shape
    tok = pl.BlockSpec((None, tm, d), lambda i, j: (i, j, 0))
    return pl.pallas_call(
        _final_body,
        grid=(b, l // tm),
        in_specs=[tok, tok, tok, pl.BlockSpec((None, tm, ROUTE_LANES), lambda i, j: (i, j, 0)),
                  pl.BlockSpec((None, 1, d), lambda i, j: (i, 0, 0)),
                  pl.BlockSpec((1, d), lambda i, j: (0, 0))],
        out_specs=tok,
        out_shape=jax.ShapeDtypeStruct((b, l, d), _F32),
        compiler_params=_cparams(("parallel", "arbitrary")),
        name="final_norm",
    )(x1, y0, y1, rt, g2, gf)


def _route_plan(eid, rank, counts, tm, nr):
    n_experts = counts.shape[0]
    n_tok = eid.shape[0]
    padded = ((counts + tm - 1) // tm) * tm
    pad_end = jnp.cumsum(padded)
    pad_start = pad_end - padded
    raw_start = jnp.cumsum(counts) - counts
    pos = (jnp.take(pad_start, eid) + rank).astype(jnp.int32)
    tile_start = jnp.arange(nr // tm, dtype=jnp.int32) * tm
    tile_expert = jnp.sum(tile_start[:, None] >= pad_end[None, :], axis=-1)
    tile_expert = jnp.minimum(tile_expert, n_experts - 1).astype(jnp.int32)
    tok_ids = jnp.arange(2 * n_tok, dtype=jnp.int32) // 2
    _, sorted_tok = lax.sort_key_val(pos.reshape(-1), tok_ids)
    row_e = jnp.repeat(tile_expert, tm)
    off = jnp.arange(nr, dtype=jnp.int32) - jnp.take(pad_start, row_e)
    valid = off < jnp.take(counts, row_e)
    j = jnp.clip(jnp.take(raw_start, row_e) + off, 0, 2 * n_tok - 1)
    row_token = jnp.where(valid, jnp.take(sorted_tok, j), 0)
    return pos, tile_expert, row_token


def kernel(x, c, ctx, c_ctx, w_ada, b_ada, norm1_g, norm2_g, w_in, s5_lam_re, s5_lam_im, s5_b_re, s5_b_im, s5_c_re, s5_c_im, s5_log_step, s5_d, s5_w_glu, fourier_w, mix_norm_s5_g, mix_norm_f_g, w_out, moe_w_group, moe_b_group, moe_w_router, moe_b_router, moe_w_gate, moe_w_up, moe_w_down, final_norm_g):
    b, l, d = x.shape
    lc = ctx.shape[1]
    depth = w_ada.shape[0]
    assert depth == 1 and l == FFT_R * FFT_R and lc % S5_CHUNK == 0 and b % 8 == 0
    ds5 = s5_d.shape[-1]
    df = w_in.shape[-1] - ds5
    n_exp = moe_w_gate.shape[1]
    row = lambda a: a.reshape(1, -1)

    cond = jnp.concatenate([c, c_ctx[None, :], jnp.zeros((7, d), _F32)], axis=0)
    mod = _adaln(cond, w_ada[0], b_ada[0])
    sh1, sc1, g1, sh2, sc2, g2 = [mod[:b, i * d:(i + 1) * d].reshape(b, 1, d) for i in range(6)]
    csh1 = mod[b:b + 1, 0:d]
    csc1 = mod[b:b + 1, d:2 * d]

    w_in_bf = w_in[0].astype(_BF)
    m1 = jnp.asarray(_fft_stage1_matrix()).astype(_BF)
    m2 = jnp.asarray(_fft_stage2_matrices()).astype(_BF)
    perm = jnp.asarray(_chunk_row_perm())
    zs, gc = _inproj(x, sh1, sc1, row(norm1_g[0]), w_in_bf, m1)
    u_ctx = _inproj_ctx(ctx, csh1, csc1, row(norm1_g[0]), w_in_bf[:, :ds5], perm)
    u_lat = _s5_pack(zs, perm)

    fg = df // FOURIER_GROUPS
    cc = np.arange(fg)
    ang = 2.0 * np.pi * ((cc[:, None] * cc[None, :]) % fg) / fg
    scale = 1.0 / math.sqrt(l * fg)
    cw = jnp.einsum("cm,gmd->gcd", jnp.asarray(np.cos(ang) * scale, _F32), fourier_w[0], precision=_HI)
    sw = jnp.einsum("cm,gmd->gcd", jnp.asarray(np.sin(ang) * scale, _F32), fourier_w[0], precision=_HI)
    fn = _fft2(gc, m2, cw.astype(_BF), sw.astype(_BF), row(mix_norm_f_g[0]))

    ws, wy, at = _s5_weights(s5_lam_re[0], s5_lam_im[0], s5_b_re[0], s5_b_im[0],
                             s5_c_re[0], s5_c_im[0], s5_log_step[0])
    y_g = _s5(u_ctx, u_lat, ws, wy, at, b)

    n_rt = N_EXPERT_GROUPS * (1 + EXPERTS_PER_GROUP)
    w_rt = jnp.concatenate([moe_w_group[0], moe_w_router[0].reshape(d, -1)], axis=-1)
    w_rt = jnp.pad(w_rt, ((0, 0), (0, ROUTE_LANES - n_rt)))
    b_rt = jnp.pad(jnp.concatenate([moe_b_group[0], moe_b_router[0].reshape(-1)]), (0, ROUTE_LANES - n_rt))
    x1, tmod, rt, cnt = _post(x, zs, y_g, fn, row(s5_d[0]), s5_w_glu[0].astype(_BF), row(mix_norm_s5_g[0]),
                              w_out[0].astype(_BF), g1, row(norm2_g[0]), sh2, sc2, w_rt, row(b_rt))

    tm = MOE_TM
    n_tok = b * l
    nr = 2 * n_tok + n_exp * tm
    rt2 = rt.reshape(n_tok, ROUTE_LANES)
    eid = rt2[:, 0:2].astype(jnp.int32)
    rank = rt2[:, 4:6].astype(jnp.int32)
    pos, tile_expert, row_token = _route_plan(eid, rank, cnt[0, :n_exp].astype(jnp.int32), tm, nr)
    xs = jnp.take(tmod.reshape(n_tok, d), row_token, axis=0)
    ys = _moe(tile_expert, xs, moe_w_gate[0], moe_w_up[0], moe_w_down[0])
    y0 = jnp.take(ys, pos[:, 0], axis=0).reshape(b, l, d)
    y1 = jnp.take(ys, pos[:, 1], axis=0).reshape(b, l, d)
    return _final(x1, y0, y1, rt, g2, row(final_norm_g))
```

```python
import functools
import math

import numpy as np
import jax
import jax.numpy as jnp
from jax import lax
from jax.experimental import pallas as pl
from jax.experimental.pallas import tpu as pltpu

EPS = 1e-6
S5_GROUP = 16
S5_STATE = 64
S5_CHUNK = 16
FOURIER_GROUPS = 4
N_EXPERT_GROUPS = 4
EXPERTS_PER_GROUP = 8
FFT_R = 64
FFT_BLK = 8
MOE_TM = 256
ROUTE_LANES = 128
ROUTE_ROWS = 8
VMEM_LIMIT = 56 * 1024 * 1024

_HI = lax.Precision.HIGHEST
_BF = jnp.bfloat16
_F32 = jnp.float32


def _cparams(sem):
    return pltpu.CompilerParams(dimension_semantics=sem, vmem_limit_bytes=VMEM_LIMIT)


def _dot(a, b):
    return jnp.dot(a, b, preferred_element_type=_F32)


def _rms(x, g):
    return x * lax.rsqrt(jnp.mean(x * x, axis=-1, keepdims=True) + EPS) * g


def _adaln_body(c_ref, w_ref, b_ref, o_ref):
    c = c_ref[...]
    a = c * jax.nn.sigmoid(c)
    o_ref[...] = jnp.dot(a, w_ref[...], preferred_element_type=_F32, precision=_HI) + b_ref[...]


def _adaln(cond, w, b):
    m, d = cond.shape
    n = w.shape[1]
    tn = 768
    return pl.pallas_call(
        _adaln_body,
        grid=(n // tn,),
        in_specs=[pl.BlockSpec((m, d), lambda j: (0, 0)),
                  pl.BlockSpec((d, tn), lambda j: (0, j)),
                  pl.BlockSpec((1, tn), lambda j: (0, j))],
        out_specs=pl.BlockSpec((m, tn), lambda j: (0, j)),
        out_shape=jax.ShapeDtypeStruct((m, n), _F32),
        compiler_params=_cparams(("arbitrary",)),
        name="adaln",
    )(cond, w, b.reshape(1, n))


def _inproj_body(x_ref, sh_ref, sc_ref, g_ref, w_ref, m1_ref, zs_ref, gc_ref):
    r, blk, d = x_ref.shape
    x = x_ref[...].reshape(r * blk, d)
    h = _rms(x, g_ref[...]) * (1.0 + sc_ref[...]) + sh_ref[...]
    z = _dot(h.astype(_BF), w_ref[...])
    ds5 = zs_ref.shape[-1]
    zs_ref[...] = z[:, :ds5].reshape(r, blk, ds5)
    v = z[:, ds5:].astype(_BF)
    g1 = _dot(m1_ref[...], v)
    gc_ref[...] = g1.astype(_BF).reshape(r, 2 * blk, v.shape[-1])


def _inproj(x, sh, sc, g, w_bf, m1):
    b, l, d = x.shape
    r, blk = FFT_R, FFT_BLK
    nj = r // blk
    dmix = w_bf.shape[1]
    ds5 = dmix // 2
    df = dmix - ds5
    x4 = x.reshape(b, r, r, d)
    zs, gc = pl.pallas_call(
        _inproj_body,
        grid=(b, nj),
        in_specs=[pl.BlockSpec((None, r, blk, d), lambda i, j: (i, 0, j, 0)),
                  pl.BlockSpec((None, 1, d), lambda i, j: (i, 0, 0)),
                  pl.BlockSpec((None, 1, d), lambda i, j: (i, 0, 0)),
                  pl.BlockSpec((1, d), lambda i, j: (0, 0)),
                  pl.BlockSpec((d, dmix), lambda i, j: (0, 0)),
                  pl.BlockSpec(m1.shape, lambda i, j: (0, 0))],
        out_specs=[pl.BlockSpec((None, r, blk, ds5), lambda i, j: (i, 0, j, 0)),
                   pl.BlockSpec((None, r, 2 * blk, df), lambda i, j: (i, 0, j, 0))],
        out_shape=[jax.ShapeDtypeStruct((b, r, r, ds5), _F32),
                   jax.ShapeDtypeStruct((b, r, 2 * r, df), _BF)],
        compiler_params=_cparams(("parallel", "arbitrary")),
        name="inproj",
    )(x4, sh, sc, g, w_bf, m1)
    return zs.reshape(b, l, ds5), gc


def _seg_transpose(arrs):
    n = len(arrs)
    seg = lax.broadcasted_iota(jnp.int32, arrs[0].shape, 1) // S5_GROUP
    d = n // 2
    while d >= 1:
        keep = (seg & d) == 0
        new = list(arrs)
        for i in range(n):
            if i & d == 0:
                a, b = arrs[i], arrs[i + d]
                new[i] = jnp.where(keep, a, pltpu.roll(b, S5_GROUP * d, 1))
                new[i + d] = jnp.where(keep, pltpu.roll(a, 128 - S5_GROUP * d, 1), b)
        arrs = new
        d //= 2
    return arrs


@functools.lru_cache(maxsize=None)
def _chunk_row_perm():
    t = S5_CHUNK
    p = np.zeros((t * t, t * t), np.float32)
    for c in range(t):
        for tl in range(t):
            p[tl * t + c, c * t + tl] = 1.0
    return p.astype(jnp.bfloat16)


def _pack_chunks(z_bf, p_ref, u_ref):
    t = S5_CHUNK
    nsub = z_bf.shape[0] // (t * t)
    pieces = [_dot(p_ref[...], z_bf[s * t * t:(s + 1) * t * t, :]) for s in range(nsub)]
    a = []
    for tl in range(t):
        rows = [pc[tl * t:(tl + 1) * t, :] for pc in pieces]
        a.append(rows[0] if nsub == 1 else jnp.concatenate(rows, axis=0))
    lanes = 128
    gpb = lanes // S5_GROUP
    for j in range(z_bf.shape[1] // lanes):
        for hi in range(t // gpb):
            outs = _seg_transpose([a[gpb * hi + tlo][:, lanes * j:lanes * (j + 1)] for tlo in range(gpb)])
            for glo in range(gpb):
                u_ref[gpb * j + glo, :, lanes * hi:lanes * (hi + 1)] = outs[glo].astype(u_ref.dtype)


def _inproj_ctx_body(x_ref, sh_ref, sc_ref, g_ref, w_ref, p_ref, u_ref):
    h = _rms(x_ref[...], g_ref[...]) * (1.0 + sc_ref[...]) + sh_ref[...]
    z = _dot(h.astype(_BF), w_ref[...])
    _pack_chunks(z.astype(_BF), p_ref, u_ref)


def _inproj_ctx(ctx, sh, sc, g, w_s5_bf, perm):
    b, lc, d = ctx.shape
    ds5 = w_s5_bf.shape[1]
    n_grp = ds5 // S5_GROUP
    kk = S5_CHUNK * S5_GROUP
    return pl.pallas_call(
        _inproj_ctx_body,
        grid=(b,),
        in_specs=[pl.BlockSpec((None, lc, d), lambda i: (i, 0, 0)),
                  pl.BlockSpec((1, d), lambda i: (0, 0)),
                  pl.BlockSpec((1, d), lambda i: (0, 0)),
                  pl.BlockSpec((1, d), lambda i: (0, 0)),
                  pl.BlockSpec((d, ds5), lambda i: (0, 0)),
                  pl.BlockSpec(perm.shape, lambda i: (0, 0))],
        out_specs=pl.BlockSpec((n_grp, lc // S5_CHUNK, kk), lambda i: (0, 0, i)),
        out_shape=jax.ShapeDtypeStruct((n_grp, lc // S5_CHUNK, b * kk), _BF),
        compiler_params=_cparams(("arbitrary",)),
        name="inproj_ctx",
    )(ctx, sh, sc, g, w_s5_bf, perm)


def _s5_pack_body(z_ref, p_ref, u_ref):
    _pack_chunks(z_ref[...].astype(_BF), p_ref, u_ref)


def _s5_pack(zs, perm, tok=2048):
    b, l, ds5 = zs.shape
    n_grp = ds5 // S5_GROUP
    kk = S5_CHUNK * S5_GROUP
    return pl.pallas_call(
        _s5_pack_body,
        grid=(b, l // tok),
        in_specs=[pl.BlockSpec((None, tok, ds5), lambda i, j: (i, j, 0)),
                  pl.BlockSpec(perm.shape, lambda i, j: (0, 0))],
        out_specs=pl.BlockSpec((n_grp, tok // S5_CHUNK, kk), lambda i, j: (0, j, i)),
        out_shape=jax.ShapeDtypeStruct((n_grp, l // S5_CHUNK, b * kk), _BF),
        compiler_params=_cparams(("parallel", "arbitrary")),
        name="s5_pack",
    )(zs, perm)


@functools.lru_cache(maxsize=None)
def _fft_stage1_matrix():
    r, blk = FFT_R, FFT_BLK
    k1 = np.arange(r)[:, None]
    l1 = np.arange(r)[None, :]
    ang = 2.0 * np.pi * ((k1 * l1) % r) / r
    f = np.stack([np.cos(ang), -np.sin(ang)], axis=1)
    m = np.einsum("kal,pq->kaplq", f, np.eye(blk))
    return m.reshape(r * 2 * blk, r * blk).astype(np.float32)


@functools.lru_cache(maxsize=None)
def _fft_stage2_matrices():
    r, blk = FFT_R, FFT_BLK
    n = r * r
    nt = r // blk
    l2 = np.arange(r)
    k2 = np.arange(r)
    out = np.zeros((nt, 2, r, blk, blk, r // blk, 2, blk), np.float32)
    for i in range(nt):
        for kl in range(blk):
            k1 = blk * i + kl
            ang = 2.0 * np.pi * (((k2[:, None] * l2[None, :] * r) + l2[None, :] * k1) % n) / n
            tr, ti = np.cos(ang), -np.sin(ang)
            tr = tr.reshape(r, r // blk, blk)
            ti = ti.reshape(r, r // blk, blk)
            out[i, 0, :, kl, kl, :, 0, :] = tr
            out[i, 0, :, kl, kl, :, 1, :] = -ti
            out[i, 1, :, kl, kl, :, 0, :] = ti
            out[i, 1, :, kl, kl, :, 1, :] = tr
    return out.reshape(nt, 2 * r * blk, blk * 2 * r)


def _fft2_body(gc_ref, m2_ref, cw_ref, sw_ref, g_ref, o_ref):
    kb, rr, df = gc_ref.shape
    gc = gc_ref[...].reshape(kb * rr, df)
    x = _dot(m2_ref[...], gc)
    half = x.shape[0] // 2
    xr = x[:half].astype(_BF)
    xi = x[half:].astype(_BF)
    ng = cw_ref.shape[0]
    fg = df // ng
    parts = []
    for g in range(ng):
        sl = slice(g * fg, (g + 1) * fg)
        parts.append(_dot(xr[:, sl], cw_ref[g]) + _dot(xi[:, sl], sw_ref[g]))
    f = jnp.concatenate(parts, axis=-1)
    fn = _rms(f, g_ref[...])
    o_ref[...] = fn.reshape(o_ref.shape)


def _fft2(gc, m2, cw, sw, gf):
    b, r, rr, df = gc.shape
    blk = FFT_BLK
    nt = r // blk
    out = pl.pallas_call(
        _fft2_body,
        grid=(b, nt),
        in_specs=[pl.BlockSpec((None, blk, rr, df), lambda i, j: (i, j, 0, 0)),
                  pl.BlockSpec((None,) + m2.shape[1:], lambda i, j: (j, 0, 0)),
                  pl.BlockSpec(cw.shape, lambda i, j: (0, 0, 0)),
                  pl.BlockSpec(sw.shape, lambda i, j: (0, 0, 0)),
                  pl.BlockSpec((1, df), lambda i, j: (0, 0))],
        out_specs=pl.BlockSpec((None, r, blk, df), lambda i, j: (i, 0, j, 0)),
        out_shape=jax.ShapeDtypeStruct((b, r, r, df), _F32),
        compiler_params=_cparams(("parallel", "arbitrary")),
        name="fft2",
    )(gc, m2, cw, sw, gf)
    return out.reshape(b, r * r, df)


def _s5_body(uc_ref, ul_ref, ws_ref, wy_ref, at_ref, y_ref,
             s_re, s_im, ha_re, hb_re, ha_im, hb_im, *, nb):
    p = S5_STATE
    kk = ws_ref.shape[0]
    n_ctx, n_lat = uc_ref.shape[0], ul_ref.shape[0]
    ws = ws_ref[...]
    for b in range(nb):
        sl = slice(b * kk, (b + 1) * kk)
        sc = _dot(uc_ref[:, sl], ws)
        sl_ = _dot(ul_ref[:, sl], ws)
        s_re[pl.ds(b, n_ctx, stride=nb), :] = sc[:, :2 * p]
        s_im[pl.ds(b, n_ctx, stride=nb), :] = sc[:, 2 * p:]
        s_re[pl.ds(n_ctx * nb + b, n_lat, stride=nb), :] = sl_[:, :2 * p]
        s_im[pl.ds(n_ctx * nb + b, n_lat, stride=nb), :] = sl_[:, 2 * p:]

    a_re = at_ref[0:1, :]
    a_im = at_ref[1:2, :]
    is_f = lax.broadcasted_iota(jnp.int32, (nb, 2 * p), 1) < p
    n_all = n_ctx + n_lat

    def load(cf, cr):
        rf = pl.multiple_of(cf * nb, nb)
        rr = pl.multiple_of(cr * nb, nb)
        return (jnp.where(is_f, s_re[pl.ds(rf, nb), :], s_re[pl.ds(rr, nb), :]),
                jnp.where(is_f, s_im[pl.ds(rf, nb), :], s_im[pl.ds(rr, nb), :]))

    def advance(h_re, h_im, x_re, x_im):
        return (a_re * h_re - a_im * h_im + x_re, a_re * h_im + a_im * h_re + x_im)

    def ctx_step(step, carry):
        x_re, x_im = load(step, n_ctx - 1 - step)
        return advance(*carry, x_re, x_im)

    def lat_step(step, carry):
        h_re, h_im = carry
        rf = pl.multiple_of(step * nb, nb)
        rr = pl.multiple_of((n_lat - 1 - step) * nb, nb)
        ha_re[pl.ds(rf, nb), :] = h_re
        hb_re[pl.ds(rr, nb), :] = h_re
        ha_im[pl.ds(rf, nb), :] = h_im
        hb_im[pl.ds(rr, nb), :] = h_im
        x_re, x_im = load(n_ctx + step, n_all - 1 - step)
        return advance(h_re, h_im, x_re, x_im)

    zero = jnp.zeros((nb, 2 * p), _F32)
    carry = lax.fori_loop(0, n_ctx, ctx_step, (zero, zero))
    lax.fori_loop(0, n_lat, lat_step, carry)

    wy = wy_ref[...]
    for b in range(nb):
        rows = pl.ds(b, n_lat, stride=nb)
        hin = jnp.concatenate([ha_re[rows, :], hb_re[rows, :], ha_im[rows, :], hb_im[rows, :]], axis=-1)
        lhs = jnp.concatenate([ul_ref[:, b * kk:(b + 1) * kk], hin.astype(_BF)], axis=-1)
        y_ref[:, b * kk:(b + 1) * kk] = _dot(lhs, wy)


def _s5(u_ctx, u_lat, ws, wy, at, nb):
    ng, n_ctx, w = u_ctx.shape
    n_lat = u_lat.shape[1]
    body = functools.partial(_s5_body, nb=nb)
    hs = pltpu.VMEM((n_lat * nb, 2 * S5_STATE), _F32)
    ss = pltpu.VMEM(((n_ctx + n_lat) * nb, 2 * S5_STATE), _F32)
    return pl.pallas_call(
        body,
        grid=(ng,),
        in_specs=[pl.BlockSpec((None, n_ctx, w), lambda g: (g, 0, 0)),
                  pl.BlockSpec((None, n_lat, w), lambda g: (g, 0, 0)),
                  pl.BlockSpec((None,) + ws.shape[1:], lambda g: (g, 0, 0)),
                  pl.BlockSpec((None,) + wy.shape[1:], lambda g: (g, 0, 0)),
                  pl.BlockSpec((None,) + at.shape[1:], lambda g: (g, 0, 0))],
        out_specs=pl.BlockSpec((None, n_lat, w), lambda g: (g, 0, 0)),
        out_shape=jax.ShapeDtypeStruct((ng, n_lat, w), _F32),
        scratch_shapes=[ss, ss, hs, hs, hs, hs],
        compiler_params=_cparams(("parallel",)),
        name="s5_scan",
    )(u_ctx, u_lat, ws, wy, at)


def _s5_weights(lam_re, lam_im, b_re, b_im, c_re, c_im, log_step):
    t = S5_CHUNK
    ng, p = lam_re.shape[1], lam_re.shape[2]
    hh = b_re.shape[-1]
    dt = jnp.exp(log_step.astype(_F32))[..., None]
    lr, li = lam_re.astype(_F32), lam_im.astype(_F32)
    mag = jnp.exp(lr * dt)
    ab_re, ab_im = mag * jnp.cos(li * dt), mag * jnp.sin(li * dt)
    den = lr * lr + li * li
    nr, ni = ab_re - 1.0, ab_im
    q_re = (nr * lr + ni * li) / den
    q_im = (ni * lr - nr * li) / den
    br, bi = b_re.astype(_F32), b_im.astype(_F32)
    bb_re = q_re[..., None] * br - q_im[..., None] * bi
    bb_im = q_re[..., None] * bi + q_im[..., None] * br
    k = jnp.arange(t + 1, dtype=_F32)[:, None, None, None]
    pm = jnp.exp(k * (lr * dt)[None])
    ap_re = pm * jnp.cos(k * (li * dt)[None])
    ap_im = pm * jnp.sin(k * (li * dt)[None])
    cr, ci = c_re.astype(_F32), c_im.astype(_F32)

    def ein(spec, *ops):
        return jnp.einsum(spec, *ops, precision=_HI)

    ca_re = cr[None] * ap_re[:, :, :, None, :] - ci[None] * ap_im[:, :, :, None, :]
    ca_im = cr[None] * ap_im[:, :, :, None, :] + ci[None] * ap_re[:, :, :, None, :]
    kk = ein("kdghp,dgpj->kdghj", ca_re, bb_re) - ein("kdghp,dgpj->kdghj", ca_im, bb_im)
    tl = jnp.arange(t)
    lag_f = tl[None, :] - tl[:, None]
    kf = jnp.where((lag_f >= 0)[:, :, None, None, None], kk[jnp.clip(lag_f, 0, t), 0], 0.0)
    kr = jnp.where((lag_f <= 0)[:, :, None, None, None], kk[jnp.clip(-lag_f, 0, t), 1], 0.0)
    msum = (kf + kr).transpose(2, 0, 4, 1, 3).reshape(ng, t * hh, t * hh)

    pw_f = (t - 1 - tl)
    pw_r = tl

    def state_w(pw, d):
        ar, ai = ap_re[pw, d], ap_im[pw, d]
        wr = ar[..., None] * bb_re[d][None] - ai[..., None] * bb_im[d][None]
        wi = ar[..., None] * bb_im[d][None] + ai[..., None] * bb_re[d][None]
        wr = wr.transpose(1, 0, 3, 2).reshape(ng, t * hh, p)
        wi = wi.transpose(1, 0, 3, 2).reshape(ng, t * hh, p)
        return wr, wi

    wf_re, wf_im = state_w(pw_f, 0)
    wr_re, wr_im = state_w(pw_r, 1)
    ws = jnp.concatenate([wf_re, wr_re, wf_im, wr_im], axis=-1)

    def out_w(pw, d):
        xr = ca_re[pw, d]
        xi = ca_im[pw, d]
        vr = xr.transpose(1, 3, 0, 2).reshape(ng, p, t * hh)
        vi = (-xi).transpose(1, 3, 0, 2).reshape(ng, p, t * hh)
        return vr, vi

    vf_re, vf_im = out_w(tl + 1, 0)
    vr_re, vr_im = out_w(t - tl, 1)
    zz = jnp.zeros_like(vf_re)
    wy = jnp.concatenate([msum, vf_re, zz, zz, vr_re, vf_im, zz, zz, vr_im], axis=1)

    at = jnp.stack([jnp.concatenate([ap_re[t, 0], ap_re[t, 1]], axis=-1),
                    jnp.concatenate([ap_im[t, 0], ap_im[t, 1]], axis=-1)], axis=1)
    at = jnp.pad(at, ((0, 0), (0, 6), (0, 0)))
    return ws.astype(_BF), wy.astype(_BF), at


def _post_body(x_ref, u_ref, yg_ref, fn_ref, d_ref, wglu_ref, gs_ref, wo_ref, g1_ref,
               n2_ref, sh_ref, sc_ref, wr_ref, br_ref, tri_ref, x1_ref, t_ref, rt_ref, rtt_ref, cnt_ref,
               y_scr, cnt_scr):
    ds5 = u_ref.shape[-1]
    first = (pl.program_id(0) == 0) & (pl.program_id(1) == 0)

    @pl.when(first)
    def _():
        cnt_scr[...] = jnp.zeros_like(cnt_scr)

    t_chunk = S5_CHUNK
    n_chunk = yg_ref.shape[1]
    lanes = 128
    gpb = lanes // S5_GROUP
    for j in range(ds5 // lanes):
        for hi in range(t_chunk // gpb):
            outs = _seg_transpose([yg_ref[gpb * j + glo, :, lanes * hi:lanes * (hi + 1)] for glo in range(gpb)])
            for tlo in range(gpb):
                y_scr[j, pl.ds(gpb * hi + tlo, n_chunk, stride=t_chunk), :] = outs[tlo]
    y_s5 = jnp.concatenate([y_scr[j] for j in range(ds5 // lanes)], axis=-1)

    y = y_s5 + d_ref[...] * u_ref[...]
    y = jax.nn.gelu(y, approximate=True)
    y = y * jax.nn.sigmoid(_dot(y.astype(_BF), wglu_ref[...]))
    yn = _rms(y, gs_ref[...]).astype(_BF)
    mix = _dot(yn, wo_ref[0:ds5, :]) + _dot(fn_ref[...].astype(_BF), wo_ref[ds5:, :])
    x1 = x_ref[...] + g1_ref[...] * mix
    x1_ref[...] = x1
    t = _rms(x1, n2_ref[...]) * (1.0 + sc_ref[...]) + sh_ref[...]
    t_ref[...] = t.astype(_BF)

    ng, epg = N_EXPERT_GROUPS, EXPERTS_PER_GROUP
    logits = jnp.dot(t, wr_ref[...], preferred_element_type=_F32, precision=_HI) + br_ref[...]
    lane = lax.broadcasted_iota(jnp.int32, logits.shape, 1)
    neg = jnp.float32(-jnp.inf)
    big = jnp.int32(1 << 20)
    gl = jnp.where(lane < ng, logits, neg)
    gmax = jnp.max(gl, axis=-1, keepdims=True)
    gidx = jnp.min(jnp.where(gl == gmax, lane, big), axis=-1, keepdims=True)
    gw = 1.0 / jnp.sum(jnp.where(lane < ng, jnp.exp(logits - gmax), 0.0), axis=-1, keepdims=True)
    lo = ng + gidx * epg
    el = jnp.where((lane >= lo) & (lane < lo + epg), logits, neg)
    v0 = jnp.max(el, axis=-1, keepdims=True)
    i0 = jnp.min(jnp.where(el == v0, lane, big), axis=-1, keepdims=True)
    el1 = jnp.where(lane == i0, neg, el)
    v1 = jnp.max(el1, axis=-1, keepdims=True)
    i1 = jnp.min(jnp.where(el1 == v1, lane, big), axis=-1, keepdims=True)
    p0 = 1.0 / (1.0 + jnp.exp(v1 - v0))
    w0 = gw * p0
    w1 = gw * (1.0 - p0)
    e0 = i0 - ng
    e1 = i1 - ng

    oh0 = lane == e0
    oh1 = lane == e1
    oh = (oh0 | oh1).astype(_F32)
    prefix = _dot(tri_ref[...], oh.astype(_BF)) + cnt_scr[...]
    r0 = jnp.sum(jnp.where(oh0, prefix, 0.0), axis=-1, keepdims=True)
    r1 = jnp.sum(jnp.where(oh1, prefix, 0.0), axis=-1, keepdims=True)
    cnt = cnt_scr[...] + jnp.sum(oh, axis=0, keepdims=True)
    cnt_scr[...] = cnt
    cnt_ref[...] = cnt

    vals = (e0.astype(_F32), e1.astype(_F32), w0, w1, r0, r1)
    rt = jnp.zeros(logits.shape, _F32)
    for k, v in enumerate(vals):
        rt = jnp.where(lane == k, v, rt)
    rt_ref[...] = rt
    rtt_ref[...] = rt.T[0:rtt_ref.shape[0], :]


@functools.lru_cache(maxsize=None)
def _strict_lower_ones(n):
    return np.tril(np.ones((n, n), np.float32), -1).astype(jnp.bfloat16)


def _post(x, u, y_g, fn, s5_d, wglu_bf, gs, wo_bf, g1, n2, sh2, sc2, wr, br, tm=512):
    b, l, d = x.shape
    ds5 = u.shape[-1]
    n_grp, _, w = y_g.shape
    kk = w // b
    tri = jnp.asarray(_strict_lower_ones(tm))
    tok = lambda w: pl.BlockSpec((None, tm, w), lambda i, j: (i, j, 0))
    per_b = pl.BlockSpec((None, 1, d), lambda i, j: (i, 0, 0))
    full = lambda a: pl.BlockSpec(a.shape, lambda i, j: (0,) * a.ndim)
    ygs = pl.BlockSpec((n_grp, tm // S5_CHUNK, kk), lambda i, j: (0, j, i))
    return pl.pallas_call(
        _post_body,
        grid=(b, l // tm),
        in_specs=[tok(d), tok(ds5), ygs, tok(fn.shape[-1]), full(s5_d), full(wglu_bf),
                  full(gs), full(wo_bf), per_b, full(n2), per_b, per_b, full(wr), full(br), full(tri)],
        out_specs=[tok(d), tok(d), tok(ROUTE_LANES),
                   pl.BlockSpec((None, ROUTE_ROWS, tm), lambda i, j: (i, 0, j)),
                   pl.BlockSpec((1, ROUTE_LANES), lambda i, j: (0, 0))],
        out_shape=[jax.ShapeDtypeStruct((b, l, d), _F32),
                   jax.ShapeDtypeStruct((b, l, d), _BF),
                   jax.ShapeDtypeStruct((b, l, ROUTE_LANES), _F32),
                   jax.ShapeDtypeStruct((b, ROUTE_ROWS, l), _F32),
                   jax.ShapeDtypeStruct((1, ROUTE_LANES), _F32)],
        scratch_shapes=[pltpu.VMEM((ds5 // 128, tm, 128), _F32),
                        pltpu.VMEM((1, ROUTE_LANES), _F32)],
        compiler_params=_cparams(("arbitrary", "arbitrary")),
        name="post_mixer",
    )(x, u, y_g, fn, s5_d, wglu_bf, gs, wo_bf, g1, n2, sh2, sc2, wr, br, tri)


def _moe_body(te_ref, xs_ref, wg_ref, wu_ref, wd_ref, o_ref, wg_bf, wu_bf, wd_bf):
    i = pl.program_id(0)
    new_expert = (i == 0) | (te_ref[i] != te_ref[jnp.maximum(i - 1, 0)])

    @pl.when(new_expert)
    def _():
        wg_bf[...] = wg_ref[...].astype(_BF)
        wu_bf[...] = wu_ref[...].astype(_BF)
        wd_bf[...] = wd_ref[...].astype(_BF)

    xs = xs_ref[...]
    hg = _dot(xs, wg_bf[...])
    hu = _dot(xs, wu_bf[...])
    a = hg * jax.nn.sigmoid(hg) * hu
    o_ref[...] = _dot(a.astype(_BF), wd_bf[...]).astype(o_ref.dtype)


def _moe(tile_expert, xs, wg, wu, wd):
    nr, d = xs.shape
    tm = MOE_TM
    de = wg.shape[-1]
    grid_spec = pltpu.PrefetchScalarGridSpec(
        num_scalar_prefetch=1,
        grid=(nr // tm,),
        in_specs=[pl.BlockSpec((tm, d), lambda i, te: (i, 0)),
                  pl.BlockSpec((None, d, de), lambda i, te: (te[i], 0, 0)),
                  pl.BlockSpec((None, d, de), lambda i, te: (te[i], 0, 0)),
                  pl.BlockSpec((None, de, d), lambda i, te: (te[i], 0, 0))],
        out_specs=pl.BlockSpec((tm, d), lambda i, te: (i, 0)),
        scratch_shapes=[pltpu.VMEM((d, de), _BF), pltpu.VMEM((d, de), _BF), pltpu.VMEM((de, d), _BF)],
    )
    return pl.pallas_call(
        _moe_body,
        grid_spec=grid_spec,
        out_shape=jax.ShapeDtypeStruct((nr, d), _BF),
        compiler_params=_cparams(("arbitrary",)),
        name="moe_experts",
    )(tile_expert, xs, wg, wu, wd)


def _final_body(x1_ref, y0_ref, y1_ref, rt_ref, g2_ref, gf_ref, o_ref):
    w0 = rt_ref[:, 2:3]
    w1 = rt_ref[:, 3:4]
    m = w0 * y0_ref[...].astype(_F32) + w1 * y1_ref[...].astype(_F32)
    o_ref[...] = _rms(x1_ref[...] + g2_ref[...] * m, gf_ref[...])


def _final(x1, y0, y1, rt, g2, gf, tm=512):
    b, l, d = x1.shape
    tok = pl.BlockSpec((None, tm, d), lambda i, j: (i, j, 0))
    return pl.pallas_call(
        _final_body,
        grid=(b, l // tm),
        in_specs=[tok, tok, tok, pl.BlockSpec((None, tm, ROUTE_LANES), lambda i, j: (i, j, 0)),
                  pl.BlockSpec((None, 1, d), lambda i, j: (i, 0, 0)),
                  pl.BlockSpec((1, d), lambda i, j: (0, 0))],
        out_specs=tok,
        out_shape=jax.ShapeDtypeStruct((b, l, d), _F32),
        compiler_params=_cparams(("parallel", "arbitrary")),
        name="final_norm",
    )(x1, y0, y1, rt, g2, gf)


def _route_plan(eid, rank, tok_ids, counts, tm, nr):
    n_experts = counts.shape[0]
    n_pairs = eid.size
    padded = ((counts + tm - 1) // tm) * tm
    pad_end = jnp.cumsum(padded)
    pad_start = pad_end - padded
    raw_start = jnp.cumsum(counts) - counts
    pos = rank
    for e in range(n_experts):
        pos = pos + jnp.where(eid == e, pad_start[e], 0)
    tile_start = jnp.arange(nr // tm, dtype=jnp.int32) * tm
    tile_expert = jnp.sum(tile_start[:, None] >= pad_end[None, :], axis=-1)
    tile_expert = jnp.minimum(tile_expert, n_experts - 1).astype(jnp.int32)
    _, sorted_tok = lax.sort_key_val((eid * n_pairs + rank).reshape(-1), tok_ids.reshape(-1))
    sel = tile_expert[:, None] == jnp.arange(n_experts, dtype=jnp.int32)[None, :]
    per_tile = lambda v: jnp.repeat(jnp.sum(jnp.where(sel, v[None, :], 0), axis=-1), tm)
    off = jnp.arange(nr, dtype=jnp.int32) - per_tile(pad_start)
    valid = off < per_tile(counts)
    j = jnp.clip(per_tile(raw_start) + off, 0, n_pairs - 1)
    row_token = jnp.where(valid, jnp.take(sorted_tok, j), 0)
    return pos, tile_expert, row_token


def kernel(x, c, ctx, c_ctx, w_ada, b_ada, norm1_g, norm2_g, w_in, s5_lam_re, s5_lam_im, s5_b_re, s5_b_im, s5_c_re, s5_c_im, s5_log_step, s5_d, s5_w_glu, fourier_w, mix_norm_s5_g, mix_norm_f_g, w_out, moe_w_group, moe_b_group, moe_w_router, moe_b_router, moe_w_gate, moe_w_up, moe_w_down, final_norm_g):
    b, l, d = x.shape
    lc = ctx.shape[1]
    depth = w_ada.shape[0]
    assert depth == 1 and l == FFT_R * FFT_R and lc % S5_CHUNK == 0 and b % 8 == 0
    ds5 = s5_d.shape[-1]
    df = w_in.shape[-1] - ds5
    n_exp = moe_w_gate.shape[1]
    row = lambda a: a.reshape(1, -1)

    cond = jnp.concatenate([c, c_ctx[None, :], jnp.zeros((7, d), _F32)], axis=0)
    mod = _adaln(cond, w_ada[0], b_ada[0])
    sh1, sc1, g1, sh2, sc2, g2 = [mod[:b, i * d:(i + 1) * d].reshape(b, 1, d) for i in range(6)]
    csh1 = mod[b:b + 1, 0:d]
    csc1 = mod[b:b + 1, d:2 * d]

    w_in_bf = w_in[0].astype(_BF)
    m1 = jnp.asarray(_fft_stage1_matrix()).astype(_BF)
    m2 = jnp.asarray(_fft_stage2_matrices()).astype(_BF)
    perm = jnp.asarray(_chunk_row_perm())
    zs, gc = _inproj(x, sh1, sc1, row(norm1_g[0]), w_in_bf, m1)
    u_ctx = _inproj_ctx(ctx, csh1, csc1, row(norm1_g[0]), w_in_bf[:, :ds5], perm)
    u_lat = _s5_pack(zs, perm)

    fg = df // FOURIER_GROUPS
    cc = np.arange(fg)
    ang = 2.0 * np.pi * ((cc[:, None] * cc[None, :]) % fg) / fg
    scale = 1.0 / math.sqrt(l * fg)
    cw = jnp.einsum("cm,gmd->gcd", jnp.asarray(np.cos(ang) * scale, _F32), fourier_w[0], precision=_HI)
    sw = jnp.einsum("cm,gmd->gcd", jnp.asarray(np.sin(ang) * scale, _F32), fourier_w[0], precision=_HI)
    fn = _fft2(gc, m2, cw.astype(_BF), sw.astype(_BF), row(mix_norm_f_g[0]))

    ws, wy, at = _s5_weights(s5_lam_re[0], s5_lam_im[0], s5_b_re[0], s5_b_im[0],
                             s5_c_re[0], s5_c_im[0], s5_log_step[0])
    y_g = _s5(u_ctx, u_lat, ws, wy, at, b)

    n_rt = N_EXPERT_GROUPS * (1 + EXPERTS_PER_GROUP)
    w_rt = jnp.concatenate([moe_w_group[0], moe_w_router[0].reshape(d, -1)], axis=-1)
    w_rt = jnp.pad(w_rt, ((0, 0), (0, ROUTE_LANES - n_rt)))
    b_rt = jnp.pad(jnp.concatenate([moe_b_group[0], moe_b_router[0].reshape(-1)]), (0, ROUTE_LANES - n_rt))
    x1, tmod, rt, rtt, cnt = _post(x, zs, y_g, fn, row(s5_d[0]), s5_w_glu[0].astype(_BF),
                                   row(mix_norm_s5_g[0]), w_out[0].astype(_BF), g1, row(norm2_g[0]),
                                   sh2, sc2, w_rt, row(b_rt))

    tm = MOE_TM
    n_tok = b * l
    nr = 2 * n_tok + n_exp * tm
    rec = rtt.transpose(1, 0, 2).reshape(ROUTE_ROWS, n_tok)
    eid = rec[0:2].astype(jnp.int32)
    rank = rec[4:6].astype(jnp.int32)
    tok_ids = jnp.broadcast_to(jnp.arange(n_tok, dtype=jnp.int32), (2, n_tok))
    pos, tile_expert, row_token = _route_plan(eid, rank, tok_ids, cnt[0, :n_exp].astype(jnp.int32), tm, nr)
    xs = jnp.take(tmod.reshape(n_tok, d), row_token, axis=0)
    ys = _moe(tile_expert, xs, moe_w_gate[0], moe_w_up[0], moe_w_down[0])
    y0 = jnp.take(ys, pos[0], axis=0).reshape(b, l, d)
    y1 = jnp.take(ys, pos[1], axis=0).reshape(b, l, d)
    return _final(x1, y0, y1, rt, g2, row(final_norm_g))
```

```python
import functools
import math

import numpy as np
import jax
import jax.numpy as jnp
from jax import lax
from jax.experimental import pallas as pl
from jax.experimental.pallas import tpu as pltpu

EPS = 1e-6
S5_GROUP = 16
S5_STATE = 64
S5_CHUNK = 16
FOURIER_GROUPS = 4
N_EXPERT_GROUPS = 4
EXPERTS_PER_GROUP = 8
FFT_R = 64
FFT_BLK = 8
MOE_TM = 256
ROUTE_LANES = 128
ROUTE_ROWS = 8
VMEM_LIMIT = 56 * 1024 * 1024

_HI = lax.Precision.HIGHEST
_BF = jnp.bfloat16
_F32 = jnp.float32


def _cparams(sem):
    return pltpu.CompilerParams(dimension_semantics=sem, vmem_limit_bytes=VMEM_LIMIT)


def _dot(a, b):
    return jnp.dot(a, b, preferred_element_type=_F32)


def _rms(x, g):
    return x * lax.rsqrt(jnp.mean(x * x, axis=-1, keepdims=True) + EPS) * g


def _adaln_body(c_ref, w_ref, b_ref, o_ref):
    c = c_ref[...]
    a = c * jax.nn.sigmoid(c)
    o_ref[...] = jnp.dot(a, w_ref[...], preferred_element_type=_F32, precision=_HI) + b_ref[...]


def _adaln(cond, w, b):
    m, d = cond.shape
    n = w.shape[1]
    tn = 768
    return pl.pallas_call(
        _adaln_body,
        grid=(n // tn,),
        in_specs=[pl.BlockSpec((m, d), lambda j: (0, 0)),
                  pl.BlockSpec((d, tn), lambda j: (0, j)),
                  pl.BlockSpec((1, tn), lambda j: (0, j))],
        out_specs=pl.BlockSpec((m, tn), lambda j: (0, j)),
        out_shape=jax.ShapeDtypeStruct((m, n), _F32),
        compiler_params=_cparams(("arbitrary",)),
        name="adaln",
    )(cond, w, b.reshape(1, n))


def _inproj_body(x_ref, sh_ref, sc_ref, g_ref, w_ref, m1_ref, zs_ref, gc_ref):
    r, blk, d = x_ref.shape
    x = x_ref[...].reshape(r * blk, d)
    h = _rms(x, g_ref[...]) * (1.0 + sc_ref[...]) + sh_ref[...]
    z = _dot(h.astype(_BF), w_ref[...])
    ds5 = zs_ref.shape[-1]
    zs_ref[...] = z[:, :ds5].reshape(r, blk, ds5)
    v = z[:, ds5:].astype(_BF)
    g1 = _dot(m1_ref[...], v)
    gc_ref[...] = g1.astype(_BF).reshape(r, 2 * blk, v.shape[-1])


def _inproj(x, sh, sc, g, w_bf, m1):
    b, l, d = x.shape
    r, blk = FFT_R, FFT_BLK
    nj = r // blk
    dmix = w_bf.shape[1]
    ds5 = dmix // 2
    df = dmix - ds5
    x4 = x.reshape(b, r, r, d)
    zs, gc = pl.pallas_call(
        _inproj_body,
        grid=(b, nj),
        in_specs=[pl.BlockSpec((None, r, blk, d), lambda i, j: (i, 0, j, 0)),
                  pl.BlockSpec((None, 1, d), lambda i, j: (i, 0, 0)),
                  pl.BlockSpec((None, 1, d), lambda i, j: (i, 0, 0)),
                  pl.BlockSpec((1, d), lambda i, j: (0, 0)),
                  pl.BlockSpec((d, dmix), lambda i, j: (0, 0)),
                  pl.BlockSpec(m1.shape, lambda i, j: (0, 0))],
        out_specs=[pl.BlockSpec((None, r, blk, ds5), lambda i, j: (i, 0, j, 0)),
                   pl.BlockSpec((None, r, 2 * blk, df), lambda i, j: (i, 0, j, 0))],
        out_shape=[jax.ShapeDtypeStruct((b, r, r, ds5), _F32),
                   jax.ShapeDtypeStruct((b, r, 2 * r, df), _BF)],
        compiler_params=_cparams(("parallel", "arbitrary")),
        name="inproj",
    )(x4, sh, sc, g, w_bf, m1)
    return zs.reshape(b, l, ds5), gc


def _seg_transpose(arrs):
    n = len(arrs)
    seg = lax.broadcasted_iota(jnp.int32, arrs[0].shape, 1) // S5_GROUP
    d = n // 2
    while d >= 1:
        keep = (seg & d) == 0
        new = list(arrs)
        for i in range(n):
            if i & d == 0:
                a, b = arrs[i], arrs[i + d]
                new[i] = jnp.where(keep, a, pltpu.roll(b, S5_GROUP * d, 1))
                new[i + d] = jnp.where(keep, pltpu.roll(a, 128 - S5_GROUP * d, 1), b)
        arrs = new
        d //= 2
    return arrs


@functools.lru_cache(maxsize=None)
def _chunk_row_perm():
    t = S5_CHUNK
    p = np.zeros((t * t, t * t), np.float32)
    for c in range(t):
        for tl in range(t):
            p[tl * t + c, c * t + tl] = 1.0
    return p.astype(jnp.bfloat16)


def _pack_chunks(z_bf, p_ref, u_ref):
    t = S5_CHUNK
    nsub = z_bf.shape[0] // (t * t)
    pieces = [_dot(p_ref[...], z_bf[s * t * t:(s + 1) * t * t, :]) for s in range(nsub)]
    a = []
    for tl in range(t):
        rows = [pc[tl * t:(tl + 1) * t, :] for pc in pieces]
        a.append(rows[0] if nsub == 1 else jnp.concatenate(rows, axis=0))
    lanes = 128
    gpb = lanes // S5_GROUP
    for j in range(z_bf.shape[1] // lanes):
        for hi in range(t // gpb):
            outs = _seg_transpose([a[gpb * hi + tlo][:, lanes * j:lanes * (j + 1)] for tlo in range(gpb)])
            for glo in range(gpb):
                u_ref[gpb * j + glo, :, lanes * hi:lanes * (hi + 1)] = outs[glo].astype(u_ref.dtype)


def _inproj_ctx_body(x_ref, sh_ref, sc_ref, g_ref, w_ref, p_ref, u_ref):
    h = _rms(x_ref[...], g_ref[...]) * (1.0 + sc_ref[...]) + sh_ref[...]
    z = _dot(h.astype(_BF), w_ref[...])
    _pack_chunks(z.astype(_BF), p_ref, u_ref)


def _inproj_ctx(ctx, sh, sc, g, w_s5_bf, perm):
    b, lc, d = ctx.shape
    ds5 = w_s5_bf.shape[1]
    n_grp = ds5 // S5_GROUP
    kk = S5_CHUNK * S5_GROUP
    return pl.pallas_call(
        _inproj_ctx_body,
        grid=(b,),
        in_specs=[pl.BlockSpec((None, lc, d), lambda i: (i, 0, 0)),
                  pl.BlockSpec((1, d), lambda i: (0, 0)),
                  pl.BlockSpec((1, d), lambda i: (0, 0)),
                  pl.BlockSpec((1, d), lambda i: (0, 0)),
                  pl.BlockSpec((d, ds5), lambda i: (0, 0)),
                  pl.BlockSpec(perm.shape, lambda i: (0, 0))],
        out_specs=pl.BlockSpec((n_grp, lc // S5_CHUNK, kk), lambda i: (0, 0, i)),
        out_shape=jax.ShapeDtypeStruct((n_grp, lc // S5_CHUNK, b * kk), _BF),
        compiler_params=_cparams(("arbitrary",)),
        name="inproj_ctx",
    )(ctx, sh, sc, g, w_s5_bf, perm)


def _s5_pack_body(z_ref, p_ref, u_ref):
    _pack_chunks(z_ref[...].astype(_BF), p_ref, u_ref)


def _s5_pack(zs, perm, tok=2048):
    b, l, ds5 = zs.shape
    n_grp = ds5 // S5_GROUP
    kk = S5_CHUNK * S5_GROUP
    return pl.pallas_call(
        _s5_pack_body,
        grid=(b, l // tok),
        in_specs=[pl.BlockSpec((None, tok, ds5), lambda i, j: (i, j, 0)),
                  pl.BlockSpec(perm.shape, lambda i, j: (0, 0))],
        out_specs=pl.BlockSpec((n_grp, tok // S5_CHUNK, kk), lambda i, j: (0, j, i)),
        out_shape=jax.ShapeDtypeStruct((n_grp, l // S5_CHUNK, b * kk), _BF),
        compiler_params=_cparams(("parallel", "arbitrary")),
        name="s5_pack",
    )(zs, perm)


@functools.lru_cache(maxsize=None)
def _fft_stage1_matrix():
    r, blk = FFT_R, FFT_BLK
    k1 = np.arange(r)[:, None]
    l1 = np.arange(r)[None, :]
    ang = 2.0 * np.pi * ((k1 * l1) % r) / r
    f = np.stack([np.cos(ang), -np.sin(ang)], axis=1)
    m = np.einsum("kal,pq->kaplq", f, np.eye(blk))
    return m.reshape(r * 2 * blk, r * blk).astype(np.float32)


@functools.lru_cache(maxsize=None)
def _fft_stage2_matrices():
    r, blk = FFT_R, FFT_BLK
    n = r * r
    nt = r // blk
    l2 = np.arange(r)
    k2 = np.arange(r)
    out = np.zeros((nt, 2, r, blk, blk, r // blk, 2, blk), np.float32)
    for i in range(nt):
        for kl in range(blk):
            k1 = blk * i + kl
            ang = 2.0 * np.pi * (((k2[:, None] * l2[None, :] * r) + l2[None, :] * k1) % n) / n
            tr, ti = np.cos(ang), -np.sin(ang)
            tr = tr.reshape(r, r // blk, blk)
            ti = ti.reshape(r, r // blk, blk)
            out[i, 0, :, kl, kl, :, 0, :] = tr
            out[i, 0, :, kl, kl, :, 1, :] = -ti
            out[i, 1, :, kl, kl, :, 0, :] = ti
            out[i, 1, :, kl, kl, :, 1, :] = tr
    return out.reshape(nt, 2 * r * blk, blk * 2 * r)


def _fft2_body(gc_ref, m2_ref, cw_ref, sw_ref, g_ref, o_ref):
    kb, rr, df = gc_ref.shape
    gc = gc_ref[...].reshape(kb * rr, df)
    x = _dot(m2_ref[...], gc)
    half = x.shape[0] // 2
    xr = x[:half].astype(_BF)
    xi = x[half:].astype(_BF)
    ng = cw_ref.shape[0]
    fg = df // ng
    parts = []
    for g in range(ng):
        sl = slice(g * fg, (g + 1) * fg)
        parts.append(_dot(xr[:, sl], cw_ref[g]) + _dot(xi[:, sl], sw_ref[g]))
    f = jnp.concatenate(parts, axis=-1)
    fn = _rms(f, g_ref[...])
    o_ref[...] = fn.reshape(o_ref.shape)


def _fft2(gc, m2, cw, sw, gf):
    b, r, rr, df = gc.shape
    blk = FFT_BLK
    nt = r // blk
    out = pl.pallas_call(
        _fft2_body,
        grid=(b, nt),
        in_specs=[pl.BlockSpec((None, blk, rr, df), lambda i, j: (i, j, 0, 0)),
                  pl.BlockSpec((None,) + m2.shape[1:], lambda i, j: (j, 0, 0)),
                  pl.BlockSpec(cw.shape, lambda i, j: (0, 0, 0)),
                  pl.BlockSpec(sw.shape, lambda i, j: (0, 0, 0)),
                  pl.BlockSpec((1, df), lambda i, j: (0, 0))],
        out_specs=pl.BlockSpec((None, r, blk, df), lambda i, j: (i, 0, j, 0)),
        out_shape=jax.ShapeDtypeStruct((b, r, r, df), _F32),
        compiler_params=_cparams(("parallel", "arbitrary")),
        name="fft2",
    )(gc, m2, cw, sw, gf)
    return out.reshape(b, r * r, df)


def _s5_body(uc_ref, ul_ref, ws_ref, wy_ref, at_ref, y_ref,
             s_re, s_im, ha_re, hb_re, ha_im, hb_im, *, nb):
    p = S5_STATE
    kk = ws_ref.shape[0]
    n_ctx, n_lat = uc_ref.shape[0], ul_ref.shape[0]
    ws = ws_ref[...]
    for b in range(nb):
        sl = slice(b * kk, (b + 1) * kk)
        sc = _dot(uc_ref[:, sl], ws)
        sl_ = _dot(ul_ref[:, sl], ws)
        s_re[pl.ds(b, n_ctx, stride=nb), :] = sc[:, :2 * p]
        s_im[pl.ds(b, n_ctx, stride=nb), :] = sc[:, 2 * p:]
        s_re[pl.ds(n_ctx * nb + b, n_lat, stride=nb), :] = sl_[:, :2 * p]
        s_im[pl.ds(n_ctx * nb + b, n_lat, stride=nb), :] = sl_[:, 2 * p:]

    a_re = at_ref[0:1, :]
    a_im = at_ref[1:2, :]
    is_f = lax.broadcasted_iota(jnp.int32, (nb, 2 * p), 1) < p
    n_all = n_ctx + n_lat

    def load(cf, cr):
        rf = pl.multiple_of(cf * nb, nb)
        rr = pl.multiple_of(cr * nb, nb)
        return (jnp.where(is_f, s_re[pl.ds(rf, nb), :], s_re[pl.ds(rr, nb), :]),
                jnp.where(is_f, s_im[pl.ds(rf, nb), :], s_im[pl.ds(rr, nb), :]))

    def advance(h_re, h_im, x_re, x_im):
        return (a_re * h_re - a_im * h_im + x_re, a_re * h_im + a_im * h_re + x_im)

    def ctx_step(step, carry):
        x_re, x_im = load(step, n_ctx - 1 - step)
        return advance(*carry, x_re, x_im)

    def lat_step(step, carry):
        h_re, h_im = carry
        rf = pl.multiple_of(step * nb, nb)
        rr = pl.multiple_of((n_lat - 1 - step) * nb, nb)
        ha_re[pl.ds(rf, nb), :] = h_re
        hb_re[pl.ds(rr, nb), :] = h_re
        ha_im[pl.ds(rf, nb), :] = h_im
        hb_im[pl.ds(rr, nb), :] = h_im
        x_re, x_im = load(n_ctx + step, n_all - 1 - step)
        return advance(h_re, h_im, x_re, x_im)

    zero = jnp.zeros((nb, 2 * p), _F32)
    carry = lax.fori_loop(0, n_ctx, ctx_step, (zero, zero))
    lax.fori_loop(0, n_lat, lat_step, carry)

    wy = wy_ref[...]
    for b in range(nb):
        rows = pl.ds(b, n_lat, stride=nb)
        hin = jnp.concatenate([ha_re[rows, :], hb_re[rows, :], ha_im[rows, :], hb_im[rows, :]], axis=-1)
        lhs = jnp.concatenate([ul_ref[:, b * kk:(b + 1) * kk], hin.astype(_BF)], axis=-1)
        y_ref[:, b * kk:(b + 1) * kk] = _dot(lhs, wy)


def _s5(u_ctx, u_lat, ws, wy, at, nb):
    ng, n_ctx, w = u_ctx.shape
    n_lat = u_lat.shape[1]
    body = functools.partial(_s5_body, nb=nb)
    hs = pltpu.VMEM((n_lat * nb, 2 * S5_STATE), _F32)
    ss = pltpu.VMEM(((n_ctx + n_lat) * nb, 2 * S5_STATE), _F32)
    return pl.pallas_call(
        body,
        grid=(ng,),
        in_specs=[pl.BlockSpec((None, n_ctx, w), lambda g: (g, 0, 0)),
                  pl.BlockSpec((None, n_lat, w), lambda g: (g, 0, 0)),
                  pl.BlockSpec((None,) + ws.shape[1:], lambda g: (g, 0, 0)),
                  pl.BlockSpec((None,) + wy.shape[1:], lambda g: (g, 0, 0)),
                  pl.BlockSpec((None,) + at.shape[1:], lambda g: (g, 0, 0))],
        out_specs=pl.BlockSpec((None, n_lat, w), lambda g: (g, 0, 0)),
        out_shape=jax.ShapeDtypeStruct((ng, n_lat, w), _F32),
        scratch_shapes=[ss, ss, hs, hs, hs, hs],
        compiler_params=_cparams(("parallel",)),
        name="s5_scan",
    )(u_ctx, u_lat, ws, wy, at)


def _s5_weights(lam_re, lam_im, b_re, b_im, c_re, c_im, log_step):
    t = S5_CHUNK
    ng, p = lam_re.shape[1], lam_re.shape[2]
    hh = b_re.shape[-1]
    dt = jnp.exp(log_step.astype(_F32))[..., None]
    lr, li = lam_re.astype(_F32), lam_im.astype(_F32)
    mag = jnp.exp(lr * dt)
    ab_re, ab_im = mag * jnp.cos(li * dt), mag * jnp.sin(li * dt)
    den = lr * lr + li * li
    nr, ni = ab_re - 1.0, ab_im
    q_re = (nr * lr + ni * li) / den
    q_im = (ni * lr - nr * li) / den
    br, bi = b_re.astype(_F32), b_im.astype(_F32)
    bb_re = q_re[..., None] * br - q_im[..., None] * bi
    bb_im = q_re[..., None] * bi + q_im[..., None] * br
    k = jnp.arange(t + 1, dtype=_F32)[:, None, None, None]
    pm = jnp.exp(k * (lr * dt)[None])
    ap_re = pm * jnp.cos(k * (li * dt)[None])
    ap_im = pm * jnp.sin(k * (li * dt)[None])
    cr, ci = c_re.astype(_F32), c_im.astype(_F32)

    def ein(spec, *ops):
        return jnp.einsum(spec, *ops, precision=_HI)

    ca_re = cr[None] * ap_re[:, :, :, None, :] - ci[None] * ap_im[:, :, :, None, :]
    ca_im = cr[None] * ap_im[:, :, :, None, :] + ci[None] * ap_re[:, :, :, None, :]
    lhs = jnp.concatenate([ca_re[:t], -ca_im[:t]], axis=-1)
    lhs = lhs.transpose(1, 2, 0, 3, 4).reshape(2, ng, t * hh, 2 * p)
    rhs = jnp.concatenate([bb_re, bb_im], axis=-2)
    kk = ein("dgmq,dgqj->dgmj", lhs, rhs).reshape(2, ng, t, hh, hh)
    tl = jnp.arange(t)

    def toeplitz(kd):
        w = jnp.pad(kd.transpose(0, 3, 2, 1), ((0, 0), (0, 0), (0, 0), (0, t)))
        x = jnp.tile(w, (1, 1, 1, t))[..., :t * (2 * t - 1)].reshape(ng, hh, hh, t, 2 * t - 1)
        return x[..., :t]

    mf = toeplitz(kk[0])
    mr = toeplitz(kk[1]).transpose(0, 1, 2, 4, 3)
    msum = (mf + mr).transpose(0, 3, 1, 4, 2).reshape(ng, t * hh, t * hh)

    pw_f = (t - 1 - tl)
    pw_r = tl

    def state_w(pw, d):
        ar, ai = ap_re[pw, d], ap_im[pw, d]
        wr = ar[..., None] * bb_re[d][None] - ai[..., None] * bb_im[d][None]
        wi = ar[..., None] * bb_im[d][None] + ai[..., None] * bb_re[d][None]
        wr = wr.transpose(1, 0, 3, 2).reshape(ng, t * hh, p)
        wi = wi.transpose(1, 0, 3, 2).reshape(ng, t * hh, p)
        return wr, wi

    wf_re, wf_im = state_w(pw_f, 0)
    wr_re, wr_im = state_w(pw_r, 1)
    ws = jnp.concatenate([wf_re, wr_re, wf_im, wr_im], axis=-1)

    def out_w(pw, d):
        xr = ca_re[pw, d]
        xi = ca_im[pw, d]
        vr = xr.transpose(1, 3, 0, 2).reshape(ng, p, t * hh)
        vi = (-xi).transpose(1, 3, 0, 2).reshape(ng, p, t * hh)
        return vr, vi

    vf_re, vf_im = out_w(tl + 1, 0)
    vr_re, vr_im = out_w(t - tl, 1)
    zz = jnp.zeros_like(vf_re)
    wy = jnp.concatenate([msum, vf_re, zz, zz, vr_re, vf_im, zz, zz, vr_im], axis=1)

    at = jnp.stack([jnp.concatenate([ap_re[t, 0], ap_re[t, 1]], axis=-1),
                    jnp.concatenate([ap_im[t, 0], ap_im[t, 1]], axis=-1)], axis=1)
    at = jnp.pad(at, ((0, 0), (0, 6), (0, 0)))
    return ws.astype(_BF), wy.astype(_BF), at


def _post_body(x_ref, u_ref, yg_ref, fn_ref, d_ref, wglu_ref, gs_ref, wo_ref, g1_ref,
               n2_ref, sh_ref, sc_ref, wr_ref, br_ref, tri_ref, x1_ref, t_ref, rt_ref, rtt_ref, cnt_ref,
               y_scr, cnt_scr):
    ds5 = u_ref.shape[-1]
    first = (pl.program_id(0) == 0) & (pl.program_id(1) == 0)

    @pl.when(first)
    def _():
        cnt_scr[...] = jnp.zeros_like(cnt_scr)

    t_chunk = S5_CHUNK
    n_chunk = yg_ref.shape[1]
    lanes = 128
    gpb = lanes // S5_GROUP
    for j in range(ds5 // lanes):
        for hi in range(t_chunk // gpb):
            outs = _seg_transpose([yg_ref[gpb * j + glo, :, lanes * hi:lanes * (hi + 1)] for glo in range(gpb)])
            for tlo in range(gpb):
                y_scr[j, pl.ds(gpb * hi + tlo, n_chunk, stride=t_chunk), :] = outs[tlo]
    y_s5 = jnp.concatenate([y_scr[j] for j in range(ds5 // lanes)], axis=-1)

    y = y_s5 + d_ref[...] * u_ref[...]
    y = jax.nn.gelu(y, approximate=True)
    y = y * jax.nn.sigmoid(_dot(y.astype(_BF), wglu_ref[...]))
    yn = _rms(y, gs_ref[...]).astype(_BF)
    mix = _dot(yn, wo_ref[0:ds5, :]) + _dot(fn_ref[...].astype(_BF), wo_ref[ds5:, :])
    x1 = x_ref[...] + g1_ref[...] * mix
    x1_ref[...] = x1
    t = _rms(x1, n2_ref[...]) * (1.0 + sc_ref[...]) + sh_ref[...]
    t_ref[...] = t.astype(_BF)

    ng, epg = N_EXPERT_GROUPS, EXPERTS_PER_GROUP
    logits = jnp.dot(t, wr_ref[...], preferred_element_type=_F32, precision=_HI) + br_ref[...]
    lane = lax.broadcasted_iota(jnp.int32, logits.shape, 1)
    neg = jnp.float32(-jnp.inf)
    big = jnp.int32(1 << 20)
    gl = jnp.where(lane < ng, logits, neg)
    gmax = jnp.max(gl, axis=-1, keepdims=True)
    gidx = jnp.min(jnp.where(gl == gmax, lane, big), axis=-1, keepdims=True)
    gw = 1.0 / jnp.sum(jnp.where(lane < ng, jnp.exp(logits - gmax), 0.0), axis=-1, keepdims=True)
    lo = ng + gidx * epg
    el = jnp.where((lane >= lo) & (lane < lo + epg), logits, neg)
    v0 = jnp.max(el, axis=-1, keepdims=True)
    i0 = jnp.min(jnp.where(el == v0, lane, big), axis=-1, keepdims=True)
    el1 = jnp.where(lane == i0, neg, el)
    v1 = jnp.max(el1, axis=-1, keepdims=True)
    i1 = jnp.min(jnp.where(el1 == v1, lane, big), axis=-1, keepdims=True)
    p0 = 1.0 / (1.0 + jnp.exp(v1 - v0))
    w0 = gw * p0
    w1 = gw * (1.0 - p0)
    e0 = i0 - ng
    e1 = i1 - ng

    oh0 = lane == e0
    oh1 = lane == e1
    oh = (oh0 | oh1).astype(_F32)
    prefix = _dot(tri_ref[...], oh.astype(_BF)) + cnt_scr[...]
    r0 = jnp.sum(jnp.where(oh0, prefix, 0.0), axis=-1, keepdims=True)
    r1 = jnp.sum(jnp.where(oh1, prefix, 0.0), axis=-1, keepdims=True)
    cnt = cnt_scr[...] + jnp.sum(oh, axis=0, keepdims=True)
    cnt_scr[...] = cnt
    cnt_ref[...] = cnt

    vals = (e0.astype(_F32), e1.astype(_F32), w0, w1, r0, r1)
    rt = jnp.zeros(logits.shape, _F32)
    for k, v in enumerate(vals):
        rt = jnp.where(lane == k, v, rt)
    rt_ref[...] = rt
    rtt_ref[...] = rt.T[0:rtt_ref.shape[0], :]


@functools.lru_cache(maxsize=None)
def _strict_lower_ones(n):
    return np.tril(np.ones((n, n), np.float32), -1).astype(jnp.bfloat16)


def _post(x, u, y_g, fn, s5_d, wglu_bf, gs, wo_bf, g1, n2, sh2, sc2, wr, br, tm=512):
    b, l, d = x.shape
    ds5 = u.shape[-1]
    n_grp, _, w = y_g.shape
    kk = w // b
    tri = jnp.asarray(_strict_lower_ones(tm))
    tok = lambda w: pl.BlockSpec((None, tm, w), lambda i, j: (i, j, 0))
    per_b = pl.BlockSpec((None, 1, d), lambda i, j: (i, 0, 0))
    full = lambda a: pl.BlockSpec(a.shape, lambda i, j: (0,) * a.ndim)
    ygs = pl.BlockSpec((n_grp, tm // S5_CHUNK, kk), lambda i, j: (0, j, i))
    return pl.pallas_call(
        _post_body,
        grid=(b, l // tm),
        in_specs=[tok(d), tok(ds5), ygs, tok(fn.shape[-1]), full(s5_d), full(wglu_bf),
                  full(gs), full(wo_bf), per_b, full(n2), per_b, per_b, full(wr), full(br), full(tri)],
        out_specs=[tok(d), tok(d), tok(ROUTE_LANES),
                   pl.BlockSpec((None, ROUTE_ROWS, tm), lambda i, j: (i, 0, j)),
                   pl.BlockSpec((1, ROUTE_LANES), lambda i, j: (0, 0))],
        out_shape=[jax.ShapeDtypeStruct((b, l, d), _F32),
                   jax.ShapeDtypeStruct((b, l, d), _BF),
                   jax.ShapeDtypeStruct((b, l, ROUTE_LANES), _F32),
                   jax.ShapeDtypeStruct((b, ROUTE_ROWS, l), _F32),
                   jax.ShapeDtypeStruct((1, ROUTE_LANES), _F32)],
        scratch_shapes=[pltpu.VMEM((ds5 // 128, tm, 128), _F32),
                        pltpu.VMEM((1, ROUTE_LANES), _F32)],
        compiler_params=_cparams(("arbitrary", "arbitrary")),
        name="post_mixer",
    )(x, u, y_g, fn, s5_d, wglu_bf, gs, wo_bf, g1, n2, sh2, sc2, wr, br, tri)


def _moe_body(te_ref, xs_ref, wg_ref, wu_ref, wd_ref, o_ref, wg_bf, wu_bf, wd_bf):
    i = pl.program_id(0)
    new_expert = (i == 0) | (te_ref[i] != te_ref[jnp.maximum(i - 1, 0)])

    @pl.when(new_expert)
    def _():
        wg_bf[...] = wg_ref[...].astype(_BF)
        wu_bf[...] = wu_ref[...].astype(_BF)
        wd_bf[...] = wd_ref[...].astype(_BF)

    xs = xs_ref[...]
    hg = _dot(xs, wg_bf[...])
    hu = _dot(xs, wu_bf[...])
    a = hg * jax.nn.sigmoid(hg) * hu
    o_ref[...] = _dot(a.astype(_BF), wd_bf[...]).astype(o_ref.dtype)


def _moe(tile_expert, xs, wg, wu, wd):
    nr, d = xs.shape
    tm = MOE_TM
    de = wg.shape[-1]
    grid_spec = pltpu.PrefetchScalarGridSpec(
        num_scalar_prefetch=1,
        grid=(nr // tm,),
        in_specs=[pl.BlockSpec((tm, d), lambda i, te: (i, 0)),
                  pl.BlockSpec((None, d, de), lambda i, te: (te[i], 0, 0)),
                  pl.BlockSpec((None, d, de), lambda i, te: (te[i], 0, 0)),
                  pl.BlockSpec((None, de, d), lambda i, te: (te[i], 0, 0))],
        out_specs=pl.BlockSpec((tm, d), lambda i, te: (i, 0)),
        scratch_shapes=[pltpu.VMEM((d, de), _BF), pltpu.VMEM((d, de), _BF), pltpu.VMEM((de, d), _BF)],
    )
    return pl.pallas_call(
        _moe_body,
        grid_spec=grid_spec,
        out_shape=jax.ShapeDtypeStruct((nr, d), _BF),
        compiler_params=_cparams(("arbitrary",)),
        name="moe_experts",
    )(tile_expert, xs, wg, wu, wd)


def _final_body(x1_ref, y0_ref, y1_ref, rt_ref, g2_ref, gf_ref, o_ref):
    w0 = rt_ref[:, 2:3]
    w1 = rt_ref[:, 3:4]
    m = w0 * y0_ref[...].astype(_F32) + w1 * y1_ref[...].astype(_F32)
    o_ref[...] = _rms(x1_ref[...] + g2_ref[...] * m, gf_ref[...])


def _final(x1, y0, y1, rt, g2, gf, tm=512):
    b, l, d = x1.shape
    tok = pl.BlockSpec((None, tm, d), lambda i, j: (i, j, 0))
    return pl.pallas_call(
        _final_body,
        grid=(b, l // tm),
        in_specs=[tok, tok, tok, pl.BlockSpec((None, tm, ROUTE_LANES), lambda i, j: (i, j, 0)),
                  pl.BlockSpec((None, 1, d), lambda i, j: (i, 0, 0)),
                  pl.BlockSpec((1, d), lambda i, j: (0, 0))],
        out_specs=tok,
        out_shape=jax.ShapeDtypeStruct((b, l, d), _F32),
        compiler_params=_cparams(("parallel", "arbitrary")),
        name="final_norm",
    )(x1, y0, y1, rt, g2, gf)


def _route_plan(eid, rank, tok_ids, counts, tm, nr):
    n_experts = counts.shape[0]
    n_pairs = eid.size
    padded = ((counts + tm - 1) // tm) * tm
    pad_end = jnp.cumsum(padded)
    pad_start = pad_end - padded
    raw_start = jnp.cumsum(counts) - counts
    pos = rank
    for e in range(n_experts):
        pos = pos + jnp.where(eid == e, pad_start[e], 0)
    tile_start = jnp.arange(nr // tm, dtype=jnp.int32) * tm
    tile_expert = jnp.sum(tile_start[:, None] >= pad_end[None, :], axis=-1)
    tile_expert = jnp.minimum(tile_expert, n_experts - 1).astype(jnp.int32)
    _, sorted_tok = lax.sort_key_val((eid * n_pairs + rank).reshape(-1), tok_ids.reshape(-1))
    sel = tile_expert[:, None] == jnp.arange(n_experts, dtype=jnp.int32)[None, :]
    per_tile = lambda v: jnp.repeat(jnp.sum(jnp.where(sel, v[None, :], 0), axis=-1), tm)
    off = jnp.arange(nr, dtype=jnp.int32) - per_tile(pad_start)
    valid = off < per_tile(counts)
    j = jnp.clip(per_tile(raw_start) + off, 0, n_pairs - 1)
    filler = jnp.arange(nr, dtype=jnp.int32) % (n_pairs // 2)
    row_token = jnp.where(valid, sorted_tok.at[j].get(mode="promise_in_bounds"), filler)
    return pos, tile_expert, row_token


def kernel(x, c, ctx, c_ctx, w_ada, b_ada, norm1_g, norm2_g, w_in, s5_lam_re, s5_lam_im, s5_b_re, s5_b_im, s5_c_re, s5_c_im, s5_log_step, s5_d, s5_w_glu, fourier_w, mix_norm_s5_g, mix_norm_f_g, w_out, moe_w_group, moe_b_group, moe_w_router, moe_b_router, moe_w_gate, moe_w_up, moe_w_down, final_norm_g):
    b, l, d = x.shape
    lc = ctx.shape[1]
    depth = w_ada.shape[0]
    assert depth == 1 and l == FFT_R * FFT_R and lc % S5_CHUNK == 0 and b % 8 == 0
    ds5 = s5_d.shape[-1]
    df = w_in.shape[-1] - ds5
    n_exp = moe_w_gate.shape[1]
    row = lambda a: a.reshape(1, -1)

    cond = jnp.concatenate([c, c_ctx[None, :], jnp.zeros((7, d), _F32)], axis=0)
    mod = _adaln(cond, w_ada[0], b_ada[0])
    sh1, sc1, g1, sh2, sc2, g2 = [mod[:b, i * d:(i + 1) * d].reshape(b, 1, d) for i in range(6)]
    csh1 = mod[b:b + 1, 0:d]
    csc1 = mod[b:b + 1, d:2 * d]

    w_in_bf = w_in[0].astype(_BF)
    m1 = jnp.asarray(_fft_stage1_matrix()).astype(_BF)
    m2 = jnp.asarray(_fft_stage2_matrices()).astype(_BF)
    perm = jnp.asarray(_chunk_row_perm())
    zs, gc = _inproj(x, sh1, sc1, row(norm1_g[0]), w_in_bf, m1)
    u_ctx = _inproj_ctx(ctx, csh1, csc1, row(norm1_g[0]), w_in_bf[:, :ds5], perm)
    u_lat = _s5_pack(zs, perm)

    fg = df // FOURIER_GROUPS
    cc = np.arange(fg)
    ang = 2.0 * np.pi * ((cc[:, None] * cc[None, :]) % fg) / fg
    scale = 1.0 / math.sqrt(l * fg)
    cw = jnp.einsum("cm,gmd->gcd", jnp.asarray(np.cos(ang) * scale, _F32), fourier_w[0], precision=_HI)
    sw = jnp.einsum("cm,gmd->gcd", jnp.asarray(np.sin(ang) * scale, _F32), fourier_w[0], precision=_HI)
    fn = _fft2(gc, m2, cw.astype(_BF), sw.astype(_BF), row(mix_norm_f_g[0]))

    ws, wy, at = _s5_weights(s5_lam_re[0], s5_lam_im[0], s5_b_re[0], s5_b_im[0],
                             s5_c_re[0], s5_c_im[0], s5_log_step[0])
    y_g = _s5(u_ctx, u_lat, ws, wy, at, b)

    n_rt = N_EXPERT_GROUPS * (1 + EXPERTS_PER_GROUP)
    w_rt = jnp.concatenate([moe_w_group[0], moe_w_router[0].reshape(d, -1)], axis=-1)
    w_rt = jnp.pad(w_rt, ((0, 0), (0, ROUTE_LANES - n_rt)))
    b_rt = jnp.pad(jnp.concatenate([moe_b_group[0], moe_b_router[0].reshape(-1)]), (0, ROUTE_LANES - n_rt))
    x1, tmod, rt, rtt, cnt = _post(x, zs, y_g, fn, row(s5_d[0]), s5_w_glu[0].astype(_BF),
                                   row(mix_norm_s5_g[0]), w_out[0].astype(_BF), g1, row(norm2_g[0]),
                                   sh2, sc2, w_rt, row(b_rt))

    tm = MOE_TM
    n_tok = b * l
    nr = 2 * n_tok + n_exp * tm
    rec = rtt.transpose(1, 0, 2).reshape(ROUTE_ROWS, n_tok)
    eid = rec[0:2].astype(jnp.int32)
    rank = rec[4:6].astype(jnp.int32)
    tok_ids = jnp.broadcast_to(jnp.arange(n_tok, dtype=jnp.int32), (2, n_tok))
    pos, tile_expert, row_token = _route_plan(eid, rank, tok_ids, cnt[0, :n_exp].astype(jnp.int32), tm, nr)
    take_rows = lambda a, idx: a.at[idx].get(mode="promise_in_bounds")
    xs = take_rows(tmod.reshape(n_tok, d), row_token)
    ys = _moe(tile_expert, xs, moe_w_gate[0], moe_w_up[0], moe_w_down[0])
    y0 = take_rows(ys, pos[0]).reshape(b, l, d)
    y1 = take_rows(ys, pos[1]).reshape(b, l, d)
    return _final(x1, y0, y1, rt, g2, row(final_norm_g))
```

```python
import functools
import math

import numpy as np
import jax
import jax.numpy as jnp
from jax import lax
from jax.experimental import pallas as pl
from jax.experimental.pallas import tpu as pltpu

EPS = 1e-6
S5_GROUP = 16
S5_STATE = 64
S5_CHUNK = 16
FOURIER_GROUPS = 4
N_EXPERT_GROUPS = 4
EXPERTS_PER_GROUP = 8
FFT_R = 64
FFT_BLK = 8
MOE_TM = 512
ROUTE_LANES = 128
ROUTE_ROWS = 8
VMEM_LIMIT = 56 * 1024 * 1024

_HI = lax.Precision.HIGHEST
_BF = jnp.bfloat16
_F32 = jnp.float32


def _cparams(sem):
    return pltpu.CompilerParams(dimension_semantics=sem, vmem_limit_bytes=VMEM_LIMIT)


def _dot(a, b):
    return jnp.dot(a, b, preferred_element_type=_F32)


def _rms(x, g):
    return x * lax.rsqrt(jnp.mean(x * x, axis=-1, keepdims=True) + EPS) * g


def _adaln_body(c_ref, w_ref, b_ref, o_ref):
    c = c_ref[...]
    a = c * jax.nn.sigmoid(c)
    o_ref[...] = jnp.dot(a, w_ref[...], preferred_element_type=_F32, precision=_HI) + b_ref[...]


def _adaln(cond, w, b):
    m, d = cond.shape
    n = w.shape[1]
    tn = 768
    return pl.pallas_call(
        _adaln_body,
        grid=(n // tn,),
        in_specs=[pl.BlockSpec((m, d), lambda j: (0, 0)),
                  pl.BlockSpec((d, tn), lambda j: (0, j)),
                  pl.BlockSpec((1, tn), lambda j: (0, j))],
        out_specs=pl.BlockSpec((m, tn), lambda j: (0, j)),
        out_shape=jax.ShapeDtypeStruct((m, n), _F32),
        compiler_params=_cparams(("arbitrary",)),
        name="adaln",
    )(cond, w, b.reshape(1, n))


def _inproj_body(x_ref, sh_ref, sc_ref, g_ref, w_ref, m1_ref, zs_ref, gc_ref):
    r, blk, d = x_ref.shape
    x = x_ref[...].reshape(r * blk, d)
    h = _rms(x, g_ref[...]) * (1.0 + sc_ref[...]) + sh_ref[...]
    z = _dot(h.astype(_BF), w_ref[...])
    ds5 = zs_ref.shape[-1]
    zs_ref[...] = z[:, :ds5].reshape(r, blk, ds5)
    v = z[:, ds5:].astype(_BF)
    g1 = _dot(m1_ref[...], v)
    gc_ref[...] = g1.astype(_BF).reshape(r, 2 * blk, v.shape[-1])


def _inproj(x, sh, sc, g, w_bf, m1):
    b, l, d = x.shape
    r, blk = FFT_R, FFT_BLK
    nj = r // blk
    dmix = w_bf.shape[1]
    ds5 = dmix // 2
    df = dmix - ds5
    x4 = x.reshape(b, r, r, d)
    zs, gc = pl.pallas_call(
        _inproj_body,
        grid=(b, nj),
        in_specs=[pl.BlockSpec((None, r, blk, d), lambda i, j: (i, 0, j, 0)),
                  pl.BlockSpec((None, 1, d), lambda i, j: (i, 0, 0)),
                  pl.BlockSpec((None, 1, d), lambda i, j: (i, 0, 0)),
                  pl.BlockSpec((1, d), lambda i, j: (0, 0)),
                  pl.BlockSpec((d, dmix), lambda i, j: (0, 0)),
                  pl.BlockSpec(m1.shape, lambda i, j: (0, 0))],
        out_specs=[pl.BlockSpec((None, r, blk, ds5), lambda i, j: (i, 0, j, 0)),
                   pl.BlockSpec((None, r, 2 * blk, df), lambda i, j: (i, 0, j, 0))],
        out_shape=[jax.ShapeDtypeStruct((b, r, r, ds5), _F32),
                   jax.ShapeDtypeStruct((b, r, 2 * r, df), _BF)],
        compiler_params=_cparams(("parallel", "arbitrary")),
        name="inproj",
    )(x4, sh, sc, g, w_bf, m1)
    return zs.reshape(b, l, ds5), gc


def _seg_transpose(arrs):
    n = len(arrs)
    seg = lax.broadcasted_iota(jnp.int32, arrs[0].shape, 1) // S5_GROUP
    d = n // 2
    while d >= 1:
        keep = (seg & d) == 0
        new = list(arrs)
        for i in range(n):
            if i & d == 0:
                a, b = arrs[i], arrs[i + d]
                new[i] = jnp.where(keep, a, pltpu.roll(b, S5_GROUP * d, 1))
                new[i + d] = jnp.where(keep, pltpu.roll(a, 128 - S5_GROUP * d, 1), b)
        arrs = new
        d //= 2
    return arrs


@functools.lru_cache(maxsize=None)
def _chunk_row_perm():
    t = S5_CHUNK
    p = np.zeros((t * t, t * t), np.float32)
    for c in range(t):
        for tl in range(t):
            p[tl * t + c, c * t + tl] = 1.0
    return p.astype(jnp.bfloat16)


def _pack_chunks(z_bf, p_ref, u_ref):
    t = S5_CHUNK
    nsub = z_bf.shape[0] // (t * t)
    pieces = [_dot(p_ref[...], z_bf[s * t * t:(s + 1) * t * t, :]) for s in range(nsub)]
    a = []
    for tl in range(t):
        rows = [pc[tl * t:(tl + 1) * t, :] for pc in pieces]
        a.append(rows[0] if nsub == 1 else jnp.concatenate(rows, axis=0))
    lanes = 128
    gpb = lanes // S5_GROUP
    for j in range(z_bf.shape[1] // lanes):
        for hi in range(t // gpb):
            outs = _seg_transpose([a[gpb * hi + tlo][:, lanes * j:lanes * (j + 1)] for tlo in range(gpb)])
            for glo in range(gpb):
                u_ref[gpb * j + glo, :, lanes * hi:lanes * (hi + 1)] = outs[glo].astype(u_ref.dtype)


def _inproj_ctx_body(x_ref, sh_ref, sc_ref, g_ref, w_ref, p_ref, u_ref):
    h = _rms(x_ref[...], g_ref[...]) * (1.0 + sc_ref[...]) + sh_ref[...]
    z = _dot(h.astype(_BF), w_ref[...])
    _pack_chunks(z.astype(_BF), p_ref, u_ref)


def _inproj_ctx(ctx, sh, sc, g, w_s5_bf, perm):
    b, lc, d = ctx.shape
    ds5 = w_s5_bf.shape[1]
    n_grp = ds5 // S5_GROUP
    kk = S5_CHUNK * S5_GROUP
    return pl.pallas_call(
        _inproj_ctx_body,
        grid=(b,),
        in_specs=[pl.BlockSpec((None, lc, d), lambda i: (i, 0, 0)),
                  pl.BlockSpec((1, d), lambda i: (0, 0)),
                  pl.BlockSpec((1, d), lambda i: (0, 0)),
                  pl.BlockSpec((1, d), lambda i: (0, 0)),
                  pl.BlockSpec((d, ds5), lambda i: (0, 0)),
                  pl.BlockSpec(perm.shape, lambda i: (0, 0))],
        out_specs=pl.BlockSpec((n_grp, lc // S5_CHUNK, kk), lambda i: (0, 0, i)),
        out_shape=jax.ShapeDtypeStruct((n_grp, lc // S5_CHUNK, b * kk), _BF),
        compiler_params=_cparams(("arbitrary",)),
        name="inproj_ctx",
    )(ctx, sh, sc, g, w_s5_bf, perm)


def _s5_pack_body(z_ref, p_ref, u_ref):
    _pack_chunks(z_ref[...].astype(_BF), p_ref, u_ref)


def _s5_pack(zs, perm, tok=2048):
    b, l, ds5 = zs.shape
    n_grp = ds5 // S5_GROUP
    kk = S5_CHUNK * S5_GROUP
    return pl.pallas_call(
        _s5_pack_body,
        grid=(b, l // tok),
        in_specs=[pl.BlockSpec((None, tok, ds5), lambda i, j: (i, j, 0)),
                  pl.BlockSpec(perm.shape, lambda i, j: (0, 0))],
        out_specs=pl.BlockSpec((n_grp, tok // S5_CHUNK, kk), lambda i, j: (0, j, i)),
        out_shape=jax.ShapeDtypeStruct((n_grp, l // S5_CHUNK, b * kk), _BF),
        compiler_params=_cparams(("parallel", "arbitrary")),
        name="s5_pack",
    )(zs, perm)


@functools.lru_cache(maxsize=None)
def _fft_stage1_matrix():
    r, blk = FFT_R, FFT_BLK
    k1 = np.arange(r)[:, None]
    l1 = np.arange(r)[None, :]
    ang = 2.0 * np.pi * ((k1 * l1) % r) / r
    f = np.stack([np.cos(ang), -np.sin(ang)], axis=1)
    m = np.einsum("kal,pq->kaplq", f, np.eye(blk))
    return m.reshape(r * 2 * blk, r * blk).astype(np.float32)


@functools.lru_cache(maxsize=None)
def _fft_stage2_matrices():
    r, blk = FFT_R, FFT_BLK
    n = r * r
    nt = r // blk
    l2 = np.arange(r)
    k2 = np.arange(r)
    out = np.zeros((nt, 2, r, blk, blk, r // blk, 2, blk), np.float32)
    for i in range(nt):
        for kl in range(blk):
            k1 = blk * i + kl
            ang = 2.0 * np.pi * (((k2[:, None] * l2[None, :] * r) + l2[None, :] * k1) % n) / n
            tr, ti = np.cos(ang), -np.sin(ang)
            tr = tr.reshape(r, r // blk, blk)
            ti = ti.reshape(r, r // blk, blk)
            out[i, 0, :, kl, kl, :, 0, :] = tr
            out[i, 0, :, kl, kl, :, 1, :] = -ti
            out[i, 1, :, kl, kl, :, 0, :] = ti
            out[i, 1, :, kl, kl, :, 1, :] = tr
    return out.reshape(nt, 2 * r * blk, blk * 2 * r)


def _fft2_body(gc_ref, m2_ref, cw_ref, sw_ref, g_ref, o_ref):
    kb, rr, df = gc_ref.shape
    gc = gc_ref[...].reshape(kb * rr, df)
    x = _dot(m2_ref[...], gc)
    half = x.shape[0] // 2
    xr = x[:half].astype(_BF)
    xi = x[half:].astype(_BF)
    ng = cw_ref.shape[0]
    fg = df // ng
    parts = []
    for g in range(ng):
        sl = slice(g * fg, (g + 1) * fg)
        parts.append(_dot(xr[:, sl], cw_ref[g]) + _dot(xi[:, sl], sw_ref[g]))
    f = jnp.concatenate(parts, axis=-1)
    fn = _rms(f, g_ref[...])
    o_ref[...] = fn.reshape(o_ref.shape)


def _fft2(gc, m2, cw, sw, gf):
    b, r, rr, df = gc.shape
    blk = FFT_BLK
    nt = r // blk
    out = pl.pallas_call(
        _fft2_body,
        grid=(b, nt),
        in_specs=[pl.BlockSpec((None, blk, rr, df), lambda i, j: (i, j, 0, 0)),
                  pl.BlockSpec((None,) + m2.shape[1:], lambda i, j: (j, 0, 0)),
                  pl.BlockSpec(cw.shape, lambda i, j: (0, 0, 0)),
                  pl.BlockSpec(sw.shape, lambda i, j: (0, 0, 0)),
                  pl.BlockSpec((1, df), lambda i, j: (0, 0))],
        out_specs=pl.BlockSpec((None, r, blk, df), lambda i, j: (i, 0, j, 0)),
        out_shape=jax.ShapeDtypeStruct((b, r, r, df), _F32),
        compiler_params=_cparams(("parallel", "arbitrary")),
        name="fft2",
    )(gc, m2, cw, sw, gf)
    return out.reshape(b, r * r, df)


def _s5_body(uc_ref, ul_ref, ws_ref, wy_ref, at_ref, y_ref,
             s_re, s_im, ha_re, hb_re, ha_im, hb_im, *, nb):
    p = S5_STATE
    kk = ws_ref.shape[0]
    n_ctx, n_lat = uc_ref.shape[0], ul_ref.shape[0]
    ws = ws_ref[...]
    for b in range(nb):
        sl = slice(b * kk, (b + 1) * kk)
        sc = _dot(uc_ref[:, sl], ws)
        sl_ = _dot(ul_ref[:, sl], ws)
        s_re[pl.ds(b, n_ctx, stride=nb), :] = sc[:, :2 * p]
        s_im[pl.ds(b, n_ctx, stride=nb), :] = sc[:, 2 * p:]
        s_re[pl.ds(n_ctx * nb + b, n_lat, stride=nb), :] = sl_[:, :2 * p]
        s_im[pl.ds(n_ctx * nb + b, n_lat, stride=nb), :] = sl_[:, 2 * p:]

    a_re = at_ref[0:1, :]
    a_im = at_ref[1:2, :]
    is_f = lax.broadcasted_iota(jnp.int32, (nb, 2 * p), 1) < p
    n_all = n_ctx + n_lat

    def load(cf, cr):
        rf = pl.multiple_of(cf * nb, nb)
        rr = pl.multiple_of(cr * nb, nb)
        return (jnp.where(is_f, s_re[pl.ds(rf, nb), :], s_re[pl.ds(rr, nb), :]),
                jnp.where(is_f, s_im[pl.ds(rf, nb), :], s_im[pl.ds(rr, nb), :]))

    def advance(h_re, h_im, x_re, x_im):
        return (a_re * h_re - a_im * h_im + x_re, a_re * h_im + a_im * h_re + x_im)

    def ctx_step(step, carry):
        x_re, x_im = load(step, n_ctx - 1 - step)
        return advance(*carry, x_re, x_im)

    def lat_step(step, carry):
        h_re, h_im = carry
        rf = pl.multiple_of(step * nb, nb)
        rr = pl.multiple_of((n_lat - 1 - step) * nb, nb)
        ha_re[pl.ds(rf, nb), :] = h_re
        hb_re[pl.ds(rr, nb), :] = h_re
        ha_im[pl.ds(rf, nb), :] = h_im
        hb_im[pl.ds(rr, nb), :] = h_im
        x_re, x_im = load(n_ctx + step, n_all - 1 - step)
        return advance(h_re, h_im, x_re, x_im)

    zero = jnp.zeros((nb, 2 * p), _F32)
    carry = lax.fori_loop(0, n_ctx, ctx_step, (zero, zero))
    lax.fori_loop(0, n_lat, lat_step, carry)

    wy = wy_ref[...]
    for b in range(nb):
        rows = pl.ds(b, n_lat, stride=nb)
        hin = jnp.concatenate([ha_re[rows, :], hb_re[rows, :], ha_im[rows, :], hb_im[rows, :]], axis=-1)
        lhs = jnp.concatenate([ul_ref[:, b * kk:(b + 1) * kk], hin.astype(_BF)], axis=-1)
        y_ref[:, b * kk:(b + 1) * kk] = _dot(lhs, wy)


def _s5(u_ctx, u_lat, ws, wy, at, nb):
    ng, n_ctx, w = u_ctx.shape
    n_lat = u_lat.shape[1]
    body = functools.partial(_s5_body, nb=nb)
    hs = pltpu.VMEM((n_lat * nb, 2 * S5_STATE), _F32)
    ss = pltpu.VMEM(((n_ctx + n_lat) * nb, 2 * S5_STATE), _F32)
    return pl.pallas_call(
        body,
        grid=(ng,),
        in_specs=[pl.BlockSpec((None, n_ctx, w), lambda g: (g, 0, 0)),
                  pl.BlockSpec((None, n_lat, w), lambda g: (g, 0, 0)),
                  pl.BlockSpec((None,) + ws.shape[1:], lambda g: (g, 0, 0)),
                  pl.BlockSpec((None,) + wy.shape[1:], lambda g: (g, 0, 0)),
                  pl.BlockSpec((None,) + at.shape[1:], lambda g: (g, 0, 0))],
        out_specs=pl.BlockSpec((None, n_lat, w), lambda g: (g, 0, 0)),
        out_shape=jax.ShapeDtypeStruct((ng, n_lat, w), _F32),
        scratch_shapes=[ss, ss, hs, hs, hs, hs],
        compiler_params=_cparams(("parallel",)),
        name="s5_scan",
    )(u_ctx, u_lat, ws, wy, at)


def _s5_param_body(lr_ref, li_ref, ls_ref, btr_ref, bti_ref, cr_ref, ci_ref, ws_ref, wy_ref, at_ref):
    t, hh, p = S5_CHUNK, S5_GROUP, S5_STATE
    kk = t * hh
    lr, li = lr_ref[...], li_ref[...]
    dt = jnp.exp(ls_ref[...])
    ldt, idt = lr * dt, li * dt
    mag = jnp.exp(ldt)
    ab_re, ab_im = mag * jnp.cos(idt), mag * jnp.sin(idt)
    den = lr * lr + li * li
    nr, ni = ab_re - 1.0, ab_im
    q_re = (nr * lr + ni * li) / den
    q_im = (ni * lr - nr * li) / den
    kf = lax.broadcasted_iota(jnp.int32, (2 * t, p), 0).astype(_F32)

    ap_re, ap_im, bb_re, bb_im = [], [], [], []
    for d in range(2):
        pm = jnp.exp(kf * ldt[d:d + 1, :])
        ap_re.append(pm * jnp.cos(kf * idt[d:d + 1, :]))
        ap_im.append(pm * jnp.sin(kf * idt[d:d + 1, :]))
        bb_re.append(q_re[d:d + 1, :] * btr_ref[d] - q_im[d:d + 1, :] * bti_ref[d])
        bb_im.append(q_re[d:d + 1, :] * bti_ref[d] + q_im[d:d + 1, :] * btr_ref[d])

    def rep(a, ks):
        return jnp.concatenate([jnp.broadcast_to(a[k:k + 1, :], (hh, p)) for k in ks], axis=0)

    def tile(a, n):
        return jnp.concatenate([a] * n, axis=0)

    def cmul(xr, xi, yr, yi):
        return xr * yr - xi * yi, xr * yi + xi * yr

    ks_f = list(range(t + 1))
    ks_r = list(range(t, -1, -1))
    caf_re, caf_im = cmul(tile(cr_ref[0], t + 1), tile(ci_ref[0], t + 1), rep(ap_re[0], ks_f), rep(ap_im[0], ks_f))
    car_re, car_im = cmul(tile(cr_ref[1], t + 1), tile(ci_ref[1], t + 1), rep(ap_re[1], ks_r), rep(ap_im[1], ks_r))

    def lag_blocks(bre, bim, ca_re, ca_im):
        lhs = jnp.concatenate([bre, bim], axis=1)
        rhs = jnp.concatenate([ca_re, -ca_im], axis=1)
        return lax.dot_general(lhs, rhs, (((1,), (1,)), ((), ())), precision=_HI,
                               preferred_element_type=_F32)

    w_f = lag_blocks(bb_re[0], bb_im[0], caf_re[:kk], caf_im[:kk])
    w_r = lag_blocks(bb_re[1], bb_im[1], car_re[hh:], car_im[hh:])
    lane = lax.broadcasted_iota(jnp.int32, (hh, kk), 1)
    for tlp in range(t):
        sf = hh * tlp
        sr = hh * (t - 1 - tlp)
        a = w_f if sf == 0 else jnp.where(lane >= sf, pltpu.roll(w_f, sf, 1), 0.0)
        b = w_r if sr == 0 else jnp.where(lane < kk - sr, pltpu.roll(w_r, kk - sr, 1), 0.0)
        wy_ref[hh * tlp:hh * (tlp + 1), :] = (a + b).astype(wy_ref.dtype)

    zeros = jnp.zeros((p, kk), _F32)
    blocks = [caf_re[hh:].T, zeros, zeros, car_re[:kk].T, (-caf_im[hh:]).T, zeros, zeros, (-car_im[:kk]).T]
    for i, blk in enumerate(blocks):
        wy_ref[kk + p * i:kk + p * (i + 1), :] = blk.astype(wy_ref.dtype)

    pf = list(range(t - 1, -1, -1))
    pr = list(range(t))
    f_re, f_im = cmul(rep(ap_re[0], pf), rep(ap_im[0], pf), tile(bb_re[0], t), tile(bb_im[0], t))
    r_re, r_im = cmul(rep(ap_re[1], pr), rep(ap_im[1], pr), tile(bb_re[1], t), tile(bb_im[1], t))
    ws_ref[...] = jnp.concatenate([f_re, r_re, f_im, r_im], axis=1).astype(ws_ref.dtype)

    a_re = jnp.concatenate([ap_re[0][t:t + 1, :], ap_re[1][t:t + 1, :]], axis=1)
    a_im = jnp.concatenate([ap_im[0][t:t + 1, :], ap_im[1][t:t + 1, :]], axis=1)
    row = lax.broadcasted_iota(jnp.int32, at_ref.shape, 0)
    at_ref[...] = jnp.where(row == 0, a_re, jnp.where(row == 1, a_im, 0.0))


def _s5_params(lam_re, lam_im, b_re, b_im, c_re, c_im, log_step):
    _, ng, p = lam_re.shape
    hh = b_re.shape[-1]
    kk = S5_CHUNK * hh
    gd = lambda a: jnp.swapaxes(a.astype(_F32), 0, 1)
    args = (gd(lam_re), gd(lam_im), gd(log_step)[..., None],
            jnp.swapaxes(gd(b_re), 2, 3), jnp.swapaxes(gd(b_im), 2, 3), gd(c_re), gd(c_im))
    spec = lambda a: pl.BlockSpec((None,) + a.shape[1:], lambda g: (g,) + (0,) * (a.ndim - 1))
    return pl.pallas_call(
        _s5_param_body,
        grid=(ng,),
        in_specs=[spec(a) for a in args],
        out_specs=[pl.BlockSpec((None, kk, 4 * p), lambda g: (g, 0, 0)),
                   pl.BlockSpec((None, kk + 8 * p, kk), lambda g: (g, 0, 0)),
                   pl.BlockSpec((None, 8, 2 * p), lambda g: (g, 0, 0))],
        out_shape=[jax.ShapeDtypeStruct((ng, kk, 4 * p), _BF),
                   jax.ShapeDtypeStruct((ng, kk + 8 * p, kk), _BF),
                   jax.ShapeDtypeStruct((ng, 8, 2 * p), _F32)],
        compiler_params=_cparams(("parallel",)),
        name="s5_params",
    )(*args)


def _post_body(x_ref, u_ref, yg_ref, fn_ref, d_ref, wglu_ref, gs_ref, wo_ref, g1_ref,
               n2_ref, sh_ref, sc_ref, wr_ref, br_ref, tri_ref, x1_ref, t_ref, rt_ref, rtt_ref, cnt_ref,
               y_scr, cnt_scr):
    ds5 = u_ref.shape[-1]
    first = (pl.program_id(0) == 0) & (pl.program_id(1) == 0)

    @pl.when(first)
    def _():
        cnt_scr[...] = jnp.zeros_like(cnt_scr)

    t_chunk = S5_CHUNK
    n_chunk = yg_ref.shape[1]
    lanes = 128
    gpb = lanes // S5_GROUP
    for j in range(ds5 // lanes):
        for hi in range(t_chunk // gpb):
            outs = _seg_transpose([yg_ref[gpb * j + glo, :, lanes * hi:lanes * (hi + 1)] for glo in range(gpb)])
            for tlo in range(gpb):
                y_scr[j, pl.ds(gpb * hi + tlo, n_chunk, stride=t_chunk), :] = outs[tlo]
    y_s5 = jnp.concatenate([y_scr[j] for j in range(ds5 // lanes)], axis=-1)

    y = y_s5 + d_ref[...] * u_ref[...]
    y = jax.nn.gelu(y, approximate=True)
    y = y * jax.nn.sigmoid(_dot(y.astype(_BF), wglu_ref[...]))
    yn = _rms(y, gs_ref[...]).astype(_BF)
    mix = _dot(yn, wo_ref[0:ds5, :]) + _dot(fn_ref[...].astype(_BF), wo_ref[ds5:, :])
    x1 = x_ref[...] + g1_ref[...] * mix
    x1_ref[...] = x1
    t = _rms(x1, n2_ref[...]) * (1.0 + sc_ref[...]) + sh_ref[...]
    t_hi = t.astype(_BF)
    t_ref[...] = t_hi

    ng, epg = N_EXPERT_GROUPS, EXPERTS_PER_GROUP
    t_lo = (t - t_hi.astype(_F32)).astype(_BF)
    logits = (_dot(t_hi, wr_ref[0]) + _dot(t_lo, wr_ref[0]) + _dot(t_hi, wr_ref[1])) + br_ref[...]
    lane = lax.broadcasted_iota(jnp.int32, logits.shape, 1)
    neg = jnp.float32(-jnp.inf)
    big = jnp.int32(1 << 20)
    gl = jnp.where(lane < ng, logits, neg)
    gmax = jnp.max(gl, axis=-1, keepdims=True)
    gidx = jnp.min(jnp.where(gl == gmax, lane, big), axis=-1, keepdims=True)
    gw = 1.0 / jnp.sum(jnp.where(lane < ng, jnp.exp(logits - gmax), 0.0), axis=-1, keepdims=True)
    lo = ng + gidx * epg
    el = jnp.where((lane >= lo) & (lane < lo + epg), logits, neg)
    v0 = jnp.max(el, axis=-1, keepdims=True)
    i0 = jnp.min(jnp.where(el == v0, lane, big), axis=-1, keepdims=True)
    el1 = jnp.where(lane == i0, neg, el)
    v1 = jnp.max(el1, axis=-1, keepdims=True)
    i1 = jnp.min(jnp.where(el1 == v1, lane, big), axis=-1, keepdims=True)
    p0 = 1.0 / (1.0 + jnp.exp(v1 - v0))
    w0 = gw * p0
    w1 = gw * (1.0 - p0)
    e0 = i0 - ng
    e1 = i1 - ng

    oh0 = lane == e0
    oh1 = lane == e1
    oh = (oh0 | oh1).astype(_F32)
    prefix = _dot(tri_ref[...], oh.astype(_BF)) + cnt_scr[...]
    r0 = jnp.sum(jnp.where(oh0, prefix, 0.0), axis=-1, keepdims=True)
    r1 = jnp.sum(jnp.where(oh1, prefix, 0.0), axis=-1, keepdims=True)
    cnt = cnt_scr[...] + jnp.sum(oh, axis=0, keepdims=True)
    cnt_scr[...] = cnt
    cnt_ref[...] = cnt

    vals = (e0.astype(_F32), e1.astype(_F32), w0, w1, r0, r1)
    rt = jnp.zeros(logits.shape, _F32)
    for k, v in enumerate(vals):
        rt = jnp.where(lane == k, v, rt)
    rt_ref[...] = rt
    rtt_ref[...] = rt.T[0:rtt_ref.shape[0], :]


@functools.lru_cache(maxsize=None)
def _strict_lower_ones(n):
    return np.tril(np.ones((n, n), np.float32), -1).astype(jnp.bfloat16)


def _post(x, u, y_g, fn, s5_d, wglu_bf, gs, wo_bf, g1, n2, sh2, sc2, wr, br, tm=512):
    b, l, d = x.shape
    ds5 = u.shape[-1]
    n_grp, _, w = y_g.shape
    kk = w // b
    tri = jnp.asarray(_strict_lower_ones(tm))
    tok = lambda w: pl.BlockSpec((None, tm, w), lambda i, j: (i, j, 0))
    per_b = pl.BlockSpec((None, 1, d), lambda i, j: (i, 0, 0))
    full = lambda a: pl.BlockSpec(a.shape, lambda i, j: (0,) * a.ndim)
    ygs = pl.BlockSpec((n_grp, tm // S5_CHUNK, kk), lambda i, j: (0, j, i))
    return pl.pallas_call(
        _post_body,
        grid=(b, l // tm),
        in_specs=[tok(d), tok(ds5), ygs, tok(fn.shape[-1]), full(s5_d), full(wglu_bf),
                  full(gs), full(wo_bf), per_b, full(n2), per_b, per_b, full(wr), full(br), full(tri)],
        out_specs=[tok(d), tok(d), tok(ROUTE_LANES),
                   pl.BlockSpec((None, ROUTE_ROWS, tm), lambda i, j: (i, 0, j)),
                   pl.BlockSpec((1, ROUTE_LANES), lambda i, j: (0, 0))],
        out_shape=[jax.ShapeDtypeStruct((b, l, d), _F32),
                   jax.ShapeDtypeStruct((b, l, d), _BF),
                   jax.ShapeDtypeStruct((b, l, ROUTE_LANES), _F32),
                   jax.ShapeDtypeStruct((b, ROUTE_ROWS, l), _F32),
                   jax.ShapeDtypeStruct((1, ROUTE_LANES), _F32)],
        scratch_shapes=[pltpu.VMEM((ds5 // 128, tm, 128), _F32),
                        pltpu.VMEM((1, ROUTE_LANES), _F32)],
        compiler_params=_cparams(("arbitrary", "arbitrary")),
        name="post_mixer",
    )(x, u, y_g, fn, s5_d, wglu_bf, gs, wo_bf, g1, n2, sh2, sc2, wr, br, tri)


def _moe_body(te_ref, xs_ref, wg_ref, wu_ref, wd_ref, o_ref, wg_bf, wu_bf, wd_bf):
    i = pl.program_id(0)
    new_expert = (i == 0) | (te_ref[i] != te_ref[jnp.maximum(i - 1, 0)])

    @pl.when(new_expert)
    def _():
        wg_bf[...] = wg_ref[...].astype(_BF)
        wu_bf[...] = wu_ref[...].astype(_BF)
        wd_bf[...] = wd_ref[...].astype(_BF)

    xs = xs_ref[...]
    hg = _dot(xs, wg_bf[...])
    hu = _dot(xs, wu_bf[...])
    a = hg * jax.nn.sigmoid(hg) * hu
    o_ref[...] = _dot(a.astype(_BF), wd_bf[...]).astype(o_ref.dtype)


def _moe(tile_expert, xs, wg, wu, wd):
    nr, d = xs.shape
    tm = MOE_TM
    de = wg.shape[-1]
    grid_spec = pltpu.PrefetchScalarGridSpec(
        num_scalar_prefetch=1,
        grid=(nr // tm,),
        in_specs=[pl.BlockSpec((tm, d), lambda i, te: (i, 0)),
                  pl.BlockSpec((None, d, de), lambda i, te: (te[i], 0, 0)),
                  pl.BlockSpec((None, d, de), lambda i, te: (te[i], 0, 0)),
                  pl.BlockSpec((None, de, d), lambda i, te: (te[i], 0, 0))],
        out_specs=pl.BlockSpec((tm, d), lambda i, te: (i, 0)),
        scratch_shapes=[pltpu.VMEM((d, de), _BF), pltpu.VMEM((d, de), _BF), pltpu.VMEM((de, d), _BF)],
    )
    return pl.pallas_call(
        _moe_body,
        grid_spec=grid_spec,
        out_shape=jax.ShapeDtypeStruct((nr, d), _BF),
        compiler_params=_cparams(("arbitrary",)),
        name="moe_experts",
    )(tile_expert, xs, wg, wu, wd)


def _final_body(x1_ref, y0_ref, y1_ref, rt_ref, g2_ref, gf_ref, o_ref):
    w0 = rt_ref[:, 2:3]
    w1 = rt_ref[:, 3:4]
    m = w0 * y0_ref[...].astype(_F32) + w1 * y1_ref[...].astype(_F32)
    o_ref[...] = _rms(x1_ref[...] + g2_ref[...] * m, gf_ref[...])


def _final(x1, y0, y1, rt, g2, gf, tm=512):
    b, l, d = x1.shape
    tok = pl.BlockSpec((None, tm, d), lambda i, j: (i, j, 0))
    return pl.pallas_call(
        _final_body,
        grid=(b, l // tm),
        in_specs=[tok, tok, tok, pl.BlockSpec((None, tm, ROUTE_LANES), lambda i, j: (i, j, 0)),
                  pl.BlockSpec((None, 1, d), lambda i, j: (i, 0, 0)),
                  pl.BlockSpec((1, d), lambda i, j: (0, 0))],
        out_specs=tok,
        out_shape=jax.ShapeDtypeStruct((b, l, d), _F32),
        compiler_params=_cparams(("parallel", "arbitrary")),
        name="final_norm",
    )(x1, y0, y1, rt, g2, gf)


def _route_plan(eid, rank, tok_ids, counts, tm, nr):
    n_experts = counts.shape[0]
    n_pairs = eid.size
    padded = ((counts + tm - 1) // tm) * tm
    pad_end = jnp.cumsum(padded)
    pad_start = pad_end - padded
    raw_start = jnp.cumsum(counts) - counts
    pos = rank
    for e in range(n_experts):
        pos = pos + jnp.where(eid == e, pad_start[e], 0)
    tile_start = jnp.arange(nr // tm, dtype=jnp.int32) * tm
    tile_expert = jnp.sum(tile_start[:, None] >= pad_end[None, :], axis=-1)
    tile_expert = jnp.minimum(tile_expert, n_experts - 1).astype(jnp.int32)
    _, sorted_tok = lax.sort_key_val((eid * n_pairs + rank).reshape(-1), tok_ids.reshape(-1))
    sel = tile_expert[:, None] == jnp.arange(n_experts, dtype=jnp.int32)[None, :]
    per_tile = lambda v: jnp.repeat(jnp.sum(jnp.where(sel, v[None, :], 0), axis=-1), tm)
    off = jnp.arange(nr, dtype=jnp.int32) - per_tile(pad_start)
    valid = off < per_tile(counts)
    j = jnp.clip(per_tile(raw_start) + off, 0, n_pairs - 1)
    filler = jnp.arange(nr, dtype=jnp.int32) % (n_pairs // 2)
    row_token = jnp.where(valid, sorted_tok.at[j].get(mode="promise_in_bounds"), filler)
    return pos, tile_expert, row_token


def kernel(x, c, ctx, c_ctx, w_ada, b_ada, norm1_g, norm2_g, w_in, s5_lam_re, s5_lam_im, s5_b_re, s5_b_im, s5_c_re, s5_c_im, s5_log_step, s5_d, s5_w_glu, fourier_w, mix_norm_s5_g, mix_norm_f_g, w_out, moe_w_group, moe_b_group, moe_w_router, moe_b_router, moe_w_gate, moe_w_up, moe_w_down, final_norm_g):
    b, l, d = x.shape
    lc = ctx.shape[1]
    depth = w_ada.shape[0]
    assert depth == 1 and l == FFT_R * FFT_R and lc % S5_CHUNK == 0 and b % 8 == 0
    ds5 = s5_d.shape[-1]
    df = w_in.shape[-1] - ds5
    n_exp = moe_w_gate.shape[1]
    row = lambda a: a.reshape(1, -1)

    cond = jnp.concatenate([c, c_ctx[None, :], jnp.zeros((7, d), _F32)], axis=0)
    mod = _adaln(cond, w_ada[0], b_ada[0])
    sh1, sc1, g1, sh2, sc2, g2 = [mod[:b, i * d:(i + 1) * d].reshape(b, 1, d) for i in range(6)]
    csh1 = mod[b:b + 1, 0:d]
    csc1 = mod[b:b + 1, d:2 * d]

    w_in_bf = w_in[0].astype(_BF)
    m1 = jnp.asarray(_fft_stage1_matrix()).astype(_BF)
    m2 = jnp.asarray(_fft_stage2_matrices()).astype(_BF)
    perm = jnp.asarray(_chunk_row_perm())
    zs, gc = _inproj(x, sh1, sc1, row(norm1_g[0]), w_in_bf, m1)
    u_ctx = _inproj_ctx(ctx, csh1, csc1, row(norm1_g[0]), w_in_bf[:, :ds5], perm)
    u_lat = _s5_pack(zs, perm)

    fg = df // FOURIER_GROUPS
    cc = np.arange(fg)
    ang = 2.0 * np.pi * ((cc[:, None] * cc[None, :]) % fg) / fg
    scale = 1.0 / math.sqrt(l * fg)
    cw = jnp.einsum("cm,gmd->gcd", jnp.asarray(np.cos(ang) * scale, _F32), fourier_w[0], precision=_HI)
    sw = jnp.einsum("cm,gmd->gcd", jnp.asarray(np.sin(ang) * scale, _F32), fourier_w[0], precision=_HI)
    fn = _fft2(gc, m2, cw.astype(_BF), sw.astype(_BF), row(mix_norm_f_g[0]))

    ws, wy, at = _s5_params(s5_lam_re[0], s5_lam_im[0], s5_b_re[0], s5_b_im[0],
                            s5_c_re[0], s5_c_im[0], s5_log_step[0])
    y_g = _s5(u_ctx, u_lat, ws, wy, at, b)

    n_rt = N_EXPERT_GROUPS * (1 + EXPERTS_PER_GROUP)
    w_rt = jnp.concatenate([moe_w_group[0], moe_w_router[0].reshape(d, -1)], axis=-1)
    w_rt = jnp.pad(w_rt, ((0, 0), (0, ROUTE_LANES - n_rt)))
    w_rt_hi = w_rt.astype(_BF)
    w_rt = jnp.stack([w_rt_hi, (w_rt - w_rt_hi.astype(_F32)).astype(_BF)])
    b_rt =jnp.pad(jnp.concatenate([moe_b_group[0], moe_b_router[0].reshape(-1)]), (0, ROUTE_LANES - n_rt))
    x1, tmod, rt, rtt, cnt = _post(x, zs, y_g, fn, row(s5_d[0]), s5_w_glu[0].astype(_BF),
                                   row(mix_norm_s5_g[0]), w_out[0].astype(_BF), g1, row(norm2_g[0]),
                                   sh2, sc2, w_rt, row(b_rt))

    tm = MOE_TM
    n_tok = b * l
    nr = 2 * n_tok + n_exp * tm
    rec = rtt.transpose(1, 0, 2).reshape(ROUTE_ROWS, n_tok)
    eid = rec[0:2].astype(jnp.int32)
    rank = rec[4:6].astype(jnp.int32)
    tok_ids = jnp.broadcast_to(jnp.arange(n_tok, dtype=jnp.int32), (2, n_tok))
    pos, tile_expert, row_token = _route_plan(eid, rank, tok_ids, cnt[0, :n_exp].astype(jnp.int32), tm, nr)
    take_rows = lambda a, idx: a.at[idx].get(mode="promise_in_bounds")
    xs = take_rows(tmod.reshape(n_tok, d), row_token)
    ys = _moe(tile_expert, xs, moe_w_gate[0], moe_w_up[0], moe_w_down[0])
    y0 = take_rows(ys, pos[0]).reshape(b, l, d)
    y1 = take_rows(ys, pos[1]).reshape(b, l, d)
    return _final(x1, y0, y1, rt, g2, row(final_norm_g))
```

```python
import functools
import math

import numpy as np
import jax
import jax.numpy as jnp
from jax import lax
from jax.experimental import pallas as pl
from jax.experimental.pallas import tpu as pltpu

EPS = 1e-6
S5_GROUP = 16
S5_STATE = 64
S5_CHUNK = 16
FOURIER_GROUPS = 4
N_EXPERT_GROUPS = 4
EXPERTS_PER_GROUP = 8
FFT_R = 64
FFT_BLK = 8
MOE_TM = 512
MOE_PARTS = 2
ROUTE_LANES = 128
ROUTE_ROWS = 8
VMEM_LIMIT = 56 * 1024 * 1024

_HI = lax.Precision.HIGHEST
_BF = jnp.bfloat16
_F32 = jnp.float32


def _cparams(sem):
    return pltpu.CompilerParams(dimension_semantics=sem, vmem_limit_bytes=VMEM_LIMIT)


def _dot(a, b):
    return jnp.dot(a, b, preferred_element_type=_F32)


def _rms(x, g):
    return x * lax.rsqrt(jnp.mean(x * x, axis=-1, keepdims=True) + EPS) * g


def _adaln_body(c_ref, w_ref, b_ref, o_ref):
    c = c_ref[...]
    a = c * jax.nn.sigmoid(c)
    o_ref[...] = jnp.dot(a, w_ref[...], preferred_element_type=_F32, precision=_HI) + b_ref[...]


def _adaln(cond, w, b):
    m, d = cond.shape
    n = w.shape[1]
    tn = 768
    return pl.pallas_call(
        _adaln_body,
        grid=(n // tn,),
        in_specs=[pl.BlockSpec((m, d), lambda j: (0, 0)),
                  pl.BlockSpec((d, tn), lambda j: (0, j)),
                  pl.BlockSpec((1, tn), lambda j: (0, j))],
        out_specs=pl.BlockSpec((m, tn), lambda j: (0, j)),
        out_shape=jax.ShapeDtypeStruct((m, n), _F32),
        compiler_params=_cparams(("arbitrary",)),
        name="adaln",
    )(cond, w, b.reshape(1, n))


def _inproj_body(x_ref, sh_ref, sc_ref, g_ref, w_ref, m1_ref, zs_ref, gc_ref):
    r, blk, d = x_ref.shape
    x = x_ref[...].reshape(r * blk, d)
    h = _rms(x, g_ref[...]) * (1.0 + sc_ref[...]) + sh_ref[...]
    z = _dot(h.astype(_BF), w_ref[...])
    ds5 = zs_ref.shape[-1]
    zs_ref[...] = z[:, :ds5].reshape(r, blk, ds5)
    v = z[:, ds5:].astype(_BF)
    g1 = _dot(m1_ref[...], v)
    gc_ref[...] = g1.astype(_BF).reshape(r, 2 * blk, v.shape[-1])


def _inproj(x, sh, sc, g, w_bf, m1):
    b, l, d = x.shape
    r, blk = FFT_R, FFT_BLK
    nj = r // blk
    dmix = w_bf.shape[1]
    ds5 = dmix // 2
    df = dmix - ds5
    x4 = x.reshape(b, r, r, d)
    zs, gc = pl.pallas_call(
        _inproj_body,
        grid=(b, nj),
        in_specs=[pl.BlockSpec((None, r, blk, d), lambda i, j: (i, 0, j, 0)),
                  pl.BlockSpec((None, 1, d), lambda i, j: (i, 0, 0)),
                  pl.BlockSpec((None, 1, d), lambda i, j: (i, 0, 0)),
                  pl.BlockSpec((1, d), lambda i, j: (0, 0)),
                  pl.BlockSpec((d, dmix), lambda i, j: (0, 0)),
                  pl.BlockSpec(m1.shape, lambda i, j: (0, 0))],
        out_specs=[pl.BlockSpec((None, r, blk, ds5), lambda i, j: (i, 0, j, 0)),
                   pl.BlockSpec((None, r, 2 * blk, df), lambda i, j: (i, 0, j, 0))],
        out_shape=[jax.ShapeDtypeStruct((b, r, r, ds5), _F32),
                   jax.ShapeDtypeStruct((b, r, 2 * r, df), _BF)],
        compiler_params=_cparams(("parallel", "arbitrary")),
        name="inproj",
    )(x4, sh, sc, g, w_bf, m1)
    return zs.reshape(b, l, ds5), gc


def _seg_transpose(arrs):
    n = len(arrs)
    seg = lax.broadcasted_iota(jnp.int32, arrs[0].shape, 1) // S5_GROUP
    d = n // 2
    while d >= 1:
        keep = (seg & d) == 0
        new = list(arrs)
        for i in range(n):
            if i & d == 0:
                a, b = arrs[i], arrs[i + d]
                new[i] = jnp.where(keep, a, pltpu.roll(b, S5_GROUP * d, 1))
                new[i + d] = jnp.where(keep, pltpu.roll(a, 128 - S5_GROUP * d, 1), b)
        arrs = new
        d //= 2
    return arrs


@functools.lru_cache(maxsize=None)
def _chunk_row_perm():
    t = S5_CHUNK
    p = np.zeros((t * t, t * t), np.float32)
    for c in range(t):
        for tl in range(t):
            p[tl * t + c, c * t + tl] = 1.0
    return p.astype(jnp.bfloat16)


def _pack_chunks(z_bf, p_ref, u_ref):
    t = S5_CHUNK
    nsub = z_bf.shape[0] // (t * t)
    pieces = [_dot(p_ref[...], z_bf[s * t * t:(s + 1) * t * t, :]) for s in range(nsub)]
    a = []
    for tl in range(t):
        rows = [pc[tl * t:(tl + 1) * t, :] for pc in pieces]
        a.append(rows[0] if nsub == 1 else jnp.concatenate(rows, axis=0))
    lanes = 128
    gpb = lanes // S5_GROUP
    for j in range(z_bf.shape[1] // lanes):
        for hi in range(t // gpb):
            outs = _seg_transpose([a[gpb * hi + tlo][:, lanes * j:lanes * (j + 1)] for tlo in range(gpb)])
            for glo in range(gpb):
                u_ref[gpb * j + glo, :, lanes * hi:lanes * (hi + 1)] = outs[glo].astype(u_ref.dtype)


def _inproj_ctx_body(x_ref, sh_ref, sc_ref, g_ref, w_ref, p_ref, u_ref):
    h = _rms(x_ref[...], g_ref[...]) * (1.0 + sc_ref[...]) + sh_ref[...]
    z = _dot(h.astype(_BF), w_ref[...])
    _pack_chunks(z.astype(_BF), p_ref, u_ref)


def _inproj_ctx(ctx, sh, sc, g, w_s5_bf, perm):
    b, lc, d = ctx.shape
    ds5 = w_s5_bf.shape[1]
    n_grp = ds5 // S5_GROUP
    kk = S5_CHUNK * S5_GROUP
    return pl.pallas_call(
        _inproj_ctx_body,
        grid=(b,),
        in_specs=[pl.BlockSpec((None, lc, d), lambda i: (i, 0, 0)),
                  pl.BlockSpec((1, d), lambda i: (0, 0)),
                  pl.BlockSpec((1, d), lambda i: (0, 0)),
                  pl.BlockSpec((1, d), lambda i: (0, 0)),
                  pl.BlockSpec((d, ds5), lambda i: (0, 0)),
                  pl.BlockSpec(perm.shape, lambda i: (0, 0))],
        out_specs=pl.BlockSpec((n_grp, lc // S5_CHUNK, kk), lambda i: (0, 0, i)),
        out_shape=jax.ShapeDtypeStruct((n_grp, lc // S5_CHUNK, b * kk), _BF),
        compiler_params=_cparams(("arbitrary",)),
        name="inproj_ctx",
    )(ctx, sh, sc, g, w_s5_bf, perm)


def _s5_pack_body(z_ref, p_ref, u_ref):
    _pack_chunks(z_ref[...].astype(_BF), p_ref, u_ref)


def _s5_pack(zs, perm, tok=2048):
    b, l, ds5 = zs.shape
    n_grp = ds5 // S5_GROUP
    kk = S5_CHUNK * S5_GROUP
    return pl.pallas_call(
        _s5_pack_body,
        grid=(b, l // tok),
        in_specs=[pl.BlockSpec((None, tok, ds5), lambda i, j: (i, j, 0)),
                  pl.BlockSpec(perm.shape, lambda i, j: (0, 0))],
        out_specs=pl.BlockSpec((n_grp, tok // S5_CHUNK, kk), lambda i, j: (0, j, i)),
        out_shape=jax.ShapeDtypeStruct((n_grp, l // S5_CHUNK, b * kk), _BF),
        compiler_params=_cparams(("parallel", "arbitrary")),
        name="s5_pack",
    )(zs, perm)


@functools.lru_cache(maxsize=None)
def _fft_stage1_matrix():
    r, blk = FFT_R, FFT_BLK
    k1 = np.arange(r)[:, None]
    l1 = np.arange(r)[None, :]
    ang = 2.0 * np.pi * ((k1 * l1) % r) / r
    f = np.stack([np.cos(ang), -np.sin(ang)], axis=1)
    m = np.einsum("kal,pq->kaplq", f, np.eye(blk))
    return m.reshape(r * 2 * blk, r * blk).astype(np.float32)


@functools.lru_cache(maxsize=None)
def _fft_stage2_matrices():
    r, blk = FFT_R, FFT_BLK
    n = r * r
    nt = r // blk
    l2 = np.arange(r)
    k2 = np.arange(r)
    out = np.zeros((nt, 2, r, blk, blk, r // blk, 2, blk), np.float32)
    for i in range(nt):
        for kl in range(blk):
            k1 = blk * i + kl
            ang = 2.0 * np.pi * (((k2[:, None] * l2[None, :] * r) + l2[None, :] * k1) % n) / n
            tr, ti = np.cos(ang), -np.sin(ang)
            tr = tr.reshape(r, r // blk, blk)
            ti = ti.reshape(r, r // blk, blk)
            out[i, 0, :, kl, kl, :, 0, :] = tr
            out[i, 0, :, kl, kl, :, 1, :] = -ti
            out[i, 1, :, kl, kl, :, 0, :] = ti
            out[i, 1, :, kl, kl, :, 1, :] = tr
    return out.reshape(nt, 2 * r * blk, blk * 2 * r)


def _fft2_body(gc_ref, m2_ref, cw_ref, sw_ref, g_ref, o_ref):
    kb, rr, df = gc_ref.shape
    gc = gc_ref[...].reshape(kb * rr, df)
    x = _dot(m2_ref[...], gc)
    half = x.shape[0] // 2
    xr = x[:half].astype(_BF)
    xi = x[half:].astype(_BF)
    ng = cw_ref.shape[0]
    fg = df // ng
    parts = []
    for g in range(ng):
        sl = slice(g * fg, (g + 1) * fg)
        parts.append(_dot(xr[:, sl], cw_ref[g]) + _dot(xi[:, sl], sw_ref[g]))
    f = jnp.concatenate(parts, axis=-1)
    fn = _rms(f, g_ref[...])
    o_ref[...] = fn.reshape(o_ref.shape)


def _fft2(gc, m2, cw, sw, gf):
    b, r, rr, df = gc.shape
    blk = FFT_BLK
    nt = r // blk
    out = pl.pallas_call(
        _fft2_body,
        grid=(b, nt),
        in_specs=[pl.BlockSpec((None, blk, rr, df), lambda i, j: (i, j, 0, 0)),
                  pl.BlockSpec((None,) + m2.shape[1:], lambda i, j: (j, 0, 0)),
                  pl.BlockSpec(cw.shape, lambda i, j: (0, 0, 0)),
                  pl.BlockSpec(sw.shape, lambda i, j: (0, 0, 0)),
                  pl.BlockSpec((1, df), lambda i, j: (0, 0))],
        out_specs=pl.BlockSpec((None, r, blk, df), lambda i, j: (i, 0, j, 0)),
        out_shape=jax.ShapeDtypeStruct((b, r, r, df), _F32),
        compiler_params=_cparams(("parallel", "arbitrary")),
        name="fft2",
    )(gc, m2, cw, sw, gf)
    return out.reshape(b, r * r, df)


def _s5_body(uc_ref, ul_ref, ws_ref, wy_ref, at_ref, y_ref,
             s_re, s_im, ha_re, hb_re, ha_im, hb_im, *, nb):
    p = S5_STATE
    kk = ws_ref.shape[0]
    n_ctx, n_lat = uc_ref.shape[0], ul_ref.shape[0]
    ws = ws_ref[...]
    for b in range(nb):
        sl = slice(b * kk, (b + 1) * kk)
        sc = _dot(uc_ref[:, sl], ws)
        sl_ = _dot(ul_ref[:, sl], ws)
        s_re[pl.ds(b, n_ctx, stride=nb), :] = sc[:, :2 * p]
        s_im[pl.ds(b, n_ctx, stride=nb), :] = sc[:, 2 * p:]
        s_re[pl.ds(n_ctx * nb + b, n_lat, stride=nb), :] = sl_[:, :2 * p]
        s_im[pl.ds(n_ctx * nb + b, n_lat, stride=nb), :] = sl_[:, 2 * p:]

    a_re = at_ref[0:1, :]
    a_im = at_ref[1:2, :]
    is_f = lax.broadcasted_iota(jnp.int32, (nb, 2 * p), 1) < p
    n_all = n_ctx + n_lat

    def load(cf, cr):
        rf = pl.multiple_of(cf * nb, nb)
        rr = pl.multiple_of(cr * nb, nb)
        return (jnp.where(is_f, s_re[pl.ds(rf, nb), :], s_re[pl.ds(rr, nb), :]),
                jnp.where(is_f, s_im[pl.ds(rf, nb), :], s_im[pl.ds(rr, nb), :]))

    def advance(h_re, h_im, x_re, x_im):
        return (a_re * h_re - a_im * h_im + x_re, a_re * h_im + a_im * h_re + x_im)

    def ctx_step(step, carry):
        x_re, x_im = load(step, n_ctx - 1 - step)
        return advance(*carry, x_re, x_im)

    def lat_step(step, carry):
        h_re, h_im = carry
        rf = pl.multiple_of(step * nb, nb)
        rr = pl.multiple_of((n_lat - 1 - step) * nb, nb)
        ha_re[pl.ds(rf, nb), :] = h_re
        hb_re[pl.ds(rr, nb), :] = h_re
        ha_im[pl.ds(rf, nb), :] = h_im
        hb_im[pl.ds(rr, nb), :] = h_im
        x_re, x_im = load(n_ctx + step, n_all - 1 - step)
        return advance(h_re, h_im, x_re, x_im)

    zero = jnp.zeros((nb, 2 * p), _F32)
    carry = lax.fori_loop(0, n_ctx, ctx_step, (zero, zero), unroll=4)
    lax.fori_loop(0, n_lat, lat_step, carry, unroll=4)

    wy = wy_ref[...]
    for b in range(nb):
        rows = pl.ds(b, n_lat, stride=nb)
        hin = jnp.concatenate([ha_re[rows, :], hb_re[rows, :], ha_im[rows, :], hb_im[rows, :]], axis=-1)
        lhs = jnp.concatenate([ul_ref[:, b * kk:(b + 1) * kk], hin.astype(_BF)], axis=-1)
        y_ref[:, b * kk:(b + 1) * kk] = _dot(lhs, wy)


def _s5(u_ctx, u_lat, ws, wy, at, nb):
    ng, n_ctx, w = u_ctx.shape
    n_lat = u_lat.shape[1]
    body = functools.partial(_s5_body, nb=nb)
    hs = pltpu.VMEM((n_lat * nb, 2 * S5_STATE), _F32)
    ss = pltpu.VMEM(((n_ctx + n_lat) * nb, 2 * S5_STATE), _F32)
    return pl.pallas_call(
        body,
        grid=(ng,),
        in_specs=[pl.BlockSpec((None, n_ctx, w), lambda g: (g, 0, 0)),
                  pl.BlockSpec((None, n_lat, w), lambda g: (g, 0, 0)),
                  pl.BlockSpec((None,) + ws.shape[1:], lambda g: (g, 0, 0)),
                  pl.BlockSpec((None,) + wy.shape[1:], lambda g: (g, 0, 0)),
                  pl.BlockSpec((None,) + at.shape[1:], lambda g: (g, 0, 0))],
        out_specs=pl.BlockSpec((None, n_lat, w), lambda g: (g, 0, 0)),
        out_shape=jax.ShapeDtypeStruct((ng, n_lat, w), _F32),
        scratch_shapes=[ss, ss, hs, hs, hs, hs],
        compiler_params=_cparams(("parallel",)),
        name="s5_scan",
    )(u_ctx, u_lat, ws, wy, at)


def _s5_param_body(lr_ref, li_ref, ls_ref, btr_ref, bti_ref, cr_ref, ci_ref, ws_ref, wy_ref, at_ref):
    t, hh, p = S5_CHUNK, S5_GROUP, S5_STATE
    kk = t * hh
    lr, li = lr_ref[...], li_ref[...]
    dt = jnp.exp(ls_ref[...])
    ldt, idt = lr * dt, li * dt
    mag = jnp.exp(ldt)
    ab_re, ab_im = mag * jnp.cos(idt), mag * jnp.sin(idt)
    den = lr * lr + li * li
    nr, ni = ab_re - 1.0, ab_im
    q_re = (nr * lr + ni * li) / den
    q_im = (ni * lr - nr * li) / den
    kf = lax.broadcasted_iota(jnp.int32, (2 * t, p), 0).astype(_F32)

    ap_re, ap_im, bb_re, bb_im = [], [], [], []
    for d in range(2):
        pm = jnp.exp(kf * ldt[d:d + 1, :])
        ap_re.append(pm * jnp.cos(kf * idt[d:d + 1, :]))
        ap_im.append(pm * jnp.sin(kf * idt[d:d + 1, :]))
        bb_re.append(q_re[d:d + 1, :] * btr_ref[d] - q_im[d:d + 1, :] * bti_ref[d])
        bb_im.append(q_re[d:d + 1, :] * bti_ref[d] + q_im[d:d + 1, :] * btr_ref[d])

    def rep(a, ks):
        return jnp.concatenate([jnp.broadcast_to(a[k:k + 1, :], (hh, p)) for k in ks], axis=0)

    def tile(a, n):
        return jnp.concatenate([a] * n, axis=0)

    def cmul(xr, xi, yr, yi):
        return xr * yr - xi * yi, xr * yi + xi * yr

    ks_f = list(range(t + 1))
    ks_r = list(range(t, -1, -1))
    caf_re, caf_im = cmul(tile(cr_ref[0], t + 1), tile(ci_ref[0], t + 1), rep(ap_re[0], ks_f), rep(ap_im[0], ks_f))
    car_re, car_im = cmul(tile(cr_ref[1], t + 1), tile(ci_ref[1], t + 1), rep(ap_re[1], ks_r), rep(ap_im[1], ks_r))

    def lag_blocks(bre, bim, ca_re, ca_im):
        lhs = jnp.concatenate([bre, bim], axis=1)
        rhs = jnp.concatenate([ca_re, -ca_im], axis=1)
        return lax.dot_general(lhs, rhs, (((1,), (1,)), ((), ())), precision=_HI,
                               preferred_element_type=_F32)

    w_f = lag_blocks(bb_re[0], bb_im[0], caf_re[:kk], caf_im[:kk])
    w_r = lag_blocks(bb_re[1], bb_im[1], car_re[hh:], car_im[hh:])
    lane = lax.broadcasted_iota(jnp.int32, (hh, kk), 1)
    for tlp in range(t):
        sf = hh * tlp
        sr = hh * (t - 1 - tlp)
        a = w_f if sf == 0 else jnp.where(lane >= sf, pltpu.roll(w_f, sf, 1), 0.0)
        b = w_r if sr == 0 else jnp.where(lane < kk - sr, pltpu.roll(w_r, kk - sr, 1), 0.0)
        wy_ref[hh * tlp:hh * (tlp + 1), :] = (a + b).astype(wy_ref.dtype)

    zeros = jnp.zeros((p, kk), _F32)
    blocks = [caf_re[hh:].T, zeros, zeros, car_re[:kk].T, (-caf_im[hh:]).T, zeros, zeros, (-car_im[:kk]).T]
    for i, blk in enumerate(blocks):
        wy_ref[kk + p * i:kk + p * (i + 1), :] = blk.astype(wy_ref.dtype)

    pf = list(range(t - 1, -1, -1))
    pr = list(range(t))
    f_re, f_im = cmul(rep(ap_re[0], pf), rep(ap_im[0], pf), tile(bb_re[0], t), tile(bb_im[0], t))
    r_re, r_im = cmul(rep(ap_re[1], pr), rep(ap_im[1], pr), tile(bb_re[1], t), tile(bb_im[1], t))
    ws_ref[...] = jnp.concatenate([f_re, r_re, f_im, r_im], axis=1).astype(ws_ref.dtype)

    a_re = jnp.concatenate([ap_re[0][t:t + 1, :], ap_re[1][t:t + 1, :]], axis=1)
    a_im = jnp.concatenate([ap_im[0][t:t + 1, :], ap_im[1][t:t + 1, :]], axis=1)
    row = lax.broadcasted_iota(jnp.int32, at_ref.shape, 0)
    at_ref[...] = jnp.where(row == 0, a_re, jnp.where(row == 1, a_im, 0.0))


def _s5_params(lam_re, lam_im, b_re, b_im, c_re, c_im, log_step):
    _, ng, p = lam_re.shape
    hh = b_re.shape[-1]
    kk = S5_CHUNK * hh
    gd = lambda a: jnp.swapaxes(a.astype(_F32), 0, 1)
    args = (gd(lam_re), gd(lam_im), gd(log_step)[..., None],
            jnp.swapaxes(gd(b_re), 2, 3), jnp.swapaxes(gd(b_im), 2, 3), gd(c_re), gd(c_im))
    spec = lambda a: pl.BlockSpec((None,) + a.shape[1:], lambda g: (g,) + (0,) * (a.ndim - 1))
    return pl.pallas_call(
        _s5_param_body,
        grid=(ng,),
        in_specs=[spec(a) for a in args],
        out_specs=[pl.BlockSpec((None, kk, 4 * p), lambda g: (g, 0, 0)),
                   pl.BlockSpec((None, kk + 8 * p, kk), lambda g: (g, 0, 0)),
                   pl.BlockSpec((None, 8, 2 * p), lambda g: (g, 0, 0))],
        out_shape=[jax.ShapeDtypeStruct((ng, kk, 4 * p), _BF),
                   jax.ShapeDtypeStruct((ng, kk + 8 * p, kk), _BF),
                   jax.ShapeDtypeStruct((ng, 8, 2 * p), _F32)],
        compiler_params=_cparams(("parallel",)),
        name="s5_params",
    )(*args)


def _post_body(x_ref, u_ref, yg_ref, fn_ref, d_ref, wglu_ref, gs_ref, wo_ref, g1_ref,
               n2_ref, sh_ref, sc_ref, wr_ref, br_ref, tri_ref, x1_ref, t_ref, rt_ref, rtt_ref, cnt_ref,
               y_scr, cnt_scr):
    ds5 = u_ref.shape[-1]
    first = (pl.program_id(0) == 0) & (pl.program_id(1) == 0)

    @pl.when(first)
    def _():
        cnt_scr[...] = jnp.zeros_like(cnt_scr)

    t_chunk = S5_CHUNK
    n_chunk = yg_ref.shape[1]
    lanes = 128
    gpb = lanes // S5_GROUP
    for j in range(ds5 // lanes):
        for hi in range(t_chunk // gpb):
            outs = _seg_transpose([yg_ref[gpb * j + glo, :, lanes * hi:lanes * (hi + 1)] for glo in range(gpb)])
            for tlo in range(gpb):
                y_scr[j, pl.ds(gpb * hi + tlo, n_chunk, stride=t_chunk), :] = outs[tlo]
    y_s5 = jnp.concatenate([y_scr[j] for j in range(ds5 // lanes)], axis=-1)

    y = y_s5 + d_ref[...] * u_ref[...]
    y = jax.nn.gelu(y, approximate=True)
    y = y * jax.nn.sigmoid(_dot(y.astype(_BF), wglu_ref[...]))
    yn = _rms(y, gs_ref[...]).astype(_BF)
    mix = _dot(yn, wo_ref[0:ds5, :]) + _dot(fn_ref[...].astype(_BF), wo_ref[ds5:, :])
    x1 = x_ref[...] + g1_ref[...] * mix
    x1_ref[...] = x1
    t = _rms(x1, n2_ref[...]) * (1.0 + sc_ref[...]) + sh_ref[...]
    t_hi = t.astype(_BF)
    t_ref[...] = t_hi

    ng, epg = N_EXPERT_GROUPS, EXPERTS_PER_GROUP
    t_lo = (t - t_hi.astype(_F32)).astype(_BF)
    logits = (_dot(t_hi, wr_ref[0]) + _dot(t_lo, wr_ref[0]) + _dot(t_hi, wr_ref[1])) + br_ref[...]
    lane = lax.broadcasted_iota(jnp.int32, logits.shape, 1)
    neg = jnp.float32(-jnp.inf)
    big = jnp.int32(1 << 20)
    gl = jnp.where(lane < ng, logits, neg)
    gmax = jnp.max(gl, axis=-1, keepdims=True)
    gidx = jnp.min(jnp.where(gl == gmax, lane, big), axis=-1, keepdims=True)
    gw = 1.0 / jnp.sum(jnp.where(lane < ng, jnp.exp(logits - gmax), 0.0), axis=-1, keepdims=True)
    lo = ng + gidx * epg
    el = jnp.where((lane >= lo) & (lane < lo + epg), logits, neg)
    v0 = jnp.max(el, axis=-1, keepdims=True)
    i0 = jnp.min(jnp.where(el == v0, lane, big), axis=-1, keepdims=True)
    el1 = jnp.where(lane == i0, neg, el)
    v1 = jnp.max(el1, axis=-1, keepdims=True)
    i1 = jnp.min(jnp.where(el1 == v1, lane, big), axis=-1, keepdims=True)
    p0 = 1.0 / (1.0 + jnp.exp(v1 - v0))
    w0 = gw * p0
    w1 = gw * (1.0 - p0)
    e0 = i0 - ng
    e1 = i1 - ng

    oh0 = lane == e0
    oh1 = lane == e1
    oh = (oh0 | oh1).astype(_F32)
    prefix = _dot(tri_ref[...], oh.astype(_BF)) + cnt_scr[...]
    r0 = jnp.sum(jnp.where(oh0, prefix, 0.0), axis=-1, keepdims=True)
    r1 = jnp.sum(jnp.where(oh1, prefix, 0.0), axis=-1, keepdims=True)
    cnt = cnt_scr[...] + jnp.sum(oh, axis=0, keepdims=True)
    cnt_scr[...] = cnt
    cnt_ref[...] = cnt

    vals = (e0.astype(_F32), e1.astype(_F32), w0, w1, r0, r1)
    rt = jnp.zeros(logits.shape, _F32)
    for k, v in enumerate(vals):
        rt = jnp.where(lane == k, v, rt)
    rt_ref[...] = rt
    rtt_ref[...] = rt.T[0:rtt_ref.shape[0], :]


@functools.lru_cache(maxsize=None)
def _strict_lower_ones(n):
    return np.tril(np.ones((n, n), np.float32), -1).astype(jnp.bfloat16)


def _post(x, u, y_g, fn, s5_d, wglu_bf, gs, wo_bf, g1, n2, sh2, sc2, wr, br, b0, nb, tm=512):
    b, l, d = x.shape
    ds5 = u.shape[-1]
    n_grp, _, w = y_g.shape
    kk = w // b
    tri = jnp.asarray(_strict_lower_ones(tm))
    tok = lambda w: pl.BlockSpec((None, tm, w), lambda i, j: (i + b0, j, 0))
    otok = lambda w: pl.BlockSpec((None, tm, w), lambda i, j: (i, j, 0))
    per_b = pl.BlockSpec((None, 1, d), lambda i, j: (i + b0, 0, 0))
    full = lambda a: pl.BlockSpec(a.shape, lambda i, j: (0,) * a.ndim)
    ygs = pl.BlockSpec((n_grp, tm // S5_CHUNK, kk), lambda i, j: (0, j, i + b0))
    return pl.pallas_call(
        _post_body,
        grid=(nb, l // tm),
        in_specs=[tok(d), tok(ds5), ygs, tok(fn.shape[-1]), full(s5_d), full(wglu_bf),
                  full(gs), full(wo_bf), per_b, full(n2), per_b, per_b, full(wr), full(br), full(tri)],
        out_specs=[otok(d), otok(d), otok(ROUTE_LANES),
                   pl.BlockSpec((None, ROUTE_ROWS, tm), lambda i, j: (i, 0, j)),
                   pl.BlockSpec((1, ROUTE_LANES), lambda i, j: (0, 0))],
        out_shape=[jax.ShapeDtypeStruct((nb, l, d), _F32),
                   jax.ShapeDtypeStruct((nb, l, d), _BF),
                   jax.ShapeDtypeStruct((nb, l, ROUTE_LANES), _F32),
                   jax.ShapeDtypeStruct((nb, ROUTE_ROWS, l), _F32),
                   jax.ShapeDtypeStruct((1, ROUTE_LANES), _F32)],
        scratch_shapes=[pltpu.VMEM((ds5 // 128, tm, 128), _F32),
                        pltpu.VMEM((1, ROUTE_LANES), _F32)],
        compiler_params=_cparams(("arbitrary", "arbitrary")),
        name="post_mixer",
    )(x, u, y_g, fn, s5_d, wglu_bf, gs, wo_bf, g1, n2, sh2, sc2, wr, br, tri)


def _moe_body(te_ref, nt_ref, xs_ref, wg_ref, wu_ref, wd_ref, o_ref, wgu_bf, wd_bf):
    i = pl.program_id(0)
    de = wg_ref.shape[-1]
    used = i < nt_ref[0]
    new_expert = (i == 0) | (te_ref[i] != te_ref[jnp.maximum(i - 1, 0)])

    @pl.when(used & new_expert)
    def _():
        wgu_bf[:, 0:de] = wg_ref[...].astype(_BF)
        wgu_bf[:, de:2 * de] = wu_ref[...].astype(_BF)
        wd_bf[...] = wd_ref[...].astype(_BF)

    @pl.when(used)
    def _():
        h = _dot(xs_ref[...], wgu_bf[...])
        hg, hu = h[:, 0:de], h[:, de:2 * de]
        a = hg * jax.nn.sigmoid(hg) * hu
        o_ref[...] = _dot(a.astype(_BF), wd_bf[...]).astype(o_ref.dtype)

    @pl.when(jnp.logical_not(used))
    def _():
        o_ref[...] = jnp.zeros_like(o_ref)


def _moe(tile_expert, n_tiles_used, xs, wg, wu, wd):
    nr, d = xs.shape
    tm = MOE_TM
    de = wg.shape[-1]
    grid_spec = pltpu.PrefetchScalarGridSpec(
        num_scalar_prefetch=2,
        grid=(nr // tm,),
        in_specs=[pl.BlockSpec((tm, d), lambda i, te, nt: (i, 0)),
                  pl.BlockSpec((None, d, de), lambda i, te, nt: (te[i], 0, 0)),
                  pl.BlockSpec((None, d, de), lambda i, te, nt: (te[i], 0, 0)),
                  pl.BlockSpec((None, de, d), lambda i, te, nt: (te[i], 0, 0))],
        out_specs=pl.BlockSpec((tm, d), lambda i, te, nt: (i, 0)),
        scratch_shapes=[pltpu.VMEM((d, 2 * de), _BF), pltpu.VMEM((de, d), _BF)],
    )
    return pl.pallas_call(
        _moe_body,
        grid_spec=grid_spec,
        out_shape=jax.ShapeDtypeStruct((nr, d), _BF),
        compiler_params=_cparams(("arbitrary",)),
        name="moe_experts",
    )(tile_expert, n_tiles_used, xs, wg, wu, wd)


def _final_body(g2_ref, gf_ref, *refs, n_part, nb):
    o_ref = refs[-1]
    i = pl.program_id(0)
    for part in range(n_part):
        x1_ref, y0_ref, y1_ref, rt_ref = refs[4 * part:4 * part + 4]

        @pl.when((i >= part * nb) & (i < (part + 1) * nb))
        def _():
            w0 = rt_ref[:, 2:3]
            w1 = rt_ref[:, 3:4]
            m = w0 * y0_ref[...].astype(_F32) + w1 * y1_ref[...].astype(_F32)
            o_ref[...] = _rms(x1_ref[...] + g2_ref[...] * m, gf_ref[...])


def _final(parts, g2, gf, tm=512):
    n_part = len(parts)
    nb, l, d = parts[0][0].shape
    nj = l // tm
    in_specs = [pl.BlockSpec((None, 1, d), lambda i, j: (i, 0, 0)),
                pl.BlockSpec((1, d), lambda i, j: (0, 0))]
    args = [g2, gf]
    for part, arrs in enumerate(parts):
        def imap(i, j, part=part):
            own = (i >= part * nb) & (i < (part + 1) * nb)
            return (jnp.clip(i - part * nb, 0, nb - 1), jnp.where(own, j, jnp.where(i < part * nb, 0, nj - 1)), 0)
        for a in arrs:
            in_specs.append(pl.BlockSpec((None, tm, a.shape[-1]), imap))
            args.append(a)
    return pl.pallas_call(
        functools.partial(_final_body, n_part=n_part, nb=nb),
        grid=(n_part * nb, nj),
        in_specs=in_specs,
        out_specs=pl.BlockSpec((None, tm, d), lambda i, j: (i, j, 0)),
        out_shape=jax.ShapeDtypeStruct((n_part * nb, l, d), _F32),
        compiler_params=_cparams(("parallel", "arbitrary")),
        name="final_norm",
    )(*args)


def _route_plan(eid, rank, tok_ids, counts, tm, nr):
    n_experts = counts.shape[0]
    n_pairs = eid.size
    padded = ((counts + tm - 1) // tm) * tm
    pad_end = jnp.cumsum(padded)
    pad_start = pad_end - padded
    raw_start = jnp.cumsum(counts) - counts
    pos = rank
    for e in range(n_experts):
        pos = pos + jnp.where(eid == e, pad_start[e], 0)
    tile_start = jnp.arange(nr // tm, dtype=jnp.int32) * tm
    tile_expert = jnp.sum(tile_start[:, None] >= pad_end[None, :], axis=-1)
    tile_expert = jnp.minimum(tile_expert, n_experts - 1).astype(jnp.int32)
    _, sorted_tok = lax.sort_key_val((eid * n_pairs + rank).reshape(-1), tok_ids.reshape(-1))
    sel = tile_expert[:, None] == jnp.arange(n_experts, dtype=jnp.int32)[None, :]
    per_tile = lambda v: jnp.repeat(jnp.sum(jnp.where(sel, v[None, :], 0), axis=-1), tm)
    off = jnp.arange(nr, dtype=jnp.int32) - per_tile(pad_start)
    valid = off < per_tile(counts)
    j = jnp.clip(per_tile(raw_start) + off, 0, n_pairs - 1)
    filler = jnp.arange(nr, dtype=jnp.int32) % (n_pairs // 2)
    row_token = jnp.where(valid, sorted_tok.at[j].get(mode="promise_in_bounds"), filler)
    n_used = (pad_end[-1:] // tm).astype(jnp.int32)
    return pos, tile_expert, n_used, row_token


def kernel(x, c, ctx, c_ctx, w_ada, b_ada, norm1_g, norm2_g, w_in, s5_lam_re, s5_lam_im, s5_b_re, s5_b_im, s5_c_re, s5_c_im, s5_log_step, s5_d, s5_w_glu, fourier_w, mix_norm_s5_g, mix_norm_f_g, w_out, moe_w_group, moe_b_group, moe_w_router, moe_b_router, moe_w_gate, moe_w_up, moe_w_down, final_norm_g):
    b, l, d = x.shape
    lc = ctx.shape[1]
    depth = w_ada.shape[0]
    assert depth == 1 and l == FFT_R * FFT_R and lc % S5_CHUNK == 0 and b % 8 == 0
    ds5 = s5_d.shape[-1]
    df = w_in.shape[-1] - ds5
    n_exp = moe_w_gate.shape[1]
    row = lambda a: a.reshape(1, -1)

    cond = jnp.concatenate([c, c_ctx[None, :], jnp.zeros((7, d), _F32)], axis=0)
    mod = _adaln(cond, w_ada[0], b_ada[0])
    sh1, sc1, g1, sh2, sc2, g2 = [mod[:b, i * d:(i + 1) * d].reshape(b, 1, d) for i in range(6)]
    csh1 = mod[b:b + 1, 0:d]
    csc1 = mod[b:b + 1, d:2 * d]

    w_in_bf = w_in[0].astype(_BF)
    m1 = jnp.asarray(_fft_stage1_matrix()).astype(_BF)
    m2 = jnp.asarray(_fft_stage2_matrices()).astype(_BF)
    perm = jnp.asarray(_chunk_row_perm())
    zs, gc = _inproj(x, sh1, sc1, row(norm1_g[0]), w_in_bf, m1)
    u_ctx = _inproj_ctx(ctx, csh1, csc1, row(norm1_g[0]), w_in_bf[:, :ds5], perm)
    u_lat = _s5_pack(zs, perm)

    fg = df // FOURIER_GROUPS
    cc = np.arange(fg)
    ang = 2.0 * np.pi * ((cc[:, None] * cc[None, :]) % fg) / fg
    scale = 1.0 / math.sqrt(l * fg)
    cw = jnp.einsum("cm,gmd->gcd", jnp.asarray(np.cos(ang) * scale, _F32), fourier_w[0], precision=_HI)
    sw = jnp.einsum("cm,gmd->gcd", jnp.asarray(np.sin(ang) * scale, _F32), fourier_w[0], precision=_HI)
    fn = _fft2(gc, m2, cw.astype(_BF), sw.astype(_BF), row(mix_norm_f_g[0]))

    ws, wy, at = _s5_params(s5_lam_re[0], s5_lam_im[0], s5_b_re[0], s5_b_im[0],
                            s5_c_re[0], s5_c_im[0], s5_log_step[0])
    y_g = _s5(u_ctx, u_lat, ws, wy, at, b)

    n_rt = N_EXPERT_GROUPS * (1 + EXPERTS_PER_GROUP)
    w_rt = jnp.concatenate([moe_w_group[0], moe_w_router[0].reshape(d, -1)], axis=-1)
    w_rt = jnp.pad(w_rt, ((0, 0), (0, ROUTE_LANES - n_rt)))
    w_rt_hi = w_rt.astype(_BF)
    w_rt = jnp.stack([w_rt_hi, (w_rt - w_rt_hi.astype(_F32)).astype(_BF)])
    b_rt =jnp.pad(jnp.concatenate([moe_b_group[0], moe_b_router[0].reshape(-1)]), (0, ROUTE_LANES - n_rt))
    tm = MOE_TM
    nb = b // MOE_PARTS
    n_tok = nb * l
    nr = 2 * n_tok + n_exp * tm
    take_rows = lambda a, idx: a.at[idx].get(mode="promise_in_bounds")
    wglu_bf, wo_bf = s5_w_glu[0].astype(_BF), w_out[0].astype(_BF)
    parts = []
    for part in range(MOE_PARTS):
        x1, tmod, rt, rtt, cnt = _post(x, zs, y_g, fn, row(s5_d[0]), wglu_bf, row(mix_norm_s5_g[0]), wo_bf,
                                       g1, row(norm2_g[0]), sh2, sc2, w_rt, row(b_rt), part * nb, nb)
        rec = rtt.transpose(1, 0, 2).reshape(ROUTE_ROWS, n_tok)
        eid = rec[0:2].astype(jnp.int32)
        rank = rec[4:6].astype(jnp.int32)
        tok_ids = jnp.broadcast_to(jnp.arange(n_tok, dtype=jnp.int32), (2, n_tok))
        pos, tile_expert, n_used, row_token = _route_plan(eid, rank, tok_ids,
                                                          cnt[0, :n_exp].astype(jnp.int32), tm, nr)
        xs = take_rows(tmod.reshape(n_tok, d), row_token)
        ys = _moe(tile_expert, n_used, xs, moe_w_gate[0], moe_w_up[0], moe_w_down[0])
        y0 = take_rows(ys, pos[0]).reshape(nb, l, d)
        y1 = take_rows(ys, pos[1]).reshape(nb, l, d)
        parts.append((x1, y0, y1, rt))
    return _final(parts, g2, row(final_norm_g))
```

```python
import functools
import math

import numpy as np
import jax
import jax.numpy as jnp
from jax import lax
from jax.experimental import pallas as pl
from jax.experimental.pallas import tpu as pltpu

EPS = 1e-6
S5_GROUP = 16
S5_STATE = 64
S5_CHUNK = 16
FOURIER_GROUPS = 4
N_EXPERT_GROUPS = 4
EXPERTS_PER_GROUP = 8
FFT_R = 64
FFT_BLK = 8
MOE_TM = 512
MOE_PARTS = 1
ROUTE_LANES = 128
ROUTE_ROWS = 8
VMEM_LIMIT = 56 * 1024 * 1024

_HI = lax.Precision.HIGHEST
_BF = jnp.bfloat16
_F32 = jnp.float32


def _cparams(sem):
    return pltpu.CompilerParams(dimension_semantics=sem, vmem_limit_bytes=VMEM_LIMIT)


def _dot(a, b):
    return jnp.dot(a, b, preferred_element_type=_F32)


def _rms(x, g):
    return x * lax.rsqrt(jnp.mean(x * x, axis=-1, keepdims=True) + EPS) * g


def _adaln_body(c_ref, w_ref, b_ref, o_ref):
    c = c_ref[...]
    a = c * jax.nn.sigmoid(c)
    o_ref[...] = jnp.dot(a, w_ref[...], preferred_element_type=_F32, precision=_HI) + b_ref[...]


def _adaln(cond, w, b):
    m, d = cond.shape
    n = w.shape[1]
    tn = 768
    return pl.pallas_call(
        _adaln_body,
        grid=(n // tn,),
        in_specs=[pl.BlockSpec((m, d), lambda j: (0, 0)),
                  pl.BlockSpec((d, tn), lambda j: (0, j)),
                  pl.BlockSpec((1, tn), lambda j: (0, j))],
        out_specs=pl.BlockSpec((m, tn), lambda j: (0, j)),
        out_shape=jax.ShapeDtypeStruct((m, n), _F32),
        compiler_params=_cparams(("arbitrary",)),
        name="adaln",
    )(cond, w, b.reshape(1, n))


def _inproj_body(x_ref, sh_ref, sc_ref, g_ref, w_ref, m1_ref, zs_ref, gc_ref):
    r, blk, d = x_ref.shape
    x = x_ref[...].reshape(r * blk, d)
    h = _rms(x, g_ref[...]) * (1.0 + sc_ref[...]) + sh_ref[...]
    z = _dot(h.astype(_BF), w_ref[...])
    ds5 = zs_ref.shape[-1]
    zs_ref[...] = z[:, :ds5].reshape(r, blk, ds5)
    v = z[:, ds5:].astype(_BF)
    g1 = _dot(m1_ref[...], v)
    gc_ref[...] = g1.astype(_BF).reshape(r, 2 * blk, v.shape[-1])


def _inproj(x, sh, sc, g, w_bf, m1):
    b, l, d = x.shape
    r, blk = FFT_R, FFT_BLK
    nj = r // blk
    dmix = w_bf.shape[1]
    ds5 = dmix // 2
    df = dmix - ds5
    x4 = x.reshape(b, r, r, d)
    zs, gc = pl.pallas_call(
        _inproj_body,
        grid=(b, nj),
        in_specs=[pl.BlockSpec((None, r, blk, d), lambda i, j: (i, 0, j, 0)),
                  pl.BlockSpec((None, 1, d), lambda i, j: (i, 0, 0)),
                  pl.BlockSpec((None, 1, d), lambda i, j: (i, 0, 0)),
                  pl.BlockSpec((1, d), lambda i, j: (0, 0)),
                  pl.BlockSpec((d, dmix), lambda i, j: (0, 0)),
                  pl.BlockSpec(m1.shape, lambda i, j: (0, 0))],
        out_specs=[pl.BlockSpec((None, r, blk, ds5), lambda i, j: (i, 0, j, 0)),
                   pl.BlockSpec((None, r, 2 * blk, df), lambda i, j: (i, 0, j, 0))],
        out_shape=[jax.ShapeDtypeStruct((b, r, r, ds5), _F32),
                   jax.ShapeDtypeStruct((b, r, 2 * r, df), _BF)],
        compiler_params=_cparams(("parallel", "arbitrary")),
        name="inproj",
    )(x4, sh, sc, g, w_bf, m1)
    return zs.reshape(b, l, ds5), gc


def _seg_transpose(arrs):
    n = len(arrs)
    seg = lax.broadcasted_iota(jnp.int32, arrs[0].shape, 1) // S5_GROUP
    d = n // 2
    while d >= 1:
        keep = (seg & d) == 0
        new = list(arrs)
        for i in range(n):
            if i & d == 0:
                a, b = arrs[i], arrs[i + d]
                new[i] = jnp.where(keep, a, pltpu.roll(b, S5_GROUP * d, 1))
                new[i + d] = jnp.where(keep, pltpu.roll(a, 128 - S5_GROUP * d, 1), b)
        arrs = new
        d //= 2
    return arrs


@functools.lru_cache(maxsize=None)
def _chunk_row_perm():
    t = S5_CHUNK
    p = np.zeros((t * t, t * t), np.float32)
    for c in range(t):
        for tl in range(t):
            p[tl * t + c, c * t + tl] = 1.0
    return p.astype(jnp.bfloat16)


def _pack_chunks(z_bf, p_ref, u_ref):
    t = S5_CHUNK
    nsub = z_bf.shape[0] // (t * t)
    pieces = [_dot(p_ref[...], z_bf[s * t * t:(s + 1) * t * t, :]) for s in range(nsub)]
    a = []
    for tl in range(t):
        rows = [pc[tl * t:(tl + 1) * t, :] for pc in pieces]
        a.append(rows[0] if nsub == 1 else jnp.concatenate(rows, axis=0))
    lanes = 128
    gpb = lanes // S5_GROUP
    for j in range(z_bf.shape[1] // lanes):
        for hi in range(t // gpb):
            outs = _seg_transpose([a[gpb * hi + tlo][:, lanes * j:lanes * (j + 1)] for tlo in range(gpb)])
            for glo in range(gpb):
                u_ref[gpb * j + glo, :, lanes * hi:lanes * (hi + 1)] = outs[glo].astype(u_ref.dtype)


def _inproj_ctx_body(x_ref, sh_ref, sc_ref, g_ref, w_ref, p_ref, u_ref):
    h = _rms(x_ref[...], g_ref[...]) * (1.0 + sc_ref[...]) + sh_ref[...]
    z = _dot(h.astype(_BF), w_ref[...])
    _pack_chunks(z.astype(_BF), p_ref, u_ref)


def _inproj_ctx(ctx, sh, sc, g, w_s5_bf, perm):
    b, lc, d = ctx.shape
    ds5 = w_s5_bf.shape[1]
    n_grp = ds5 // S5_GROUP
    kk = S5_CHUNK * S5_GROUP
    return pl.pallas_call(
        _inproj_ctx_body,
        grid=(b,),
        in_specs=[pl.BlockSpec((None, lc, d), lambda i: (i, 0, 0)),
                  pl.BlockSpec((1, d), lambda i: (0, 0)),
                  pl.BlockSpec((1, d), lambda i: (0, 0)),
                  pl.BlockSpec((1, d), lambda i: (0, 0)),
                  pl.BlockSpec((d, ds5), lambda i: (0, 0)),
                  pl.BlockSpec(perm.shape, lambda i: (0, 0))],
        out_specs=pl.BlockSpec((n_grp, lc // S5_CHUNK, kk), lambda i: (0, 0, i)),
        out_shape=jax.ShapeDtypeStruct((n_grp, lc // S5_CHUNK, b * kk), _BF),
        compiler_params=_cparams(("arbitrary",)),
        name="inproj_ctx",
    )(ctx, sh, sc, g, w_s5_bf, perm)


def _s5_pack_body(z_ref, p_ref, u_ref):
    _pack_chunks(z_ref[...].astype(_BF), p_ref, u_ref)


def _s5_pack(zs, perm, tok=2048):
    b, l, ds5 = zs.shape
    n_grp = ds5 // S5_GROUP
    kk = S5_CHUNK * S5_GROUP
    return pl.pallas_call(
        _s5_pack_body,
        grid=(b, l // tok),
        in_specs=[pl.BlockSpec((None, tok, ds5), lambda i, j: (i, j, 0)),
                  pl.BlockSpec(perm.shape, lambda i, j: (0, 0))],
        out_specs=pl.BlockSpec((n_grp, tok // S5_CHUNK, kk), lambda i, j: (0, j, i)),
        out_shape=jax.ShapeDtypeStruct((n_grp, l // S5_CHUNK, b * kk), _BF),
        compiler_params=_cparams(("parallel", "arbitrary")),
        name="s5_pack",
    )(zs, perm)


@functools.lru_cache(maxsize=None)
def _fft_stage1_matrix():
    r, blk = FFT_R, FFT_BLK
    k1 = np.arange(r)[:, None]
    l1 = np.arange(r)[None, :]
    ang = 2.0 * np.pi * ((k1 * l1) % r) / r
    f = np.stack([np.cos(ang), -np.sin(ang)], axis=1)
    m = np.einsum("kal,pq->kaplq", f, np.eye(blk))
    return m.reshape(r * 2 * blk, r * blk).astype(np.float32)


@functools.lru_cache(maxsize=None)
def _fft_stage2_matrices():
    r, blk = FFT_R, FFT_BLK
    n = r * r
    nt = r // blk
    l2 = np.arange(r)
    k2 = np.arange(r)
    out = np.zeros((nt, 2, r, blk, blk, r // blk, 2, blk), np.float32)
    for i in range(nt):
        for kl in range(blk):
            k1 = blk * i + kl
            ang = 2.0 * np.pi * (((k2[:, None] * l2[None, :] * r) + l2[None, :] * k1) % n) / n
            tr, ti = np.cos(ang), -np.sin(ang)
            tr = tr.reshape(r, r // blk, blk)
            ti = ti.reshape(r, r // blk, blk)
            out[i, 0, :, kl, kl, :, 0, :] = tr
            out[i, 0, :, kl, kl, :, 1, :] = -ti
            out[i, 1, :, kl, kl, :, 0, :] = ti
            out[i, 1, :, kl, kl, :, 1, :] = tr
    return out.reshape(nt, 2 * r * blk, blk * 2 * r)


def _fft2_body(gc_ref, m2_ref, cw_ref, sw_ref, g_ref, o_ref):
    kb, rr, df = gc_ref.shape
    gc = gc_ref[...].reshape(kb * rr, df)
    x = _dot(m2_ref[...], gc)
    half = x.shape[0] // 2
    xr = x[:half].astype(_BF)
    xi = x[half:].astype(_BF)
    ng = cw_ref.shape[0]
    fg = df // ng
    parts = []
    for g in range(ng):
        sl = slice(g * fg, (g + 1) * fg)
        parts.append(_dot(xr[:, sl], cw_ref[g]) + _dot(xi[:, sl], sw_ref[g]))
    f = jnp.concatenate(parts, axis=-1)
    fn = _rms(f, g_ref[...])
    o_ref[...] = fn.reshape(o_ref.shape)


def _fft2(gc, m2, cw, sw, gf):
    b, r, rr, df = gc.shape
    blk = FFT_BLK
    nt = r // blk
    out = pl.pallas_call(
        _fft2_body,
        grid=(b, nt),
        in_specs=[pl.BlockSpec((None, blk, rr, df), lambda i, j: (i, j, 0, 0)),
                  pl.BlockSpec((None,) + m2.shape[1:], lambda i, j: (j, 0, 0)),
                  pl.BlockSpec(cw.shape, lambda i, j: (0, 0, 0)),
                  pl.BlockSpec(sw.shape, lambda i, j: (0, 0, 0)),
                  pl.BlockSpec((1, df), lambda i, j: (0, 0))],
        out_specs=pl.BlockSpec((None, r, blk, df), lambda i, j: (i, 0, j, 0)),
        out_shape=jax.ShapeDtypeStruct((b, r, r, df), _F32),
        compiler_params=_cparams(("parallel", "arbitrary")),
        name="fft2",
    )(gc, m2, cw, sw, gf)
    return out.reshape(b, r * r, df)


def _s5_body(uc_ref, ul_ref, ws_ref, wy_ref, at_ref, y_ref,
             s_re, s_im, ha_re, hb_re, ha_im, hb_im, *, nb):
    p = S5_STATE
    kk = ws_ref.shape[0]
    n_ctx, n_lat = uc_ref.shape[0], ul_ref.shape[0]
    ws = ws_ref[...]
    for b in range(nb):
        sl = slice(b * kk, (b + 1) * kk)
        sc = _dot(uc_ref[:, sl], ws)
        sl_ = _dot(ul_ref[:, sl], ws)
        s_re[pl.ds(b, n_ctx, stride=nb), :] = sc[:, :2 * p]
        s_im[pl.ds(b, n_ctx, stride=nb), :] = sc[:, 2 * p:]
        s_re[pl.ds(n_ctx * nb + b, n_lat, stride=nb), :] = sl_[:, :2 * p]
        s_im[pl.ds(n_ctx * nb + b, n_lat, stride=nb), :] = sl_[:, 2 * p:]

    a_re = at_ref[0:1, :]
    a_im = at_ref[1:2, :]
    is_f = lax.broadcasted_iota(jnp.int32, (nb, 2 * p), 1) < p
    n_all = n_ctx + n_lat

    def load(cf, cr):
        rf = pl.multiple_of(cf * nb, nb)
        rr = pl.multiple_of(cr * nb, nb)
        return (jnp.where(is_f, s_re[pl.ds(rf, nb), :], s_re[pl.ds(rr, nb), :]),
                jnp.where(is_f, s_im[pl.ds(rf, nb), :], s_im[pl.ds(rr, nb), :]))

    def advance(h_re, h_im, x_re, x_im):
        return (a_re * h_re - a_im * h_im + x_re, a_re * h_im + a_im * h_re + x_im)

    def ctx_step(step, carry):
        x_re, x_im = load(step, n_ctx - 1 - step)
        return advance(*carry, x_re, x_im)

    def lat_step(step, carry):
        h_re, h_im = carry
        rf = pl.multiple_of(step * nb, nb)
        rr = pl.multiple_of((n_lat - 1 - step) * nb, nb)
        ha_re[pl.ds(rf, nb), :] = h_re
        hb_re[pl.ds(rr, nb), :] = h_re
        ha_im[pl.ds(rf, nb), :] = h_im
        hb_im[pl.ds(rr, nb), :] = h_im
        x_re, x_im = load(n_ctx + step, n_all - 1 - step)
        return advance(h_re, h_im, x_re, x_im)

    zero = jnp.zeros((nb, 2 * p), _F32)
    carry = lax.fori_loop(0, n_ctx, ctx_step, (zero, zero), unroll=4)
    lax.fori_loop(0, n_lat, lat_step, carry, unroll=4)

    wy = wy_ref[...]
    for b in range(nb):
        rows = pl.ds(b, n_lat, stride=nb)
        hin = jnp.concatenate([ha_re[rows, :], hb_re[rows, :], ha_im[rows, :], hb_im[rows, :]], axis=-1)
        lhs = jnp.concatenate([ul_ref[:, b * kk:(b + 1) * kk], hin.astype(_BF)], axis=-1)
        y_ref[:, b * kk:(b + 1) * kk] = _dot(lhs, wy).astype(y_ref.dtype)


def _s5(u_ctx, u_lat, ws, wy, at, nb):
    ng, n_ctx, w = u_ctx.shape
    n_lat = u_lat.shape[1]
    body = functools.partial(_s5_body, nb=nb)
    hs = pltpu.VMEM((n_lat * nb, 2 * S5_STATE), _F32)
    ss = pltpu.VMEM(((n_ctx + n_lat) * nb, 2 * S5_STATE), _F32)
    return pl.pallas_call(
        body,
        grid=(ng,),
        in_specs=[pl.BlockSpec((None, n_ctx, w), lambda g: (g, 0, 0)),
                  pl.BlockSpec((None, n_lat, w), lambda g: (g, 0, 0)),
                  pl.BlockSpec((None,) + ws.shape[1:], lambda g: (g, 0, 0)),
                  pl.BlockSpec((None,) + wy.shape[1:], lambda g: (g, 0, 0)),
                  pl.BlockSpec((None,) + at.shape[1:], lambda g: (g, 0, 0))],
        out_specs=pl.BlockSpec((None, n_lat, w), lambda g: (g, 0, 0)),
        out_shape=jax.ShapeDtypeStruct((ng, n_lat, w), _BF),
        scratch_shapes=[ss, ss, hs, hs, hs, hs],
        compiler_params=_cparams(("parallel",)),
        name="s5_scan",
    )(u_ctx, u_lat, ws, wy, at)


def _s5_param_body(lr_ref, li_ref, ls_ref, btr_ref, bti_ref, cr_ref, ci_ref, ws_ref, wy_ref, at_ref):
    t, hh, p = S5_CHUNK, S5_GROUP, S5_STATE
    kk = t * hh
    lr, li = lr_ref[...], li_ref[...]
    dt = jnp.exp(ls_ref[...])
    ldt, idt = lr * dt, li * dt
    mag = jnp.exp(ldt)
    ab_re, ab_im = mag * jnp.cos(idt), mag * jnp.sin(idt)
    den = lr * lr + li * li
    nr, ni = ab_re - 1.0, ab_im
    q_re = (nr * lr + ni * li) / den
    q_im = (ni * lr - nr * li) / den
    kf = lax.broadcasted_iota(jnp.int32, (2 * t, p), 0).astype(_F32)

    ap_re, ap_im, bb_re, bb_im = [], [], [], []
    for d in range(2):
        pm = jnp.exp(kf * ldt[d:d + 1, :])
        ap_re.append(pm * jnp.cos(kf * idt[d:d + 1, :]))
        ap_im.append(pm * jnp.sin(kf * idt[d:d + 1, :]))
        bb_re.append(q_re[d:d + 1, :] * btr_ref[d] - q_im[d:d + 1, :] * bti_ref[d])
        bb_im.append(q_re[d:d + 1, :] * bti_ref[d] + q_im[d:d + 1, :] * btr_ref[d])

    def rep(a, ks):
        return jnp.concatenate([jnp.broadcast_to(a[k:k + 1, :], (hh, p)) for k in ks], axis=0)

    def tile(a, n):
        return jnp.concatenate([a] * n, axis=0)

    def cmul(xr, xi, yr, yi):
        return xr * yr - xi * yi, xr * yi + xi * yr

    ks_f = list(range(t + 1))
    ks_r = list(range(t, -1, -1))
    caf_re, caf_im = cmul(tile(cr_ref[0], t + 1), tile(ci_ref[0], t + 1), rep(ap_re[0], ks_f), rep(ap_im[0], ks_f))
    car_re, car_im = cmul(tile(cr_ref[1], t + 1), tile(ci_ref[1], t + 1), rep(ap_re[1], ks_r), rep(ap_im[1], ks_r))

    def lag_blocks(bre, bim, ca_re, ca_im):
        lhs = jnp.concatenate([bre, bim], axis=1)
        rhs = jnp.concatenate([ca_re, -ca_im], axis=1)
        return lax.dot_general(lhs, rhs, (((1,), (1,)), ((), ())), precision=_HI,
                               preferred_element_type=_F32)

    w_f = lag_blocks(bb_re[0], bb_im[0], caf_re[:kk], caf_im[:kk])
    w_r = lag_blocks(bb_re[1], bb_im[1], car_re[hh:], car_im[hh:])
    lane = lax.broadcasted_iota(jnp.int32, (hh, kk), 1)
    for tlp in range(t):
        sf = hh * tlp
        sr = hh * (t - 1 - tlp)
        a = w_f if sf == 0 else jnp.where(lane >= sf, pltpu.roll(w_f, sf, 1), 0.0)
        b = w_r if sr == 0 else jnp.where(lane < kk - sr, pltpu.roll(w_r, kk - sr, 1), 0.0)
        wy_ref[hh * tlp:hh * (tlp + 1), :] = (a + b).astype(wy_ref.dtype)

    zeros = jnp.zeros((p, kk), _F32)
    blocks = [caf_re[hh:].T, zeros, zeros, car_re[:kk].T, (-caf_im[hh:]).T, zeros, zeros, (-car_im[:kk]).T]
    for i, blk in enumerate(blocks):
        wy_ref[kk + p * i:kk + p * (i + 1), :] = blk.astype(wy_ref.dtype)

    pf = list(range(t - 1, -1, -1))
    pr = list(range(t))
    f_re, f_im = cmul(rep(ap_re[0], pf), rep(ap_im[0], pf), tile(bb_re[0], t), tile(bb_im[0], t))
    r_re, r_im = cmul(rep(ap_re[1], pr), rep(ap_im[1], pr), tile(bb_re[1], t), tile(bb_im[1], t))
    ws_ref[...] = jnp.concatenate([f_re, r_re, f_im, r_im], axis=1).astype(ws_ref.dtype)

    a_re = jnp.concatenate([ap_re[0][t:t + 1, :], ap_re[1][t:t + 1, :]], axis=1)
    a_im = jnp.concatenate([ap_im[0][t:t + 1, :], ap_im[1][t:t + 1, :]], axis=1)
    row = lax.broadcasted_iota(jnp.int32, at_ref.shape, 0)
    at_ref[...] = jnp.where(row == 0, a_re, jnp.where(row == 1, a_im, 0.0))


def _s5_params(lam_re, lam_im, b_re, b_im, c_re, c_im, log_step):
    _, ng, p = lam_re.shape
    hh = b_re.shape[-1]
    kk = S5_CHUNK * hh
    gd = lambda a: jnp.swapaxes(a.astype(_F32), 0, 1)
    args = (gd(lam_re), gd(lam_im), gd(log_step)[..., None],
            jnp.swapaxes(gd(b_re), 2, 3), jnp.swapaxes(gd(b_im), 2, 3), gd(c_re), gd(c_im))
    spec = lambda a: pl.BlockSpec((None,) + a.shape[1:], lambda g: (g,) + (0,) * (a.ndim - 1))
    return pl.pallas_call(
        _s5_param_body,
        grid=(ng,),
        in_specs=[spec(a) for a in args],
        out_specs=[pl.BlockSpec((None, kk, 4 * p), lambda g: (g, 0, 0)),
                   pl.BlockSpec((None, kk + 8 * p, kk), lambda g: (g, 0, 0)),
                   pl.BlockSpec((None, 8, 2 * p), lambda g: (g, 0, 0))],
        out_shape=[jax.ShapeDtypeStruct((ng, kk, 4 * p), _BF),
                   jax.ShapeDtypeStruct((ng, kk + 8 * p, kk), _BF),
                   jax.ShapeDtypeStruct((ng, 8, 2 * p), _F32)],
        compiler_params=_cparams(("parallel",)),
        name="s5_params",
    )(*args)


def _post_body(x_ref, u_ref, yg_ref, fn_ref, d_ref, wglu_ref, gs_ref, wo_ref, g1_ref,
               n2_ref, sh_ref, sc_ref, wr_ref, br_ref, tri_ref, x1_ref, t_ref, rt_ref, rtt_ref, cnt_ref,
               y_scr, cnt_scr):
    ds5 = u_ref.shape[-1]
    first = (pl.program_id(0) == 0) & (pl.program_id(1) == 0)

    @pl.when(first)
    def _():
        cnt_scr[...] = jnp.zeros_like(cnt_scr)

    t_chunk = S5_CHUNK
    n_chunk = yg_ref.shape[1]
    lanes = 128
    gpb = lanes // S5_GROUP
    for j in range(ds5 // lanes):
        for hi in range(t_chunk // gpb):
            outs = _seg_transpose([yg_ref[gpb * j + glo, :, lanes * hi:lanes * (hi + 1)].astype(_F32)
                                   for glo in range(gpb)])
            for tlo in range(gpb):
                y_scr[j, pl.ds(gpb * hi + tlo, n_chunk, stride=t_chunk), :] = outs[tlo]
    y_s5 = jnp.concatenate([y_scr[j] for j in range(ds5 // lanes)], axis=-1)

    y = y_s5 + d_ref[...] * u_ref[...]
    y = jax.nn.gelu(y, approximate=True)
    y = y * jax.nn.sigmoid(_dot(y.astype(_BF), wglu_ref[...]))
    yn = _rms(y, gs_ref[...]).astype(_BF)
    mix = _dot(yn, wo_ref[0:ds5, :]) + _dot(fn_ref[...].astype(_BF), wo_ref[ds5:, :])
    x1 = x_ref[...] + g1_ref[...] * mix
    x1_ref[...] = x1
    t = _rms(x1, n2_ref[...]) * (1.0 + sc_ref[...]) + sh_ref[...]
    t_hi = t.astype(_BF)
    t_ref[...] = t_hi

    ng, epg = N_EXPERT_GROUPS, EXPERTS_PER_GROUP
    t_lo = (t - t_hi.astype(_F32)).astype(_BF)
    logits = (_dot(t_hi, wr_ref[0]) + _dot(t_lo, wr_ref[0]) + _dot(t_hi, wr_ref[1])) + br_ref[...]
    lane = lax.broadcasted_iota(jnp.int32, logits.shape, 1)
    neg = jnp.float32(-jnp.inf)
    big = jnp.int32(1 << 20)
    gl = jnp.where(lane < ng, logits, neg)
    gmax = jnp.max(gl, axis=-1, keepdims=True)
    gidx = jnp.min(jnp.where(gl == gmax, lane, big), axis=-1, keepdims=True)
    gw = 1.0 / jnp.sum(jnp.where(lane < ng, jnp.exp(logits - gmax), 0.0), axis=-1, keepdims=True)
    lo = ng + gidx * epg
    el = jnp.where((lane >= lo) & (lane < lo + epg), logits, neg)
    v0 = jnp.max(el, axis=-1, keepdims=True)
    i0 = jnp.min(jnp.where(el == v0, lane, big), axis=-1, keepdims=True)
    el1 = jnp.where(lane == i0, neg, el)
    v1 = jnp.max(el1, axis=-1, keepdims=True)
    i1 = jnp.min(jnp.where(el1 == v1, lane, big), axis=-1, keepdims=True)
    p0 = 1.0 / (1.0 + jnp.exp(v1 - v0))
    w0 = gw * p0
    w1 = gw * (1.0 - p0)
    e0 = i0 - ng
    e1 = i1 - ng

    oh0 = lane == e0
    oh1 = lane == e1
    oh = (oh0 | oh1).astype(_F32)
    prefix = _dot(tri_ref[...], oh.astype(_BF)) + cnt_scr[...]
    r0 = jnp.sum(jnp.where(oh0, prefix, 0.0), axis=-1, keepdims=True)
    r1 = jnp.sum(jnp.where(oh1, prefix, 0.0), axis=-1, keepdims=True)
    cnt = cnt_scr[...] + jnp.sum(oh, axis=0, keepdims=True)
    cnt_scr[...] = cnt
    cnt_ref[...] = cnt

    vals = (e0.astype(_F32), e1.astype(_F32), w0, w1, r0, r1)
    rt = jnp.zeros(logits.shape, _F32)
    for k, v in enumerate(vals):
        rt = jnp.where(lane == k, v, rt)
    rt_ref[...] = rt
    rtt_ref[...] = rt.T[0:rtt_ref.shape[0], :]


@functools.lru_cache(maxsize=None)
def _strict_lower_ones(n):
    return np.tril(np.ones((n, n), np.float32), -1).astype(jnp.bfloat16)


def _post(x, u, y_g, fn, s5_d, wglu_bf, gs, wo_bf, g1, n2, sh2, sc2, wr, br, b0, nb, tm=512):
    b, l, d = x.shape
    ds5 = u.shape[-1]
    n_grp, _, w = y_g.shape
    kk = w // b
    tri = jnp.asarray(_strict_lower_ones(tm))
    tok = lambda w: pl.BlockSpec((None, tm, w), lambda i, j: (i + b0, j, 0))
    otok = lambda w: pl.BlockSpec((None, tm, w), lambda i, j: (i, j, 0))
    per_b = pl.BlockSpec((None, 1, d), lambda i, j: (i + b0, 0, 0))
    full = lambda a: pl.BlockSpec(a.shape, lambda i, j: (0,) * a.ndim)
    ygs = pl.BlockSpec((n_grp, tm // S5_CHUNK, kk), lambda i, j: (0, j, i + b0))
    return pl.pallas_call(
        _post_body,
        grid=(nb, l // tm),
        in_specs=[tok(d), tok(ds5), ygs, tok(fn.shape[-1]), full(s5_d), full(wglu_bf),
                  full(gs), full(wo_bf), per_b, full(n2), per_b, per_b, full(wr), full(br), full(tri)],
        out_specs=[otok(d), otok(d), otok(ROUTE_LANES),
                   pl.BlockSpec((None, ROUTE_ROWS, tm), lambda i, j: (i, 0, j)),
                   pl.BlockSpec((1, ROUTE_LANES), lambda i, j: (0, 0))],
        out_shape=[jax.ShapeDtypeStruct((nb, l, d), _F32),
                   jax.ShapeDtypeStruct((nb, l, d), _BF),
                   jax.ShapeDtypeStruct((nb, l, ROUTE_LANES), _F32),
                   jax.ShapeDtypeStruct((nb, ROUTE_ROWS, l), _F32),
                   jax.ShapeDtypeStruct((1, ROUTE_LANES), _F32)],
        scratch_shapes=[pltpu.VMEM((ds5 // 128, tm, 128), _F32),
                        pltpu.VMEM((1, ROUTE_LANES), _F32)],
        compiler_params=_cparams(("arbitrary", "arbitrary")),
        name="post_mixer",
    )(x, u, y_g, fn, s5_d, wglu_bf, gs, wo_bf, g1, n2, sh2, sc2, wr, br, tri)


def _moe_body(te_ref, nt_ref, xs_ref, wg_ref, wu_ref, wd_ref, o_ref, wgu_bf, wd_bf):
    i = pl.program_id(0)
    de = wg_ref.shape[-1]
    used = i < nt_ref[0]
    new_expert = (i == 0) | (te_ref[i] != te_ref[jnp.maximum(i - 1, 0)])

    @pl.when(used & new_expert)
    def _():
        wgu_bf[:, 0:de] = wg_ref[...].astype(_BF)
        wgu_bf[:, de:2 * de] = wu_ref[...].astype(_BF)
        wd_bf[...] = wd_ref[...].astype(_BF)

    @pl.when(used)
    def _():
        h = _dot(xs_ref[...], wgu_bf[...])
        hg, hu = h[:, 0:de], h[:, de:2 * de]
        a = hg * jax.nn.sigmoid(hg) * hu
        o_ref[...] = _dot(a.astype(_BF), wd_bf[...]).astype(o_ref.dtype)

    @pl.when(jnp.logical_not(used))
    def _():
        o_ref[...] = jnp.zeros_like(o_ref)


def _moe(tile_expert, n_tiles_used, xs, wg, wu, wd):
    nr, d = xs.shape
    tm = MOE_TM
    de = wg.shape[-1]
    grid_spec = pltpu.PrefetchScalarGridSpec(
        num_scalar_prefetch=2,
        grid=(nr // tm,),
        in_specs=[pl.BlockSpec((tm, d), lambda i, te, nt: (i, 0)),
                  pl.BlockSpec((None, d, de), lambda i, te, nt: (te[i], 0, 0)),
                  pl.BlockSpec((None, d, de), lambda i, te, nt: (te[i], 0, 0)),
                  pl.BlockSpec((None, de, d), lambda i, te, nt: (te[i], 0, 0))],
        out_specs=pl.BlockSpec((tm, d), lambda i, te, nt: (i, 0)),
        scratch_shapes=[pltpu.VMEM((d, 2 * de), _BF), pltpu.VMEM((de, d), _BF)],
    )
    return pl.pallas_call(
        _moe_body,
        grid_spec=grid_spec,
        out_shape=jax.ShapeDtypeStruct((nr, d), _BF),
        compiler_params=_cparams(("arbitrary",)),
        name="moe_experts",
    )(tile_expert, n_tiles_used, xs, wg, wu, wd)


def _final_body(g2_ref, gf_ref, *refs, n_part, nb):
    o_ref = refs[-1]
    i = pl.program_id(0)
    for part in range(n_part):
        x1_ref, y0_ref, y1_ref, rt_ref = refs[4 * part:4 * part + 4]

        @pl.when((i >= part * nb) & (i < (part + 1) * nb))
        def _():
            w0 = rt_ref[:, 2:3]
            w1 = rt_ref[:, 3:4]
            m = w0 * y0_ref[...].astype(_F32) + w1 * y1_ref[...].astype(_F32)
            o_ref[...] = _rms(x1_ref[...] + g2_ref[...] * m, gf_ref[...])


def _final(parts, g2, gf, tm=512):
    n_part = len(parts)
    nb, l, d = parts[0][0].shape
    nj = l // tm
    in_specs = [pl.BlockSpec((None, 1, d), lambda i, j: (i, 0, 0)),
                pl.BlockSpec((1, d), lambda i, j: (0, 0))]
    args = [g2, gf]
    for part, arrs in enumerate(parts):
        def imap(i, j, part=part):
            own = (i >= part * nb) & (i < (part + 1) * nb)
            return (jnp.clip(i - part * nb, 0, nb - 1), jnp.where(own, j, jnp.where(i < part * nb, 0, nj - 1)), 0)
        for a in arrs:
            in_specs.append(pl.BlockSpec((None, tm, a.shape[-1]), imap))
            args.append(a)
    return pl.pallas_call(
        functools.partial(_final_body, n_part=n_part, nb=nb),
        grid=(n_part * nb, nj),
        in_specs=in_specs,
        out_specs=pl.BlockSpec((None, tm, d), lambda i, j: (i, j, 0)),
        out_shape=jax.ShapeDtypeStruct((n_part * nb, l, d), _F32),
        compiler_params=_cparams(("parallel", "arbitrary")),
        name="final_norm",
    )(*args)


def _route_plan(eid, rank, tok_ids, counts, tm, nr):
    n_experts = counts.shape[0]
    n_pairs = eid.size
    padded = ((counts + tm - 1) // tm) * tm
    pad_end = jnp.cumsum(padded)
    pad_start = pad_end - padded
    raw_start = jnp.cumsum(counts) - counts
    pos = rank
    for e in range(n_experts):
        pos = pos + jnp.where(eid == e, pad_start[e], 0)
    tile_start = jnp.arange(nr // tm, dtype=jnp.int32) * tm
    tile_expert = jnp.sum(tile_start[:, None] >= pad_end[None, :], axis=-1)
    tile_expert = jnp.minimum(tile_expert, n_experts - 1).astype(jnp.int32)
    _, sorted_tok = lax.sort_key_val((eid * n_pairs + rank).reshape(-1), tok_ids.reshape(-1))
    sel = tile_expert[:, None] == jnp.arange(n_experts, dtype=jnp.int32)[None, :]
    per_tile = lambda v: jnp.repeat(jnp.sum(jnp.where(sel, v[None, :], 0), axis=-1), tm)
    off = jnp.arange(nr, dtype=jnp.int32) - per_tile(pad_start)
    valid = off < per_tile(counts)
    j = jnp.clip(per_tile(raw_start) + off, 0, n_pairs - 1)
    filler = jnp.arange(nr, dtype=jnp.int32) % (n_pairs // 2)
    row_token = jnp.where(valid, sorted_tok.at[j].get(mode="promise_in_bounds"), filler)
    n_used = (pad_end[-1:] // tm).astype(jnp.int32)
    return pos, tile_expert, n_used, row_token


def kernel(x, c, ctx, c_ctx, w_ada, b_ada, norm1_g, norm2_g, w_in, s5_lam_re, s5_lam_im, s5_b_re, s5_b_im, s5_c_re, s5_c_im, s5_log_step, s5_d, s5_w_glu, fourier_w, mix_norm_s5_g, mix_norm_f_g, w_out, moe_w_group, moe_b_group, moe_w_router, moe_b_router, moe_w_gate, moe_w_up, moe_w_down, final_norm_g):
    b, l, d = x.shape
    lc = ctx.shape[1]
    depth = w_ada.shape[0]
    assert depth == 1 and l == FFT_R * FFT_R and lc % S5_CHUNK == 0 and b % 8 == 0
    ds5 = s5_d.shape[-1]
    df = w_in.shape[-1] - ds5
    n_exp = moe_w_gate.shape[1]
    row = lambda a: a.reshape(1, -1)

    cond = jnp.concatenate([c, c_ctx[None, :], jnp.zeros((7, d), _F32)], axis=0)
    mod = _adaln(cond, w_ada[0], b_ada[0])
    sh1, sc1, g1, sh2, sc2, g2 = [mod[:b, i * d:(i + 1) * d].reshape(b, 1, d) for i in range(6)]
    csh1 = mod[b:b + 1, 0:d]
    csc1 = mod[b:b + 1, d:2 * d]

    w_in_bf = w_in[0].astype(_BF)
    m1 = jnp.asarray(_fft_stage1_matrix()).astype(_BF)
    m2 = jnp.asarray(_fft_stage2_matrices()).astype(_BF)
    perm = jnp.asarray(_chunk_row_perm())
    zs, gc = _inproj(x, sh1, sc1, row(norm1_g[0]), w_in_bf, m1)
    u_ctx = _inproj_ctx(ctx, csh1, csc1, row(norm1_g[0]), w_in_bf[:, :ds5], perm)
    u_lat = _s5_pack(zs, perm)

    fg = df // FOURIER_GROUPS
    cc = np.arange(fg)
    ang = 2.0 * np.pi * ((cc[:, None] * cc[None, :]) % fg) / fg
    scale = 1.0 / math.sqrt(l * fg)
    cw = jnp.einsum("cm,gmd->gcd", jnp.asarray(np.cos(ang) * scale, _F32), fourier_w[0], precision=_HI)
    sw = jnp.einsum("cm,gmd->gcd", jnp.asarray(np.sin(ang) * scale, _F32), fourier_w[0], precision=_HI)
    fn = _fft2(gc, m2, cw.astype(_BF), sw.astype(_BF), row(mix_norm_f_g[0]))

    ws, wy, at = _s5_params(s5_lam_re[0], s5_lam_im[0], s5_b_re[0], s5_b_im[0],
                            s5_c_re[0], s5_c_im[0], s5_log_step[0])
    y_g = _s5(u_ctx, u_lat, ws, wy, at, b)

    n_rt = N_EXPERT_GROUPS * (1 + EXPERTS_PER_GROUP)
    w_rt = jnp.concatenate([moe_w_group[0], moe_w_router[0].reshape(d, -1)], axis=-1)
    w_rt = jnp.pad(w_rt, ((0, 0), (0, ROUTE_LANES - n_rt)))
    w_rt_hi = w_rt.astype(_BF)
    w_rt = jnp.stack([w_rt_hi, (w_rt - w_rt_hi.astype(_F32)).astype(_BF)])
    b_rt =jnp.pad(jnp.concatenate([moe_b_group[0], moe_b_router[0].reshape(-1)]), (0, ROUTE_LANES - n_rt))
    tm = MOE_TM
    nb = b // MOE_PARTS
    n_tok = nb * l
    nr = 2 * n_tok + n_exp * tm
    take_rows = lambda a, idx: a.at[idx].get(mode="promise_in_bounds")
    wglu_bf, wo_bf = s5_w_glu[0].astype(_BF), w_out[0].astype(_BF)
    parts = []
    for part in range(MOE_PARTS):
        x1, tmod, rt, rtt, cnt = _post(x, zs, y_g, fn, row(s5_d[0]), wglu_bf, row(mix_norm_s5_g[0]), wo_bf,
                                       g1, row(norm2_g[0]), sh2, sc2, w_rt, row(b_rt), part * nb, nb)
        rec = rtt.transpose(1, 0, 2).reshape(ROUTE_ROWS, n_tok)
        eid = rec[0:2].astype(jnp.int32)
        rank = rec[4:6].astype(jnp.int32)
        tok_ids = jnp.broadcast_to(jnp.arange(n_tok, dtype=jnp.int32), (2, n_tok))
        pos, tile_expert, n_used, row_token = _route_plan(eid, rank, tok_ids,
                                                          cnt[0, :n_exp].astype(jnp.int32), tm, nr)
        xs = take_rows(tmod.reshape(n_tok, d), row_token)
        ys = _moe(tile_expert, n_used, xs, moe_w_gate[0], moe_w_up[0], moe_w_down[0])
        y0 = take_rows(ys, pos[0]).reshape(nb, l, d)
        y1 = take_rows(ys, pos[1]).reshape(nb, l, d)
        parts.append((x1, y0, y1, rt))
    return _final(parts, g2, row(final_norm_g))
```

```python
import functools
import math

import numpy as np
import jax
import jax.numpy as jnp
from jax import lax
from jax.experimental import pallas as pl
from jax.experimental.pallas import tpu as pltpu

EPS = 1e-6
S5_GROUP = 16
S5_STATE = 64
S5_CHUNK = 16
S5_SCAN_GROUPS = 2
FOURIER_GROUPS = 4
N_EXPERT_GROUPS = 4
EXPERTS_PER_GROUP = 8
FFT_R = 64
FFT_BLK = 8
MOE_TM = 512
MOE_PARTS = 1
POST_SUBTILES = 4
MOE_SUBTILES = 4
ROUTE_LANES = 128
ROUTE_ROWS = 8
VMEM_LIMIT = 56 * 1024 * 1024

_HI = lax.Precision.HIGHEST
_BF = jnp.bfloat16
_F32 = jnp.float32


def _cparams(sem):
    return pltpu.CompilerParams(dimension_semantics=sem, vmem_limit_bytes=VMEM_LIMIT)


def _dot(a, b):
    return jnp.dot(a, b, preferred_element_type=_F32)


def _rms(x, g):
    return x * lax.rsqrt(jnp.mean(x * x, axis=-1, keepdims=True) + EPS) * g


def _adaln_body(c_ref, w_ref, b_ref, o_ref):
    c = c_ref[...]
    a = c * jax.nn.sigmoid(c)
    o_ref[...] = jnp.dot(a, w_ref[...], preferred_element_type=_F32, precision=_HI) + b_ref[...]


def _adaln(cond, w, b):
    m, d = cond.shape
    n = w.shape[1]
    tn = 768
    return pl.pallas_call(
        _adaln_body,
        grid=(n // tn,),
        in_specs=[pl.BlockSpec((m, d), lambda j: (0, 0)),
                  pl.BlockSpec((d, tn), lambda j: (0, j)),
                  pl.BlockSpec((1, tn), lambda j: (0, j))],
        out_specs=pl.BlockSpec((m, tn), lambda j: (0, j)),
        out_shape=jax.ShapeDtypeStruct((m, n), _F32),
        compiler_params=_cparams(("arbitrary",)),
        name="adaln",
    )(cond, w, b.reshape(1, n))


def _inproj_body(x_ref, sh_ref, sc_ref, g_ref, w_ref, m1_ref, zs_ref, gc_ref):
    r, blk, d = x_ref.shape
    x = x_ref[...].reshape(r * blk, d)
    h = _rms(x, g_ref[...]) * (1.0 + sc_ref[...]) + sh_ref[...]
    z = _dot(h.astype(_BF), w_ref[...])
    ds5 = zs_ref.shape[-1]
    zs_ref[...] = z[:, :ds5].reshape(r, blk, ds5)
    v = z[:, ds5:].astype(_BF)
    g1 = _dot(m1_ref[...], v)
    gc_ref[...] = g1.astype(_BF).reshape(r, 2 * blk, v.shape[-1])


def _inproj(x, sh, sc, g, w_bf, m1):
    b, l, d = x.shape
    r, blk = FFT_R, FFT_BLK
    nj = r // blk
    dmix = w_bf.shape[1]
    ds5 = dmix // 2
    df = dmix - ds5
    x4 = x.reshape(b, r, r, d)
    zs, gc = pl.pallas_call(
        _inproj_body,
        grid=(b, nj),
        in_specs=[pl.BlockSpec((None, r, blk, d), lambda i, j: (i, 0, j, 0)),
                  pl.BlockSpec((None, 1, d), lambda i, j: (i, 0, 0)),
                  pl.BlockSpec((None, 1, d), lambda i, j: (i, 0, 0)),
                  pl.BlockSpec((1, d), lambda i, j: (0, 0)),
                  pl.BlockSpec((d, dmix), lambda i, j: (0, 0)),
                  pl.BlockSpec(m1.shape, lambda i, j: (0, 0))],
        out_specs=[pl.BlockSpec((None, r, blk, ds5), lambda i, j: (i, 0, j, 0)),
                   pl.BlockSpec((None, r, 2 * blk, df), lambda i, j: (i, 0, j, 0))],
        out_shape=[jax.ShapeDtypeStruct((b, r, r, ds5), _F32),
                   jax.ShapeDtypeStruct((b, r, 2 * r, df), _BF)],
        compiler_params=_cparams(("parallel", "arbitrary")),
        name="inproj",
    )(x4, sh, sc, g, w_bf, m1)
    return zs.reshape(b, l, ds5), gc


def _seg_transpose(arrs):
    n = len(arrs)
    seg = lax.broadcasted_iota(jnp.int32, arrs[0].shape, 1) // S5_GROUP
    d = n // 2
    while d >= 1:
        keep = (seg & d) == 0
        new = list(arrs)
        for i in range(n):
            if i & d == 0:
                a, b = arrs[i], arrs[i + d]
                new[i] = jnp.where(keep, a, pltpu.roll(b, S5_GROUP * d, 1))
                new[i + d] = jnp.where(keep, pltpu.roll(a, 128 - S5_GROUP * d, 1), b)
        arrs = new
        d //= 2
    return arrs


@functools.lru_cache(maxsize=None)
def _chunk_row_perm():
    t = S5_CHUNK
    p = np.zeros((t * t, t * t), np.float32)
    for c in range(t):
        for tl in range(t):
            p[tl * t + c, c * t + tl] = 1.0
    return p.astype(jnp.bfloat16)


def _pack_chunks(z_bf, p_ref, u_ref):
    t = S5_CHUNK
    nsub = z_bf.shape[0] // (t * t)
    pieces = [_dot(p_ref[...], z_bf[s * t * t:(s + 1) * t * t, :]) for s in range(nsub)]
    a = []
    for tl in range(t):
        rows = [pc[tl * t:(tl + 1) * t, :] for pc in pieces]
        a.append(rows[0] if nsub == 1 else jnp.concatenate(rows, axis=0))
    lanes = 128
    gpb = lanes // S5_GROUP
    for j in range(z_bf.shape[1] // lanes):
        for hi in range(t // gpb):
            outs = _seg_transpose([a[gpb * hi + tlo][:, lanes * j:lanes * (j + 1)] for tlo in range(gpb)])
            for glo in range(gpb):
                u_ref[gpb * j + glo, :, lanes * hi:lanes * (hi + 1)] = outs[glo].astype(u_ref.dtype)


def _inproj_ctx_body(x_ref, sh_ref, sc_ref, g_ref, w_ref, p_ref, u_ref):
    h = _rms(x_ref[...], g_ref[...]) * (1.0 + sc_ref[...]) + sh_ref[...]
    z = _dot(h.astype(_BF), w_ref[...])
    _pack_chunks(z.astype(_BF), p_ref, u_ref)


def _inproj_ctx(ctx, sh, sc, g, w_s5_bf, perm):
    b, lc, d = ctx.shape
    ds5 = w_s5_bf.shape[1]
    n_grp = ds5 // S5_GROUP
    kk = S5_CHUNK * S5_GROUP
    return pl.pallas_call(
        _inproj_ctx_body,
        grid=(b,),
        in_specs=[pl.BlockSpec((None, lc, d), lambda i: (i, 0, 0)),
                  pl.BlockSpec((1, d), lambda i: (0, 0)),
                  pl.BlockSpec((1, d), lambda i: (0, 0)),
                  pl.BlockSpec((1, d), lambda i: (0, 0)),
                  pl.BlockSpec((d, ds5), lambda i: (0, 0)),
                  pl.BlockSpec(perm.shape, lambda i: (0, 0))],
        out_specs=pl.BlockSpec((n_grp, lc // S5_CHUNK, kk), lambda i: (0, 0, i)),
        out_shape=jax.ShapeDtypeStruct((n_grp, lc // S5_CHUNK, b * kk), _BF),
        compiler_params=_cparams(("arbitrary",)),
        name="inproj_ctx",
    )(ctx, sh, sc, g, w_s5_bf, perm)


def _s5_pack_body(z_ref, p_ref, u_ref):
    _pack_chunks(z_ref[...].astype(_BF), p_ref, u_ref)


def _s5_pack(zs, perm, tok=2048):
    b, l, ds5 = zs.shape
    n_grp = ds5 // S5_GROUP
    kk = S5_CHUNK * S5_GROUP
    return pl.pallas_call(
        _s5_pack_body,
        grid=(b, l // tok),
        in_specs=[pl.BlockSpec((None, tok, ds5), lambda i, j: (i, j, 0)),
                  pl.BlockSpec(perm.shape, lambda i, j: (0, 0))],
        out_specs=pl.BlockSpec((n_grp, tok // S5_CHUNK, kk), lambda i, j: (0, j, i)),
        out_shape=jax.ShapeDtypeStruct((n_grp, l // S5_CHUNK, b * kk), _BF),
        compiler_params=_cparams(("parallel", "arbitrary")),
        name="s5_pack",
    )(zs, perm)


@functools.lru_cache(maxsize=None)
def _fft_stage1_matrix():
    r, blk = FFT_R, FFT_BLK
    k1 = np.arange(r)[:, None]
    l1 = np.arange(r)[None, :]
    ang = 2.0 * np.pi * ((k1 * l1) % r) / r
    f = np.stack([np.cos(ang), -np.sin(ang)], axis=1)
    m = np.einsum("kal,pq->kaplq", f, np.eye(blk))
    return m.reshape(r * 2 * blk, r * blk).astype(np.float32)


@functools.lru_cache(maxsize=None)
def _fft_stage2_matrices():
    r, blk = FFT_R, FFT_BLK
    n = r * r
    nt = r // blk
    l2 = np.arange(r)
    k2 = np.arange(r)
    out = np.zeros((nt, 2, r, blk, blk, r // blk, 2, blk), np.float32)
    for i in range(nt):
        for kl in range(blk):
            k1 = blk * i + kl
            ang = 2.0 * np.pi * (((k2[:, None] * l2[None, :] * r) + l2[None, :] * k1) % n) / n
            tr, ti = np.cos(ang), -np.sin(ang)
            tr = tr.reshape(r, r // blk, blk)
            ti = ti.reshape(r, r // blk, blk)
            out[i, 0, :, kl, kl, :, 0, :] = tr
            out[i, 0, :, kl, kl, :, 1, :] = -ti
            out[i, 1, :, kl, kl, :, 0, :] = ti
            out[i, 1, :, kl, kl, :, 1, :] = tr
    return out.reshape(nt, 2 * r * blk, blk * 2 * r)


def _fft2_body(gc_ref, m2_ref, cw_ref, sw_ref, g_ref, o_ref):
    kb, rr, df = gc_ref.shape
    gc = gc_ref[...].reshape(kb * rr, df)
    x = _dot(m2_ref[...], gc)
    half = x.shape[0] // 2
    xr = x[:half].astype(_BF)
    xi = x[half:].astype(_BF)
    ng = cw_ref.shape[0]
    fg = df // ng
    parts = []
    for g in range(ng):
        sl = slice(g * fg, (g + 1) * fg)
        parts.append(_dot(xr[:, sl], cw_ref[g]) + _dot(xi[:, sl], sw_ref[g]))
    f = jnp.concatenate(parts, axis=-1)
    fn = _rms(f, g_ref[...])
    o_ref[...] = fn.reshape(o_ref.shape)


def _fft2(gc, m2, cw, sw, gf):
    b, r, rr, df = gc.shape
    blk = FFT_BLK
    nt = r // blk
    out = pl.pallas_call(
        _fft2_body,
        grid=(b, nt),
        in_specs=[pl.BlockSpec((None, blk, rr, df), lambda i, j: (i, j, 0, 0)),
                  pl.BlockSpec((None,) + m2.shape[1:], lambda i, j: (j, 0, 0)),
                  pl.BlockSpec(cw.shape, lambda i, j: (0, 0, 0)),
                  pl.BlockSpec(sw.shape, lambda i, j: (0, 0, 0)),
                  pl.BlockSpec((1, df), lambda i, j: (0, 0))],
        out_specs=pl.BlockSpec((None, r, blk, df), lambda i, j: (i, 0, j, 0)),
        out_shape=jax.ShapeDtypeStruct((b, r, r, df), _F32),
        compiler_params=_cparams(("parallel", "arbitrary")),
        name="fft2",
    )(gc, m2, cw, sw, gf)
    return out.reshape(b, r * r, df)


def _s5_body(uc_ref, ul_ref, ws_ref, wy_ref, at_ref, y_ref,
             s_re, s_im, ha_re, hb_re, ha_im, hb_im, *, nb):
    p = S5_STATE
    gs, n_ctx, _ = uc_ref.shape
    n_lat = ul_ref.shape[1]
    kk = ws_ref.shape[1]
    for gi in range(gs):
        ws = ws_ref[gi]
        for b in range(nb):
            sl = slice(b * kk, (b + 1) * kk)
            sc = _dot(uc_ref[gi, :, sl], ws)
            sl_ = _dot(ul_ref[gi, :, sl], ws)
            s_re[gi, pl.ds(b, n_ctx, stride=nb), :] = sc[:, :2 * p]
            s_im[gi, pl.ds(b, n_ctx, stride=nb), :] = sc[:, 2 * p:]
            s_re[gi, pl.ds(n_ctx * nb + b, n_lat, stride=nb), :] = sl_[:, :2 * p]
            s_im[gi, pl.ds(n_ctx * nb + b, n_lat, stride=nb), :] = sl_[:, 2 * p:]

    a_re = [at_ref[gi, 0:1, :] for gi in range(gs)]
    a_im = [at_ref[gi, 1:2, :] for gi in range(gs)]
    is_f = lax.broadcasted_iota(jnp.int32, (nb, 2 * p), 1) < p
    n_all = n_ctx + n_lat

    def load(gi, cf, cr):
        rf = pl.multiple_of(cf * nb, nb)
        rr = pl.multiple_of(cr * nb, nb)
        return (jnp.where(is_f, s_re[gi, pl.ds(rf, nb), :], s_re[gi, pl.ds(rr, nb), :]),
                jnp.where(is_f, s_im[gi, pl.ds(rf, nb), :], s_im[gi, pl.ds(rr, nb), :]))

    def advance(gi, h_re, h_im, x_re, x_im):
        return (a_re[gi] * h_re - a_im[gi] * h_im + x_re, a_re[gi] * h_im + a_im[gi] * h_re + x_im)

    def ctx_step(step, carry):
        out = []
        for gi in range(gs):
            x_re, x_im = load(gi, step, n_ctx - 1 - step)
            out.append(advance(gi, *carry[gi], x_re, x_im))
        return tuple(out)

    def lat_step(step, carry):
        rf = pl.multiple_of(step * nb, nb)
        rr = pl.multiple_of((n_lat - 1 - step) * nb, nb)
        out = []
        for gi in range(gs):
            h_re, h_im = carry[gi]
            ha_re[gi, pl.ds(rf, nb), :] = h_re
            hb_re[gi, pl.ds(rr, nb), :] = h_re
            ha_im[gi, pl.ds(rf, nb), :] = h_im
            hb_im[gi, pl.ds(rr, nb), :] = h_im
            x_re, x_im = load(gi, n_ctx + step, n_all - 1 - step)
            out.append(advance(gi, h_re, h_im, x_re, x_im))
        return tuple(out)

    zero = jnp.zeros((nb, 2 * p), _F32)
    carry = lax.fori_loop(0, n_ctx, ctx_step, tuple((zero, zero) for _ in range(gs)), unroll=2)
    lax.fori_loop(0, n_lat, lat_step, carry, unroll=2)

    for gi in range(gs):
        wy = wy_ref[gi]
        for b in range(nb):
            rows = pl.ds(b, n_lat, stride=nb)
            hin = jnp.concatenate([ha_re[gi, rows, :], hb_re[gi, rows, :],
                                   ha_im[gi, rows, :], hb_im[gi, rows, :]], axis=-1)
            lhs = jnp.concatenate([ul_ref[gi, :, b * kk:(b + 1) * kk], hin.astype(_BF)], axis=-1)
            y_ref[gi, :, b * kk:(b + 1) * kk] = _dot(lhs, wy).astype(y_ref.dtype)


def _s5(u_ctx, u_lat, ws, wy, at, nb):
    ng, n_ctx, w = u_ctx.shape
    n_lat = u_lat.shape[1]
    gs = S5_SCAN_GROUPS
    body = functools.partial(_s5_body, nb=nb)
    hs = pltpu.VMEM((gs, n_lat * nb, 2 * S5_STATE), _F32)
    ss = pltpu.VMEM((gs, (n_ctx + n_lat) * nb, 2 * S5_STATE), _F32)
    grp = lambda a: pl.BlockSpec((gs,) + a.shape[1:], lambda g: (g, 0, 0))
    return pl.pallas_call(
        body,
        grid=(ng // gs,),
        in_specs=[grp(u_ctx), grp(u_lat), grp(ws), grp(wy), grp(at)],
        out_specs=pl.BlockSpec((gs, n_lat, w), lambda g: (g, 0, 0)),
        out_shape=jax.ShapeDtypeStruct((ng, n_lat, w), _BF),
        scratch_shapes=[ss, ss, hs, hs, hs, hs],
        compiler_params=_cparams(("parallel",)),
        name="s5_scan",
    )(u_ctx, u_lat, ws, wy, at)


def _s5_param_body(lr_ref, li_ref, ls_ref, btr_ref, bti_ref, cr_ref, ci_ref, ws_ref, wy_ref, at_ref):
    t, hh, p = S5_CHUNK, S5_GROUP, S5_STATE
    kk = t * hh
    lr, li = lr_ref[...], li_ref[...]
    dt = jnp.exp(ls_ref[...])
    ldt, idt = lr * dt, li * dt
    mag = jnp.exp(ldt)
    ab_re, ab_im = mag * jnp.cos(idt), mag * jnp.sin(idt)
    den = lr * lr + li * li
    nr, ni = ab_re - 1.0, ab_im
    q_re = (nr * lr + ni * li) / den
    q_im = (ni * lr - nr * li) / den
    kf = lax.broadcasted_iota(jnp.int32, (2 * t, p), 0).astype(_F32)

    ap_re, ap_im, bb_re, bb_im = [], [], [], []
    for d in range(2):
        pm = jnp.exp(kf * ldt[d:d + 1, :])
        ap_re.append(pm * jnp.cos(kf * idt[d:d + 1, :]))
        ap_im.append(pm * jnp.sin(kf * idt[d:d + 1, :]))
        bb_re.append(q_re[d:d + 1, :] * btr_ref[d] - q_im[d:d + 1, :] * bti_ref[d])
        bb_im.append(q_re[d:d + 1, :] * bti_ref[d] + q_im[d:d + 1, :] * btr_ref[d])

    def rep(a, ks):
        return jnp.concatenate([jnp.broadcast_to(a[k:k + 1, :], (hh, p)) for k in ks], axis=0)

    def tile(a, n):
        return jnp.concatenate([a] * n, axis=0)

    def cmul(xr, xi, yr, yi):
        return xr * yr - xi * yi, xr * yi + xi * yr

    ks_f = list(range(t + 1))
    ks_r = list(range(t, -1, -1))
    caf_re, caf_im = cmul(tile(cr_ref[0], t + 1), tile(ci_ref[0], t + 1), rep(ap_re[0], ks_f), rep(ap_im[0], ks_f))
    car_re, car_im = cmul(tile(cr_ref[1], t + 1), tile(ci_ref[1], t + 1), rep(ap_re[1], ks_r), rep(ap_im[1], ks_r))

    def lag_blocks(bre, bim, ca_re, ca_im):
        lhs = jnp.concatenate([bre, bim], axis=1)
        rhs = jnp.concatenate([ca_re, -ca_im], axis=1)
        return lax.dot_general(lhs, rhs, (((1,), (1,)), ((), ())), precision=_HI,
                               preferred_element_type=_F32)

    w_f = lag_blocks(bb_re[0], bb_im[0], caf_re[:kk], caf_im[:kk])
    w_r = lag_blocks(bb_re[1], bb_im[1], car_re[hh:], car_im[hh:])
    lane = lax.broadcasted_iota(jnp.int32, (hh, kk), 1)
    for tlp in range(t):
        sf = hh * tlp
        sr = hh * (t - 1 - tlp)
        a = w_f if sf == 0 else jnp.where(lane >= sf, pltpu.roll(w_f, sf, 1), 0.0)
        b = w_r if sr == 0 else jnp.where(lane < kk - sr, pltpu.roll(w_r, kk - sr, 1), 0.0)
        wy_ref[hh * tlp:hh * (tlp + 1), :] = (a + b).astype(wy_ref.dtype)

    zeros = jnp.zeros((p, kk), _F32)
    blocks = [caf_re[hh:].T, zeros, zeros, car_re[:kk].T, (-caf_im[hh:]).T, zeros, zeros, (-car_im[:kk]).T]
    for i, blk in enumerate(blocks):
        wy_ref[kk + p * i:kk + p * (i + 1), :] = blk.astype(wy_ref.dtype)

    pf = list(range(t - 1, -1, -1))
    pr = list(range(t))
    f_re, f_im = cmul(rep(ap_re[0], pf), rep(ap_im[0], pf), tile(bb_re[0], t), tile(bb_im[0], t))
    r_re, r_im = cmul(rep(ap_re[1], pr), rep(ap_im[1], pr), tile(bb_re[1], t), tile(bb_im[1], t))
    ws_ref[...] = jnp.concatenate([f_re, r_re, f_im, r_im], axis=1).astype(ws_ref.dtype)

    a_re = jnp.concatenate([ap_re[0][t:t + 1, :], ap_re[1][t:t + 1, :]], axis=1)
    a_im = jnp.concatenate([ap_im[0][t:t + 1, :], ap_im[1][t:t + 1, :]], axis=1)
    row = lax.broadcasted_iota(jnp.int32, at_ref.shape, 0)
    at_ref[...] = jnp.where(row == 0, a_re, jnp.where(row == 1, a_im, 0.0))


def _s5_params(lam_re, lam_im, b_re, b_im, c_re, c_im, log_step):
    _, ng, p = lam_re.shape
    hh = b_re.shape[-1]
    kk = S5_CHUNK * hh
    gd = lambda a: jnp.swapaxes(a.astype(_F32), 0, 1)
    args = (gd(lam_re), gd(lam_im), gd(log_step)[..., None],
            jnp.swapaxes(gd(b_re), 2, 3), jnp.swapaxes(gd(b_im), 2, 3), gd(c_re), gd(c_im))
    spec = lambda a: pl.BlockSpec((None,) + a.shape[1:], lambda g: (g,) + (0,) * (a.ndim - 1))
    return pl.pallas_call(
        _s5_param_body,
        grid=(ng,),
        in_specs=[spec(a) for a in args],
        out_specs=[pl.BlockSpec((None, kk, 4 * p), lambda g: (g, 0, 0)),
                   pl.BlockSpec((None, kk + 8 * p, kk), lambda g: (g, 0, 0)),
                   pl.BlockSpec((None, 8, 2 * p), lambda g: (g, 0, 0))],
        out_shape=[jax.ShapeDtypeStruct((ng, kk, 4 * p), _BF),
                   jax.ShapeDtypeStruct((ng, kk + 8 * p, kk), _BF),
                   jax.ShapeDtypeStruct((ng, 8, 2 * p), _F32)],
        compiler_params=_cparams(("parallel",)),
        name="s5_params",
    )(*args)


def _post_body(x_ref, u_ref, yg_ref, fn_ref, d_ref, wglu_ref, gs_ref, wo_ref, g1_ref,
               n2_ref, sh_ref, sc_ref, wr_ref, br_ref, tri_ref, x1_ref, t_ref, rt_ref, rtt_ref, cnt_ref,
               y_scr, cnt_scr):
    ds5 = u_ref.shape[-1]
    first = (pl.program_id(0) == 0) & (pl.program_id(1) == 0)

    @pl.when(first)
    def _():
        cnt_scr[...] = jnp.zeros_like(cnt_scr)

    t_chunk = S5_CHUNK
    n_chunk = yg_ref.shape[1]
    lanes = 128
    gpb = lanes // S5_GROUP
    for j in range(ds5 // lanes):
        for hi in range(t_chunk // gpb):
            outs = _seg_transpose([yg_ref[gpb * j + glo, :, lanes * hi:lanes * (hi + 1)].astype(_F32)
                                   for glo in range(gpb)])
            for tlo in range(gpb):
                y_scr[j, pl.ds(gpb * hi + tlo, n_chunk, stride=t_chunk), :] = outs[tlo]
    tm = x_ref.shape[0]
    sub = tm // POST_SUBTILES
    parts = []
    for s in range(POST_SUBTILES):
        r = slice(s * sub, (s + 1) * sub)
        y_s5 = jnp.concatenate([y_scr[j, r, :] for j in range(ds5 // lanes)], axis=-1)
        y = y_s5 + d_ref[...] * u_ref[r, :]
        y = jax.nn.gelu(y, approximate=True)
        y = y * jax.nn.sigmoid(_dot(y.astype(_BF), wglu_ref[...]))
        yn = _rms(y, gs_ref[...]).astype(_BF)
        mix = _dot(yn, wo_ref[0:ds5, :]) + _dot(fn_ref[r, :].astype(_BF), wo_ref[ds5:, :])
        x1 = x_ref[r, :] + g1_ref[...] * mix
        x1_ref[r, :] = x1
        t = _rms(x1, n2_ref[...]) * (1.0 + sc_ref[...]) + sh_ref[...]
        t_hi = t.astype(_BF)
        t_ref[r, :] = t_hi
        t_lo = (t - t_hi.astype(_F32)).astype(_BF)
        parts.append((_dot(t_hi, wr_ref[0]) + _dot(t_lo, wr_ref[0]) + _dot(t_hi, wr_ref[1])) + br_ref[...])
    logits = jnp.concatenate(parts, axis=0)

    ng, epg = N_EXPERT_GROUPS, EXPERTS_PER_GROUP
    lane = lax.broadcasted_iota(jnp.int32, logits.shape, 1)
    neg = jnp.float32(-jnp.inf)
    big = jnp.int32(1 << 20)
    gl = jnp.where(lane < ng, logits, neg)
    gmax = jnp.max(gl, axis=-1, keepdims=True)
    gidx = jnp.min(jnp.where(gl == gmax, lane, big), axis=-1, keepdims=True)
    gw = 1.0 / jnp.sum(jnp.where(lane < ng, jnp.exp(logits - gmax), 0.0), axis=-1, keepdims=True)
    lo = ng + gidx * epg
    el = jnp.where((lane >= lo) & (lane < lo + epg), logits, neg)
    v0 = jnp.max(el, axis=-1, keepdims=True)
    i0 = jnp.min(jnp.where(el == v0, lane, big), axis=-1, keepdims=True)
    el1 = jnp.where(lane == i0, neg, el)
    v1 = jnp.max(el1, axis=-1, keepdims=True)
    i1 = jnp.min(jnp.where(el1 == v1, lane, big), axis=-1, keepdims=True)
    p0 = 1.0 / (1.0 + jnp.exp(v1 - v0))
    w0 = gw * p0
    w1 = gw * (1.0 - p0)
    e0 = i0 - ng
    e1 = i1 - ng

    oh0 = lane == e0
    oh1 = lane == e1
    oh = (oh0 | oh1).astype(_F32)
    prefix = _dot(tri_ref[...], oh.astype(_BF)) + cnt_scr[...]
    r0 = jnp.sum(jnp.where(oh0, prefix, 0.0), axis=-1, keepdims=True)
    r1 = jnp.sum(jnp.where(oh1, prefix, 0.0), axis=-1, keepdims=True)
    cnt = cnt_scr[...] + jnp.sum(oh, axis=0, keepdims=True)
    cnt_scr[...] = cnt
    cnt_ref[...] = cnt

    vals = (e0.astype(_F32), e1.astype(_F32), w0, w1, r0, r1)
    rt = jnp.zeros(logits.shape, _F32)
    for k, v in enumerate(vals):
        rt = jnp.where(lane == k, v, rt)
    rt_ref[...] = rt
    rtt_ref[...] = rt.T[0:rtt_ref.shape[0], :]


@functools.lru_cache(maxsize=None)
def _strict_lower_ones(n):
    return np.tril(np.ones((n, n), np.float32), -1).astype(jnp.bfloat16)


def _post(x, u, y_g, fn, s5_d, wglu_bf, gs, wo_bf, g1, n2, sh2, sc2, wr, br, b0, nb, tm=512):
    b, l, d = x.shape
    ds5 = u.shape[-1]
    n_grp, _, w = y_g.shape
    kk = w // b
    tri = jnp.asarray(_strict_lower_ones(tm))
    tok = lambda w: pl.BlockSpec((None, tm, w), lambda i, j: (i + b0, j, 0))
    otok = lambda w: pl.BlockSpec((None, tm, w), lambda i, j: (i, j, 0))
    per_b = pl.BlockSpec((None, 1, d), lambda i, j: (i + b0, 0, 0))
    full = lambda a: pl.BlockSpec(a.shape, lambda i, j: (0,) * a.ndim)
    ygs = pl.BlockSpec((n_grp, tm // S5_CHUNK, kk), lambda i, j: (0, j, i + b0))
    return pl.pallas_call(
        _post_body,
        grid=(nb, l // tm),
        in_specs=[tok(d), tok(ds5), ygs, tok(fn.shape[-1]), full(s5_d), full(wglu_bf),
                  full(gs), full(wo_bf), per_b, full(n2), per_b, per_b, full(wr), full(br), full(tri)],
        out_specs=[otok(d), otok(d), otok(ROUTE_LANES),
                   pl.BlockSpec((None, ROUTE_ROWS, tm), lambda i, j: (i, 0, j)),
                   pl.BlockSpec((1, ROUTE_LANES), lambda i, j: (0, 0))],
        out_shape=[jax.ShapeDtypeStruct((nb, l, d), _F32),
                   jax.ShapeDtypeStruct((nb, l, d), _BF),
                   jax.ShapeDtypeStruct((nb, l, ROUTE_LANES), _F32),
                   jax.ShapeDtypeStruct((nb, ROUTE_ROWS, l), _F32),
                   jax.ShapeDtypeStruct((1, ROUTE_LANES), _F32)],
        scratch_shapes=[pltpu.VMEM((ds5 // 128, tm, 128), _F32),
                        pltpu.VMEM((1, ROUTE_LANES), _F32)],
        compiler_params=_cparams(("arbitrary", "arbitrary")),
        name="post_mixer",
    )(x, u, y_g, fn, s5_d, wglu_bf, gs, wo_bf, g1, n2, sh2, sc2, wr, br, tri)


def _moe_body(te_ref, nt_ref, xs_ref, wg_ref, wu_ref, wd_ref, o_ref, wgu_bf, wd_bf):
    i = pl.program_id(0)
    de = wg_ref.shape[-1]
    used = i < nt_ref[0]
    new_expert = (i == 0) | (te_ref[i] != te_ref[jnp.maximum(i - 1, 0)])

    @pl.when(used & new_expert)
    def _():
        wgu_bf[:, 0:de] = wg_ref[...].astype(_BF)
        wgu_bf[:, de:2 * de] = wu_ref[...].astype(_BF)
        wd_bf[...] = wd_ref[...].astype(_BF)

    @pl.when(used)
    def _():
        sub = xs_ref.shape[0] // MOE_SUBTILES
        for s in range(MOE_SUBTILES):
            r = slice(s * sub, (s + 1) * sub)
            h = _dot(xs_ref[r, :], wgu_bf[...])
            hg, hu = h[:, 0:de], h[:, de:2 * de]
            a = hg * jax.nn.sigmoid(hg) * hu
            o_ref[r, :] = _dot(a.astype(_BF), wd_bf[...]).astype(o_ref.dtype)

    @pl.when(jnp.logical_not(used))
    def _():
        o_ref[...] = jnp.zeros_like(o_ref)


def _moe(tile_expert, n_tiles_used, xs, wg, wu, wd):
    nr, d = xs.shape
    tm = MOE_TM
    de = wg.shape[-1]
    grid_spec = pltpu.PrefetchScalarGridSpec(
        num_scalar_prefetch=2,
        grid=(nr // tm,),
        in_specs=[pl.BlockSpec((tm, d), lambda i, te, nt: (i, 0)),
                  pl.BlockSpec((None, d, de), lambda i, te, nt: (te[i], 0, 0)),
                  pl.BlockSpec((None, d, de), lambda i, te, nt: (te[i], 0, 0)),
                  pl.BlockSpec((None, de, d), lambda i, te, nt: (te[i], 0, 0))],
        out_specs=pl.BlockSpec((tm, d), lambda i, te, nt: (i, 0)),
        scratch_shapes=[pltpu.VMEM((d, 2 * de), _BF), pltpu.VMEM((de, d), _BF)],
    )
    return pl.pallas_call(
        _moe_body,
        grid_spec=grid_spec,
        out_shape=jax.ShapeDtypeStruct((nr, d), _BF),
        compiler_params=_cparams(("arbitrary",)),
        name="moe_experts",
    )(tile_expert, n_tiles_used, xs, wg, wu, wd)


def _final_body(g2_ref, gf_ref, *refs, n_part, nb):
    o_ref = refs[-1]
    i = pl.program_id(0)
    for part in range(n_part):
        x1_ref, y0_ref, y1_ref, rt_ref = refs[4 * part:4 * part + 4]

        @pl.when((i >= part * nb) & (i < (part + 1) * nb))
        def _():
            w0 = rt_ref[:, 2:3]
            w1 = rt_ref[:, 3:4]
            m = w0 * y0_ref[...].astype(_F32) + w1 * y1_ref[...].astype(_F32)
            o_ref[...] = _rms(x1_ref[...] + g2_ref[...] * m, gf_ref[...])


def _final(parts, g2, gf, tm=512):
    n_part = len(parts)
    nb, l, d = parts[0][0].shape
    nj = l // tm
    in_specs = [pl.BlockSpec((None, 1, d), lambda i, j: (i, 0, 0)),
                pl.BlockSpec((1, d), lambda i, j: (0, 0))]
    args = [g2, gf]
    for part, arrs in enumerate(parts):
        def imap(i, j, part=part):
            own = (i >= part * nb) & (i < (part + 1) * nb)
            return (jnp.clip(i - part * nb, 0, nb - 1), jnp.where(own, j, jnp.where(i < part * nb, 0, nj - 1)), 0)
        for a in arrs:
            in_specs.append(pl.BlockSpec((None, tm, a.shape[-1]), imap))
            args.append(a)
    return pl.pallas_call(
        functools.partial(_final_body, n_part=n_part, nb=nb),
        grid=(n_part * nb, nj),
        in_specs=in_specs,
        out_specs=pl.BlockSpec((None, tm, d), lambda i, j: (i, j, 0)),
        out_shape=jax.ShapeDtypeStruct((n_part * nb, l, d), _F32),
        compiler_params=_cparams(("parallel", "arbitrary")),
        name="final_norm",
    )(*args)


def _route_plan(eid, rank, tok_ids, counts, tm, nr):
    n_experts = counts.shape[0]
    n_pairs = eid.size
    padded = ((counts + tm - 1) // tm) * tm
    pad_end = jnp.cumsum(padded)
    pad_start = pad_end - padded
    raw_start = jnp.cumsum(counts) - counts
    pos = rank
    for e in range(n_experts):
        pos = pos + jnp.where(eid == e, pad_start[e], 0)
    tile_start = jnp.arange(nr // tm, dtype=jnp.int32) * tm
    tile_expert = jnp.sum(tile_start[:, None] >= pad_end[None, :], axis=-1)
    tile_expert = jnp.minimum(tile_expert, n_experts - 1).astype(jnp.int32)
    _, sorted_tok = lax.sort_key_val((eid * n_pairs + rank).reshape(-1), tok_ids.reshape(-1))
    sel = tile_expert[:, None] == jnp.arange(n_experts, dtype=jnp.int32)[None, :]
    per_tile = lambda v: jnp.repeat(jnp.sum(jnp.where(sel, v[None, :], 0), axis=-1), tm)
    off = jnp.arange(nr, dtype=jnp.int32) - per_tile(pad_start)
    valid = off < per_tile(counts)
    j = jnp.clip(per_tile(raw_start) + off, 0, n_pairs - 1)
    filler = jnp.arange(nr, dtype=jnp.int32) % (n_pairs // 2)
    row_token = jnp.where(valid, sorted_tok.at[j].get(mode="promise_in_bounds"), filler)
    n_used = (pad_end[-1:] // tm).astype(jnp.int32)
    return pos, tile_expert, n_used, row_token


def kernel(x, c, ctx, c_ctx, w_ada, b_ada, norm1_g, norm2_g, w_in, s5_lam_re, s5_lam_im, s5_b_re, s5_b_im, s5_c_re, s5_c_im, s5_log_step, s5_d, s5_w_glu, fourier_w, mix_norm_s5_g, mix_norm_f_g, w_out, moe_w_group, moe_b_group, moe_w_router, moe_b_router, moe_w_gate, moe_w_up, moe_w_down, final_norm_g):
    b, l, d = x.shape
    lc = ctx.shape[1]
    depth = w_ada.shape[0]
    assert depth == 1 and l == FFT_R * FFT_R and lc % S5_CHUNK == 0 and b % 8 == 0
    ds5 = s5_d.shape[-1]
    df = w_in.shape[-1] - ds5
    n_exp = moe_w_gate.shape[1]
    row = lambda a: a.reshape(1, -1)

    cond = jnp.concatenate([c, c_ctx[None, :], jnp.zeros((7, d), _F32)], axis=0)
    mod = _adaln(cond, w_ada[0], b_ada[0])
    sh1, sc1, g1, sh2, sc2, g2 = [mod[:b, i * d:(i + 1) * d].reshape(b, 1, d) for i in range(6)]
    csh1 = mod[b:b + 1, 0:d]
    csc1 = mod[b:b + 1, d:2 * d]

    w_in_bf = w_in[0].astype(_BF)
    m1 = jnp.asarray(_fft_stage1_matrix()).astype(_BF)
    m2 = jnp.asarray(_fft_stage2_matrices()).astype(_BF)
    perm = jnp.asarray(_chunk_row_perm())
    zs, gc = _inproj(x, sh1, sc1, row(norm1_g[0]), w_in_bf, m1)
    u_ctx = _inproj_ctx(ctx, csh1, csc1, row(norm1_g[0]), w_in_bf[:, :ds5], perm)
    u_lat = _s5_pack(zs, perm)

    fg = df // FOURIER_GROUPS
    cc = np.arange(fg)
    ang = 2.0 * np.pi * ((cc[:, None] * cc[None, :]) % fg) / fg
    scale = 1.0 / math.sqrt(l * fg)
    cw = jnp.einsum("cm,gmd->gcd", jnp.asarray(np.cos(ang) * scale, _F32), fourier_w[0], precision=_HI)
    sw = jnp.einsum("cm,gmd->gcd", jnp.asarray(np.sin(ang) * scale, _F32), fourier_w[0], precision=_HI)
    fn = _fft2(gc, m2, cw.astype(_BF), sw.astype(_BF), row(mix_norm_f_g[0]))

    ws, wy, at = _s5_params(s5_lam_re[0], s5_lam_im[0], s5_b_re[0], s5_b_im[0],
                            s5_c_re[0], s5_c_im[0], s5_log_step[0])
    y_g = _s5(u_ctx, u_lat, ws, wy, at, b)

    n_rt = N_EXPERT_GROUPS * (1 + EXPERTS_PER_GROUP)
    w_rt = jnp.concatenate([moe_w_group[0], moe_w_router[0].reshape(d, -1)], axis=-1)
    w_rt = jnp.pad(w_rt, ((0, 0), (0, ROUTE_LANES - n_rt)))
    w_rt_hi = w_rt.astype(_BF)
    w_rt = jnp.stack([w_rt_hi, (w_rt - w_rt_hi.astype(_F32)).astype(_BF)])
    b_rt =jnp.pad(jnp.concatenate([moe_b_group[0], moe_b_router[0].reshape(-1)]), (0, ROUTE_LANES - n_rt))
    tm = MOE_TM
    nb = b // MOE_PARTS
    n_tok = nb * l
    nr = 2 * n_tok + n_exp * tm
    take_rows = lambda a, idx: a.at[idx].get(mode="promise_in_bounds")
    wglu_bf, wo_bf = s5_w_glu[0].astype(_BF), w_out[0].astype(_BF)
    parts = []
    for part in range(MOE_PARTS):
        x1, tmod, rt, rtt, cnt = _post(x, zs, y_g, fn, row(s5_d[0]), wglu_bf, row(mix_norm_s5_g[0]), wo_bf,
                                       g1, row(norm2_g[0]), sh2, sc2, w_rt, row(b_rt), part * nb, nb)
        rec = rtt.transpose(1, 0, 2).reshape(ROUTE_ROWS, n_tok)
        eid = rec[0:2].astype(jnp.int32)
        rank = rec[4:6].astype(jnp.int32)
        tok_ids = jnp.broadcast_to(jnp.arange(n_tok, dtype=jnp.int32), (2, n_tok))
        pos, tile_expert, n_used, row_token = _route_plan(eid, rank, tok_ids,
                                                          cnt[0, :n_exp].astype(jnp.int32), tm, nr)
        xs = take_rows(tmod.reshape(n_tok, d), row_token)
        ys = _moe(tile_expert, n_used, xs, moe_w_gate[0], moe_w_up[0], moe_w_down[0])
        y0 = take_rows(ys, pos[0]).reshape(nb, l, d)
        y1 = take_rows(ys, pos[1]).reshape(nb, l, d)
        parts.append((x1, y0, y1, rt))
    return _final(parts, g2, row(final_norm_g))
```

```python
import functools
import math

import numpy as np
import jax
import jax.numpy as jnp
from jax import lax
from jax.experimental import pallas as pl
from jax.experimental.pallas import tpu as pltpu

EPS = 1e-6
S5_GROUP = 16
S5_STATE = 64
S5_CHUNK = 16
S5_SCAN_GROUPS = 4
FOURIER_GROUPS = 4
N_EXPERT_GROUPS = 4
EXPERTS_PER_GROUP = 8
FFT_R = 64
FFT_BLK = 8
MOE_TM = 512
MOE_PARTS = 1
ROUTE_LANES = 128
ROUTE_ROWS = 8
VMEM_LIMIT = 56 * 1024 * 1024

_HI = lax.Precision.HIGHEST
_BF = jnp.bfloat16
_F32 = jnp.float32


def _cparams(sem):
    return pltpu.CompilerParams(dimension_semantics=sem, vmem_limit_bytes=VMEM_LIMIT)


def _dot(a, b):
    return jnp.dot(a, b, preferred_element_type=_F32)


def _rms(x, g):
    return x * lax.rsqrt(jnp.mean(x * x, axis=-1, keepdims=True) + EPS) * g


def _adaln_body(c_ref, w_ref, b_ref, o_ref):
    c = c_ref[...]
    a = c * jax.nn.sigmoid(c)
    o_ref[...] = jnp.dot(a, w_ref[...], preferred_element_type=_F32, precision=_HI) + b_ref[...]


def _adaln(cond, w, b):
    m, d = cond.shape
    n = w.shape[1]
    tn = 768
    return pl.pallas_call(
        _adaln_body,
        grid=(n // tn,),
        in_specs=[pl.BlockSpec((m, d), lambda j: (0, 0)),
                  pl.BlockSpec((d, tn), lambda j: (0, j)),
                  pl.BlockSpec((1, tn), lambda j: (0, j))],
        out_specs=pl.BlockSpec((m, tn), lambda j: (0, j)),
        out_shape=jax.ShapeDtypeStruct((m, n), _F32),
        compiler_params=_cparams(("arbitrary",)),
        name="adaln",
    )(cond, w, b.reshape(1, n))


def _inproj_body(x_ref, sh_ref, sc_ref, g_ref, w_ref, m1_ref, zs_ref, gc_ref):
    r, blk, d = x_ref.shape
    x = x_ref[...].reshape(r * blk, d)
    h = _rms(x, g_ref[...]) * (1.0 + sc_ref[...]) + sh_ref[...]
    z = _dot(h.astype(_BF), w_ref[...])
    ds5 = zs_ref.shape[-1]
    zs_ref[...] = z[:, :ds5].reshape(r, blk, ds5)
    v = z[:, ds5:].astype(_BF)
    g1 = _dot(m1_ref[...], v)
    gc_ref[...] = g1.astype(_BF).reshape(r, 2 * blk, v.shape[-1])


def _inproj(x, sh, sc, g, w_bf, m1):
    b, l, d = x.shape
    r, blk = FFT_R, FFT_BLK
    nj = r // blk
    dmix = w_bf.shape[1]
    ds5 = dmix // 2
    df = dmix - ds5
    x4 = x.reshape(b, r, r, d)
    zs, gc = pl.pallas_call(
        _inproj_body,
        grid=(b, nj),
        in_specs=[pl.BlockSpec((None, r, blk, d), lambda i, j: (i, 0, j, 0)),
                  pl.BlockSpec((None, 1, d), lambda i, j: (i, 0, 0)),
                  pl.BlockSpec((None, 1, d), lambda i, j: (i, 0, 0)),
                  pl.BlockSpec((1, d), lambda i, j: (0, 0)),
                  pl.BlockSpec((d, dmix), lambda i, j: (0, 0)),
                  pl.BlockSpec(m1.shape, lambda i, j: (0, 0))],
        out_specs=[pl.BlockSpec((None, r, blk, ds5), lambda i, j: (i, 0, j, 0)),
                   pl.BlockSpec((None, r, 2 * blk, df), lambda i, j: (i, 0, j, 0))],
        out_shape=[jax.ShapeDtypeStruct((b, r, r, ds5), _F32),
                   jax.ShapeDtypeStruct((b, r, 2 * r, df), _BF)],
        compiler_params=_cparams(("parallel", "arbitrary")),
        name="inproj",
    )(x4, sh, sc, g, w_bf, m1)
    return zs.reshape(b, l, ds5), gc


def _seg_transpose(arrs):
    n = len(arrs)
    seg = lax.broadcasted_iota(jnp.int32, arrs[0].shape, 1) // S5_GROUP
    d = n // 2
    while d >= 1:
        keep = (seg & d) == 0
        new = list(arrs)
        for i in range(n):
            if i & d == 0:
                a, b = arrs[i], arrs[i + d]
                new[i] = jnp.where(keep, a, pltpu.roll(b, S5_GROUP * d, 1))
                new[i + d] = jnp.where(keep, pltpu.roll(a, 128 - S5_GROUP * d, 1), b)
        arrs = new
        d //= 2
    return arrs


@functools.lru_cache(maxsize=None)
def _chunk_row_perm():
    t = S5_CHUNK
    p = np.zeros((t * t, t * t), np.float32)
    for c in range(t):
        for tl in range(t):
            p[tl * t + c, c * t + tl] = 1.0
    return p.astype(jnp.bfloat16)


def _pack_chunks(z_bf, p_ref, u_ref):
    t = S5_CHUNK
    nsub = z_bf.shape[0] // (t * t)
    pieces = [_dot(p_ref[...], z_bf[s * t * t:(s + 1) * t * t, :]) for s in range(nsub)]
    a = []
    for tl in range(t):
        rows = [pc[tl * t:(tl + 1) * t, :] for pc in pieces]
        a.append(rows[0] if nsub == 1 else jnp.concatenate(rows, axis=0))
    lanes = 128
    gpb = lanes // S5_GROUP
    for j in range(z_bf.shape[1] // lanes):
        for hi in range(t // gpb):
            outs = _seg_transpose([a[gpb * hi + tlo][:, lanes * j:lanes * (j + 1)] for tlo in range(gpb)])
            for glo in range(gpb):
                u_ref[gpb * j + glo, :, lanes * hi:lanes * (hi + 1)] = outs[glo].astype(u_ref.dtype)


def _inproj_ctx_body(x_ref, sh_ref, sc_ref, g_ref, w_ref, p_ref, u_ref):
    h = _rms(x_ref[...], g_ref[...]) * (1.0 + sc_ref[...]) + sh_ref[...]
    z = _dot(h.astype(_BF), w_ref[...])
    _pack_chunks(z.astype(_BF), p_ref, u_ref)


def _inproj_ctx(ctx, sh, sc, g, w_s5_bf, perm):
    b, lc, d = ctx.shape
    ds5 = w_s5_bf.shape[1]
    n_grp = ds5 // S5_GROUP
    kk = S5_CHUNK * S5_GROUP
    return pl.pallas_call(
        _inproj_ctx_body,
        grid=(b,),
        in_specs=[pl.BlockSpec((None, lc, d), lambda i: (i, 0, 0)),
                  pl.BlockSpec((1, d), lambda i: (0, 0)),
                  pl.BlockSpec((1, d), lambda i: (0, 0)),
                  pl.BlockSpec((1, d), lambda i: (0, 0)),
                  pl.BlockSpec((d, ds5), lambda i: (0, 0)),
                  pl.BlockSpec(perm.shape, lambda i: (0, 0))],
        out_specs=pl.BlockSpec((n_grp, lc // S5_CHUNK, kk), lambda i: (0, 0, i)),
        out_shape=jax.ShapeDtypeStruct((n_grp, lc // S5_CHUNK, b * kk), _BF),
        compiler_params=_cparams(("arbitrary",)),
        name="inproj_ctx",
    )(ctx, sh, sc, g, w_s5_bf, perm)


def _s5_pack_body(z_ref, p_ref, u_ref):
    _pack_chunks(z_ref[...].astype(_BF), p_ref, u_ref)


def _s5_pack(zs, perm, tok=2048):
    b, l, ds5 = zs.shape
    n_grp = ds5 // S5_GROUP
    kk = S5_CHUNK * S5_GROUP
    return pl.pallas_call(
        _s5_pack_body,
        grid=(b, l // tok),
        in_specs=[pl.BlockSpec((None, tok, ds5), lambda i, j: (i, j, 0)),
                  pl.BlockSpec(perm.shape, lambda i, j: (0, 0))],
        out_specs=pl.BlockSpec((n_grp, tok // S5_CHUNK, kk), lambda i, j: (0, j, i)),
        out_shape=jax.ShapeDtypeStruct((n_grp, l // S5_CHUNK, b * kk), _BF),
        compiler_params=_cparams(("parallel", "arbitrary")),
        name="s5_pack",
    )(zs, perm)


@functools.lru_cache(maxsize=None)
def _fft_stage1_matrix():
    r, blk = FFT_R, FFT_BLK
    k1 = np.arange(r)[:, None]
    l1 = np.arange(r)[None, :]
    ang = 2.0 * np.pi * ((k1 * l1) % r) / r
    f = np.stack([np.cos(ang), -np.sin(ang)], axis=1)
    m = np.einsum("kal,pq->kaplq", f, np.eye(blk))
    return m.reshape(r * 2 * blk, r * blk).astype(np.float32)


@functools.lru_cache(maxsize=None)
def _fft_stage2_matrices():
    r, blk = FFT_R, FFT_BLK
    n = r * r
    nt = r // blk
    l2 = np.arange(r)
    k2 = np.arange(r)
    out = np.zeros((nt, 2, r, blk, blk, r // blk, 2, blk), np.float32)
    for i in range(nt):
        for kl in range(blk):
            k1 = blk * i + kl
            ang = 2.0 * np.pi * (((k2[:, None] * l2[None, :] * r) + l2[None, :] * k1) % n) / n
            tr, ti = np.cos(ang), -np.sin(ang)
            tr = tr.reshape(r, r // blk, blk)
            ti = ti.reshape(r, r // blk, blk)
            out[i, 0, :, kl, kl, :, 0, :] = tr
            out[i, 0, :, kl, kl, :, 1, :] = -ti
            out[i, 1, :, kl, kl, :, 0, :] = ti
            out[i, 1, :, kl, kl, :, 1, :] = tr
    return out.reshape(nt, 2 * r * blk, blk * 2 * r)


def _fft2_body(gc_ref, m2_ref, cw_ref, sw_ref, g_ref, o_ref):
    kb, rr, df = gc_ref.shape
    gc = gc_ref[...].reshape(kb * rr, df)
    x = _dot(m2_ref[...], gc)
    half = x.shape[0] // 2
    xr = x[:half].astype(_BF)
    xi = x[half:].astype(_BF)
    ng = cw_ref.shape[0]
    fg = df // ng
    parts = []
    for g in range(ng):
        sl = slice(g * fg, (g + 1) * fg)
        parts.append(_dot(xr[:, sl], cw_ref[g]) + _dot(xi[:, sl], sw_ref[g]))
    f = jnp.concatenate(parts, axis=-1)
    fn = _rms(f, g_ref[...])
    o_ref[...] = fn.reshape(o_ref.shape)


def _fft2(gc, m2, cw, sw, gf):
    b, r, rr, df = gc.shape
    blk = FFT_BLK
    nt = r // blk
    out = pl.pallas_call(
        _fft2_body,
        grid=(b, nt),
        in_specs=[pl.BlockSpec((None, blk, rr, df), lambda i, j: (i, j, 0, 0)),
                  pl.BlockSpec((None,) + m2.shape[1:], lambda i, j: (j, 0, 0)),
                  pl.BlockSpec(cw.shape, lambda i, j: (0, 0, 0)),
                  pl.BlockSpec(sw.shape, lambda i, j: (0, 0, 0)),
                  pl.BlockSpec((1, df), lambda i, j: (0, 0))],
        out_specs=pl.BlockSpec((None, r, blk, df), lambda i, j: (i, 0, j, 0)),
        out_shape=jax.ShapeDtypeStruct((b, r, r, df), _F32),
        compiler_params=_cparams(("parallel", "arbitrary")),
        name="fft2",
    )(gc, m2, cw, sw, gf)
    return out.reshape(b, r * r, df)


def _s5_body(uc_ref, ul_ref, ws_ref, wy_ref, at_ref, y_ref,
             s_re, s_im, ha_re, hb_re, ha_im, hb_im, *, nb):
    p = S5_STATE
    gs, n_ctx, _ = uc_ref.shape
    n_lat = ul_ref.shape[1]
    kk = ws_ref.shape[1]
    for gi in range(gs):
        ws = ws_ref[gi]
        for b in range(nb):
            sl = slice(b * kk, (b + 1) * kk)
            sc = _dot(uc_ref[gi, :, sl], ws)
            sl_ = _dot(ul_ref[gi, :, sl], ws)
            s_re[gi, pl.ds(b, n_ctx, stride=nb), :] = sc[:, :2 * p]
            s_im[gi, pl.ds(b, n_ctx, stride=nb), :] = sc[:, 2 * p:]
            s_re[gi, pl.ds(n_ctx * nb + b, n_lat, stride=nb), :] = sl_[:, :2 * p]
            s_im[gi, pl.ds(n_ctx * nb + b, n_lat, stride=nb), :] = sl_[:, 2 * p:]

    a_re = [at_ref[gi, 0:1, :] for gi in range(gs)]
    a_im = [at_ref[gi, 1:2, :] for gi in range(gs)]
    is_f = lax.broadcasted_iota(jnp.int32, (nb, 2 * p), 1) < p
    n_all = n_ctx + n_lat

    def load(gi, cf, cr):
        rf = pl.multiple_of(cf * nb, nb)
        rr = pl.multiple_of(cr * nb, nb)
        return (jnp.where(is_f, s_re[gi, pl.ds(rf, nb), :], s_re[gi, pl.ds(rr, nb), :]),
                jnp.where(is_f, s_im[gi, pl.ds(rf, nb), :], s_im[gi, pl.ds(rr, nb), :]))

    def advance(gi, h_re, h_im, x_re, x_im):
        return (a_re[gi] * h_re - a_im[gi] * h_im + x_re, a_re[gi] * h_im + a_im[gi] * h_re + x_im)

    def ctx_step(step, carry):
        out = []
        for gi in range(gs):
            x_re, x_im = load(gi, step, n_ctx - 1 - step)
            out.append(advance(gi, *carry[gi], x_re, x_im))
        return tuple(out)

    def lat_step(step, carry):
        rf = pl.multiple_of(step * nb, nb)
        rr = pl.multiple_of((n_lat - 1 - step) * nb, nb)
        out = []
        for gi in range(gs):
            h_re, h_im = carry[gi]
            ha_re[gi, pl.ds(rf, nb), :] = h_re
            hb_re[gi, pl.ds(rr, nb), :] = h_re
            ha_im[gi, pl.ds(rf, nb), :] = h_im
            hb_im[gi, pl.ds(rr, nb), :] = h_im
            x_re, x_im = load(gi, n_ctx + step, n_all - 1 - step)
            out.append(advance(gi, h_re, h_im, x_re, x_im))
        return tuple(out)

    zero = jnp.zeros((nb, 2 * p), _F32)
    carry = lax.fori_loop(0, n_ctx, ctx_step, tuple((zero, zero) for _ in range(gs)), unroll=2)
    lax.fori_loop(0, n_lat, lat_step, carry, unroll=2)

    for gi in range(gs):
        wy = wy_ref[gi]
        for b in range(nb):
            rows = pl.ds(b, n_lat, stride=nb)
            hin = jnp.concatenate([ha_re[gi, rows, :], hb_re[gi, rows, :],
                                   ha_im[gi, rows, :], hb_im[gi, rows, :]], axis=-1)
            lhs = jnp.concatenate([ul_ref[gi, :, b * kk:(b + 1) * kk], hin.astype(_BF)], axis=-1)
            y_ref[gi, :, b * kk:(b + 1) * kk] = _dot(lhs, wy).astype(y_ref.dtype)


def _s5(u_ctx, u_lat, ws, wy, at, nb):
    ng, n_ctx, w = u_ctx.shape
    n_lat = u_lat.shape[1]
    gs = S5_SCAN_GROUPS
    body = functools.partial(_s5_body, nb=nb)
    hs = pltpu.VMEM((gs, n_lat * nb, 2 * S5_STATE), _F32)
    ss = pltpu.VMEM((gs, (n_ctx + n_lat) * nb, 2 * S5_STATE), _F32)
    grp = lambda a: pl.BlockSpec((gs,) + a.shape[1:], lambda g: (g, 0, 0))
    return pl.pallas_call(
        body,
        grid=(ng // gs,),
        in_specs=[grp(u_ctx), grp(u_lat), grp(ws), grp(wy), grp(at)],
        out_specs=pl.BlockSpec((gs, n_lat, w), lambda g: (g, 0, 0)),
        out_shape=jax.ShapeDtypeStruct((ng, n_lat, w), _BF),
        scratch_shapes=[ss, ss, hs, hs, hs, hs],
        compiler_params=_cparams(("parallel",)),
        name="s5_scan",
    )(u_ctx, u_lat, ws, wy, at)


def _s5_param_body(lr_ref, li_ref, ls_ref, btr_ref, bti_ref, cr_ref, ci_ref, ws_ref, wy_ref, at_ref):
    t, hh, p = S5_CHUNK, S5_GROUP, S5_STATE
    kk = t * hh
    lr, li = lr_ref[...], li_ref[...]
    dt = jnp.exp(ls_ref[...])
    ldt, idt = lr * dt, li * dt
    mag = jnp.exp(ldt)
    ab_re, ab_im = mag * jnp.cos(idt), mag * jnp.sin(idt)
    den = lr * lr + li * li
    nr, ni = ab_re - 1.0, ab_im
    q_re = (nr * lr + ni * li) / den
    q_im = (ni * lr - nr * li) / den
    kf = lax.broadcasted_iota(jnp.int32, (2 * t, p), 0).astype(_F32)

    ap_re, ap_im, bb_re, bb_im = [], [], [], []
    for d in range(2):
        pm = jnp.exp(kf * ldt[d:d + 1, :])
        ap_re.append(pm * jnp.cos(kf * idt[d:d + 1, :]))
        ap_im.append(pm * jnp.sin(kf * idt[d:d + 1, :]))
        bb_re.append(q_re[d:d + 1, :] * btr_ref[d] - q_im[d:d + 1, :] * bti_ref[d])
        bb_im.append(q_re[d:d + 1, :] * bti_ref[d] + q_im[d:d + 1, :] * btr_ref[d])

    def rep(a, ks):
        return jnp.concatenate([jnp.broadcast_to(a[k:k + 1, :], (hh, p)) for k in ks], axis=0)

    def tile(a, n):
        return jnp.concatenate([a] * n, axis=0)

    def cmul(xr, xi, yr, yi):
        return xr * yr - xi * yi, xr * yi + xi * yr

    ks_f = list(range(t + 1))
    ks_r = list(range(t, -1, -1))
    caf_re, caf_im = cmul(tile(cr_ref[0], t + 1), tile(ci_ref[0], t + 1), rep(ap_re[0], ks_f), rep(ap_im[0], ks_f))
    car_re, car_im = cmul(tile(cr_ref[1], t + 1), tile(ci_ref[1], t + 1), rep(ap_re[1], ks_r), rep(ap_im[1], ks_r))

    def lag_blocks(bre, bim, ca_re, ca_im):
        lhs = jnp.concatenate([bre, bim], axis=1)
        rhs = jnp.concatenate([ca_re, -ca_im], axis=1)
        return lax.dot_general(lhs, rhs, (((1,), (1,)), ((), ())), precision=_HI,
                               preferred_element_type=_F32)

    w_f = lag_blocks(bb_re[0], bb_im[0], caf_re[:kk], caf_im[:kk])
    w_r = lag_blocks(bb_re[1], bb_im[1], car_re[hh:], car_im[hh:])
    lane = lax.broadcasted_iota(jnp.int32, (hh, kk), 1)
    for tlp in range(t):
        sf = hh * tlp
        sr = hh * (t - 1 - tlp)
        a = w_f if sf == 0 else jnp.where(lane >= sf, pltpu.roll(w_f, sf, 1), 0.0)
        b = w_r if sr == 0 else jnp.where(lane < kk - sr, pltpu.roll(w_r, kk - sr, 1), 0.0)
        wy_ref[hh * tlp:hh * (tlp + 1), :] = (a + b).astype(wy_ref.dtype)

    zeros = jnp.zeros((p, kk), _F32)
    blocks = [caf_re[hh:].T, zeros, zeros, car_re[:kk].T, (-caf_im[hh:]).T, zeros, zeros, (-car_im[:kk]).T]
    for i, blk in enumerate(blocks):
        wy_ref[kk + p * i:kk + p * (i + 1), :] = blk.astype(wy_ref.dtype)

    pf = list(range(t - 1, -1, -1))
    pr = list(range(t))
    f_re, f_im = cmul(rep(ap_re[0], pf), rep(ap_im[0], pf), tile(bb_re[0], t), tile(bb_im[0], t))
    r_re, r_im = cmul(rep(ap_re[1], pr), rep(ap_im[1], pr), tile(bb_re[1], t), tile(bb_im[1], t))
    ws_ref[...] = jnp.concatenate([f_re, r_re, f_im, r_im], axis=1).astype(ws_ref.dtype)

    a_re = jnp.concatenate([ap_re[0][t:t + 1, :], ap_re[1][t:t + 1, :]], axis=1)
    a_im = jnp.concatenate([ap_im[0][t:t + 1, :], ap_im[1][t:t + 1, :]], axis=1)
    row = lax.broadcasted_iota(jnp.int32, at_ref.shape, 0)
    at_ref[...] = jnp.where(row == 0, a_re, jnp.where(row == 1, a_im, 0.0))


def _s5_params(lam_re, lam_im, b_re, b_im, c_re, c_im, log_step):
    _, ng, p = lam_re.shape
    hh = b_re.shape[-1]
    kk = S5_CHUNK * hh
    gd = lambda a: jnp.swapaxes(a.astype(_F32), 0, 1)
    args = (gd(lam_re), gd(lam_im), gd(log_step)[..., None],
            jnp.swapaxes(gd(b_re), 2, 3), jnp.swapaxes(gd(b_im), 2, 3), gd(c_re), gd(c_im))
    spec = lambda a: pl.BlockSpec((None,) + a.shape[1:], lambda g: (g,) + (0,) * (a.ndim - 1))
    return pl.pallas_call(
        _s5_param_body,
        grid=(ng,),
        in_specs=[spec(a) for a in args],
        out_specs=[pl.BlockSpec((None, kk, 4 * p), lambda g: (g, 0, 0)),
                   pl.BlockSpec((None, kk + 8 * p, kk), lambda g: (g, 0, 0)),
                   pl.BlockSpec((None, 8, 2 * p), lambda g: (g, 0, 0))],
        out_shape=[jax.ShapeDtypeStruct((ng, kk, 4 * p), _BF),
                   jax.ShapeDtypeStruct((ng, kk + 8 * p, kk), _BF),
                   jax.ShapeDtypeStruct((ng, 8, 2 * p), _F32)],
        compiler_params=_cparams(("parallel",)),
        name="s5_params",
    )(*args)


def _post_body(x_ref, u_ref, yg_ref, fn_ref, d_ref, wglu_ref, gs_ref, wo_ref, g1_ref,
               n2_ref, sh_ref, sc_ref, wr_ref, br_ref, tri_ref, x1_ref, t_ref, rt_ref, rtt_ref, cnt_ref,
               y_scr, cnt_scr):
    ds5 = u_ref.shape[-1]
    first = (pl.program_id(0) == 0) & (pl.program_id(1) == 0)

    @pl.when(first)
    def _():
        cnt_scr[...] = jnp.zeros_like(cnt_scr)

    t_chunk = S5_CHUNK
    n_chunk = yg_ref.shape[1]
    lanes = 128
    gpb = lanes // S5_GROUP
    for j in range(ds5 // lanes):
        for hi in range(t_chunk // gpb):
            outs = _seg_transpose([yg_ref[gpb * j + glo, :, lanes * hi:lanes * (hi + 1)].astype(_F32)
                                   for glo in range(gpb)])
            for tlo in range(gpb):
                y_scr[j, pl.ds(gpb * hi + tlo, n_chunk, stride=t_chunk), :] = outs[tlo]
    y_s5 = jnp.concatenate([y_scr[j] for j in range(ds5 // lanes)], axis=-1)

    y = y_s5 + d_ref[...] * u_ref[...]
    y = jax.nn.gelu(y, approximate=True)
    y = y * jax.nn.sigmoid(_dot(y.astype(_BF), wglu_ref[...]))
    yn = _rms(y, gs_ref[...]).astype(_BF)
    mix = _dot(yn, wo_ref[0:ds5, :]) + _dot(fn_ref[...].astype(_BF), wo_ref[ds5:, :])
    x1 = x_ref[...] + g1_ref[...] * mix
    x1_ref[...] = x1
    t = _rms(x1, n2_ref[...]) * (1.0 + sc_ref[...]) + sh_ref[...]
    t_hi = t.astype(_BF)
    t_ref[...] = t_hi

    ng, epg = N_EXPERT_GROUPS, EXPERTS_PER_GROUP
    t_lo = (t - t_hi.astype(_F32)).astype(_BF)
    logits = (_dot(t_hi, wr_ref[0]) + _dot(t_lo, wr_ref[0]) + _dot(t_hi, wr_ref[1])) + br_ref[...]
    lane = lax.broadcasted_iota(jnp.int32, logits.shape, 1)
    neg = jnp.float32(-jnp.inf)
    big = jnp.int32(1 << 20)
    gl = jnp.where(lane < ng, logits, neg)
    gmax = jnp.max(gl, axis=-1, keepdims=True)
    gidx = jnp.min(jnp.where(gl == gmax, lane, big), axis=-1, keepdims=True)
    gw = 1.0 / jnp.sum(jnp.where(lane < ng, jnp.exp(logits - gmax), 0.0), axis=-1, keepdims=True)
    lo = ng + gidx * epg
    el = jnp.where((lane >= lo) & (lane < lo + epg), logits, neg)
    v0 = jnp.max(el, axis=-1, keepdims=True)
    i0 = jnp.min(jnp.where(el == v0, lane, big), axis=-1, keepdims=True)
    el1 = jnp.where(lane == i0, neg, el)
    v1 = jnp.max(el1, axis=-1, keepdims=True)
    i1 = jnp.min(jnp.where(el1 == v1, lane, big), axis=-1, keepdims=True)
    p0 = 1.0 / (1.0 + jnp.exp(v1 - v0))
    w0 = gw * p0
    w1 = gw * (1.0 - p0)
    e0 = i0 - ng
    e1 = i1 - ng

    oh0 = lane == e0
    oh1 = lane == e1
    oh = (oh0 | oh1).astype(_F32)
    prefix = _dot(tri_ref[...], oh.astype(_BF)) + cnt_scr[...]
    r0 = jnp.sum(jnp.where(oh0, prefix, 0.0), axis=-1, keepdims=True)
    r1 = jnp.sum(jnp.where(oh1, prefix, 0.0), axis=-1, keepdims=True)
    cnt = cnt_scr[...] + jnp.sum(oh, axis=0, keepdims=True)
    cnt_scr[...] = cnt
    cnt_ref[...] = cnt

    vals = (e0.astype(_F32), e1.astype(_F32), w0, w1, r0, r1)
    rt = jnp.zeros(logits.shape, _F32)
    for k, v in enumerate(vals):
        rt = jnp.where(lane == k, v, rt)
    rt_ref[...] = rt
    rtt_ref[...] = rt.T[0:rtt_ref.shape[0], :]


@functools.lru_cache(maxsize=None)
def _strict_lower_ones(n):
    return np.tril(np.ones((n, n), np.float32), -1).astype(jnp.bfloat16)


def _post(x, u, y_g, fn, s5_d, wglu_bf, gs, wo_bf, g1, n2, sh2, sc2, wr, br, b0, nb, tm=512):
    b, l, d = x.shape
    ds5 = u.shape[-1]
    n_grp, _, w = y_g.shape
    kk = w // b
    tri = jnp.asarray(_strict_lower_ones(tm))
    tok = lambda w: pl.BlockSpec((None, tm, w), lambda i, j: (i + b0, j, 0))
    otok = lambda w: pl.BlockSpec((None, tm, w), lambda i, j: (i, j, 0))
    per_b = pl.BlockSpec((None, 1, d), lambda i, j: (i + b0, 0, 0))
    full = lambda a: pl.BlockSpec(a.shape, lambda i, j: (0,) * a.ndim)
    ygs = pl.BlockSpec((n_grp, tm // S5_CHUNK, kk), lambda i, j: (0, j, i + b0))
    return pl.pallas_call(
        _post_body,
        grid=(nb, l // tm),
        in_specs=[tok(d), tok(ds5), ygs, tok(fn.shape[-1]), full(s5_d), full(wglu_bf),
                  full(gs), full(wo_bf), per_b, full(n2), per_b, per_b, full(wr), full(br), full(tri)],
        out_specs=[otok(d), otok(d), otok(ROUTE_LANES),
                   pl.BlockSpec((None, ROUTE_ROWS, tm), lambda i, j: (i, 0, j)),
                   pl.BlockSpec((1, ROUTE_LANES), lambda i, j: (0, 0))],
        out_shape=[jax.ShapeDtypeStruct((nb, l, d), _F32),
                   jax.ShapeDtypeStruct((nb, l, d), _BF),
                   jax.ShapeDtypeStruct((nb, l, ROUTE_LANES), _F32),
                   jax.ShapeDtypeStruct((nb, ROUTE_ROWS, l), _F32),
                   jax.ShapeDtypeStruct((1, ROUTE_LANES), _F32)],
        scratch_shapes=[pltpu.VMEM((ds5 // 128, tm, 128), _F32),
                        pltpu.VMEM((1, ROUTE_LANES), _F32)],
        compiler_params=_cparams(("arbitrary", "arbitrary")),
        name="post_mixer",
    )(x, u, y_g, fn, s5_d, wglu_bf, gs, wo_bf, g1, n2, sh2, sc2, wr, br, tri)


def _moe_body(te_ref, nt_ref, xs_ref, wg_ref, wu_ref, wd_ref, o_ref, wgu_bf, wd_bf):
    i = pl.program_id(0)
    de = wg_ref.shape[-1]
    used = i < nt_ref[0]
    new_expert = (i == 0) | (te_ref[i] != te_ref[jnp.maximum(i - 1, 0)])

    @pl.when(used & new_expert)
    def _():
        wgu_bf[:, 0:de] = wg_ref[...].astype(_BF)
        wgu_bf[:, de:2 * de] = wu_ref[...].astype(_BF)
        wd_bf[...] = wd_ref[...].astype(_BF)

    @pl.when(used)
    def _():
        h = _dot(xs_ref[...], wgu_bf[...])
        hg, hu = h[:, 0:de], h[:, de:2 * de]
        a = hg * jax.nn.sigmoid(hg) * hu
        o_ref[...] = _dot(a.astype(_BF), wd_bf[...]).astype(o_ref.dtype)

    @pl.when(jnp.logical_not(used))
    def _():
        o_ref[...] = jnp.zeros_like(o_ref)


def _moe(tile_expert, n_tiles_used, xs, wg, wu, wd):
    nr, d = xs.shape
    tm = MOE_TM
    de = wg.shape[-1]
    grid_spec = pltpu.PrefetchScalarGridSpec(
        num_scalar_prefetch=2,
        grid=(nr // tm,),
        in_specs=[pl.BlockSpec((tm, d), lambda i, te, nt: (i, 0)),
                  pl.BlockSpec((None, d, de), lambda i, te, nt: (te[i], 0, 0)),
                  pl.BlockSpec((None, d, de), lambda i, te, nt: (te[i], 0, 0)),
                  pl.BlockSpec((None, de, d), lambda i, te, nt: (te[i], 0, 0))],
        out_specs=pl.BlockSpec((tm, d), lambda i, te, nt: (i, 0)),
        scratch_shapes=[pltpu.VMEM((d, 2 * de), _BF), pltpu.VMEM((de, d), _BF)],
    )
    return pl.pallas_call(
        _moe_body,
        grid_spec=grid_spec,
        out_shape=jax.ShapeDtypeStruct((nr, d), _BF),
        compiler_params=_cparams(("arbitrary",)),
        name="moe_experts",
    )(tile_expert, n_tiles_used, xs, wg, wu, wd)


def _final_body(g2_ref, gf_ref, *refs, n_part, nb):
    o_ref = refs[-1]
    i = pl.program_id(0)
    for part in range(n_part):
        x1_ref, y0_ref, y1_ref, rt_ref = refs[4 * part:4 * part + 4]

        @pl.when((i >= part * nb) & (i < (part + 1) * nb))
        def _():
            w0 = rt_ref[:, 2:3]
            w1 = rt_ref[:, 3:4]
            m = w0 * y0_ref[...].astype(_F32) + w1 * y1_ref[...].astype(_F32)
            o_ref[...] = _rms(x1_ref[...] + g2_ref[...] * m, gf_ref[...])


def _final(parts, g2, gf, tm=512):
    n_part = len(parts)
    nb, l, d = parts[0][0].shape
    nj = l // tm
    in_specs = [pl.BlockSpec((None, 1, d), lambda i, j: (i, 0, 0)),
                pl.BlockSpec((1, d), lambda i, j: (0, 0))]
    args = [g2, gf]
    for part, arrs in enumerate(parts):
        def imap(i, j, part=part):
            own = (i >= part * nb) & (i < (part + 1) * nb)
            return (jnp.clip(i - part * nb, 0, nb - 1), jnp.where(own, j, jnp.where(i < part * nb, 0, nj - 1)), 0)
        for a in arrs:
            in_specs.append(pl.BlockSpec((None, tm, a.shape[-1]), imap))
            args.append(a)
    return pl.pallas_call(
        functools.partial(_final_body, n_part=n_part, nb=nb),
        grid=(n_part * nb, nj),
        in_specs=in_specs,
        out_specs=pl.BlockSpec((None, tm, d), lambda i, j: (i, j, 0)),
        out_shape=jax.ShapeDtypeStruct((n_part * nb, l, d), _F32),
        compiler_params=_cparams(("parallel", "arbitrary")),
        name="final_norm",
    )(*args)


def _route_plan(eid, rank, tok_ids, counts, tm, nr):
    n_experts = counts.shape[0]
    n_pairs = eid.size
    padded = ((counts + tm - 1) // tm) * tm
    pad_end = jnp.cumsum(padded)
    pad_start = pad_end - padded
    raw_start = jnp.cumsum(counts) - counts
    pos = rank
    for e in range(n_experts):
        pos = pos + jnp.where(eid == e, pad_start[e], 0)
    tile_start = jnp.arange(nr // tm, dtype=jnp.int32) * tm
    tile_expert = jnp.sum(tile_start[:, None] >= pad_end[None, :], axis=-1)
    tile_expert = jnp.minimum(tile_expert, n_experts - 1).astype(jnp.int32)
    _, sorted_tok = lax.sort_key_val((eid * n_pairs + rank).reshape(-1), tok_ids.reshape(-1))
    sel = tile_expert[:, None] == jnp.arange(n_experts, dtype=jnp.int32)[None, :]
    per_tile = lambda v: jnp.sum(jnp.where(sel, v[None, :], 0), axis=-1)
    tile_off = tile_start - per_tile(pad_start)
    tile_j0 = jnp.clip(per_tile(raw_start) + tile_off, 0, n_pairs)
    padded_tok = jnp.concatenate([sorted_tok, jnp.zeros((tm,), jnp.int32)])
    rows = jax.vmap(lambda s: lax.dynamic_slice(padded_tok, (s,), (tm,)))(tile_j0).reshape(nr)
    off = jnp.arange(nr, dtype=jnp.int32) - jnp.repeat(per_tile(pad_start), tm)
    valid = off < jnp.repeat(per_tile(counts), tm)
    filler = jnp.arange(nr, dtype=jnp.int32) % (n_pairs // 2)
    row_token = jnp.where(valid, rows, filler)
    n_used = (pad_end[-1:] // tm).astype(jnp.int32)
    return pos, tile_expert, n_used, row_token


def kernel(x, c, ctx, c_ctx, w_ada, b_ada, norm1_g, norm2_g, w_in, s5_lam_re, s5_lam_im, s5_b_re, s5_b_im, s5_c_re, s5_c_im, s5_log_step, s5_d, s5_w_glu, fourier_w, mix_norm_s5_g, mix_norm_f_g, w_out, moe_w_group, moe_b_group, moe_w_router, moe_b_router, moe_w_gate, moe_w_up, moe_w_down, final_norm_g):
    b, l, d = x.shape
    lc = ctx.shape[1]
    depth = w_ada.shape[0]
    assert depth == 1 and l == FFT_R * FFT_R and lc % S5_CHUNK == 0 and b % 8 == 0
    ds5 = s5_d.shape[-1]
    df = w_in.shape[-1] - ds5
    n_exp = moe_w_gate.shape[1]
    row = lambda a: a.reshape(1, -1)

    cond = jnp.concatenate([c, c_ctx[None, :], jnp.zeros((7, d), _F32)], axis=0)
    mod = _adaln(cond, w_ada[0], b_ada[0])
    sh1, sc1, g1, sh2, sc2, g2 = [mod[:b, i * d:(i + 1) * d].reshape(b, 1, d) for i in range(6)]
    csh1 = mod[b:b + 1, 0:d]
    csc1 = mod[b:b + 1, d:2 * d]

    w_in_bf = w_in[0].astype(_BF)
    m1 = jnp.asarray(_fft_stage1_matrix()).astype(_BF)
    m2 = jnp.asarray(_fft_stage2_matrices()).astype(_BF)
    perm = jnp.asarray(_chunk_row_perm())
    zs, gc = _inproj(x, sh1, sc1, row(norm1_g[0]), w_in_bf, m1)
    u_ctx = _inproj_ctx(ctx, csh1, csc1, row(norm1_g[0]), w_in_bf[:, :ds5], perm)
    u_lat = _s5_pack(zs, perm)

    fg = df // FOURIER_GROUPS
    cc = np.arange(fg)
    ang = 2.0 * np.pi * ((cc[:, None] * cc[None, :]) % fg) / fg
    scale = 1.0 / math.sqrt(l * fg)
    cw = jnp.einsum("cm,gmd->gcd", jnp.asarray(np.cos(ang) * scale, _F32), fourier_w[0], precision=_HI)
    sw = jnp.einsum("cm,gmd->gcd", jnp.asarray(np.sin(ang) * scale, _F32), fourier_w[0], precision=_HI)
    fn = _fft2(gc, m2, cw.astype(_BF), sw.astype(_BF), row(mix_norm_f_g[0]))

    ws, wy, at = _s5_params(s5_lam_re[0], s5_lam_im[0], s5_b_re[0], s5_b_im[0],
                            s5_c_re[0], s5_c_im[0], s5_log_step[0])
    y_g = _s5(u_ctx, u_lat, ws, wy, at, b)

    n_rt = N_EXPERT_GROUPS * (1 + EXPERTS_PER_GROUP)
    w_rt = jnp.concatenate([moe_w_group[0], moe_w_router[0].reshape(d, -1)], axis=-1)
    w_rt = jnp.pad(w_rt, ((0, 0), (0, ROUTE_LANES - n_rt)))
    w_rt_hi = w_rt.astype(_BF)
    w_rt = jnp.stack([w_rt_hi, (w_rt - w_rt_hi.astype(_F32)).astype(_BF)])
    b_rt =jnp.pad(jnp.concatenate([moe_b_group[0], moe_b_router[0].reshape(-1)]), (0, ROUTE_LANES - n_rt))
    tm = MOE_TM
    nb = b // MOE_PARTS
    n_tok = nb * l
    nr = 2 * n_tok + n_exp * tm
    take_rows = lambda a, idx: a.at[idx].get(mode="promise_in_bounds")
    wglu_bf, wo_bf = s5_w_glu[0].astype(_BF), w_out[0].astype(_BF)
    parts = []
    for part in range(MOE_PARTS):
        x1, tmod, rt, rtt, cnt = _post(x, zs, y_g, fn, row(s5_d[0]), wglu_bf, row(mix_norm_s5_g[0]), wo_bf,
                                       g1, row(norm2_g[0]), sh2, sc2, w_rt, row(b_rt), part * nb, nb)
        rec = rtt.transpose(1, 0, 2).reshape(ROUTE_ROWS, n_tok)
        eid = rec[0:2].astype(jnp.int32)
        rank = rec[4:6].astype(jnp.int32)
        tok_ids = jnp.broadcast_to(jnp.arange(n_tok, dtype=jnp.int32), (2, n_tok))
        pos, tile_expert, n_used, row_token = _route_plan(eid, rank, tok_ids,
                                                          cnt[0, :n_exp].astype(jnp.int32), tm, nr)
        xs = take_rows(tmod.reshape(n_tok, d), row_token)
        ys = _moe(tile_expert, n_used, xs, moe_w_gate[0], moe_w_up[0], moe_w_down[0])
        y0 = take_rows(ys, pos[0]).reshape(nb, l, d)
        y1 = take_rows(ys, pos[1]).reshape(nb, l, d)
        parts.append((x1, y0, y1, rt))
    return _final(parts, g2, row(final_norm_g))
```

```python
import functools
import math

import numpy as np
import jax
import jax.numpy as jnp
from jax import lax
from jax.experimental import pallas as pl
from jax.experimental.pallas import tpu as pltpu

EPS = 1e-6
S5_GROUP = 16
S5_STATE = 64
S5_CHUNK = 16
S5_SCAN_GROUPS = 4
FOURIER_GROUPS = 4
N_EXPERT_GROUPS = 4
EXPERTS_PER_GROUP = 8
FFT_R = 64
FFT_BLK = 8
MOE_TM = 512
MOE_PARTS = 1
ROUTE_LANES = 128
ROUTE_ROWS = 8
VMEM_LIMIT = 56 * 1024 * 1024

_HI = lax.Precision.HIGHEST
_BF = jnp.bfloat16
_F32 = jnp.float32


def _cparams(sem):
    return pltpu.CompilerParams(dimension_semantics=sem, vmem_limit_bytes=VMEM_LIMIT)


def _dot(a, b):
    return jnp.dot(a, b, preferred_element_type=_F32)


def _rms(x, g):
    return x * lax.rsqrt(jnp.mean(x * x, axis=-1, keepdims=True) + EPS) * g


def _adaln_body(c_ref, w_ref, b_ref, o_ref):
    c = c_ref[...]
    a = c * jax.nn.sigmoid(c)
    o_ref[...] = jnp.dot(a, w_ref[...], preferred_element_type=_F32, precision=_HI) + b_ref[...]


def _adaln(cond, w, b):
    m, d = cond.shape
    n = w.shape[1]
    tn = 768
    return pl.pallas_call(
        _adaln_body,
        grid=(n // tn,),
        in_specs=[pl.BlockSpec((m, d), lambda j: (0, 0)),
                  pl.BlockSpec((d, tn), lambda j: (0, j)),
                  pl.BlockSpec((1, tn), lambda j: (0, j))],
        out_specs=pl.BlockSpec((m, tn), lambda j: (0, j)),
        out_shape=jax.ShapeDtypeStruct((m, n), _F32),
        compiler_params=_cparams(("arbitrary",)),
        name="adaln",
    )(cond, w, b.reshape(1, n))


def _inproj_body(x_ref, sh_ref, sc_ref, g_ref, w_ref, m1_ref, zs_ref, gc_ref):
    r, blk, d = x_ref.shape
    x = x_ref[...].reshape(r * blk, d)
    h = _rms(x, g_ref[...]) * (1.0 + sc_ref[...]) + sh_ref[...]
    z = _dot(h.astype(_BF), w_ref[...])
    ds5 = zs_ref.shape[-1]
    zs_ref[...] = z[:, :ds5].reshape(r, blk, ds5)
    v = z[:, ds5:].astype(_BF)
    g1 = _dot(m1_ref[...], v)
    gc_ref[...] = g1.astype(_BF).reshape(r, 2 * blk, v.shape[-1])


def _inproj(x, sh, sc, g, w_bf, m1):
    b, l, d = x.shape
    r, blk = FFT_R, FFT_BLK
    nj = r // blk
    dmix = w_bf.shape[1]
    ds5 = dmix // 2
    df = dmix - ds5
    x4 = x.reshape(b, r, r, d)
    zs, gc = pl.pallas_call(
        _inproj_body,
        grid=(b, nj),
        in_specs=[pl.BlockSpec((None, r, blk, d), lambda i, j: (i, 0, j, 0)),
                  pl.BlockSpec((None, 1, d), lambda i, j: (i, 0, 0)),
                  pl.BlockSpec((None, 1, d), lambda i, j: (i, 0, 0)),
                  pl.BlockSpec((1, d), lambda i, j: (0, 0)),
                  pl.BlockSpec((d, dmix), lambda i, j: (0, 0)),
                  pl.BlockSpec(m1.shape, lambda i, j: (0, 0))],
        out_specs=[pl.BlockSpec((None, r, blk, ds5), lambda i, j: (i, 0, j, 0)),
                   pl.BlockSpec((None, r, 2 * blk, df), lambda i, j: (i, 0, j, 0))],
        out_shape=[jax.ShapeDtypeStruct((b, r, r, ds5), _F32),
                   jax.ShapeDtypeStruct((b, r, 2 * r, df), _BF)],
        compiler_params=_cparams(("parallel", "arbitrary")),
        name="inproj",
    )(x4, sh, sc, g, w_bf, m1)
    return zs.reshape(b, l, ds5), gc


def _seg_transpose(arrs):
    n = len(arrs)
    seg = lax.broadcasted_iota(jnp.int32, arrs[0].shape, 1) // S5_GROUP
    d = n // 2
    while d >= 1:
        keep = (seg & d) == 0
        new = list(arrs)
        for i in range(n):
            if i & d == 0:
                a, b = arrs[i], arrs[i + d]
                new[i] = jnp.where(keep, a, pltpu.roll(b, S5_GROUP * d, 1))
                new[i + d] = jnp.where(keep, pltpu.roll(a, 128 - S5_GROUP * d, 1), b)
        arrs = new
        d //= 2
    return arrs


@functools.lru_cache(maxsize=None)
def _chunk_row_perm():
    t = S5_CHUNK
    p = np.zeros((t * t, t * t), np.float32)
    for c in range(t):
        for tl in range(t):
            p[tl * t + c, c * t + tl] = 1.0
    return p.astype(jnp.bfloat16)


def _pack_chunks(z_bf, p_ref, u_ref):
    t = S5_CHUNK
    nsub = z_bf.shape[0] // (t * t)
    pieces = [_dot(p_ref[...], z_bf[s * t * t:(s + 1) * t * t, :]) for s in range(nsub)]
    a = []
    for tl in range(t):
        rows = [pc[tl * t:(tl + 1) * t, :] for pc in pieces]
        a.append(rows[0] if nsub == 1 else jnp.concatenate(rows, axis=0))
    lanes = 128
    gpb = lanes // S5_GROUP
    for j in range(z_bf.shape[1] // lanes):
        for hi in range(t // gpb):
            outs = _seg_transpose([a[gpb * hi + tlo][:, lanes * j:lanes * (j + 1)] for tlo in range(gpb)])
            for glo in range(gpb):
                u_ref[gpb * j + glo, :, lanes * hi:lanes * (hi + 1)] = outs[glo].astype(u_ref.dtype)


def _inproj_ctx_body(x_ref, sh_ref, sc_ref, g_ref, w_ref, p_ref, u_ref):
    h = _rms(x_ref[...], g_ref[...]) * (1.0 + sc_ref[...]) + sh_ref[...]
    z = _dot(h.astype(_BF), w_ref[...])
    _pack_chunks(z.astype(_BF), p_ref, u_ref)


def _inproj_ctx(ctx, sh, sc, g, w_s5_bf, perm):
    b, lc, d = ctx.shape
    ds5 = w_s5_bf.shape[1]
    n_grp = ds5 // S5_GROUP
    kk = S5_CHUNK * S5_GROUP
    return pl.pallas_call(
        _inproj_ctx_body,
        grid=(b,),
        in_specs=[pl.BlockSpec((None, lc, d), lambda i: (i, 0, 0)),
                  pl.BlockSpec((1, d), lambda i: (0, 0)),
                  pl.BlockSpec((1, d), lambda i: (0, 0)),
                  pl.BlockSpec((1, d), lambda i: (0, 0)),
                  pl.BlockSpec((d, ds5), lambda i: (0, 0)),
                  pl.BlockSpec(perm.shape, lambda i: (0, 0))],
        out_specs=pl.BlockSpec((n_grp, lc // S5_CHUNK, kk), lambda i: (0, 0, i)),
        out_shape=jax.ShapeDtypeStruct((n_grp, lc // S5_CHUNK, b * kk), _BF),
        compiler_params=_cparams(("arbitrary",)),
        name="inproj_ctx",
    )(ctx, sh, sc, g, w_s5_bf, perm)


def _s5_pack_body(z_ref, p_ref, u_ref):
    _pack_chunks(z_ref[...].astype(_BF), p_ref, u_ref)


def _s5_pack(zs, perm, tok=2048):
    b, l, ds5 = zs.shape
    n_grp = ds5 // S5_GROUP
    kk = S5_CHUNK * S5_GROUP
    return pl.pallas_call(
        _s5_pack_body,
        grid=(b, l // tok),
        in_specs=[pl.BlockSpec((None, tok, ds5), lambda i, j: (i, j, 0)),
                  pl.BlockSpec(perm.shape, lambda i, j: (0, 0))],
        out_specs=pl.BlockSpec((n_grp, tok // S5_CHUNK, kk), lambda i, j: (0, j, i)),
        out_shape=jax.ShapeDtypeStruct((n_grp, l // S5_CHUNK, b * kk), _BF),
        compiler_params=_cparams(("parallel", "arbitrary")),
        name="s5_pack",
    )(zs, perm)


@functools.lru_cache(maxsize=None)
def _fft_stage1_matrix():
    r, blk = FFT_R, FFT_BLK
    k1 = np.arange(r)[:, None]
    l1 = np.arange(r)[None, :]
    ang = 2.0 * np.pi * ((k1 * l1) % r) / r
    f = np.stack([np.cos(ang), -np.sin(ang)], axis=1)
    m = np.einsum("kal,pq->kaplq", f, np.eye(blk))
    return m.reshape(r * 2 * blk, r * blk).astype(np.float32)


@functools.lru_cache(maxsize=None)
def _fft_stage2_matrices():
    r, blk = FFT_R, FFT_BLK
    n = r * r
    nt = r // blk
    l2 = np.arange(r)
    k2 = np.arange(r)
    out = np.zeros((nt, 2, r, blk, blk, r // blk, 2, blk), np.float32)
    for i in range(nt):
        for kl in range(blk):
            k1 = blk * i + kl
            ang = 2.0 * np.pi * (((k2[:, None] * l2[None, :] * r) + l2[None, :] * k1) % n) / n
            tr, ti = np.cos(ang), -np.sin(ang)
            tr = tr.reshape(r, r // blk, blk)
            ti = ti.reshape(r, r // blk, blk)
            out[i, 0, :, kl, kl, :, 0, :] = tr
            out[i, 0, :, kl, kl, :, 1, :] = -ti
            out[i, 1, :, kl, kl, :, 0, :] = ti
            out[i, 1, :, kl, kl, :, 1, :] = tr
    return out.reshape(nt, 2 * r * blk, blk * 2 * r)


def _fft2_body(gc_ref, m2_ref, cw_ref, sw_ref, g_ref, o_ref):
    kb, rr, df = gc_ref.shape
    gc = gc_ref[...].reshape(kb * rr, df)
    x = _dot(m2_ref[...], gc)
    half = x.shape[0] // 2
    xr = x[:half].astype(_BF)
    xi = x[half:].astype(_BF)
    ng = cw_ref.shape[0]
    fg = df // ng
    parts = []
    for g in range(ng):
        sl = slice(g * fg, (g + 1) * fg)
        parts.append(_dot(xr[:, sl], cw_ref[g]) + _dot(xi[:, sl], sw_ref[g]))
    f = jnp.concatenate(parts, axis=-1)
    fn = _rms(f, g_ref[...])
    o_ref[...] = fn.reshape(o_ref.shape)


def _fft2(gc, m2, cw, sw, gf):
    b, r, rr, df = gc.shape
    blk = FFT_BLK
    nt = r // blk
    out = pl.pallas_call(
        _fft2_body,
        grid=(b, nt),
        in_specs=[pl.BlockSpec((None, blk, rr, df), lambda i, j: (i, j, 0, 0)),
                  pl.BlockSpec((None,) + m2.shape[1:], lambda i, j: (j, 0, 0)),
                  pl.BlockSpec(cw.shape, lambda i, j: (0, 0, 0)),
                  pl.BlockSpec(sw.shape, lambda i, j: (0, 0, 0)),
                  pl.BlockSpec((1, df), lambda i, j: (0, 0))],
        out_specs=pl.BlockSpec((None, r, blk, df), lambda i, j: (i, 0, j, 0)),
        out_shape=jax.ShapeDtypeStruct((b, r, r, df), _F32),
        compiler_params=_cparams(("parallel", "arbitrary")),
        name="fft2",
    )(gc, m2, cw, sw, gf)
    return out.reshape(b, r * r, df)


def _s5_body(uc_ref, ul_ref, ws_ref, wy_ref, at_ref, y_ref,
             s_re, s_im, ha_re, hb_re, ha_im, hb_im, *, nb):
    p = S5_STATE
    gs, n_ctx, _ = uc_ref.shape
    n_lat = ul_ref.shape[1]
    kk = ws_ref.shape[1]
    for gi in range(gs):
        ws = ws_ref[gi]
        for b in range(nb):
            sl = slice(b * kk, (b + 1) * kk)
            sc = _dot(uc_ref[gi, :, sl], ws)
            sl_ = _dot(ul_ref[gi, :, sl], ws)
            s_re[gi, pl.ds(b, n_ctx, stride=nb), :] = sc[:, :2 * p]
            s_im[gi, pl.ds(b, n_ctx, stride=nb), :] = sc[:, 2 * p:]
            s_re[gi, pl.ds(n_ctx * nb + b, n_lat, stride=nb), :] = sl_[:, :2 * p]
            s_im[gi, pl.ds(n_ctx * nb + b, n_lat, stride=nb), :] = sl_[:, 2 * p:]

    a_re = [at_ref[gi, 0:1, :] for gi in range(gs)]
    a_im = [at_ref[gi, 1:2, :] for gi in range(gs)]
    is_f = lax.broadcasted_iota(jnp.int32, (nb, 2 * p), 1) < p
    n_all = n_ctx + n_lat

    def load(gi, cf, cr):
        rf = pl.multiple_of(cf * nb, nb)
        rr = pl.multiple_of(cr * nb, nb)
        return (jnp.where(is_f, s_re[gi, pl.ds(rf, nb), :], s_re[gi, pl.ds(rr, nb), :]),
                jnp.where(is_f, s_im[gi, pl.ds(rf, nb), :], s_im[gi, pl.ds(rr, nb), :]))

    def advance(gi, h_re, h_im, x_re, x_im):
        return (a_re[gi] * h_re - a_im[gi] * h_im + x_re, a_re[gi] * h_im + a_im[gi] * h_re + x_im)

    def ctx_step(step, carry):
        out = []
        for gi in range(gs):
            x_re, x_im = load(gi, step, n_ctx - 1 - step)
            out.append(advance(gi, *carry[gi], x_re, x_im))
        return tuple(out)

    def lat_step(step, carry):
        rf = pl.multiple_of(step * nb, nb)
        rr = pl.multiple_of((n_lat - 1 - step) * nb, nb)
        out = []
        for gi in range(gs):
            h_re, h_im = carry[gi]
            ha_re[gi, pl.ds(rf, nb), :] = h_re
            hb_re[gi, pl.ds(rr, nb), :] = h_re
            ha_im[gi, pl.ds(rf, nb), :] = h_im
            hb_im[gi, pl.ds(rr, nb), :] = h_im
            x_re, x_im = load(gi, n_ctx + step, n_all - 1 - step)
            out.append(advance(gi, h_re, h_im, x_re, x_im))
        return tuple(out)

    zero = jnp.zeros((nb, 2 * p), _F32)
    carry = lax.fori_loop(0, n_ctx, ctx_step, tuple((zero, zero) for _ in range(gs)), unroll=2)
    lax.fori_loop(0, n_lat, lat_step, carry, unroll=2)

    for gi in range(gs):
        wy = wy_ref[gi]
        for b in range(nb):
            rows = pl.ds(b, n_lat, stride=nb)
            hin = jnp.concatenate([ha_re[gi, rows, :], hb_re[gi, rows, :],
                                   ha_im[gi, rows, :], hb_im[gi, rows, :]], axis=-1)
            lhs = jnp.concatenate([ul_ref[gi, :, b * kk:(b + 1) * kk], hin.astype(_BF)], axis=-1)
            y_ref[gi, :, b * kk:(b + 1) * kk] = _dot(lhs, wy).astype(y_ref.dtype)


def _s5(u_ctx, u_lat, ws, wy, at, nb):
    ng, n_ctx, w = u_ctx.shape
    n_lat = u_lat.shape[1]
    gs = S5_SCAN_GROUPS
    body = functools.partial(_s5_body, nb=nb)
    hs = pltpu.VMEM((gs, n_lat * nb, 2 * S5_STATE), _F32)
    ss = pltpu.VMEM((gs, (n_ctx + n_lat) * nb, 2 * S5_STATE), _F32)
    grp = lambda a: pl.BlockSpec((gs,) + a.shape[1:], lambda g: (g, 0, 0))
    return pl.pallas_call(
        body,
        grid=(ng // gs,),
        in_specs=[grp(u_ctx), grp(u_lat), grp(ws), grp(wy), grp(at)],
        out_specs=pl.BlockSpec((gs, n_lat, w), lambda g: (g, 0, 0)),
        out_shape=jax.ShapeDtypeStruct((ng, n_lat, w), _BF),
        scratch_shapes=[ss, ss, hs, hs, hs, hs],
        compiler_params=_cparams(("parallel",)),
        name="s5_scan",
    )(u_ctx, u_lat, ws, wy, at)


def _s5_param_body(lr_ref, li_ref, ls_ref, btr_ref, bti_ref, cr_ref, ci_ref, ws_ref, wy_ref, at_ref):
    t, hh, p = S5_CHUNK, S5_GROUP, S5_STATE
    kk = t * hh
    lr, li = lr_ref[...], li_ref[...]
    dt = jnp.exp(ls_ref[...])
    ldt, idt = lr * dt, li * dt
    mag = jnp.exp(ldt)
    ab_re, ab_im = mag * jnp.cos(idt), mag * jnp.sin(idt)
    den = lr * lr + li * li
    nr, ni = ab_re - 1.0, ab_im
    q_re = (nr * lr + ni * li) / den
    q_im = (ni * lr - nr * li) / den
    kf = lax.broadcasted_iota(jnp.int32, (2 * t, p), 0).astype(_F32)

    ap_re, ap_im, bb_re, bb_im = [], [], [], []
    for d in range(2):
        pm = jnp.exp(kf * ldt[d:d + 1, :])
        ap_re.append(pm * jnp.cos(kf * idt[d:d + 1, :]))
        ap_im.append(pm * jnp.sin(kf * idt[d:d + 1, :]))
        bb_re.append(q_re[d:d + 1, :] * btr_ref[d] - q_im[d:d + 1, :] * bti_ref[d])
        bb_im.append(q_re[d:d + 1, :] * bti_ref[d] + q_im[d:d + 1, :] * btr_ref[d])

    def rep(a, ks):
        return jnp.concatenate([jnp.broadcast_to(a[k:k + 1, :], (hh, p)) for k in ks], axis=0)

    def tile(a, n):
        return jnp.concatenate([a] * n, axis=0)

    def cmul(xr, xi, yr, yi):
        return xr * yr - xi * yi, xr * yi + xi * yr

    ks_f = list(range(t + 1))
    ks_r = list(range(t, -1, -1))
    caf_re, caf_im = cmul(tile(cr_ref[0], t + 1), tile(ci_ref[0], t + 1), rep(ap_re[0], ks_f), rep(ap_im[0], ks_f))
    car_re, car_im = cmul(tile(cr_ref[1], t + 1), tile(ci_ref[1], t + 1), rep(ap_re[1], ks_r), rep(ap_im[1], ks_r))

    def lag_blocks(bre, bim, ca_re, ca_im):
        lhs = jnp.concatenate([bre, bim], axis=1)
        rhs = jnp.concatenate([ca_re, -ca_im], axis=1)
        return lax.dot_general(lhs, rhs, (((1,), (1,)), ((), ())), precision=_HI,
                               preferred_element_type=_F32)

    w_f = lag_blocks(bb_re[0], bb_im[0], caf_re[:kk], caf_im[:kk])
    w_r = lag_blocks(bb_re[1], bb_im[1], car_re[hh:], car_im[hh:])
    lane = lax.broadcasted_iota(jnp.int32, (hh, kk), 1)
    for tlp in range(t):
        sf = hh * tlp
        sr = hh * (t - 1 - tlp)
        a = w_f if sf == 0 else jnp.where(lane >= sf, pltpu.roll(w_f, sf, 1), 0.0)
        b = w_r if sr == 0 else jnp.where(lane < kk - sr, pltpu.roll(w_r, kk - sr, 1), 0.0)
        wy_ref[hh * tlp:hh * (tlp + 1), :] = (a + b).astype(wy_ref.dtype)

    zeros = jnp.zeros((p, kk), _F32)
    blocks = [caf_re[hh:].T, zeros, zeros, car_re[:kk].T, (-caf_im[hh:]).T, zeros, zeros, (-car_im[:kk]).T]
    for i, blk in enumerate(blocks):
        wy_ref[kk + p * i:kk + p * (i + 1), :] = blk.astype(wy_ref.dtype)

    pf = list(range(t - 1, -1, -1))
    pr = list(range(t))
    f_re, f_im = cmul(rep(ap_re[0], pf), rep(ap_im[0], pf), tile(bb_re[0], t), tile(bb_im[0], t))
    r_re, r_im = cmul(rep(ap_re[1], pr), rep(ap_im[1], pr), tile(bb_re[1], t), tile(bb_im[1], t))
    ws_ref[...] = jnp.concatenate([f_re, r_re, f_im, r_im], axis=1).astype(ws_ref.dtype)

    a_re = jnp.concatenate([ap_re[0][t:t + 1, :], ap_re[1][t:t + 1, :]], axis=1)
    a_im = jnp.concatenate([ap_im[0][t:t + 1, :], ap_im[1][t:t + 1, :]], axis=1)
    row = lax.broadcasted_iota(jnp.int32, at_ref.shape, 0)
    at_ref[...] = jnp.where(row == 0, a_re, jnp.where(row == 1, a_im, 0.0))


def _s5_params(lam_re, lam_im, b_re, b_im, c_re, c_im, log_step):
    _, ng, p = lam_re.shape
    hh = b_re.shape[-1]
    kk = S5_CHUNK * hh
    gd = lambda a: jnp.swapaxes(a.astype(_F32), 0, 1)
    args = (gd(lam_re), gd(lam_im), gd(log_step)[..., None],
            jnp.swapaxes(gd(b_re), 2, 3), jnp.swapaxes(gd(b_im), 2, 3), gd(c_re), gd(c_im))
    spec = lambda a: pl.BlockSpec((None,) + a.shape[1:], lambda g: (g,) + (0,) * (a.ndim - 1))
    return pl.pallas_call(
        _s5_param_body,
        grid=(ng,),
        in_specs=[spec(a) for a in args],
        out_specs=[pl.BlockSpec((None, kk, 4 * p), lambda g: (g, 0, 0)),
                   pl.BlockSpec((None, kk + 8 * p, kk), lambda g: (g, 0, 0)),
                   pl.BlockSpec((None, 8, 2 * p), lambda g: (g, 0, 0))],
        out_shape=[jax.ShapeDtypeStruct((ng, kk, 4 * p), _BF),
                   jax.ShapeDtypeStruct((ng, kk + 8 * p, kk), _BF),
                   jax.ShapeDtypeStruct((ng, 8, 2 * p), _F32)],
        compiler_params=_cparams(("parallel",)),
        name="s5_params",
    )(*args)


def _post_body(x_ref, u_ref, yg_ref, fn_ref, d_ref, wglu_ref, gs_ref, wo_ref, g1_ref,
               n2_ref, sh_ref, sc_ref, wr_ref, br_ref, tri_ref, x1_ref, t_ref, rt_ref, rtt_ref, cnt_ref,
               y_scr, cnt_scr):
    ds5 = u_ref.shape[-1]
    first = (pl.program_id(0) == 0) & (pl.program_id(1) == 0)

    @pl.when(first)
    def _():
        cnt_scr[...] = jnp.zeros_like(cnt_scr)

    t_chunk = S5_CHUNK
    n_chunk = yg_ref.shape[1]
    lanes = 128
    gpb = lanes // S5_GROUP
    for j in range(ds5 // lanes):
        for hi in range(t_chunk // gpb):
            outs = _seg_transpose([yg_ref[gpb * j + glo, :, lanes * hi:lanes * (hi + 1)].astype(_F32)
                                   for glo in range(gpb)])
            for tlo in range(gpb):
                y_scr[j, pl.ds(gpb * hi + tlo, n_chunk, stride=t_chunk), :] = outs[tlo]
    y_s5 = jnp.concatenate([y_scr[j] for j in range(ds5 // lanes)], axis=-1)

    y = y_s5 + d_ref[...] * u_ref[...]
    y = jax.nn.gelu(y, approximate=True)
    y = y * jax.nn.sigmoid(_dot(y.astype(_BF), wglu_ref[...]))
    yn = _rms(y, gs_ref[...]).astype(_BF)
    mix = _dot(yn, wo_ref[0:ds5, :]) + _dot(fn_ref[...].astype(_BF), wo_ref[ds5:, :])
    x1 = x_ref[...] + g1_ref[...] * mix
    x1_ref[...] = x1
    t = _rms(x1, n2_ref[...]) * (1.0 + sc_ref[...]) + sh_ref[...]
    t_hi = t.astype(_BF)
    t_ref[...] = t_hi

    ng, epg = N_EXPERT_GROUPS, EXPERTS_PER_GROUP
    t_lo = (t - t_hi.astype(_F32)).astype(_BF)
    logits = (_dot(t_hi, wr_ref[0]) + _dot(t_lo, wr_ref[0]) + _dot(t_hi, wr_ref[1])) + br_ref[...]
    lane = lax.broadcasted_iota(jnp.int32, logits.shape, 1)
    neg = jnp.float32(-jnp.inf)
    big = jnp.int32(1 << 20)
    gl = jnp.where(lane < ng, logits, neg)
    gmax = jnp.max(gl, axis=-1, keepdims=True)
    gidx = jnp.min(jnp.where(gl == gmax, lane, big), axis=-1, keepdims=True)
    gw = 1.0 / jnp.sum(jnp.where(lane < ng, jnp.exp(logits - gmax), 0.0), axis=-1, keepdims=True)
    lo = ng + gidx * epg
    el = jnp.where((lane >= lo) & (lane < lo + epg), logits, neg)
    v0 = jnp.max(el, axis=-1, keepdims=True)
    i0 = jnp.min(jnp.where(el == v0, lane, big), axis=-1, keepdims=True)
    el1 = jnp.where(lane == i0, neg, el)
    v1 = jnp.max(el1, axis=-1, keepdims=True)
    i1 = jnp.min(jnp.where(el1 == v1, lane, big), axis=-1, keepdims=True)
    p0 = 1.0 / (1.0 + jnp.exp(v1 - v0))
    w0 = gw * p0
    w1 = gw * (1.0 - p0)
    e0 = i0 - ng
    e1 = i1 - ng

    oh0 = lane == e0
    oh1 = lane == e1
    oh = (oh0 | oh1).astype(_F32)
    prefix = _dot(tri_ref[...], oh.astype(_BF)) + cnt_scr[...]
    r0 = jnp.sum(jnp.where(oh0, prefix, 0.0), axis=-1, keepdims=True)
    r1 = jnp.sum(jnp.where(oh1, prefix, 0.0), axis=-1, keepdims=True)
    cnt = cnt_scr[...] + jnp.sum(oh, axis=0, keepdims=True)
    cnt_scr[...] = cnt
    cnt_ref[...] = cnt

    vals = (e0.astype(_F32), e1.astype(_F32), w0, w1, r0, r1)
    rt = jnp.zeros(logits.shape, _F32)
    for k, v in enumerate(vals):
        rt = jnp.where(lane == k, v, rt)
    rt_ref[...] = rt
    rtt_ref[...] = rt.T[0:rtt_ref.shape[0], :]


@functools.lru_cache(maxsize=None)
def _strict_lower_ones(n):
    return np.tril(np.ones((n, n), np.float32), -1).astype(jnp.bfloat16)


def _post(x, u, y_g, fn, s5_d, wglu_bf, gs, wo_bf, g1, n2, sh2, sc2, wr, br, b0, nb, tm=512):
    b, l, d = x.shape
    ds5 = u.shape[-1]
    n_grp, _, w = y_g.shape
    kk = w // b
    tri = jnp.asarray(_strict_lower_ones(tm))
    tok = lambda w: pl.BlockSpec((None, tm, w), lambda i, j: (i + b0, j, 0))
    otok = lambda w: pl.BlockSpec((None, tm, w), lambda i, j: (i, j, 0))
    per_b = pl.BlockSpec((None, 1, d), lambda i, j: (i + b0, 0, 0))
    full = lambda a: pl.BlockSpec(a.shape, lambda i, j: (0,) * a.ndim)
    ygs = pl.BlockSpec((n_grp, tm // S5_CHUNK, kk), lambda i, j: (0, j, i + b0))
    return pl.pallas_call(
        _post_body,
        grid=(nb, l // tm),
        in_specs=[tok(d), tok(ds5), ygs, tok(fn.shape[-1]), full(s5_d), full(wglu_bf),
                  full(gs), full(wo_bf), per_b, full(n2), per_b, per_b, full(wr), full(br), full(tri)],
        out_specs=[otok(d), otok(d), otok(ROUTE_LANES),
                   pl.BlockSpec((None, ROUTE_ROWS, tm), lambda i, j: (i, 0, j)),
                   pl.BlockSpec((1, ROUTE_LANES), lambda i, j: (0, 0))],
        out_shape=[jax.ShapeDtypeStruct((nb, l, d), _F32),
                   jax.ShapeDtypeStruct((nb, l, d), _BF),
                   jax.ShapeDtypeStruct((nb, l, ROUTE_LANES), _F32),
                   jax.ShapeDtypeStruct((nb, ROUTE_ROWS, l), _F32),
                   jax.ShapeDtypeStruct((1, ROUTE_LANES), _F32)],
        scratch_shapes=[pltpu.VMEM((ds5 // 128, tm, 128), _F32),
                        pltpu.VMEM((1, ROUTE_LANES), _F32)],
        compiler_params=_cparams(("arbitrary", "arbitrary")),
        name="post_mixer",
    )(x, u, y_g, fn, s5_d, wglu_bf, gs, wo_bf, g1, n2, sh2, sc2, wr, br, tri)


def _moe_body(te_ref, nt_ref, xs_ref, wg_ref, wu_ref, wd_ref, o_ref, wgu_bf, wd_bf):
    i = pl.program_id(0)
    de = wg_ref.shape[-1]
    used = i < nt_ref[0]
    new_expert = (i == 0) | (te_ref[i] != te_ref[jnp.maximum(i - 1, 0)])

    @pl.when(used & new_expert)
    def _():
        wgu_bf[:, 0:de] = wg_ref[...].astype(_BF)
        wgu_bf[:, de:2 * de] = wu_ref[...].astype(_BF)
        wd_bf[...] = wd_ref[...].astype(_BF)

    @pl.when(used)
    def _():
        h = _dot(xs_ref[...], wgu_bf[...])
        hg, hu = h[:, 0:de], h[:, de:2 * de]
        a = hg * jax.nn.sigmoid(hg) * hu
        o_ref[...] = _dot(a.astype(_BF), wd_bf[...]).astype(o_ref.dtype)

    @pl.when(jnp.logical_not(used))
    def _():
        o_ref[...] = jnp.zeros_like(o_ref)


def _moe(tile_expert, n_tiles_used, xs, wg, wu, wd):
    nr, d = xs.shape
    tm = MOE_TM
    de = wg.shape[-1]
    grid_spec = pltpu.PrefetchScalarGridSpec(
        num_scalar_prefetch=2,
        grid=(nr // tm,),
        in_specs=[pl.BlockSpec((tm, d), lambda i, te, nt: (i, 0)),
                  pl.BlockSpec((None, d, de), lambda i, te, nt: (te[i], 0, 0)),
                  pl.BlockSpec((None, d, de), lambda i, te, nt: (te[i], 0, 0)),
                  pl.BlockSpec((None, de, d), lambda i, te, nt: (te[i], 0, 0))],
        out_specs=pl.BlockSpec((tm, d), lambda i, te, nt: (i, 0)),
        scratch_shapes=[pltpu.VMEM((d, 2 * de), _BF), pltpu.VMEM((de, d), _BF)],
    )
    return pl.pallas_call(
        _moe_body,
        grid_spec=grid_spec,
        out_shape=jax.ShapeDtypeStruct((nr, d), _BF),
        compiler_params=_cparams(("arbitrary",)),
        name="moe_experts",
    )(tile_expert, n_tiles_used, xs, wg, wu, wd)


def _final_body(g2_ref, gf_ref, *refs, n_part, nb):
    o_ref = refs[-1]
    i = pl.program_id(0)
    for part in range(n_part):
        x1_ref, y0_ref, y1_ref, rt_ref = refs[4 * part:4 * part + 4]

        @pl.when((i >= part * nb) & (i < (part + 1) * nb))
        def _():
            w0 = rt_ref[:, 2:3]
            w1 = rt_ref[:, 3:4]
            m = w0 * y0_ref[...].astype(_F32) + w1 * y1_ref[...].astype(_F32)
            o_ref[...] = _rms(x1_ref[...] + g2_ref[...] * m, gf_ref[...])


def _final(parts, g2, gf, tm=512):
    n_part = len(parts)
    nb, l, d = parts[0][0].shape
    nj = l // tm
    in_specs = [pl.BlockSpec((None, 1, d), lambda i, j: (i, 0, 0)),
                pl.BlockSpec((1, d), lambda i, j: (0, 0))]
    args = [g2, gf]
    for part, arrs in enumerate(parts):
        def imap(i, j, part=part):
            own = (i >= part * nb) & (i < (part + 1) * nb)
            return (jnp.clip(i - part * nb, 0, nb - 1), jnp.where(own, j, jnp.where(i < part * nb, 0, nj - 1)), 0)
        for a in arrs:
            in_specs.append(pl.BlockSpec((None, tm, a.shape[-1]), imap))
            args.append(a)
    return pl.pallas_call(
        functools.partial(_final_body, n_part=n_part, nb=nb),
        grid=(n_part * nb, nj),
        in_specs=in_specs,
        out_specs=pl.BlockSpec((None, tm, d), lambda i, j: (i, j, 0)),
        out_shape=jax.ShapeDtypeStruct((n_part * nb, l, d), _F32),
        compiler_params=_cparams(("parallel", "arbitrary")),
        name="final_norm",
    )(*args)


def _route_plan(eid, rank, tok_ids, counts, tm, nr):
    n_experts = counts.shape[0]
    n_pairs = eid.size
    padded = ((counts + tm - 1) // tm) * tm
    pad_end = jnp.cumsum(padded)
    pad_start = pad_end - padded
    raw_start = jnp.cumsum(counts) - counts
    pos = rank
    for e in range(n_experts):
        pos = pos + jnp.where(eid == e, pad_start[e], 0)
    tile_start = jnp.arange(nr // tm, dtype=jnp.int32) * tm
    tile_expert = jnp.sum(tile_start[:, None] >= pad_end[None, :], axis=-1)
    tile_expert = jnp.minimum(tile_expert, n_experts - 1).astype(jnp.int32)
    _, sorted_tok = lax.sort_key_val((eid * n_pairs + rank).reshape(-1), tok_ids.reshape(-1))
    sel = tile_expert[:, None] == jnp.arange(n_experts, dtype=jnp.int32)[None, :]
    per_tile = lambda v: jnp.repeat(jnp.sum(jnp.where(sel, v[None, :], 0), axis=-1), tm)
    off = jnp.arange(nr, dtype=jnp.int32) - per_tile(pad_start)
    valid = off < per_tile(counts)
    j = jnp.clip(per_tile(raw_start) + off, 0, n_pairs - 1)
    filler = jnp.arange(nr, dtype=jnp.int32) % (n_pairs // 2)
    row_token = jnp.where(valid, sorted_tok.at[j].get(mode="promise_in_bounds"), filler)
    n_used = (pad_end[-1:] // tm).astype(jnp.int32)
    return pos, tile_expert, n_used, row_token


def kernel(x, c, ctx, c_ctx, w_ada, b_ada, norm1_g, norm2_g, w_in, s5_lam_re, s5_lam_im, s5_b_re, s5_b_im, s5_c_re, s5_c_im, s5_log_step, s5_d, s5_w_glu, fourier_w, mix_norm_s5_g, mix_norm_f_g, w_out, moe_w_group, moe_b_group, moe_w_router, moe_b_router, moe_w_gate, moe_w_up, moe_w_down, final_norm_g):
    b, l, d = x.shape
    lc = ctx.shape[1]
    depth = w_ada.shape[0]
    assert depth == 1 and l == FFT_R * FFT_R and lc % S5_CHUNK == 0 and b % 8 == 0
    ds5 = s5_d.shape[-1]
    df = w_in.shape[-1] - ds5
    n_exp = moe_w_gate.shape[1]
    row = lambda a: a.reshape(1, -1)

    cond = jnp.concatenate([c, c_ctx[None, :], jnp.zeros((7, d), _F32)], axis=0)
    mod = _adaln(cond, w_ada[0], b_ada[0])
    sh1, sc1, g1, sh2, sc2, g2 = [mod[:b, i * d:(i + 1) * d].reshape(b, 1, d) for i in range(6)]
    csh1 = mod[b:b + 1, 0:d]
    csc1 = mod[b:b + 1, d:2 * d]

    w_in_bf = w_in[0].astype(_BF)
    m1 = jnp.asarray(_fft_stage1_matrix()).astype(_BF)
    m2 = jnp.asarray(_fft_stage2_matrices()).astype(_BF)
    perm = jnp.asarray(_chunk_row_perm())
    zs, gc = _inproj(x, sh1, sc1, row(norm1_g[0]), w_in_bf, m1)
    u_ctx = _inproj_ctx(ctx, csh1, csc1, row(norm1_g[0]), w_in_bf[:, :ds5], perm)
    u_lat = _s5_pack(zs, perm)

    fg = df // FOURIER_GROUPS
    cc = np.arange(fg)
    ang = 2.0 * np.pi * ((cc[:, None] * cc[None, :]) % fg) / fg
    scale = 1.0 / math.sqrt(l * fg)
    cw = jnp.einsum("cm,gmd->gcd", jnp.asarray(np.cos(ang) * scale, _F32), fourier_w[0], precision=_HI)
    sw = jnp.einsum("cm,gmd->gcd", jnp.asarray(np.sin(ang) * scale, _F32), fourier_w[0], precision=_HI)
    fn = _fft2(gc, m2, cw.astype(_BF), sw.astype(_BF), row(mix_norm_f_g[0]))

    ws, wy, at = _s5_params(s5_lam_re[0], s5_lam_im[0], s5_b_re[0], s5_b_im[0],
                            s5_c_re[0], s5_c_im[0], s5_log_step[0])
    y_g = _s5(u_ctx, u_lat, ws, wy, at, b)

    n_rt = N_EXPERT_GROUPS * (1 + EXPERTS_PER_GROUP)
    w_rt = jnp.concatenate([moe_w_group[0], moe_w_router[0].reshape(d, -1)], axis=-1)
    w_rt = jnp.pad(w_rt, ((0, 0), (0, ROUTE_LANES - n_rt)))
    w_rt_hi = w_rt.astype(_BF)
    w_rt = jnp.stack([w_rt_hi, (w_rt - w_rt_hi.astype(_F32)).astype(_BF)])
    b_rt =jnp.pad(jnp.concatenate([moe_b_group[0], moe_b_router[0].reshape(-1)]), (0, ROUTE_LANES - n_rt))
    tm = MOE_TM
    nb = b // MOE_PARTS
    n_tok = nb * l
    nr = 2 * n_tok + n_exp * tm
    take_rows = lambda a, idx: a.at[idx].get(mode="promise_in_bounds")
    wglu_bf, wo_bf = s5_w_glu[0].astype(_BF), w_out[0].astype(_BF)
    parts = []
    for part in range(MOE_PARTS):
        x1, tmod, rt, rtt, cnt = _post(x, zs, y_g, fn, row(s5_d[0]), wglu_bf, row(mix_norm_s5_g[0]), wo_bf,
                                       g1, row(norm2_g[0]), sh2, sc2, w_rt, row(b_rt), part * nb, nb)
        rec = rtt.transpose(1, 0, 2).reshape(ROUTE_ROWS, n_tok)
        eid = rec[0:2].astype(jnp.int32)
        rank = rec[4:6].astype(jnp.int32)
        tok_ids = jnp.broadcast_to(jnp.arange(n_tok, dtype=jnp.int32), (2, n_tok))
        pos, tile_expert, n_used, row_token = _route_plan(eid, rank, tok_ids,
                                                          cnt[0, :n_exp].astype(jnp.int32), tm, nr)
        xs = take_rows(tmod.reshape(n_tok, d), row_token)
        ys = _moe(tile_expert, n_used, xs, moe_w_gate[0], moe_w_up[0], moe_w_down[0])
        y0 = take_rows(ys, pos[0]).reshape(nb, l, d)
        y1 = take_rows(ys, pos[1]).reshape(nb, l, d)
        parts.append((x1, y0, y1, rt))
    return _final(parts, g2, row(final_norm_g))
```

```python
import functools
import math

import numpy as np
import jax
import jax.numpy as jnp
from jax import lax
from jax.experimental import pallas as pl
from jax.experimental.pallas import tpu as pltpu

EPS = 1e-6
S5_GROUP = 16
S5_STATE = 64
S5_CHUNK = 16
S5_SCAN_GROUPS = 4
FOURIER_GROUPS = 4
N_EXPERT_GROUPS = 4
EXPERTS_PER_GROUP = 8
FFT_R = 64
FFT_BLK = 8
MOE_TM = 512
MOE_PARTS = 1
ROUTE_LANES = 128
ROUTE_ROWS = 8
VMEM_LIMIT = 56 * 1024 * 1024

_HI = lax.Precision.HIGHEST
_BF = jnp.bfloat16
_F32 = jnp.float32


def _cparams(sem):
    return pltpu.CompilerParams(dimension_semantics=sem, vmem_limit_bytes=VMEM_LIMIT)


def _dot(a, b):
    return jnp.dot(a, b, preferred_element_type=_F32)


def _rms(x, g):
    return x * lax.rsqrt(jnp.mean(x * x, axis=-1, keepdims=True) + EPS) * g


def _adaln_body(c_ref, w_ref, b_ref, o_ref):
    c = c_ref[...]
    a = c * jax.nn.sigmoid(c)
    o_ref[...] = jnp.dot(a, w_ref[...], preferred_element_type=_F32, precision=_HI) + b_ref[...]


def _adaln(cond, w, b):
    m, d = cond.shape
    n = w.shape[1]
    tn = 768
    return pl.pallas_call(
        _adaln_body,
        grid=(n // tn,),
        in_specs=[pl.BlockSpec((m, d), lambda j: (0, 0)),
                  pl.BlockSpec((d, tn), lambda j: (0, j)),
                  pl.BlockSpec((1, tn), lambda j: (0, j))],
        out_specs=pl.BlockSpec((m, tn), lambda j: (0, j)),
        out_shape=jax.ShapeDtypeStruct((m, n), _F32),
        compiler_params=_cparams(("arbitrary",)),
        name="adaln",
    )(cond, w, b.reshape(1, n))


def _inproj_body(x_ref, sh_ref, sc_ref, g_ref, w_ref, m1_ref, zs_ref, gc_ref):
    r, blk, d = x_ref.shape
    x = x_ref[...].reshape(r * blk, d)
    h = _rms(x, g_ref[...]) * (1.0 + sc_ref[...]) + sh_ref[...]
    z = _dot(h.astype(_BF), w_ref[...])
    ds5 = zs_ref.shape[-1]
    zs_ref[...] = z[:, :ds5].reshape(r, blk, ds5)
    v = z[:, ds5:].astype(_BF)
    g1 = _dot(m1_ref[...], v)
    gc_ref[...] = g1.astype(_BF).reshape(r, 2 * blk, v.shape[-1])


def _inproj(x, sh, sc, g, w_bf, m1):
    b, l, d = x.shape
    r, blk = FFT_R, FFT_BLK
    nj = r // blk
    dmix = w_bf.shape[1]
    ds5 = dmix // 2
    df = dmix - ds5
    x4 = x.reshape(b, r, r, d)
    zs, gc = pl.pallas_call(
        _inproj_body,
        grid=(b, nj),
        in_specs=[pl.BlockSpec((None, r, blk, d), lambda i, j: (i, 0, j, 0)),
                  pl.BlockSpec((None, 1, d), lambda i, j: (i, 0, 0)),
                  pl.BlockSpec((None, 1, d), lambda i, j: (i, 0, 0)),
                  pl.BlockSpec((1, d), lambda i, j: (0, 0)),
                  pl.BlockSpec((d, dmix), lambda i, j: (0, 0)),
                  pl.BlockSpec(m1.shape, lambda i, j: (0, 0))],
        out_specs=[pl.BlockSpec((None, r, blk, ds5), lambda i, j: (i, 0, j, 0)),
                   pl.BlockSpec((None, r, 2 * blk, df), lambda i, j: (i, 0, j, 0))],
        out_shape=[jax.ShapeDtypeStruct((b, r, r, ds5), _F32),
                   jax.ShapeDtypeStruct((b, r, 2 * r, df), _BF)],
        compiler_params=_cparams(("parallel", "arbitrary")),
        name="inproj",
    )(x4, sh, sc, g, w_bf, m1)
    return zs.reshape(b, l, ds5), gc


def _seg_transpose(arrs):
    n = len(arrs)
    seg = lax.broadcasted_iota(jnp.int32, arrs[0].shape, 1) // S5_GROUP
    d = n // 2
    while d >= 1:
        keep = (seg & d) == 0
        new = list(arrs)
        for i in range(n):
            if i & d == 0:
                a, b = arrs[i], arrs[i + d]
                new[i] = jnp.where(keep, a, pltpu.roll(b, S5_GROUP * d, 1))
                new[i + d] = jnp.where(keep, pltpu.roll(a, 128 - S5_GROUP * d, 1), b)
        arrs = new
        d //= 2
    return arrs


@functools.lru_cache(maxsize=None)
def _chunk_row_perm():
    t = S5_CHUNK
    p = np.zeros((t * t, t * t), np.float32)
    for c in range(t):
        for tl in range(t):
            p[tl * t + c, c * t + tl] = 1.0
    return p.astype(jnp.bfloat16)


def _pack_chunks(z_bf, p_ref, u_ref):
    t = S5_CHUNK
    nsub = z_bf.shape[0] // (t * t)
    pieces = [_dot(p_ref[...], z_bf[s * t * t:(s + 1) * t * t, :]) for s in range(nsub)]
    a = []
    for tl in range(t):
        rows = [pc[tl * t:(tl + 1) * t, :] for pc in pieces]
        a.append(rows[0] if nsub == 1 else jnp.concatenate(rows, axis=0))
    lanes = 128
    gpb = lanes // S5_GROUP
    for j in range(z_bf.shape[1] // lanes):
        for hi in range(t // gpb):
            outs = _seg_transpose([a[gpb * hi + tlo][:, lanes * j:lanes * (j + 1)] for tlo in range(gpb)])
            for glo in range(gpb):
                u_ref[gpb * j + glo, :, lanes * hi:lanes * (hi + 1)] = outs[glo].astype(u_ref.dtype)


def _inproj_ctx_body(x_ref, sh_ref, sc_ref, g_ref, w_ref, p_ref, u_ref):
    h = _rms(x_ref[...], g_ref[...]) * (1.0 + sc_ref[...]) + sh_ref[...]
    z = _dot(h.astype(_BF), w_ref[...])
    _pack_chunks(z.astype(_BF), p_ref, u_ref)


def _inproj_ctx(ctx, sh, sc, g, w_s5_bf, perm):
    b, lc, d = ctx.shape
    ds5 = w_s5_bf.shape[1]
    n_grp = ds5 // S5_GROUP
    kk = S5_CHUNK * S5_GROUP
    return pl.pallas_call(
        _inproj_ctx_body,
        grid=(b,),
        in_specs=[pl.BlockSpec((None, lc, d), lambda i: (i, 0, 0)),
                  pl.BlockSpec((1, d), lambda i: (0, 0)),
                  pl.BlockSpec((1, d), lambda i: (0, 0)),
                  pl.BlockSpec((1, d), lambda i: (0, 0)),
                  pl.BlockSpec((d, ds5), lambda i: (0, 0)),
                  pl.BlockSpec(perm.shape, lambda i: (0, 0))],
        out_specs=pl.BlockSpec((n_grp, lc // S5_CHUNK, kk), lambda i: (0, 0, i)),
        out_shape=jax.ShapeDtypeStruct((n_grp, lc // S5_CHUNK, b * kk), _BF),
        compiler_params=_cparams(("arbitrary",)),
        name="inproj_ctx",
    )(ctx, sh, sc, g, w_s5_bf, perm)


def _s5_pack_body(z_ref, p_ref, u_ref):
    _pack_chunks(z_ref[...].astype(_BF), p_ref, u_ref)


def _s5_pack(zs, perm, tok=2048):
    b, l, ds5 = zs.shape
    n_grp = ds5 // S5_GROUP
    kk = S5_CHUNK * S5_GROUP
    return pl.pallas_call(
        _s5_pack_body,
        grid=(b, l // tok),
        in_specs=[pl.BlockSpec((None, tok, ds5), lambda i, j: (i, j, 0)),
                  pl.BlockSpec(perm.shape, lambda i, j: (0, 0))],
        out_specs=pl.BlockSpec((n_grp, tok // S5_CHUNK, kk), lambda i, j: (0, j, i)),
        out_shape=jax.ShapeDtypeStruct((n_grp, l // S5_CHUNK, b * kk), _BF),
        compiler_params=_cparams(("parallel", "arbitrary")),
        name="s5_pack",
    )(zs, perm)


@functools.lru_cache(maxsize=None)
def _fft_stage1_matrix():
    r, blk = FFT_R, FFT_BLK
    k1 = np.arange(r)[:, None]
    l1 = np.arange(r)[None, :]
    ang = 2.0 * np.pi * ((k1 * l1) % r) / r
    f = np.stack([np.cos(ang), -np.sin(ang)], axis=1)
    m = np.einsum("kal,pq->kaplq", f, np.eye(blk))
    return m.reshape(r * 2 * blk, r * blk).astype(np.float32)


@functools.lru_cache(maxsize=None)
def _fft_stage2_matrices():
    r, blk = FFT_R, FFT_BLK
    n = r * r
    nt = r // blk
    l2 = np.arange(r)
    k2 = np.arange(r)
    out = np.zeros((nt, 2, r, blk, blk, r // blk, 2, blk), np.float32)
    for i in range(nt):
        for kl in range(blk):
            k1 = blk * i + kl
            ang = 2.0 * np.pi * (((k2[:, None] * l2[None, :] * r) + l2[None, :] * k1) % n) / n
            tr, ti = np.cos(ang), -np.sin(ang)
            tr = tr.reshape(r, r // blk, blk)
            ti = ti.reshape(r, r // blk, blk)
            out[i, 0, :, kl, kl, :, 0, :] = tr
            out[i, 0, :, kl, kl, :, 1, :] = -ti
            out[i, 1, :, kl, kl, :, 0, :] = ti
            out[i, 1, :, kl, kl, :, 1, :] = tr
    return out.reshape(nt, 2 * r * blk, blk * 2 * r)


def _fft2_body(gc_ref, m2_ref, cw_ref, sw_ref, g_ref, o_ref):
    kb, rr, df = gc_ref.shape
    gc = gc_ref[...].reshape(kb * rr, df)
    x = _dot(m2_ref[...], gc)
    half = x.shape[0] // 2
    xr = x[:half].astype(_BF)
    xi = x[half:].astype(_BF)
    ng = cw_ref.shape[0]
    fg = df // ng
    parts = []
    for g in range(ng):
        sl = slice(g * fg, (g + 1) * fg)
        parts.append(_dot(xr[:, sl], cw_ref[g]) + _dot(xi[:, sl], sw_ref[g]))
    f = jnp.concatenate(parts, axis=-1)
    fn = _rms(f, g_ref[...])
    o_ref[...] = fn.reshape(o_ref.shape)


def _fft2(gc, m2, cw, sw, gf):
    b, r, rr, df = gc.shape
    blk = FFT_BLK
    nt = r // blk
    out = pl.pallas_call(
        _fft2_body,
        grid=(b, nt),
        in_specs=[pl.BlockSpec((None, blk, rr, df), lambda i, j: (i, j, 0, 0)),
                  pl.BlockSpec((None,) + m2.shape[1:], lambda i, j: (j, 0, 0)),
                  pl.BlockSpec(cw.shape, lambda i, j: (0, 0, 0)),
                  pl.BlockSpec(sw.shape, lambda i, j: (0, 0, 0)),
                  pl.BlockSpec((1, df), lambda i, j: (0, 0))],
        out_specs=pl.BlockSpec((None, r, blk, df), lambda i, j: (i, 0, j, 0)),
        out_shape=jax.ShapeDtypeStruct((b, r, r, df), _F32),
        compiler_params=_cparams(("parallel", "arbitrary")),
        name="fft2",
    )(gc, m2, cw, sw, gf)
    return out.reshape(b, r * r, df)


def _s5_body(uc_ref, ul_ref, ws_ref, wy_ref, at_ref, y_ref,
             s_re, s_im, ha_re, hb_re, ha_im, hb_im, *, nb):
    p = S5_STATE
    gs, n_ctx, _ = uc_ref.shape
    n_lat = ul_ref.shape[1]
    kk = ws_ref.shape[1]
    for gi in range(gs):
        ws = ws_ref[gi]
        for b in range(nb):
            sl = slice(b * kk, (b + 1) * kk)
            sc = _dot(uc_ref[gi, :, sl], ws)
            sl_ = _dot(ul_ref[gi, :, sl], ws)
            s_re[gi, pl.ds(b, n_ctx, stride=nb), :] = sc[:, :2 * p]
            s_im[gi, pl.ds(b, n_ctx, stride=nb), :] = sc[:, 2 * p:]
            s_re[gi, pl.ds(n_ctx * nb + b, n_lat, stride=nb), :] = sl_[:, :2 * p]
            s_im[gi, pl.ds(n_ctx * nb + b, n_lat, stride=nb), :] = sl_[:, 2 * p:]

    a_re = [at_ref[gi, 0:1, :] for gi in range(gs)]
    a_im = [at_ref[gi, 1:2, :] for gi in range(gs)]
    is_f = lax.broadcasted_iota(jnp.int32, (nb, 2 * p), 1) < p
    n_all = n_ctx + n_lat

    def load(gi, cf, cr):
        rf = pl.multiple_of(cf * nb, nb)
        rr = pl.multiple_of(cr * nb, nb)
        return (jnp.where(is_f, s_re[gi, pl.ds(rf, nb), :], s_re[gi, pl.ds(rr, nb), :]),
                jnp.where(is_f, s_im[gi, pl.ds(rf, nb), :], s_im[gi, pl.ds(rr, nb), :]))

    def advance(gi, h_re, h_im, x_re, x_im):
        return (a_re[gi] * h_re - a_im[gi] * h_im + x_re, a_re[gi] * h_im + a_im[gi] * h_re + x_im)

    def ctx_step(step, carry):
        out = []
        for gi in range(gs):
            x_re, x_im = load(gi, step, n_ctx - 1 - step)
            out.append(advance(gi, *carry[gi], x_re, x_im))
        return tuple(out)

    def lat_step(step, carry):
        rf = pl.multiple_of(step * nb, nb)
        rr = pl.multiple_of((n_lat - 1 - step) * nb, nb)
        out = []
        for gi in range(gs):
            h_re, h_im = carry[gi]
            ha_re[gi, pl.ds(rf, nb), :] = h_re
            hb_re[gi, pl.ds(rr, nb), :] = h_re
            ha_im[gi, pl.ds(rf, nb), :] = h_im
            hb_im[gi, pl.ds(rr, nb), :] = h_im
            x_re, x_im = load(gi, n_ctx + step, n_all - 1 - step)
            out.append(advance(gi, h_re, h_im, x_re, x_im))
        return tuple(out)

    zero = jnp.zeros((nb, 2 * p), _F32)
    carry = lax.fori_loop(0, n_ctx, ctx_step, tuple((zero, zero) for _ in range(gs)), unroll=2)
    lax.fori_loop(0, n_lat, lat_step, carry, unroll=2)

    for gi in range(gs):
        wy = wy_ref[gi]
        for b in range(nb):
            rows = pl.ds(b, n_lat, stride=nb)
            hin = jnp.concatenate([ha_re[gi, rows, :], hb_re[gi, rows, :],
                                   ha_im[gi, rows, :], hb_im[gi, rows, :]], axis=-1)
            lhs = jnp.concatenate([ul_ref[gi, :, b * kk:(b + 1) * kk], hin.astype(_BF)], axis=-1)
            y_ref[gi, :, b * kk:(b + 1) * kk] = _dot(lhs, wy).astype(y_ref.dtype)


def _s5(u_ctx, u_lat, ws, wy, at, nb):
    ng, n_ctx, w = u_ctx.shape
    n_lat = u_lat.shape[1]
    gs = S5_SCAN_GROUPS
    body = functools.partial(_s5_body, nb=nb)
    hs = pltpu.VMEM((gs, n_lat * nb, 2 * S5_STATE), _F32)
    ss = pltpu.VMEM((gs, (n_ctx + n_lat) * nb, 2 * S5_STATE), _F32)
    grp = lambda a: pl.BlockSpec((gs,) + a.shape[1:], lambda g: (g, 0, 0))
    return pl.pallas_call(
        body,
        grid=(ng // gs,),
        in_specs=[grp(u_ctx), grp(u_lat), grp(ws), grp(wy), grp(at)],
        out_specs=pl.BlockSpec((gs, n_lat, w), lambda g: (g, 0, 0)),
        out_shape=jax.ShapeDtypeStruct((ng, n_lat, w), _BF),
        scratch_shapes=[ss, ss, hs, hs, hs, hs],
        compiler_params=_cparams(("parallel",)),
        name="s5_scan",
    )(u_ctx, u_lat, ws, wy, at)


def _s5_param_body(lr_ref, li_ref, ls_ref, btr_ref, bti_ref, cr_ref, ci_ref, ws_ref, wy_ref, at_ref):
    t, hh, p = S5_CHUNK, S5_GROUP, S5_STATE
    kk = t * hh
    lr, li = lr_ref[...], li_ref[...]
    dt = jnp.exp(ls_ref[...])
    ldt, idt = lr * dt, li * dt
    mag = jnp.exp(ldt)
    ab_re, ab_im = mag * jnp.cos(idt), mag * jnp.sin(idt)
    den = lr * lr + li * li
    nr, ni = ab_re - 1.0, ab_im
    q_re = (nr * lr + ni * li) / den
    q_im = (ni * lr - nr * li) / den
    kf = lax.broadcasted_iota(jnp.int32, (2 * t, p), 0).astype(_F32)

    ap_re, ap_im, bb_re, bb_im = [], [], [], []
    for d in range(2):
        pm = jnp.exp(kf * ldt[d:d + 1, :])
        ap_re.append(pm * jnp.cos(kf * idt[d:d + 1, :]))
        ap_im.append(pm * jnp.sin(kf * idt[d:d + 1, :]))
        bb_re.append(q_re[d:d + 1, :] * btr_ref[d] - q_im[d:d + 1, :] * bti_ref[d])
        bb_im.append(q_re[d:d + 1, :] * bti_ref[d] + q_im[d:d + 1, :] * btr_ref[d])

    def rep(a, ks):
        return jnp.concatenate([jnp.broadcast_to(a[k:k + 1, :], (hh, p)) for k in ks], axis=0)

    def tile(a, n):
        return jnp.concatenate([a] * n, axis=0)

    def cmul(xr, xi, yr, yi):
        return xr * yr - xi * yi, xr * yi + xi * yr

    ks_f = list(range(t + 1))
    ks_r = list(range(t, -1, -1))
    caf_re, caf_im = cmul(tile(cr_ref[0], t + 1), tile(ci_ref[0], t + 1), rep(ap_re[0], ks_f), rep(ap_im[0], ks_f))
    car_re, car_im = cmul(tile(cr_ref[1], t + 1), tile(ci_ref[1], t + 1), rep(ap_re[1], ks_r), rep(ap_im[1], ks_r))

    def lag_blocks(bre, bim, ca_re, ca_im):
        lhs = jnp.concatenate([bre, bim], axis=1)
        rhs = jnp.concatenate([ca_re, -ca_im], axis=1)
        return lax.dot_general(lhs, rhs, (((1,), (1,)), ((), ())), precision=_HI,
                               preferred_element_type=_F32)

    w_f = lag_blocks(bb_re[0], bb_im[0], caf_re[:kk], caf_im[:kk])
    w_r = lag_blocks(bb_re[1], bb_im[1], car_re[hh:], car_im[hh:])
    lane = lax.broadcasted_iota(jnp.int32, (hh, kk), 1)
    for tlp in range(t):
        sf = hh * tlp
        sr = hh * (t - 1 - tlp)
        a = w_f if sf == 0 else jnp.where(lane >= sf, pltpu.roll(w_f, sf, 1), 0.0)
        b = w_r if sr == 0 else jnp.where(lane < kk - sr, pltpu.roll(w_r, kk - sr, 1), 0.0)
        wy_ref[hh * tlp:hh * (tlp + 1), :] = (a + b).astype(wy_ref.dtype)

    zeros = jnp.zeros((p, kk), _F32)
    blocks = [caf_re[hh:].T, zeros, zeros, car_re[:kk].T, (-caf_im[hh:]).T, zeros, zeros, (-car_im[:kk]).T]
    for i, blk in enumerate(blocks):
        wy_ref[kk + p * i:kk + p * (i + 1), :] = blk.astype(wy_ref.dtype)

    pf = list(range(t - 1, -1, -1))
    pr = list(range(t))
    f_re, f_im = cmul(rep(ap_re[0], pf), rep(ap_im[0], pf), tile(bb_re[0], t), tile(bb_im[0], t))
    r_re, r_im = cmul(rep(ap_re[1], pr), rep(ap_im[1], pr), tile(bb_re[1], t), tile(bb_im[1], t))
    ws_ref[...] = jnp.concatenate([f_re, r_re, f_im, r_im], axis=1).astype(ws_ref.dtype)

    a_re = jnp.concatenate([ap_re[0][t:t + 1, :], ap_re[1][t:t + 1, :]], axis=1)
    a_im = jnp.concatenate([ap_im[0][t:t + 1, :], ap_im[1][t:t + 1, :]], axis=1)
    row = lax.broadcasted_iota(jnp.int32, at_ref.shape, 0)
    at_ref[...] = jnp.where(row == 0, a_re, jnp.where(row == 1, a_im, 0.0))


def _s5_params(lam_re, lam_im, b_re, b_im, c_re, c_im, log_step):
    _, ng, p = lam_re.shape
    hh = b_re.shape[-1]
    kk = S5_CHUNK * hh
    gd = lambda a: jnp.swapaxes(a.astype(_F32), 0, 1)
    args = (gd(lam_re), gd(lam_im), gd(log_step)[..., None],
            jnp.swapaxes(gd(b_re), 2, 3), jnp.swapaxes(gd(b_im), 2, 3), gd(c_re), gd(c_im))
    spec = lambda a: pl.BlockSpec((None,) + a.shape[1:], lambda g: (g,) + (0,) * (a.ndim - 1))
    return pl.pallas_call(
        _s5_param_body,
        grid=(ng,),
        in_specs=[spec(a) for a in args],
        out_specs=[pl.BlockSpec((None, kk, 4 * p), lambda g: (g, 0, 0)),
                   pl.BlockSpec((None, kk + 8 * p, kk), lambda g: (g, 0, 0)),
                   pl.BlockSpec((None, 8, 2 * p), lambda g: (g, 0, 0))],
        out_shape=[jax.ShapeDtypeStruct((ng, kk, 4 * p), _BF),
                   jax.ShapeDtypeStruct((ng, kk + 8 * p, kk), _BF),
                   jax.ShapeDtypeStruct((ng, 8, 2 * p), _F32)],
        compiler_params=_cparams(("parallel",)),
        name="s5_params",
    )(*args)


def _post_body(x_ref, u_ref, yg_ref, fn_ref, d_ref, wglu_ref, gs_ref, wo_ref, g1_ref,
               n2_ref, sh_ref, sc_ref, wr_ref, br_ref, tri_ref, x1_ref, t_ref, rt_ref, rtt_ref, cnt_ref,
               y_scr, cnt_scr):
    ds5 = u_ref.shape[-1]
    first = (pl.program_id(0) == 0) & (pl.program_id(1) == 0)

    @pl.when(first)
    def _():
        cnt_scr[...] = jnp.zeros_like(cnt_scr)

    t_chunk = S5_CHUNK
    n_chunk = yg_ref.shape[1]
    lanes = 128
    gpb = lanes // S5_GROUP
    for j in range(ds5 // lanes):
        for hi in range(t_chunk // gpb):
            outs = _seg_transpose([yg_ref[gpb * j + glo, :, lanes * hi:lanes * (hi + 1)].astype(_F32)
                                   for glo in range(gpb)])
            for tlo in range(gpb):
                y_scr[j, pl.ds(gpb * hi + tlo, n_chunk, stride=t_chunk), :] = outs[tlo]
    y_s5 = jnp.concatenate([y_scr[j] for j in range(ds5 // lanes)], axis=-1)

    y = y_s5 + d_ref[...] * u_ref[...]
    y = jax.nn.gelu(y, approximate=True)
    y = y * jax.nn.sigmoid(_dot(y.astype(_BF), wglu_ref[...]))
    yn = _rms(y, gs_ref[...]).astype(_BF)
    mix = _dot(yn, wo_ref[0:ds5, :]) + _dot(fn_ref[...].astype(_BF), wo_ref[ds5:, :])
    x1 = x_ref[...] + g1_ref[...] * mix
    x1_ref[...] = x1
    t = _rms(x1, n2_ref[...]) * (1.0 + sc_ref[...]) + sh_ref[...]
    t_hi = t.astype(_BF)
    t_ref[...] = t_hi

    ng, epg = N_EXPERT_GROUPS, EXPERTS_PER_GROUP
    t_lo = (t - t_hi.astype(_F32)).astype(_BF)
    logits = (_dot(t_hi, wr_ref[0]) + _dot(t_lo, wr_ref[0]) + _dot(t_hi, wr_ref[1])) + br_ref[...]
    tm = logits.shape[0]
    lt = logits.T
    neg = jnp.float32(-jnp.inf)
    big = jnp.int32(1 << 20)
    row8 = lax.broadcasted_iota(jnp.int32, (epg, tm), 0)
    amax = lambda v: jnp.max(v, axis=0, keepdims=True)
    first = lambda v, m: jnp.min(jnp.where(v == m, row8, big), axis=0, keepdims=True)
    gt = lt[ng * epg:ng * epg + epg, :]
    gl = jnp.where(row8 < ng, gt, neg)
    gmax = amax(gl)
    gidx = first(gl, gmax)
    gw = 1.0 / jnp.sum(jnp.where(row8 < ng, jnp.exp(gt - gmax), 0.0), axis=0, keepdims=True)
    el = lt[0:epg, :]
    for g in range(1, ng):
        el = jnp.where(gidx == g, lt[g * epg:(g + 1) * epg, :], el)
    v0 = amax(el)
    i0 = first(el, v0)
    el1 = jnp.where(row8 == i0, neg, el)
    v1 = amax(el1)
    i1 = first(el1, v1)
    p0 = 1.0 / (1.0 + jnp.exp(v1 - v0))
    w0 = gw * p0
    w1 = gw * (1.0 - p0)
    e0 = gidx * epg + i0
    e1 = gidx * epg + i1

    rowe = lax.broadcasted_iota(jnp.int32, (ng * epg, tm), 0)
    oh0 = rowe == e0
    oh1 = rowe == e1
    oh = (oh0 | oh1).astype(_F32)
    prefix = _dot(oh.astype(_BF), tri_ref[...]) + cnt_scr[...]
    r0 = jnp.sum(jnp.where(oh0, prefix, 0.0), axis=0, keepdims=True)
    r1 = jnp.sum(jnp.where(oh1, prefix, 0.0), axis=0, keepdims=True)
    cnt = cnt_scr[...] + jnp.sum(oh, axis=1, keepdims=True)
    cnt_scr[...] = cnt
    cnt_ref[...] = cnt

    rec = jnp.concatenate([e0.astype(_F32), e1.astype(_F32), w0, w1, r0, r1,
                           jnp.zeros((ROUTE_ROWS - 6, tm), _F32)], axis=0)
    rtt_ref[...] = rec
    rt_ref[...] = jnp.concatenate([rec, jnp.zeros((ROUTE_LANES - ROUTE_ROWS, tm), _F32)], axis=0).T


@functools.lru_cache(maxsize=None)
def _strict_upper_ones(n):
    return np.triu(np.ones((n, n), np.float32), 1).astype(jnp.bfloat16)


def _post(x, u, y_g, fn, s5_d, wglu_bf, gs, wo_bf, g1, n2, sh2, sc2, wr, br, b0, nb, tm=512):
    b, l, d = x.shape
    ds5 = u.shape[-1]
    n_grp, _, w = y_g.shape
    kk = w // b
    n_exp = N_EXPERT_GROUPS * EXPERTS_PER_GROUP
    tri = jnp.asarray(_strict_upper_ones(tm))
    tok = lambda w: pl.BlockSpec((None, tm, w), lambda i, j: (i + b0, j, 0))
    otok = lambda w: pl.BlockSpec((None, tm, w), lambda i, j: (i, j, 0))
    per_b = pl.BlockSpec((None, 1, d), lambda i, j: (i + b0, 0, 0))
    full = lambda a: pl.BlockSpec(a.shape, lambda i, j: (0,) * a.ndim)
    ygs = pl.BlockSpec((n_grp, tm // S5_CHUNK, kk), lambda i, j: (0, j, i + b0))
    return pl.pallas_call(
        _post_body,
        grid=(nb, l // tm),
        in_specs=[tok(d), tok(ds5), ygs, tok(fn.shape[-1]), full(s5_d), full(wglu_bf),
                  full(gs), full(wo_bf), per_b, full(n2), per_b, per_b, full(wr), full(br), full(tri)],
        out_specs=[otok(d), otok(d), otok(ROUTE_LANES),
                   pl.BlockSpec((None, ROUTE_ROWS, tm), lambda i, j: (i, 0, j)),
                   pl.BlockSpec((n_exp, 1), lambda i, j: (0, 0))],
        out_shape=[jax.ShapeDtypeStruct((nb, l, d), _F32),
                   jax.ShapeDtypeStruct((nb, l, d), _BF),
                   jax.ShapeDtypeStruct((nb, l, ROUTE_LANES), _F32),
                   jax.ShapeDtypeStruct((nb, ROUTE_ROWS, l), _F32),
                   jax.ShapeDtypeStruct((n_exp, 1), _F32)],
        scratch_shapes=[pltpu.VMEM((ds5 // 128, tm, 128), _F32),
                        pltpu.VMEM((n_exp, 1), _F32)],
        compiler_params=_cparams(("arbitrary", "arbitrary")),
        name="post_mixer",
    )(x, u, y_g, fn, s5_d, wglu_bf, gs, wo_bf, g1, n2, sh2, sc2, wr, br, tri)


def _moe_body(te_ref, nt_ref, xs_ref, wg_ref, wu_ref, wd_ref, o_ref, wgu_bf, wd_bf):
    i = pl.program_id(0)
    de = wg_ref.shape[-1]
    used = i < nt_ref[0]
    new_expert = (i == 0) | (te_ref[i] != te_ref[jnp.maximum(i - 1, 0)])

    @pl.when(used & new_expert)
    def _():
        wgu_bf[:, 0:de] = wg_ref[...].astype(_BF)
        wgu_bf[:, de:2 * de] = wu_ref[...].astype(_BF)
        wd_bf[...] = wd_ref[...].astype(_BF)

    @pl.when(used)
    def _():
        h = _dot(xs_ref[...], wgu_bf[...])
        hg, hu = h[:, 0:de], h[:, de:2 * de]
        a = hg * jax.nn.sigmoid(hg) * hu
        o_ref[...] = _dot(a.astype(_BF), wd_bf[...]).astype(o_ref.dtype)

    @pl.when(jnp.logical_not(used))
    def _():
        o_ref[...] = jnp.zeros_like(o_ref)


def _moe(tile_expert, n_tiles_used, xs, wg, wu, wd):
    nr, d = xs.shape
    tm = MOE_TM
    de = wg.shape[-1]
    grid_spec = pltpu.PrefetchScalarGridSpec(
        num_scalar_prefetch=2,
        grid=(nr // tm,),
        in_specs=[pl.BlockSpec((tm, d), lambda i, te, nt: (i, 0)),
                  pl.BlockSpec((None, d, de), lambda i, te, nt: (te[i], 0, 0)),
                  pl.BlockSpec((None, d, de), lambda i, te, nt: (te[i], 0, 0)),
                  pl.BlockSpec((None, de, d), lambda i, te, nt: (te[i], 0, 0))],
        out_specs=pl.BlockSpec((tm, d), lambda i, te, nt: (i, 0)),
        scratch_shapes=[pltpu.VMEM((d, 2 * de), _BF), pltpu.VMEM((de, d), _BF)],
    )
    return pl.pallas_call(
        _moe_body,
        grid_spec=grid_spec,
        out_shape=jax.ShapeDtypeStruct((nr, d), _BF),
        compiler_params=_cparams(("arbitrary",)),
        name="moe_experts",
    )(tile_expert, n_tiles_used, xs, wg, wu, wd)


def _final_body(g2_ref, gf_ref, *refs, n_part, nb):
    o_ref = refs[-1]
    i = pl.program_id(0)
    for part in range(n_part):
        x1_ref, y0_ref, y1_ref, rt_ref = refs[4 * part:4 * part + 4]

        @pl.when((i >= part * nb) & (i < (part + 1) * nb))
        def _():
            w0 = rt_ref[:, 2:3]
            w1 = rt_ref[:, 3:4]
            m = w0 * y0_ref[...].astype(_F32) + w1 * y1_ref[...].astype(_F32)
            o_ref[...] = _rms(x1_ref[...] + g2_ref[...] * m, gf_ref[...])


def _final(parts, g2, gf, tm=512):
    n_part = len(parts)
    nb, l, d = parts[0][0].shape
    nj = l // tm
    in_specs = [pl.BlockSpec((None, 1, d), lambda i, j: (i, 0, 0)),
                pl.BlockSpec((1, d), lambda i, j: (0, 0))]
    args = [g2, gf]
    for part, arrs in enumerate(parts):
        def imap(i, j, part=part):
            own = (i >= part * nb) & (i < (part + 1) * nb)
            return (jnp.clip(i - part * nb, 0, nb - 1), jnp.where(own, j, jnp.where(i < part * nb, 0, nj - 1)), 0)
        for a in arrs:
            in_specs.append(pl.BlockSpec((None, tm, a.shape[-1]), imap))
            args.append(a)
    return pl.pallas_call(
        functools.partial(_final_body, n_part=n_part, nb=nb),
        grid=(n_part * nb, nj),
        in_specs=in_specs,
        out_specs=pl.BlockSpec((None, tm, d), lambda i, j: (i, j, 0)),
        out_shape=jax.ShapeDtypeStruct((n_part * nb, l, d), _F32),
        compiler_params=_cparams(("parallel", "arbitrary")),
        name="final_norm",
    )(*args)


def _route_plan(eid, rank, tok_ids, counts, tm, nr):
    n_experts = counts.shape[0]
    n_pairs = eid.size
    padded = ((counts + tm - 1) // tm) * tm
    pad_end = jnp.cumsum(padded)
    pad_start = pad_end - padded
    raw_start = jnp.cumsum(counts) - counts
    pos = rank
    for e in range(n_experts):
        pos = pos + jnp.where(eid == e, pad_start[e], 0)
    tile_start = jnp.arange(nr // tm, dtype=jnp.int32) * tm
    tile_expert = jnp.sum(tile_start[:, None] >= pad_end[None, :], axis=-1)
    tile_expert = jnp.minimum(tile_expert, n_experts - 1).astype(jnp.int32)
    _, sorted_tok = lax.sort_key_val((eid * n_pairs + rank).reshape(-1), tok_ids.reshape(-1))
    sel = tile_expert[:, None] == jnp.arange(n_experts, dtype=jnp.int32)[None, :]
    per_tile = lambda v: jnp.repeat(jnp.sum(jnp.where(sel, v[None, :], 0), axis=-1), tm)
    off = jnp.arange(nr, dtype=jnp.int32) - per_tile(pad_start)
    valid = off < per_tile(counts)
    j = jnp.clip(per_tile(raw_start) + off, 0, n_pairs - 1)
    filler = jnp.arange(nr, dtype=jnp.int32) % (n_pairs // 2)
    row_token = jnp.where(valid, sorted_tok.at[j].get(mode="promise_in_bounds"), filler)
    n_used = (pad_end[-1:] // tm).astype(jnp.int32)
    return pos, tile_expert, n_used, row_token


def kernel(x, c, ctx, c_ctx, w_ada, b_ada, norm1_g, norm2_g, w_in, s5_lam_re, s5_lam_im, s5_b_re, s5_b_im, s5_c_re, s5_c_im, s5_log_step, s5_d, s5_w_glu, fourier_w, mix_norm_s5_g, mix_norm_f_g, w_out, moe_w_group, moe_b_group, moe_w_router, moe_b_router, moe_w_gate, moe_w_up, moe_w_down, final_norm_g):
    b, l, d = x.shape
    lc = ctx.shape[1]
    depth = w_ada.shape[0]
    assert depth == 1 and l == FFT_R * FFT_R and lc % S5_CHUNK == 0 and b % 8 == 0
    ds5 = s5_d.shape[-1]
    df = w_in.shape[-1] - ds5
    n_exp = moe_w_gate.shape[1]
    row = lambda a: a.reshape(1, -1)

    cond = jnp.concatenate([c, c_ctx[None, :], jnp.zeros((7, d), _F32)], axis=0)
    mod = _adaln(cond, w_ada[0], b_ada[0])
    sh1, sc1, g1, sh2, sc2, g2 = [mod[:b, i * d:(i + 1) * d].reshape(b, 1, d) for i in range(6)]
    csh1 = mod[b:b + 1, 0:d]
    csc1 = mod[b:b + 1, d:2 * d]

    w_in_bf = w_in[0].astype(_BF)
    m1 = jnp.asarray(_fft_stage1_matrix()).astype(_BF)
    m2 = jnp.asarray(_fft_stage2_matrices()).astype(_BF)
    perm = jnp.asarray(_chunk_row_perm())
    zs, gc = _inproj(x, sh1, sc1, row(norm1_g[0]), w_in_bf, m1)
    u_ctx = _inproj_ctx(ctx, csh1, csc1, row(norm1_g[0]), w_in_bf[:, :ds5], perm)
    u_lat = _s5_pack(zs, perm)

    fg = df // FOURIER_GROUPS
    cc = np.arange(fg)
    ang = 2.0 * np.pi * ((cc[:, None] * cc[None, :]) % fg) / fg
    scale = 1.0 / math.sqrt(l * fg)
    cw = jnp.einsum("cm,gmd->gcd", jnp.asarray(np.cos(ang) * scale, _F32), fourier_w[0], precision=_HI)
    sw = jnp.einsum("cm,gmd->gcd", jnp.asarray(np.sin(ang) * scale, _F32), fourier_w[0], precision=_HI)
    fn = _fft2(gc, m2, cw.astype(_BF), sw.astype(_BF), row(mix_norm_f_g[0]))

    ws, wy, at = _s5_params(s5_lam_re[0], s5_lam_im[0], s5_b_re[0], s5_b_im[0],
                            s5_c_re[0], s5_c_im[0], s5_log_step[0])
    y_g = _s5(u_ctx, u_lat, ws, wy, at, b)

    n_rt = N_EXPERT_GROUPS * (1 + EXPERTS_PER_GROUP)
    w_rt = jnp.concatenate([moe_w_router[0].reshape(d, -1), moe_w_group[0]], axis=-1)
    w_rt = jnp.pad(w_rt, ((0, 0), (0, ROUTE_LANES - n_rt)))
    w_rt_hi = w_rt.astype(_BF)
    w_rt = jnp.stack([w_rt_hi, (w_rt - w_rt_hi.astype(_F32)).astype(_BF)])
    b_rt = jnp.pad(jnp.concatenate([moe_b_router[0].reshape(-1), moe_b_group[0]]), (0, ROUTE_LANES - n_rt))
    tm = MOE_TM
    nb = b // MOE_PARTS
    n_tok = nb * l
    nr = 2 * n_tok + n_exp * tm
    take_rows = lambda a, idx: a.at[idx].get(mode="promise_in_bounds")
    wglu_bf, wo_bf = s5_w_glu[0].astype(_BF), w_out[0].astype(_BF)
    parts = []
    for part in range(MOE_PARTS):
        x1, tmod, rt, rtt, cnt = _post(x, zs, y_g, fn, row(s5_d[0]), wglu_bf, row(mix_norm_s5_g[0]), wo_bf,
                                       g1, row(norm2_g[0]), sh2, sc2, w_rt, row(b_rt), part * nb, nb)
        rec = rtt.transpose(1, 0, 2).reshape(ROUTE_ROWS, n_tok)
        eid = rec[0:2].astype(jnp.int32)
        rank = rec[4:6].astype(jnp.int32)
        tok_ids = jnp.broadcast_to(jnp.arange(n_tok, dtype=jnp.int32), (2, n_tok))
        pos, tile_expert, n_used, row_token = _route_plan(eid, rank, tok_ids,
                                                          cnt[:, 0].astype(jnp.int32), tm, nr)
        xs = take_rows(tmod.reshape(n_tok, d), row_token)
        ys = _moe(tile_expert, n_used, xs, moe_w_gate[0], moe_w_up[0], moe_w_down[0])
        y0 = take_rows(ys, pos[0]).reshape(nb, l, d)
        y1 = take_rows(ys, pos[1]).reshape(nb, l, d)
        parts.append((x1, y0, y1, rt))
    return _final(parts, g2, row(final_norm_g))
```

```python
import functools
import math

import numpy as np
import jax
import jax.numpy as jnp
from jax import lax
from jax.experimental import pallas as pl
from jax.experimental.pallas import tpu as pltpu

EPS = 1e-6
S5_GROUP = 16
S5_STATE = 64
S5_CHUNK = 16
S5_SCAN_GROUPS = 4
FOURIER_GROUPS = 4
N_EXPERT_GROUPS = 4
EXPERTS_PER_GROUP = 8
FFT_R = 64
FFT_BLK = 8
MOE_TM = 512
MOE_PARTS = 1
ROUTE_LANES = 128
ROUTE_ROWS = 8
VMEM_LIMIT = 56 * 1024 * 1024

_HI = lax.Precision.HIGHEST
_BF = jnp.bfloat16
_F32 = jnp.float32


def _cparams(sem):
    return pltpu.CompilerParams(dimension_semantics=sem, vmem_limit_bytes=VMEM_LIMIT)


def _dot(a, b):
    return jnp.dot(a, b, preferred_element_type=_F32)


def _sigmoid(x):
    return 0.5 * jnp.tanh(0.5 * x) + 0.5


def _rms(x, g):
    return x * lax.rsqrt(jnp.mean(x * x, axis=-1, keepdims=True) + EPS) * g


def _adaln_body(c_ref, w_ref, b_ref, o_ref):
    c = c_ref[...]
    a = c * jax.nn.sigmoid(c)
    o_ref[...] = jnp.dot(a, w_ref[...], preferred_element_type=_F32, precision=_HI) + b_ref[...]


def _adaln(cond, w, b):
    m, d = cond.shape
    n = w.shape[1]
    tn = 768
    return pl.pallas_call(
        _adaln_body,
        grid=(n // tn,),
        in_specs=[pl.BlockSpec((m, d), lambda j: (0, 0)),
                  pl.BlockSpec((d, tn), lambda j: (0, j)),
                  pl.BlockSpec((1, tn), lambda j: (0, j))],
        out_specs=pl.BlockSpec((m, tn), lambda j: (0, j)),
        out_shape=jax.ShapeDtypeStruct((m, n), _F32),
        compiler_params=_cparams(("arbitrary",)),
        name="adaln",
    )(cond, w, b.reshape(1, n))


def _inproj_body(x_ref, sh_ref, sc_ref, g_ref, w_ref, m1_ref, zs_ref, gc_ref):
    r, blk, d = x_ref.shape
    x = x_ref[...].reshape(r * blk, d)
    h = _rms(x, g_ref[...]) * (1.0 + sc_ref[...]) + sh_ref[...]
    z = _dot(h.astype(_BF), w_ref[...])
    ds5 = zs_ref.shape[-1]
    zs_ref[...] = z[:, :ds5].reshape(r, blk, ds5)
    v = z[:, ds5:].astype(_BF)
    g1 = _dot(m1_ref[...], v)
    gc_ref[...] = g1.astype(_BF).reshape(r, 2 * blk, v.shape[-1])


def _inproj(x, sh, sc, g, w_bf, m1):
    b, l, d = x.shape
    r, blk = FFT_R, FFT_BLK
    nj = r // blk
    dmix = w_bf.shape[1]
    ds5 = dmix // 2
    df = dmix - ds5
    x4 = x.reshape(b, r, r, d)
    zs, gc = pl.pallas_call(
        _inproj_body,
        grid=(b, nj),
        in_specs=[pl.BlockSpec((None, r, blk, d), lambda i, j: (i, 0, j, 0)),
                  pl.BlockSpec((None, 1, d), lambda i, j: (i, 0, 0)),
                  pl.BlockSpec((None, 1, d), lambda i, j: (i, 0, 0)),
                  pl.BlockSpec((1, d), lambda i, j: (0, 0)),
                  pl.BlockSpec((d, dmix), lambda i, j: (0, 0)),
                  pl.BlockSpec(m1.shape, lambda i, j: (0, 0))],
        out_specs=[pl.BlockSpec((None, r, blk, ds5), lambda i, j: (i, 0, j, 0)),
                   pl.BlockSpec((None, r, 2 * blk, df), lambda i, j: (i, 0, j, 0))],
        out_shape=[jax.ShapeDtypeStruct((b, r, r, ds5), _F32),
                   jax.ShapeDtypeStruct((b, r, 2 * r, df), _BF)],
        compiler_params=_cparams(("parallel", "arbitrary")),
        name="inproj",
    )(x4, sh, sc, g, w_bf, m1)
    return zs.reshape(b, l, ds5), gc


def _seg_transpose(arrs):
    n = len(arrs)
    seg = lax.broadcasted_iota(jnp.int32, arrs[0].shape, 1) // S5_GROUP
    d = n // 2
    while d >= 1:
        keep = (seg & d) == 0
        new = list(arrs)
        for i in range(n):
            if i & d == 0:
                a, b = arrs[i], arrs[i + d]
                new[i] = jnp.where(keep, a, pltpu.roll(b, S5_GROUP * d, 1))
                new[i + d] = jnp.where(keep, pltpu.roll(a, 128 - S5_GROUP * d, 1), b)
        arrs = new
        d //= 2
    return arrs


@functools.lru_cache(maxsize=None)
def _chunk_row_perm():
    t = S5_CHUNK
    p = np.zeros((t * t, t * t), np.float32)
    for c in range(t):
        for tl in range(t):
            p[tl * t + c, c * t + tl] = 1.0
    return p.astype(jnp.bfloat16)


def _pack_chunks(z_bf, p_ref, u_ref):
    t = S5_CHUNK
    nsub = z_bf.shape[0] // (t * t)
    pieces = [_dot(p_ref[...], z_bf[s * t * t:(s + 1) * t * t, :]) for s in range(nsub)]
    a = []
    for tl in range(t):
        rows = [pc[tl * t:(tl + 1) * t, :] for pc in pieces]
        a.append(rows[0] if nsub == 1 else jnp.concatenate(rows, axis=0))
    lanes = 128
    gpb = lanes // S5_GROUP
    for j in range(z_bf.shape[1] // lanes):
        for hi in range(t // gpb):
            outs = _seg_transpose([a[gpb * hi + tlo][:, lanes * j:lanes * (j + 1)] for tlo in range(gpb)])
            for glo in range(gpb):
                u_ref[gpb * j + glo, :, lanes * hi:lanes * (hi + 1)] = outs[glo].astype(u_ref.dtype)


def _inproj_ctx_body(x_ref, sh_ref, sc_ref, g_ref, w_ref, p_ref, u_ref):
    h = _rms(x_ref[...], g_ref[...]) * (1.0 + sc_ref[...]) + sh_ref[...]
    z = _dot(h.astype(_BF), w_ref[...])
    _pack_chunks(z.astype(_BF), p_ref, u_ref)


def _inproj_ctx(ctx, sh, sc, g, w_s5_bf, perm):
    b, lc, d = ctx.shape
    ds5 = w_s5_bf.shape[1]
    n_grp = ds5 // S5_GROUP
    kk = S5_CHUNK * S5_GROUP
    return pl.pallas_call(
        _inproj_ctx_body,
        grid=(b,),
        in_specs=[pl.BlockSpec((None, lc, d), lambda i: (i, 0, 0)),
                  pl.BlockSpec((1, d), lambda i: (0, 0)),
                  pl.BlockSpec((1, d), lambda i: (0, 0)),
                  pl.BlockSpec((1, d), lambda i: (0, 0)),
                  pl.BlockSpec((d, ds5), lambda i: (0, 0)),
                  pl.BlockSpec(perm.shape, lambda i: (0, 0))],
        out_specs=pl.BlockSpec((n_grp, lc // S5_CHUNK, kk), lambda i: (0, 0, i)),
        out_shape=jax.ShapeDtypeStruct((n_grp, lc // S5_CHUNK, b * kk), _BF),
        compiler_params=_cparams(("arbitrary",)),
        name="inproj_ctx",
    )(ctx, sh, sc, g, w_s5_bf, perm)


def _s5_pack_body(z_ref, p_ref, u_ref):
    _pack_chunks(z_ref[...].astype(_BF), p_ref, u_ref)


def _s5_pack(zs, perm, tok=2048):
    b, l, ds5 = zs.shape
    n_grp = ds5 // S5_GROUP
    kk = S5_CHUNK * S5_GROUP
    return pl.pallas_call(
        _s5_pack_body,
        grid=(b, l // tok),
        in_specs=[pl.BlockSpec((None, tok, ds5), lambda i, j: (i, j, 0)),
                  pl.BlockSpec(perm.shape, lambda i, j: (0, 0))],
        out_specs=pl.BlockSpec((n_grp, tok // S5_CHUNK, kk), lambda i, j: (0, j, i)),
        out_shape=jax.ShapeDtypeStruct((n_grp, l // S5_CHUNK, b * kk), _BF),
        compiler_params=_cparams(("parallel", "arbitrary")),
        name="s5_pack",
    )(zs, perm)


@functools.lru_cache(maxsize=None)
def _fft_stage1_matrix():
    r, blk = FFT_R, FFT_BLK
    k1 = np.arange(r)[:, None]
    l1 = np.arange(r)[None, :]
    ang = 2.0 * np.pi * ((k1 * l1) % r) / r
    f = np.stack([np.cos(ang), -np.sin(ang)], axis=1)
    m = np.einsum("kal,pq->kaplq", f, np.eye(blk))
    return m.reshape(r * 2 * blk, r * blk).astype(np.float32)


@functools.lru_cache(maxsize=None)
def _fft_stage2_matrices():
    r, blk = FFT_R, FFT_BLK
    n = r * r
    nt = r // blk
    l2 = np.arange(r)
    k2 = np.arange(r)
    out = np.zeros((nt, 2, r, blk, blk, r // blk, 2, blk), np.float32)
    for i in range(nt):
        for kl in range(blk):
            k1 = blk * i + kl
            ang = 2.0 * np.pi * (((k2[:, None] * l2[None, :] * r) + l2[None, :] * k1) % n) / n
            tr, ti = np.cos(ang), -np.sin(ang)
            tr = tr.reshape(r, r // blk, blk)
            ti = ti.reshape(r, r // blk, blk)
            out[i, 0, :, kl, kl, :, 0, :] = tr
            out[i, 0, :, kl, kl, :, 1, :] = -ti
            out[i, 1, :, kl, kl, :, 0, :] = ti
            out[i, 1, :, kl, kl, :, 1, :] = tr
    return out.reshape(nt, 2 * r * blk, blk * 2 * r)


def _fft2_body(gc_ref, m2_ref, cw_ref, sw_ref, g_ref, o_ref):
    kb, rr, df = gc_ref.shape
    gc = gc_ref[...].reshape(kb * rr, df)
    x = _dot(m2_ref[...], gc)
    half = x.shape[0] // 2
    xr = x[:half].astype(_BF)
    xi = x[half:].astype(_BF)
    ng = cw_ref.shape[0]
    fg = df // ng
    parts = []
    for g in range(ng):
        sl = slice(g * fg, (g + 1) * fg)
        parts.append(_dot(xr[:, sl], cw_ref[g]) + _dot(xi[:, sl], sw_ref[g]))
    f = jnp.concatenate(parts, axis=-1)
    fn = _rms(f, g_ref[...])
    o_ref[...] = fn.reshape(o_ref.shape)


def _fft2(gc, m2, cw, sw, gf):
    b, r, rr, df = gc.shape
    blk = FFT_BLK
    nt = r // blk
    out = pl.pallas_call(
        _fft2_body,
        grid=(b, nt),
        in_specs=[pl.BlockSpec((None, blk, rr, df), lambda i, j: (i, j, 0, 0)),
                  pl.BlockSpec((None,) + m2.shape[1:], lambda i, j: (j, 0, 0)),
                  pl.BlockSpec(cw.shape, lambda i, j: (0, 0, 0)),
                  pl.BlockSpec(sw.shape, lambda i, j: (0, 0, 0)),
                  pl.BlockSpec((1, df), lambda i, j: (0, 0))],
        out_specs=pl.BlockSpec((None, r, blk, df), lambda i, j: (i, 0, j, 0)),
        out_shape=jax.ShapeDtypeStruct((b, r, r, df), _F32),
        compiler_params=_cparams(("parallel", "arbitrary")),
        name="fft2",
    )(gc, m2, cw, sw, gf)
    return out.reshape(b, r * r, df)


def _s5_body(uc_ref, ul_ref, ws_ref, wy_ref, at_ref, y_ref,
             s_re, s_im, ha_re, hb_re, ha_im, hb_im, *, nb):
    p = S5_STATE
    gs, n_ctx, _ = uc_ref.shape
    n_lat = ul_ref.shape[1]
    kk = ws_ref.shape[1]
    for gi in range(gs):
        ws = ws_ref[gi]
        for b in range(nb):
            sl = slice(b * kk, (b + 1) * kk)
            sc = _dot(uc_ref[gi, :, sl], ws)
            sl_ = _dot(ul_ref[gi, :, sl], ws)
            s_re[gi, pl.ds(b, n_ctx, stride=nb), :] = sc[:, :2 * p]
            s_im[gi, pl.ds(b, n_ctx, stride=nb), :] = sc[:, 2 * p:]
            s_re[gi, pl.ds(n_ctx * nb + b, n_lat, stride=nb), :] = sl_[:, :2 * p]
            s_im[gi, pl.ds(n_ctx * nb + b, n_lat, stride=nb), :] = sl_[:, 2 * p:]

    a_re = [at_ref[gi, 0:1, :] for gi in range(gs)]
    a_im = [at_ref[gi, 1:2, :] for gi in range(gs)]
    is_f = lax.broadcasted_iota(jnp.int32, (nb, 2 * p), 1) < p
    n_all = n_ctx + n_lat

    def load(gi, cf, cr):
        rf = pl.multiple_of(cf * nb, nb)
        rr = pl.multiple_of(cr * nb, nb)
        return (jnp.where(is_f, s_re[gi, pl.ds(rf, nb), :], s_re[gi, pl.ds(rr, nb), :]),
                jnp.where(is_f, s_im[gi, pl.ds(rf, nb), :], s_im[gi, pl.ds(rr, nb), :]))

    def advance(gi, h_re, h_im, x_re, x_im):
        return (a_re[gi] * h_re - a_im[gi] * h_im + x_re, a_re[gi] * h_im + a_im[gi] * h_re + x_im)

    def ctx_step(step, carry):
        out = []
        for gi in range(gs):
            x_re, x_im = load(gi, step, n_ctx - 1 - step)
            out.append(advance(gi, *carry[gi], x_re, x_im))
        return tuple(out)

    def lat_step(step, carry):
        rf = pl.multiple_of(step * nb, nb)
        rr = pl.multiple_of((n_lat - 1 - step) * nb, nb)
        out = []
        for gi in range(gs):
            h_re, h_im = carry[gi]
            ha_re[gi, pl.ds(rf, nb), :] = h_re
            hb_re[gi, pl.ds(rr, nb), :] = h_re
            ha_im[gi, pl.ds(rf, nb), :] = h_im
            hb_im[gi, pl.ds(rr, nb), :] = h_im
            x_re, x_im = load(gi, n_ctx + step, n_all - 1 - step)
            out.append(advance(gi, h_re, h_im, x_re, x_im))
        return tuple(out)

    zero = jnp.zeros((nb, 2 * p), _F32)
    carry = lax.fori_loop(0, n_ctx, ctx_step, tuple((zero, zero) for _ in range(gs)), unroll=2)
    lax.fori_loop(0, n_lat, lat_step, carry, unroll=2)

    for gi in range(gs):
        wy = wy_ref[gi]
        for b in range(nb):
            rows = pl.ds(b, n_lat, stride=nb)
            hin = jnp.concatenate([ha_re[gi, rows, :], hb_re[gi, rows, :],
                                   ha_im[gi, rows, :], hb_im[gi, rows, :]], axis=-1)
            lhs = jnp.concatenate([ul_ref[gi, :, b * kk:(b + 1) * kk], hin.astype(_BF)], axis=-1)
            y_ref[gi, :, b * kk:(b + 1) * kk] = _dot(lhs, wy).astype(y_ref.dtype)


def _s5(u_ctx, u_lat, ws, wy, at, nb):
    ng, n_ctx, w = u_ctx.shape
    n_lat = u_lat.shape[1]
    gs = S5_SCAN_GROUPS
    body = functools.partial(_s5_body, nb=nb)
    hs = pltpu.VMEM((gs, n_lat * nb, 2 * S5_STATE), _F32)
    ss = pltpu.VMEM((gs, (n_ctx + n_lat) * nb, 2 * S5_STATE), _F32)
    grp = lambda a: pl.BlockSpec((gs,) + a.shape[1:], lambda g: (g, 0, 0))
    return pl.pallas_call(
        body,
        grid=(ng // gs,),
        in_specs=[grp(u_ctx), grp(u_lat), grp(ws), grp(wy), grp(at)],
        out_specs=pl.BlockSpec((gs, n_lat, w), lambda g: (g, 0, 0)),
        out_shape=jax.ShapeDtypeStruct((ng, n_lat, w), _BF),
        scratch_shapes=[ss, ss, hs, hs, hs, hs],
        compiler_params=_cparams(("parallel",)),
        name="s5_scan",
    )(u_ctx, u_lat, ws, wy, at)


def _s5_param_body(lr_ref, li_ref, ls_ref, btr_ref, bti_ref, cr_ref, ci_ref, ws_ref, wy_ref, at_ref):
    t, hh, p = S5_CHUNK, S5_GROUP, S5_STATE
    kk = t * hh
    lr, li = lr_ref[...], li_ref[...]
    dt = jnp.exp(ls_ref[...])
    ldt, idt = lr * dt, li * dt
    mag = jnp.exp(ldt)
    ab_re, ab_im = mag * jnp.cos(idt), mag * jnp.sin(idt)
    den = lr * lr + li * li
    nr, ni = ab_re - 1.0, ab_im
    q_re = (nr * lr + ni * li) / den
    q_im = (ni * lr - nr * li) / den
    kf = lax.broadcasted_iota(jnp.int32, (2 * t, p), 0).astype(_F32)

    ap_re, ap_im, bb_re, bb_im = [], [], [], []
    for d in range(2):
        pm = jnp.exp(kf * ldt[d:d + 1, :])
        ap_re.append(pm * jnp.cos(kf * idt[d:d + 1, :]))
        ap_im.append(pm * jnp.sin(kf * idt[d:d + 1, :]))
        bb_re.append(q_re[d:d + 1, :] * btr_ref[d] - q_im[d:d + 1, :] * bti_ref[d])
        bb_im.append(q_re[d:d + 1, :] * bti_ref[d] + q_im[d:d + 1, :] * btr_ref[d])

    def rep(a, ks):
        return jnp.concatenate([jnp.broadcast_to(a[k:k + 1, :], (hh, p)) for k in ks], axis=0)

    def tile(a, n):
        return jnp.concatenate([a] * n, axis=0)

    def cmul(xr, xi, yr, yi):
        return xr * yr - xi * yi, xr * yi + xi * yr

    ks_f = list(range(t + 1))
    ks_r = list(range(t, -1, -1))
    caf_re, caf_im = cmul(tile(cr_ref[0], t + 1), tile(ci_ref[0], t + 1), rep(ap_re[0], ks_f), rep(ap_im[0], ks_f))
    car_re, car_im = cmul(tile(cr_ref[1], t + 1), tile(ci_ref[1], t + 1), rep(ap_re[1], ks_r), rep(ap_im[1], ks_r))

    def lag_blocks(bre, bim, ca_re, ca_im):
        lhs = jnp.concatenate([bre, bim], axis=1)
        rhs = jnp.concatenate([ca_re, -ca_im], axis=1)
        return lax.dot_general(lhs, rhs, (((1,), (1,)), ((), ())), precision=_HI,
                               preferred_element_type=_F32)

    w_f = lag_blocks(bb_re[0], bb_im[0], caf_re[:kk], caf_im[:kk])
    w_r = lag_blocks(bb_re[1], bb_im[1], car_re[hh:], car_im[hh:])
    lane = lax.broadcasted_iota(jnp.int32, (hh, kk), 1)
    for tlp in range(t):
        sf = hh * tlp
        sr = hh * (t - 1 - tlp)
        a = w_f if sf == 0 else jnp.where(lane >= sf, pltpu.roll(w_f, sf, 1), 0.0)
        b = w_r if sr == 0 else jnp.where(lane < kk - sr, pltpu.roll(w_r, kk - sr, 1), 0.0)
        wy_ref[hh * tlp:hh * (tlp + 1), :] = (a + b).astype(wy_ref.dtype)

    zeros = jnp.zeros((p, kk), _F32)
    blocks = [caf_re[hh:].T, zeros, zeros, car_re[:kk].T, (-caf_im[hh:]).T, zeros, zeros, (-car_im[:kk]).T]
    for i, blk in enumerate(blocks):
        wy_ref[kk + p * i:kk + p * (i + 1), :] = blk.astype(wy_ref.dtype)

    pf = list(range(t - 1, -1, -1))
    pr = list(range(t))
    f_re, f_im = cmul(rep(ap_re[0], pf), rep(ap_im[0], pf), tile(bb_re[0], t), tile(bb_im[0], t))
    r_re, r_im = cmul(rep(ap_re[1], pr), rep(ap_im[1], pr), tile(bb_re[1], t), tile(bb_im[1], t))
    ws_ref[...] = jnp.concatenate([f_re, r_re, f_im, r_im], axis=1).astype(ws_ref.dtype)

    a_re = jnp.concatenate([ap_re[0][t:t + 1, :], ap_re[1][t:t + 1, :]], axis=1)
    a_im = jnp.concatenate([ap_im[0][t:t + 1, :], ap_im[1][t:t + 1, :]], axis=1)
    row = lax.broadcasted_iota(jnp.int32, at_ref.shape, 0)
    at_ref[...] = jnp.where(row == 0, a_re, jnp.where(row == 1, a_im, 0.0))


def _s5_params(lam_re, lam_im, b_re, b_im, c_re, c_im, log_step):
    _, ng, p = lam_re.shape
    hh = b_re.shape[-1]
    kk = S5_CHUNK * hh
    gd = lambda a: jnp.swapaxes(a.astype(_F32), 0, 1)
    args = (gd(lam_re), gd(lam_im), gd(log_step)[..., None],
            jnp.swapaxes(gd(b_re), 2, 3), jnp.swapaxes(gd(b_im), 2, 3), gd(c_re), gd(c_im))
    spec = lambda a: pl.BlockSpec((None,) + a.shape[1:], lambda g: (g,) + (0,) * (a.ndim - 1))
    return pl.pallas_call(
        _s5_param_body,
        grid=(ng,),
        in_specs=[spec(a) for a in args],
        out_specs=[pl.BlockSpec((None, kk, 4 * p), lambda g: (g, 0, 0)),
                   pl.BlockSpec((None, kk + 8 * p, kk), lambda g: (g, 0, 0)),
                   pl.BlockSpec((None, 8, 2 * p), lambda g: (g, 0, 0))],
        out_shape=[jax.ShapeDtypeStruct((ng, kk, 4 * p), _BF),
                   jax.ShapeDtypeStruct((ng, kk + 8 * p, kk), _BF),
                   jax.ShapeDtypeStruct((ng, 8, 2 * p), _F32)],
        compiler_params=_cparams(("parallel",)),
        name="s5_params",
    )(*args)


def _post_body(x_ref, u_ref, yg_ref, fn_ref, d_ref, wglu_ref, gs_ref, wo_ref, g1_ref,
               n2_ref, sh_ref, sc_ref, wr_ref, br_ref, tri_ref, x1_ref, t_ref, rt_ref, rtt_ref, cnt_ref,
               y_scr, cnt_scr):
    ds5 = u_ref.shape[-1]
    first = (pl.program_id(0) == 0) & (pl.program_id(1) == 0)

    @pl.when(first)
    def _():
        cnt_scr[...] = jnp.zeros_like(cnt_scr)

    t_chunk = S5_CHUNK
    n_chunk = yg_ref.shape[1]
    lanes = 128
    gpb = lanes // S5_GROUP
    for j in range(ds5 // lanes):
        for hi in range(t_chunk // gpb):
            outs = _seg_transpose([yg_ref[gpb * j + glo, :, lanes * hi:lanes * (hi + 1)].astype(_F32)
                                   for glo in range(gpb)])
            for tlo in range(gpb):
                y_scr[j, pl.ds(gpb * hi + tlo, n_chunk, stride=t_chunk), :] = outs[tlo]
    y_s5 = jnp.concatenate([y_scr[j] for j in range(ds5 // lanes)], axis=-1)

    y = y_s5 + d_ref[...] * u_ref[...]
    y = jax.nn.gelu(y, approximate=True)
    y = y * _sigmoid(_dot(y.astype(_BF), wglu_ref[...]))
    yn = _rms(y, gs_ref[...]).astype(_BF)
    mix = _dot(yn, wo_ref[0:ds5, :]) + _dot(fn_ref[...].astype(_BF), wo_ref[ds5:, :])
    x1 = x_ref[...] + g1_ref[...] * mix
    x1_ref[...] = x1
    t = _rms(x1, n2_ref[...]) * (1.0 + sc_ref[...]) + sh_ref[...]
    t_hi = t.astype(_BF)
    t_ref[...] = t_hi

    ng, epg = N_EXPERT_GROUPS, EXPERTS_PER_GROUP
    t_lo = (t - t_hi.astype(_F32)).astype(_BF)
    logits = (_dot(t_hi, wr_ref[0]) + _dot(t_lo, wr_ref[0]) + _dot(t_hi, wr_ref[1])) + br_ref[...]
    tm = logits.shape[0]
    lt = logits.T
    neg = jnp.float32(-jnp.inf)
    big = jnp.int32(1 << 20)
    row8 = lax.broadcasted_iota(jnp.int32, (epg, tm), 0)
    amax = lambda v: jnp.max(v, axis=0, keepdims=True)
    first = lambda v, m: jnp.min(jnp.where(v == m, row8, big), axis=0, keepdims=True)
    gt = lt[ng * epg:ng * epg + epg, :]
    gl = jnp.where(row8 < ng, gt, neg)
    gmax = amax(gl)
    gidx = first(gl, gmax)
    gw = 1.0 / jnp.sum(jnp.where(row8 < ng, jnp.exp(gt - gmax), 0.0), axis=0, keepdims=True)
    el = lt[0:epg, :]
    for g in range(1, ng):
        el = jnp.where(gidx == g, lt[g * epg:(g + 1) * epg, :], el)
    v0 = amax(el)
    i0 = first(el, v0)
    el1 = jnp.where(row8 == i0, neg, el)
    v1 = amax(el1)
    i1 = first(el1, v1)
    p0 = 1.0 / (1.0 + jnp.exp(v1 - v0))
    w0 = gw * p0
    w1 = gw * (1.0 - p0)
    e0 = gidx * epg + i0
    e1 = gidx * epg + i1

    rowe = lax.broadcasted_iota(jnp.int32, (ng * epg, tm), 0)
    oh0 = rowe == e0
    oh1 = rowe == e1
    oh = (oh0 | oh1).astype(_F32)
    prefix = _dot(oh.astype(_BF), tri_ref[...]) + cnt_scr[...]
    r0 = jnp.sum(jnp.where(oh0, prefix, 0.0), axis=0, keepdims=True)
    r1 = jnp.sum(jnp.where(oh1, prefix, 0.0), axis=0, keepdims=True)
    cnt = cnt_scr[...] + jnp.sum(oh, axis=1, keepdims=True)
    cnt_scr[...] = cnt
    cnt_ref[...] = cnt

    rec = jnp.concatenate([e0.astype(_F32), e1.astype(_F32), w0, w1, r0, r1,
                           jnp.zeros((ROUTE_ROWS - 6, tm), _F32)], axis=0)
    rtt_ref[...] = rec
    rt_ref[...] = jnp.concatenate([rec, jnp.zeros((ROUTE_LANES - ROUTE_ROWS, tm), _F32)], axis=0).T


@functools.lru_cache(maxsize=None)
def _strict_upper_ones(n):
    return np.triu(np.ones((n, n), np.float32), 1).astype(jnp.bfloat16)


def _post(x, u, y_g, fn, s5_d, wglu_bf, gs, wo_bf, g1, n2, sh2, sc2, wr, br, b0, nb, tm=512):
    b, l, d = x.shape
    ds5 = u.shape[-1]
    n_grp, _, w = y_g.shape
    kk = w // b
    n_exp = N_EXPERT_GROUPS * EXPERTS_PER_GROUP
    tri = jnp.asarray(_strict_upper_ones(tm))
    tok = lambda w: pl.BlockSpec((None, tm, w), lambda i, j: (i + b0, j, 0))
    otok = lambda w: pl.BlockSpec((None, tm, w), lambda i, j: (i, j, 0))
    per_b = pl.BlockSpec((None, 1, d), lambda i, j: (i + b0, 0, 0))
    full = lambda a: pl.BlockSpec(a.shape, lambda i, j: (0,) * a.ndim)
    ygs = pl.BlockSpec((n_grp, tm // S5_CHUNK, kk), lambda i, j: (0, j, i + b0))
    return pl.pallas_call(
        _post_body,
        grid=(nb, l // tm),
        in_specs=[tok(d), tok(ds5), ygs, tok(fn.shape[-1]), full(s5_d), full(wglu_bf),
                  full(gs), full(wo_bf), per_b, full(n2), per_b, per_b, full(wr), full(br), full(tri)],
        out_specs=[otok(d), otok(d), otok(ROUTE_LANES),
                   pl.BlockSpec((None, ROUTE_ROWS, tm), lambda i, j: (i, 0, j)),
                   pl.BlockSpec((n_exp, 1), lambda i, j: (0, 0))],
        out_shape=[jax.ShapeDtypeStruct((nb, l, d), _F32),
                   jax.ShapeDtypeStruct((nb, l, d), _BF),
                   jax.ShapeDtypeStruct((nb, l, ROUTE_LANES), _F32),
                   jax.ShapeDtypeStruct((nb, ROUTE_ROWS, l), _F32),
                   jax.ShapeDtypeStruct((n_exp, 1), _F32)],
        scratch_shapes=[pltpu.VMEM((ds5 // 128, tm, 128), _F32),
                        pltpu.VMEM((n_exp, 1), _F32)],
        compiler_params=_cparams(("arbitrary", "arbitrary")),
        name="post_mixer",
    )(x, u, y_g, fn, s5_d, wglu_bf, gs, wo_bf, g1, n2, sh2, sc2, wr, br, tri)


def _moe_body(te_ref, nt_ref, xs_ref, wg_ref, wu_ref, wd_ref, o_ref, wgu_bf, wd_bf):
    i = pl.program_id(0)
    de = wg_ref.shape[-1]
    used = i < nt_ref[0]
    new_expert = (i == 0) | (te_ref[i] != te_ref[jnp.maximum(i - 1, 0)])

    @pl.when(used & new_expert)
    def _():
        wgu_bf[:, 0:de] = wg_ref[...].astype(_BF)
        wgu_bf[:, de:2 * de] = wu_ref[...].astype(_BF)
        wd_bf[...] = wd_ref[...].astype(_BF)

    @pl.when(used)
    def _():
        h = _dot(xs_ref[...], wgu_bf[...])
        hg, hu = h[:, 0:de], h[:, de:2 * de]
        a = hg * _sigmoid(hg) * hu
        o_ref[...] = _dot(a.astype(_BF), wd_bf[...]).astype(o_ref.dtype)

    @pl.when(jnp.logical_not(used))
    def _():
        o_ref[...] = jnp.zeros_like(o_ref)


def _moe(tile_expert, n_tiles_used, xs, wg, wu, wd):
    nr, d = xs.shape
    tm = MOE_TM
    de = wg.shape[-1]
    grid_spec = pltpu.PrefetchScalarGridSpec(
        num_scalar_prefetch=2,
        grid=(nr // tm,),
        in_specs=[pl.BlockSpec((tm, d), lambda i, te, nt: (i, 0)),
                  pl.BlockSpec((None, d, de), lambda i, te, nt: (te[i], 0, 0)),
                  pl.BlockSpec((None, d, de), lambda i, te, nt: (te[i], 0, 0)),
                  pl.BlockSpec((None, de, d), lambda i, te, nt: (te[i], 0, 0))],
        out_specs=pl.BlockSpec((tm, d), lambda i, te, nt: (i, 0)),
        scratch_shapes=[pltpu.VMEM((d, 2 * de), _BF), pltpu.VMEM((de, d), _BF)],
    )
    return pl.pallas_call(
        _moe_body,
        grid_spec=grid_spec,
        out_shape=jax.ShapeDtypeStruct((nr, d), _BF),
        compiler_params=_cparams(("arbitrary",)),
        name="moe_experts",
    )(tile_expert, n_tiles_used, xs, wg, wu, wd)


def _final_body(g2_ref, gf_ref, *refs, n_part, nb):
    o_ref = refs[-1]
    i = pl.program_id(0)
    for part in range(n_part):
        x1_ref, y0_ref, y1_ref, rt_ref = refs[4 * part:4 * part + 4]

        @pl.when((i >= part * nb) & (i < (part + 1) * nb))
        def _():
            w0 = rt_ref[:, 2:3]
            w1 = rt_ref[:, 3:4]
            m = w0 * y0_ref[...].astype(_F32) + w1 * y1_ref[...].astype(_F32)
            o_ref[...] = _rms(x1_ref[...] + g2_ref[...] * m, gf_ref[...])


def _final(parts, g2, gf, tm=512):
    n_part = len(parts)
    nb, l, d = parts[0][0].shape
    nj = l // tm
    in_specs = [pl.BlockSpec((None, 1, d), lambda i, j: (i, 0, 0)),
                pl.BlockSpec((1, d), lambda i, j: (0, 0))]
    args = [g2, gf]
    for part, arrs in enumerate(parts):
        def imap(i, j, part=part):
            own = (i >= part * nb) & (i < (part + 1) * nb)
            return (jnp.clip(i - part * nb, 0, nb - 1), jnp.where(own, j, jnp.where(i < part * nb, 0, nj - 1)), 0)
        for a in arrs:
            in_specs.append(pl.BlockSpec((None, tm, a.shape[-1]), imap))
            args.append(a)
    return pl.pallas_call(
        functools.partial(_final_body, n_part=n_part, nb=nb),
        grid=(n_part * nb, nj),
        in_specs=in_specs,
        out_specs=pl.BlockSpec((None, tm, d), lambda i, j: (i, j, 0)),
        out_shape=jax.ShapeDtypeStruct((n_part * nb, l, d), _F32),
        compiler_params=_cparams(("parallel", "arbitrary")),
        name="final_norm",
    )(*args)


def _route_plan(eid, rank, tok_ids, counts, tm, nr):
    n_experts = counts.shape[0]
    n_pairs = eid.size
    padded = ((counts + tm - 1) // tm) * tm
    pad_end = jnp.cumsum(padded)
    pad_start = pad_end - padded
    raw_start = jnp.cumsum(counts) - counts
    pos = rank
    for e in range(n_experts):
        pos = pos + jnp.where(eid == e, pad_start[e], 0)
    tile_start = jnp.arange(nr // tm, dtype=jnp.int32) * tm
    tile_expert = jnp.sum(tile_start[:, None] >= pad_end[None, :], axis=-1)
    tile_expert = jnp.minimum(tile_expert, n_experts - 1).astype(jnp.int32)
    _, sorted_tok = lax.sort_key_val((eid * n_pairs + rank).reshape(-1), tok_ids.reshape(-1))
    sel = tile_expert[:, None] == jnp.arange(n_experts, dtype=jnp.int32)[None, :]
    per_tile = lambda v: jnp.repeat(jnp.sum(jnp.where(sel, v[None, :], 0), axis=-1), tm)
    off = jnp.arange(nr, dtype=jnp.int32) - per_tile(pad_start)
    valid = off < per_tile(counts)
    j = jnp.clip(per_tile(raw_start) + off, 0, n_pairs - 1)
    filler = jnp.arange(nr, dtype=jnp.int32) % (n_pairs // 2)
    row_token = jnp.where(valid, sorted_tok.at[j].get(mode="promise_in_bounds"), filler)
    n_used = (pad_end[-1:] // tm).astype(jnp.int32)
    return pos, tile_expert, n_used, row_token


def kernel(x, c, ctx, c_ctx, w_ada, b_ada, norm1_g, norm2_g, w_in, s5_lam_re, s5_lam_im, s5_b_re, s5_b_im, s5_c_re, s5_c_im, s5_log_step, s5_d, s5_w_glu, fourier_w, mix_norm_s5_g, mix_norm_f_g, w_out, moe_w_group, moe_b_group, moe_w_router, moe_b_router, moe_w_gate, moe_w_up, moe_w_down, final_norm_g):
    b, l, d = x.shape
    lc = ctx.shape[1]
    depth = w_ada.shape[0]
    assert depth == 1 and l == FFT_R * FFT_R and lc % S5_CHUNK == 0 and b % 8 == 0
    ds5 = s5_d.shape[-1]
    df = w_in.shape[-1] - ds5
    n_exp = moe_w_gate.shape[1]
    row = lambda a: a.reshape(1, -1)

    cond = jnp.concatenate([c, c_ctx[None, :], jnp.zeros((7, d), _F32)], axis=0)
    mod = _adaln(cond, w_ada[0], b_ada[0])
    sh1, sc1, g1, sh2, sc2, g2 = [mod[:b, i * d:(i + 1) * d].reshape(b, 1, d) for i in range(6)]
    csh1 = mod[b:b + 1, 0:d]
    csc1 = mod[b:b + 1, d:2 * d]

    w_in_bf = w_in[0].astype(_BF)
    m1 = jnp.asarray(_fft_stage1_matrix()).astype(_BF)
    m2 = jnp.asarray(_fft_stage2_matrices()).astype(_BF)
    perm = jnp.asarray(_chunk_row_perm())
    zs, gc = _inproj(x, sh1, sc1, row(norm1_g[0]), w_in_bf, m1)
    u_ctx = _inproj_ctx(ctx, csh1, csc1, row(norm1_g[0]), w_in_bf[:, :ds5], perm)
    u_lat = _s5_pack(zs, perm)

    fg = df // FOURIER_GROUPS
    cc = np.arange(fg)
    ang = 2.0 * np.pi * ((cc[:, None] * cc[None, :]) % fg) / fg
    scale = 1.0 / math.sqrt(l * fg)
    cw = jnp.einsum("cm,gmd->gcd", jnp.asarray(np.cos(ang) * scale, _F32), fourier_w[0], precision=_HI)
    sw = jnp.einsum("cm,gmd->gcd", jnp.asarray(np.sin(ang) * scale, _F32), fourier_w[0], precision=_HI)
    fn = _fft2(gc, m2, cw.astype(_BF), sw.astype(_BF), row(mix_norm_f_g[0]))

    ws, wy, at = _s5_params(s5_lam_re[0], s5_lam_im[0], s5_b_re[0], s5_b_im[0],
                            s5_c_re[0], s5_c_im[0], s5_log_step[0])
    y_g = _s5(u_ctx, u_lat, ws, wy, at, b)

    n_rt = N_EXPERT_GROUPS * (1 + EXPERTS_PER_GROUP)
    w_rt = jnp.concatenate([moe_w_router[0].reshape(d, -1), moe_w_group[0]], axis=-1)
    w_rt = jnp.pad(w_rt, ((0, 0), (0, ROUTE_LANES - n_rt)))
    w_rt_hi = w_rt.astype(_BF)
    w_rt = jnp.stack([w_rt_hi, (w_rt - w_rt_hi.astype(_F32)).astype(_BF)])
    b_rt = jnp.pad(jnp.concatenate([moe_b_router[0].reshape(-1), moe_b_group[0]]), (0, ROUTE_LANES - n_rt))
    tm = MOE_TM
    nb = b // MOE_PARTS
    n_tok = nb * l
    nr = 2 * n_tok + n_exp * tm
    take_rows = lambda a, idx: a.at[idx].get(mode="promise_in_bounds")
    wglu_bf, wo_bf = s5_w_glu[0].astype(_BF), w_out[0].astype(_BF)
    parts = []
    for part in range(MOE_PARTS):
        x1, tmod, rt, rtt, cnt = _post(x, zs, y_g, fn, row(s5_d[0]), wglu_bf, row(mix_norm_s5_g[0]), wo_bf,
                                       g1, row(norm2_g[0]), sh2, sc2, w_rt, row(b_rt), part * nb, nb)
        rec = rtt.transpose(1, 0, 2).reshape(ROUTE_ROWS, n_tok)
        eid = rec[0:2].astype(jnp.int32)
        rank = rec[4:6].astype(jnp.int32)
        tok_ids = jnp.broadcast_to(jnp.arange(n_tok, dtype=jnp.int32), (2, n_tok))
        pos, tile_expert, n_used, row_token = _route_plan(eid, rank, tok_ids,
                                                          cnt[:, 0].astype(jnp.int32), tm, nr)
        xs = take_rows(tmod.reshape(n_tok, d), row_token)
        ys = _moe(tile_expert, n_used, xs, moe_w_gate[0], moe_w_up[0], moe_w_down[0])
        y0 = take_rows(ys, pos[0]).reshape(nb, l, d)
        y1 = take_rows(ys, pos[1]).reshape(nb, l, d)
        parts.append((x1, y0, y1, rt))
    return _final(parts, g2, row(final_norm_g))
```

```python
import functools
import math

import numpy as np
import jax
import jax.numpy as jnp
from jax import lax
from jax.experimental import pallas as pl
from jax.experimental.pallas import tpu as pltpu

EPS = 1e-6
S5_GROUP = 16
S5_STATE = 64
S5_CHUNK = 16
S5_SCAN_GROUPS = 4
FOURIER_GROUPS = 4
N_EXPERT_GROUPS = 4
EXPERTS_PER_GROUP = 8
FFT_R = 64
FFT_BLK = 8
MOE_TM = 512
MOE_PARTS = 1
ROUTE_LANES = 128
ROUTE_ROWS = 8
VMEM_LIMIT = 56 * 1024 * 1024

_HI = lax.Precision.HIGHEST
_BF = jnp.bfloat16
_F32 = jnp.float32


def _cparams(sem):
    return pltpu.CompilerParams(dimension_semantics=sem, vmem_limit_bytes=VMEM_LIMIT)


def _dot(a, b):
    return jnp.dot(a, b, preferred_element_type=_F32)


def _rms(x, g):
    return x * lax.rsqrt(jnp.mean(x * x, axis=-1, keepdims=True) + EPS) * g


def _adaln_body(c_ref, w_ref, b_ref, o_ref):
    c = c_ref[...]
    a = c * jax.nn.sigmoid(c)
    o_ref[...] = jnp.dot(a, w_ref[...], preferred_element_type=_F32, precision=_HI) + b_ref[...]


def _adaln(cond, w, b):
    m, d = cond.shape
    n = w.shape[1]
    tn = 768
    return pl.pallas_call(
        _adaln_body,
        grid=(n // tn,),
        in_specs=[pl.BlockSpec((m, d), lambda j: (0, 0)),
                  pl.BlockSpec((d, tn), lambda j: (0, j)),
                  pl.BlockSpec((1, tn), lambda j: (0, j))],
        out_specs=pl.BlockSpec((m, tn), lambda j: (0, j)),
        out_shape=jax.ShapeDtypeStruct((m, n), _F32),
        compiler_params=_cparams(("arbitrary",)),
        name="adaln",
    )(cond, w, b.reshape(1, n))


def _inproj_body(x_ref, sh_ref, sc_ref, g_ref, w_ref, m1_ref, zs_ref, gc_ref):
    r, blk, d = x_ref.shape
    x = x_ref[...].reshape(r * blk, d)
    h = _rms(x, g_ref[...]) * (1.0 + sc_ref[...]) + sh_ref[...]
    z = _dot(h.astype(_BF), w_ref[...])
    ds5 = zs_ref.shape[-1]
    zs_ref[...] = z[:, :ds5].reshape(r, blk, ds5)
    v = z[:, ds5:].astype(_BF)
    g1 = _dot(m1_ref[...], v)
    gc_ref[...] = g1.astype(_BF).reshape(r, 2 * blk, v.shape[-1])


def _inproj(x, sh, sc, g, w_bf, m1):
    b, l, d = x.shape
    r, blk = FFT_R, FFT_BLK
    nj = r // blk
    dmix = w_bf.shape[1]
    ds5 = dmix // 2
    df = dmix - ds5
    x4 = x.reshape(b, r, r, d)
    zs, gc = pl.pallas_call(
        _inproj_body,
        grid=(b, nj),
        in_specs=[pl.BlockSpec((None, r, blk, d), lambda i, j: (i, 0, j, 0)),
                  pl.BlockSpec((None, 1, d), lambda i, j: (i, 0, 0)),
                  pl.BlockSpec((None, 1, d), lambda i, j: (i, 0, 0)),
                  pl.BlockSpec((1, d), lambda i, j: (0, 0)),
                  pl.BlockSpec((d, dmix), lambda i, j: (0, 0)),
                  pl.BlockSpec(m1.shape, lambda i, j: (0, 0))],
        out_specs=[pl.BlockSpec((None, r, blk, ds5), lambda i, j: (i, 0, j, 0)),
                   pl.BlockSpec((None, r, 2 * blk, df), lambda i, j: (i, 0, j, 0))],
        out_shape=[jax.ShapeDtypeStruct((b, r, r, ds5), _F32),
                   jax.ShapeDtypeStruct((b, r, 2 * r, df), _BF)],
        compiler_params=_cparams(("parallel", "arbitrary")),
        name="inproj",
    )(x4, sh, sc, g, w_bf, m1)
    return zs.reshape(b, l, ds5), gc


def _seg_transpose(arrs):
    n = len(arrs)
    seg = lax.broadcasted_iota(jnp.int32, arrs[0].shape, 1) // S5_GROUP
    d = n // 2
    while d >= 1:
        keep = (seg & d) == 0
        new = list(arrs)
        for i in range(n):
            if i & d == 0:
                a, b = arrs[i], arrs[i + d]
                new[i] = jnp.where(keep, a, pltpu.roll(b, S5_GROUP * d, 1))
                new[i + d] = jnp.where(keep, pltpu.roll(a, 128 - S5_GROUP * d, 1), b)
        arrs = new
        d //= 2
    return arrs


@functools.lru_cache(maxsize=None)
def _chunk_row_perm():
    t = S5_CHUNK
    p = np.zeros((t * t, t * t), np.float32)
    for c in range(t):
        for tl in range(t):
            p[tl * t + c, c * t + tl] = 1.0
    return p.astype(jnp.bfloat16)


def _pack_chunks(z_bf, p_ref, u_ref):
    t = S5_CHUNK
    nsub = z_bf.shape[0] // (t * t)
    pieces = [_dot(p_ref[...], z_bf[s * t * t:(s + 1) * t * t, :]) for s in range(nsub)]
    a = []
    for tl in range(t):
        rows = [pc[tl * t:(tl + 1) * t, :] for pc in pieces]
        a.append(rows[0] if nsub == 1 else jnp.concatenate(rows, axis=0))
    lanes = 128
    gpb = lanes // S5_GROUP
    for j in range(z_bf.shape[1] // lanes):
        for hi in range(t // gpb):
            outs = _seg_transpose([a[gpb * hi + tlo][:, lanes * j:lanes * (j + 1)] for tlo in range(gpb)])
            for glo in range(gpb):
                u_ref[gpb * j + glo, :, lanes * hi:lanes * (hi + 1)] = outs[glo].astype(u_ref.dtype)


def _inproj_ctx_body(x_ref, sh_ref, sc_ref, g_ref, w_ref, p_ref, u_ref):
    h = _rms(x_ref[...], g_ref[...]) * (1.0 + sc_ref[...]) + sh_ref[...]
    z = _dot(h.astype(_BF), w_ref[...])
    _pack_chunks(z.astype(_BF), p_ref, u_ref)


def _inproj_ctx(ctx, sh, sc, g, w_s5_bf, perm):
    b, lc, d = ctx.shape
    ds5 = w_s5_bf.shape[1]
    n_grp = ds5 // S5_GROUP
    kk = S5_CHUNK * S5_GROUP
    return pl.pallas_call(
        _inproj_ctx_body,
        grid=(b,),
        in_specs=[pl.BlockSpec((None, lc, d), lambda i: (i, 0, 0)),
                  pl.BlockSpec((1, d), lambda i: (0, 0)),
                  pl.BlockSpec((1, d), lambda i: (0, 0)),
                  pl.BlockSpec((1, d), lambda i: (0, 0)),
                  pl.BlockSpec((d, ds5), lambda i: (0, 0)),
                  pl.BlockSpec(perm.shape, lambda i: (0, 0))],
        out_specs=pl.BlockSpec((n_grp, lc // S5_CHUNK, kk), lambda i: (0, 0, i)),
        out_shape=jax.ShapeDtypeStruct((n_grp, lc // S5_CHUNK, b * kk), _BF),
        compiler_params=_cparams(("arbitrary",)),
        name="inproj_ctx",
    )(ctx, sh, sc, g, w_s5_bf, perm)


def _s5_pack_body(z_ref, p_ref, u_ref):
    _pack_chunks(z_ref[...].astype(_BF), p_ref, u_ref)


def _s5_pack(zs, perm, tok=2048):
    b, l, ds5 = zs.shape
    n_grp = ds5 // S5_GROUP
    kk = S5_CHUNK * S5_GROUP
    return pl.pallas_call(
        _s5_pack_body,
        grid=(b, l // tok),
        in_specs=[pl.BlockSpec((None, tok, ds5), lambda i, j: (i, j, 0)),
                  pl.BlockSpec(perm.shape, lambda i, j: (0, 0))],
        out_specs=pl.BlockSpec((n_grp, tok // S5_CHUNK, kk), lambda i, j: (0, j, i)),
        out_shape=jax.ShapeDtypeStruct((n_grp, l // S5_CHUNK, b * kk), _BF),
        compiler_params=_cparams(("parallel", "arbitrary")),
        name="s5_pack",
    )(zs, perm)


@functools.lru_cache(maxsize=None)
def _fft_stage1_matrix():
    r, blk = FFT_R, FFT_BLK
    k1 = np.arange(r)[:, None]
    l1 = np.arange(r)[None, :]
    ang = 2.0 * np.pi * ((k1 * l1) % r) / r
    f = np.stack([np.cos(ang), -np.sin(ang)], axis=1)
    m = np.einsum("kal,pq->kaplq", f, np.eye(blk))
    return m.reshape(r * 2 * blk, r * blk).astype(np.float32)


@functools.lru_cache(maxsize=None)
def _fft_stage2_matrices():
    r, blk = FFT_R, FFT_BLK
    n = r * r
    nt = r // blk
    l2 = np.arange(r)
    k2 = np.arange(r)
    out = np.zeros((nt, 2, r, blk, blk, r // blk, 2, blk), np.float32)
    for i in range(nt):
        for kl in range(blk):
            k1 = blk * i + kl
            ang = 2.0 * np.pi * (((k2[:, None] * l2[None, :] * r) + l2[None, :] * k1) % n) / n
            tr, ti = np.cos(ang), -np.sin(ang)
            tr = tr.reshape(r, r // blk, blk)
            ti = ti.reshape(r, r // blk, blk)
            out[i, 0, :, kl, kl, :, 0, :] = tr
            out[i, 0, :, kl, kl, :, 1, :] = -ti
            out[i, 1, :, kl, kl, :, 0, :] = ti
            out[i, 1, :, kl, kl, :, 1, :] = tr
    return out.reshape(nt, 2 * r * blk, blk * 2 * r)


def _fft2_body(gc_ref, m2_ref, cw_ref, sw_ref, g_ref, o_ref):
    kb, rr, df = gc_ref.shape
    gc = gc_ref[...].reshape(kb * rr, df)
    x = _dot(m2_ref[...], gc)
    half = x.shape[0] // 2
    xr = x[:half].astype(_BF)
    xi = x[half:].astype(_BF)
    ng = cw_ref.shape[0]
    fg = df // ng
    parts = []
    for g in range(ng):
        sl = slice(g * fg, (g + 1) * fg)
        parts.append(_dot(xr[:, sl], cw_ref[g]) + _dot(xi[:, sl], sw_ref[g]))
    f = jnp.concatenate(parts, axis=-1)
    fn = _rms(f, g_ref[...])
    o_ref[...] = fn.reshape(o_ref.shape)


def _fft2(gc, m2, cw, sw, gf):
    b, r, rr, df = gc.shape
    blk = FFT_BLK
    nt = r // blk
    out = pl.pallas_call(
        _fft2_body,
        grid=(nt, b),
        in_specs=[pl.BlockSpec((None, blk, rr, df), lambda j, i: (i, j, 0, 0)),
                  pl.BlockSpec((None,) + m2.shape[1:], lambda j, i: (j, 0, 0)),
                  pl.BlockSpec(cw.shape, lambda j, i: (0, 0, 0)),
                  pl.BlockSpec(sw.shape, lambda j, i: (0, 0, 0)),
                  pl.BlockSpec((1, df), lambda j, i: (0, 0))],
        out_specs=pl.BlockSpec((None, r, blk, df), lambda j, i: (i, 0, j, 0)),
        out_shape=jax.ShapeDtypeStruct((b, r, r, df), _F32),
        compiler_params=_cparams(("parallel", "arbitrary")),
        name="fft2",
    )(gc, m2, cw, sw, gf)
    return out.reshape(b, r * r, df)


def _s5_body(uc_ref, ul_ref, ws_ref, wy_ref, at_ref, y_ref,
             s_re, s_im, ha_re, hb_re, ha_im, hb_im, *, nb):
    p = S5_STATE
    gs, n_ctx, _ = uc_ref.shape
    n_lat = ul_ref.shape[1]
    kk = ws_ref.shape[1]
    for gi in range(gs):
        ws = ws_ref[gi]
        for b in range(nb):
            sl = slice(b * kk, (b + 1) * kk)
            sc = _dot(uc_ref[gi, :, sl], ws)
            sl_ = _dot(ul_ref[gi, :, sl], ws)
            s_re[gi, pl.ds(b, n_ctx, stride=nb), :] = sc[:, :2 * p]
            s_im[gi, pl.ds(b, n_ctx, stride=nb), :] = sc[:, 2 * p:]
            s_re[gi, pl.ds(n_ctx * nb + b, n_lat, stride=nb), :] = sl_[:, :2 * p]
            s_im[gi, pl.ds(n_ctx * nb + b, n_lat, stride=nb), :] = sl_[:, 2 * p:]

    a_re = [at_ref[gi, 0:1, :] for gi in range(gs)]
    a_im = [at_ref[gi, 1:2, :] for gi in range(gs)]
    is_f = lax.broadcasted_iota(jnp.int32, (nb, 2 * p), 1) < p
    n_all = n_ctx + n_lat

    def load(gi, cf, cr):
        rf = pl.multiple_of(cf * nb, nb)
        rr = pl.multiple_of(cr * nb, nb)
        return (jnp.where(is_f, s_re[gi, pl.ds(rf, nb), :], s_re[gi, pl.ds(rr, nb), :]),
                jnp.where(is_f, s_im[gi, pl.ds(rf, nb), :], s_im[gi, pl.ds(rr, nb), :]))

    def advance(gi, h_re, h_im, x_re, x_im):
        return (a_re[gi] * h_re - a_im[gi] * h_im + x_re, a_re[gi] * h_im + a_im[gi] * h_re + x_im)

    def ctx_step(step, carry):
        out = []
        for gi in range(gs):
            x_re, x_im = load(gi, step, n_ctx - 1 - step)
            out.append(advance(gi, *carry[gi], x_re, x_im))
        return tuple(out)

    def lat_step(step, carry):
        rf = pl.multiple_of(step * nb, nb)
        rr = pl.multiple_of((n_lat - 1 - step) * nb, nb)
        out = []
        for gi in range(gs):
            h_re, h_im = carry[gi]
            ha_re[gi, pl.ds(rf, nb), :] = h_re
            hb_re[gi, pl.ds(rr, nb), :] = h_re
            ha_im[gi, pl.ds(rf, nb), :] = h_im
            hb_im[gi, pl.ds(rr, nb), :] = h_im
            x_re, x_im = load(gi, n_ctx + step, n_all - 1 - step)
            out.append(advance(gi, h_re, h_im, x_re, x_im))
        return tuple(out)

    zero = jnp.zeros((nb, 2 * p), _F32)
    carry = lax.fori_loop(0, n_ctx, ctx_step, tuple((zero, zero) for _ in range(gs)), unroll=2)
    lax.fori_loop(0, n_lat, lat_step, carry, unroll=2)

    for gi in range(gs):
        wy = wy_ref[gi]
        for b in range(nb):
            rows = pl.ds(b, n_lat, stride=nb)
            hin = jnp.concatenate([ha_re[gi, rows, :], hb_re[gi, rows, :],
                                   ha_im[gi, rows, :], hb_im[gi, rows, :]], axis=-1)
            lhs = jnp.concatenate([ul_ref[gi, :, b * kk:(b + 1) * kk], hin.astype(_BF)], axis=-1)
            y_ref[gi, :, b * kk:(b + 1) * kk] = _dot(lhs, wy).astype(y_ref.dtype)


def _s5(u_ctx, u_lat, ws, wy, at, nb):
    ng, n_ctx, w = u_ctx.shape
    n_lat = u_lat.shape[1]
    gs = S5_SCAN_GROUPS
    body = functools.partial(_s5_body, nb=nb)
    hs = pltpu.VMEM((gs, n_lat * nb, 2 * S5_STATE), _F32)
    ss = pltpu.VMEM((gs, (n_ctx + n_lat) * nb, 2 * S5_STATE), _F32)
    grp = lambda a: pl.BlockSpec((gs,) + a.shape[1:], lambda g: (g, 0, 0))
    return pl.pallas_call(
        body,
        grid=(ng // gs,),
        in_specs=[grp(u_ctx), grp(u_lat), grp(ws), grp(wy), grp(at)],
        out_specs=pl.BlockSpec((gs, n_lat, w), lambda g: (g, 0, 0)),
        out_shape=jax.ShapeDtypeStruct((ng, n_lat, w), _BF),
        scratch_shapes=[ss, ss, hs, hs, hs, hs],
        compiler_params=_cparams(("parallel",)),
        name="s5_scan",
    )(u_ctx, u_lat, ws, wy, at)


def _s5_param_body(lr_ref, li_ref, ls_ref, btr_ref, bti_ref, cr_ref, ci_ref, ws_ref, wy_ref, at_ref):
    t, hh, p = S5_CHUNK, S5_GROUP, S5_STATE
    kk = t * hh
    lr, li = lr_ref[...], li_ref[...]
    dt = jnp.exp(ls_ref[...])
    ldt, idt = lr * dt, li * dt
    mag = jnp.exp(ldt)
    ab_re, ab_im = mag * jnp.cos(idt), mag * jnp.sin(idt)
    den = lr * lr + li * li
    nr, ni = ab_re - 1.0, ab_im
    q_re = (nr * lr + ni * li) / den
    q_im = (ni * lr - nr * li) / den
    kf = lax.broadcasted_iota(jnp.int32, (2 * t, p), 0).astype(_F32)

    ap_re, ap_im, bb_re, bb_im = [], [], [], []
    for d in range(2):
        pm = jnp.exp(kf * ldt[d:d + 1, :])
        ap_re.append(pm * jnp.cos(kf * idt[d:d + 1, :]))
        ap_im.append(pm * jnp.sin(kf * idt[d:d + 1, :]))
        bb_re.append(q_re[d:d + 1, :] * btr_ref[d] - q_im[d:d + 1, :] * bti_ref[d])
        bb_im.append(q_re[d:d + 1, :] * bti_ref[d] + q_im[d:d + 1, :] * btr_ref[d])

    def rep(a, ks):
        return jnp.concatenate([jnp.broadcast_to(a[k:k + 1, :], (hh, p)) for k in ks], axis=0)

    def tile(a, n):
        return jnp.concatenate([a] * n, axis=0)

    def cmul(xr, xi, yr, yi):
        return xr * yr - xi * yi, xr * yi + xi * yr

    ks_f = list(range(t + 1))
    ks_r = list(range(t, -1, -1))
    caf_re, caf_im = cmul(tile(cr_ref[0], t + 1), tile(ci_ref[0], t + 1), rep(ap_re[0], ks_f), rep(ap_im[0], ks_f))
    car_re, car_im = cmul(tile(cr_ref[1], t + 1), tile(ci_ref[1], t + 1), rep(ap_re[1], ks_r), rep(ap_im[1], ks_r))

    def lag_blocks(bre, bim, ca_re, ca_im):
        lhs = jnp.concatenate([bre, bim], axis=1)
        rhs = jnp.concatenate([ca_re, -ca_im], axis=1)
        return lax.dot_general(lhs, rhs, (((1,), (1,)), ((), ())), precision=_HI,
                               preferred_element_type=_F32)

    w_f = lag_blocks(bb_re[0], bb_im[0], caf_re[:kk], caf_im[:kk])
    w_r = lag_blocks(bb_re[1], bb_im[1], car_re[hh:], car_im[hh:])
    lane = lax.broadcasted_iota(jnp.int32, (hh, kk), 1)
    for tlp in range(t):
        sf = hh * tlp
        sr = hh * (t - 1 - tlp)
        a = w_f if sf == 0 else jnp.where(lane >= sf, pltpu.roll(w_f, sf, 1), 0.0)
        b = w_r if sr == 0 else jnp.where(lane < kk - sr, pltpu.roll(w_r, kk - sr, 1), 0.0)
        wy_ref[hh * tlp:hh * (tlp + 1), :] = (a + b).astype(wy_ref.dtype)

    zeros = jnp.zeros((p, kk), _F32)
    blocks = [caf_re[hh:].T, zeros, zeros, car_re[:kk].T, (-caf_im[hh:]).T, zeros, zeros, (-car_im[:kk]).T]
    for i, blk in enumerate(blocks):
        wy_ref[kk + p * i:kk + p * (i + 1), :] = blk.astype(wy_ref.dtype)

    pf = list(range(t - 1, -1, -1))
    pr = list(range(t))
    f_re, f_im = cmul(rep(ap_re[0], pf), rep(ap_im[0], pf), tile(bb_re[0], t), tile(bb_im[0], t))
    r_re, r_im = cmul(rep(ap_re[1], pr), rep(ap_im[1], pr), tile(bb_re[1], t), tile(bb_im[1], t))
    ws_ref[...] = jnp.concatenate([f_re, r_re, f_im, r_im], axis=1).astype(ws_ref.dtype)

    a_re = jnp.concatenate([ap_re[0][t:t + 1, :], ap_re[1][t:t + 1, :]], axis=1)
    a_im = jnp.concatenate([ap_im[0][t:t + 1, :], ap_im[1][t:t + 1, :]], axis=1)
    row = lax.broadcasted_iota(jnp.int32, at_ref.shape, 0)
    at_ref[...] = jnp.where(row == 0, a_re, jnp.where(row == 1, a_im, 0.0))


def _s5_params(lam_re, lam_im, b_re, b_im, c_re, c_im, log_step):
    _, ng, p = lam_re.shape
    hh = b_re.shape[-1]
    kk = S5_CHUNK * hh
    gd = lambda a: jnp.swapaxes(a.astype(_F32), 0, 1)
    args = (gd(lam_re), gd(lam_im), gd(log_step)[..., None],
            jnp.swapaxes(gd(b_re), 2, 3), jnp.swapaxes(gd(b_im), 2, 3), gd(c_re), gd(c_im))
    spec = lambda a: pl.BlockSpec((None,) + a.shape[1:], lambda g: (g,) + (0,) * (a.ndim - 1))
    return pl.pallas_call(
        _s5_param_body,
        grid=(ng,),
        in_specs=[spec(a) for a in args],
        out_specs=[pl.BlockSpec((None, kk, 4 * p), lambda g: (g, 0, 0)),
                   pl.BlockSpec((None, kk + 8 * p, kk), lambda g: (g, 0, 0)),
                   pl.BlockSpec((None, 8, 2 * p), lambda g: (g, 0, 0))],
        out_shape=[jax.ShapeDtypeStruct((ng, kk, 4 * p), _BF),
                   jax.ShapeDtypeStruct((ng, kk + 8 * p, kk), _BF),
                   jax.ShapeDtypeStruct((ng, 8, 2 * p), _F32)],
        compiler_params=_cparams(("parallel",)),
        name="s5_params",
    )(*args)


def _post_body(x_ref, u_ref, yg_ref, fn_ref, d_ref, wglu_ref, gs_ref, wo_ref, g1_ref,
               n2_ref, sh_ref, sc_ref, wr_ref, br_ref, tri_ref, x1_ref, t_ref, rt_ref, rtt_ref, cnt_ref,
               y_scr, cnt_scr):
    ds5 = u_ref.shape[-1]
    first = (pl.program_id(0) == 0) & (pl.program_id(1) == 0)

    @pl.when(first)
    def _():
        cnt_scr[...] = jnp.zeros_like(cnt_scr)

    t_chunk = S5_CHUNK
    n_chunk = yg_ref.shape[1]
    lanes = 128
    gpb = lanes // S5_GROUP
    for j in range(ds5 // lanes):
        for hi in range(t_chunk // gpb):
            outs = _seg_transpose([yg_ref[gpb * j + glo, :, lanes * hi:lanes * (hi + 1)].astype(_F32)
                                   for glo in range(gpb)])
            for tlo in range(gpb):
                y_scr[j, pl.ds(gpb * hi + tlo, n_chunk, stride=t_chunk), :] = outs[tlo]
    y_s5 = jnp.concatenate([y_scr[j] for j in range(ds5 // lanes)], axis=-1)

    y = y_s5 + d_ref[...] * u_ref[...]
    y = jax.nn.gelu(y, approximate=True)
    y = y * jax.nn.sigmoid(_dot(y.astype(_BF), wglu_ref[...]))
    yn = _rms(y, gs_ref[...]).astype(_BF)
    mix = _dot(yn, wo_ref[0:ds5, :]) + _dot(fn_ref[...].astype(_BF), wo_ref[ds5:, :])
    x1 = x_ref[...] + g1_ref[...] * mix
    x1_ref[...] = x1
    t = _rms(x1, n2_ref[...]) * (1.0 + sc_ref[...]) + sh_ref[...]
    t_hi = t.astype(_BF)
    t_ref[...] = t_hi

    ng, epg = N_EXPERT_GROUPS, EXPERTS_PER_GROUP
    t_lo = (t - t_hi.astype(_F32)).astype(_BF)
    logits = (_dot(t_hi, wr_ref[0]) + _dot(t_lo, wr_ref[0]) + _dot(t_hi, wr_ref[1])) + br_ref[...]
    tm = logits.shape[0]
    lt = logits.T
    neg = jnp.float32(-jnp.inf)
    big = jnp.int32(1 << 20)
    row8 = lax.broadcasted_iota(jnp.int32, (epg, tm), 0)
    amax = lambda v: jnp.max(v, axis=0, keepdims=True)
    first = lambda v, m: jnp.min(jnp.where(v == m, row8, big), axis=0, keepdims=True)
    gt = lt[ng * epg:ng * epg + epg, :]
    gl = jnp.where(row8 < ng, gt, neg)
    gmax = amax(gl)
    gidx = first(gl, gmax)
    gw = 1.0 / jnp.sum(jnp.where(row8 < ng, jnp.exp(gt - gmax), 0.0), axis=0, keepdims=True)
    el = lt[0:epg, :]
    for g in range(1, ng):
        el = jnp.where(gidx == g, lt[g * epg:(g + 1) * epg, :], el)
    v0 = amax(el)
    i0 = first(el, v0)
    el1 = jnp.where(row8 == i0, neg, el)
    v1 = amax(el1)
    i1 = first(el1, v1)
    p0 = 1.0 / (1.0 + jnp.exp(v1 - v0))
    w0 = gw * p0
    w1 = gw * (1.0 - p0)
    e0 = gidx * epg + i0
    e1 = gidx * epg + i1

    rowe = lax.broadcasted_iota(jnp.int32, (ng * epg, tm), 0)
    oh0 = rowe == e0
    oh1 = rowe == e1
    oh = (oh0 | oh1).astype(_F32)
    prefix = _dot(oh.astype(_BF), tri_ref[...]) + cnt_scr[...]
    r0 = jnp.sum(jnp.where(oh0, prefix, 0.0), axis=0, keepdims=True)
    r1 = jnp.sum(jnp.where(oh1, prefix, 0.0), axis=0, keepdims=True)
    cnt = cnt_scr[...] + jnp.sum(oh, axis=1, keepdims=True)
    cnt_scr[...] = cnt
    cnt_ref[...] = cnt

    rec = jnp.concatenate([e0.astype(_F32), e1.astype(_F32), w0, w1, r0, r1,
                           jnp.zeros((ROUTE_ROWS - 6, tm), _F32)], axis=0)
    rtt_ref[...] = rec
    rt_ref[...] = jnp.concatenate([rec, jnp.zeros((ROUTE_LANES - ROUTE_ROWS, tm), _F32)], axis=0).T


@functools.lru_cache(maxsize=None)
def _strict_upper_ones(n):
    return np.triu(np.ones((n, n), np.float32), 1).astype(jnp.bfloat16)


def _post(x, u, y_g, fn, s5_d, wglu_bf, gs, wo_bf, g1, n2, sh2, sc2, wr, br, b0, nb, tm=512):
    b, l, d = x.shape
    ds5 = u.shape[-1]
    n_grp, _, w = y_g.shape
    kk = w // b
    n_exp = N_EXPERT_GROUPS * EXPERTS_PER_GROUP
    tri = jnp.asarray(_strict_upper_ones(tm))
    tok = lambda w: pl.BlockSpec((None, tm, w), lambda i, j: (i + b0, j, 0))
    otok = lambda w: pl.BlockSpec((None, tm, w), lambda i, j: (i, j, 0))
    per_b = pl.BlockSpec((None, 1, d), lambda i, j: (i + b0, 0, 0))
    full = lambda a: pl.BlockSpec(a.shape, lambda i, j: (0,) * a.ndim)
    ygs = pl.BlockSpec((n_grp, tm // S5_CHUNK, kk), lambda i, j: (0, j, i + b0))
    return pl.pallas_call(
        _post_body,
        grid=(nb, l // tm),
        in_specs=[tok(d), tok(ds5), ygs, tok(fn.shape[-1]), full(s5_d), full(wglu_bf),
                  full(gs), full(wo_bf), per_b, full(n2), per_b, per_b, full(wr), full(br), full(tri)],
        out_specs=[otok(d), otok(d), otok(ROUTE_LANES),
                   pl.BlockSpec((None, ROUTE_ROWS, tm), lambda i, j: (i, 0, j)),
                   pl.BlockSpec((n_exp, 1), lambda i, j: (0, 0))],
        out_shape=[jax.ShapeDtypeStruct((nb, l, d), _F32),
                   jax.ShapeDtypeStruct((nb, l, d), _BF),
                   jax.ShapeDtypeStruct((nb, l, ROUTE_LANES), _F32),
                   jax.ShapeDtypeStruct((nb, ROUTE_ROWS, l), _F32),
                   jax.ShapeDtypeStruct((n_exp, 1), _F32)],
        scratch_shapes=[pltpu.VMEM((ds5 // 128, tm, 128), _F32),
                        pltpu.VMEM((n_exp, 1), _F32)],
        compiler_params=_cparams(("arbitrary", "arbitrary")),
        name="post_mixer",
    )(x, u, y_g, fn, s5_d, wglu_bf, gs, wo_bf, g1, n2, sh2, sc2, wr, br, tri)


def _moe_body(te_ref, nt_ref, xs_ref, wg_ref, wu_ref, wd_ref, o_ref, wgu_bf, wd_bf):
    i = pl.program_id(0)
    de = wg_ref.shape[-1]
    used = i < nt_ref[0]
    new_expert = (i == 0) | (te_ref[i] != te_ref[jnp.maximum(i - 1, 0)])

    @pl.when(used & new_expert)
    def _():
        wgu_bf[:, 0:de] = wg_ref[...].astype(_BF)
        wgu_bf[:, de:2 * de] = wu_ref[...].astype(_BF)
        wd_bf[...] = wd_ref[...].astype(_BF)

    @pl.when(used)
    def _():
        h = _dot(xs_ref[...], wgu_bf[...])
        hg, hu = h[:, 0:de], h[:, de:2 * de]
        a = hg * jax.nn.sigmoid(hg) * hu
        o_ref[...] = _dot(a.astype(_BF), wd_bf[...]).astype(o_ref.dtype)

    @pl.when(jnp.logical_not(used))
    def _():
        o_ref[...] = jnp.zeros_like(o_ref)


def _moe(tile_expert, n_tiles_used, xs, wg, wu, wd):
    nr, d = xs.shape
    tm = MOE_TM
    de = wg.shape[-1]
    grid_spec = pltpu.PrefetchScalarGridSpec(
        num_scalar_prefetch=2,
        grid=(nr // tm,),
        in_specs=[pl.BlockSpec((tm, d), lambda i, te, nt: (i, 0)),
                  pl.BlockSpec((None, d, de), lambda i, te, nt: (te[i], 0, 0)),
                  pl.BlockSpec((None, d, de), lambda i, te, nt: (te[i], 0, 0)),
                  pl.BlockSpec((None, de, d), lambda i, te, nt: (te[i], 0, 0))],
        out_specs=pl.BlockSpec((tm, d), lambda i, te, nt: (i, 0)),
        scratch_shapes=[pltpu.VMEM((d, 2 * de), _BF), pltpu.VMEM((de, d), _BF)],
    )
    return pl.pallas_call(
        _moe_body,
        grid_spec=grid_spec,
        out_shape=jax.ShapeDtypeStruct((nr, d), _BF),
        compiler_params=_cparams(("arbitrary",)),
        name="moe_experts",
    )(tile_expert, n_tiles_used, xs, wg, wu, wd)


def _final_body(g2_ref, gf_ref, *refs, n_part, nb):
    o_ref = refs[-1]
    i = pl.program_id(0)
    for part in range(n_part):
        x1_ref, y0_ref, y1_ref, rt_ref = refs[4 * part:4 * part + 4]

        @pl.when((i >= part * nb) & (i < (part + 1) * nb))
        def _():
            w0 = rt_ref[:, 2:3]
            w1 = rt_ref[:, 3:4]
            m = w0 * y0_ref[...].astype(_F32) + w1 * y1_ref[...].astype(_F32)
            o_ref[...] = _rms(x1_ref[...] + g2_ref[...] * m, gf_ref[...])


def _final(parts, g2, gf, tm=512):
    n_part = len(parts)
    nb, l, d = parts[0][0].shape
    nj = l // tm
    in_specs = [pl.BlockSpec((None, 1, d), lambda i, j: (i, 0, 0)),
                pl.BlockSpec((1, d), lambda i, j: (0, 0))]
    args = [g2, gf]
    for part, arrs in enumerate(parts):
        def imap(i, j, part=part):
            own = (i >= part * nb) & (i < (part + 1) * nb)
            return (jnp.clip(i - part * nb, 0, nb - 1), jnp.where(own, j, jnp.where(i < part * nb, 0, nj - 1)), 0)
        for a in arrs:
            in_specs.append(pl.BlockSpec((None, tm, a.shape[-1]), imap))
            args.append(a)
    return pl.pallas_call(
        functools.partial(_final_body, n_part=n_part, nb=nb),
        grid=(n_part * nb, nj),
        in_specs=in_specs,
        out_specs=pl.BlockSpec((None, tm, d), lambda i, j: (i, j, 0)),
        out_shape=jax.ShapeDtypeStruct((n_part * nb, l, d), _F32),
        compiler_params=_cparams(("parallel", "arbitrary")),
        name="final_norm",
    )(*args)


def _route_plan(eid, rank, tok_ids, counts, tm, nr):
    n_experts = counts.shape[0]
    n_pairs = eid.size
    padded = ((counts + tm - 1) // tm) * tm
    pad_end = jnp.cumsum(padded)
    pad_start = pad_end - padded
    raw_start = jnp.cumsum(counts) - counts
    pos = rank
    for e in range(n_experts):
        pos = pos + jnp.where(eid == e, pad_start[e], 0)
    tile_start = jnp.arange(nr // tm, dtype=jnp.int32) * tm
    tile_expert = jnp.sum(tile_start[:, None] >= pad_end[None, :], axis=-1)
    tile_expert = jnp.minimum(tile_expert, n_experts - 1).astype(jnp.int32)
    _, sorted_tok = lax.sort_key_val((eid * n_pairs + rank).reshape(-1), tok_ids.reshape(-1))
    sel = tile_expert[:, None] == jnp.arange(n_experts, dtype=jnp.int32)[None, :]
    per_tile = lambda v: jnp.repeat(jnp.sum(jnp.where(sel, v[None, :], 0), axis=-1), tm)
    off = jnp.arange(nr, dtype=jnp.int32) - per_tile(pad_start)
    valid = off < per_tile(counts)
    j = jnp.clip(per_tile(raw_start) + off, 0, n_pairs - 1)
    filler = jnp.arange(nr, dtype=jnp.int32) % (n_pairs // 2)
    row_token = jnp.where(valid, sorted_tok.at[j].get(mode="promise_in_bounds"), filler)
    n_used = (pad_end[-1:] // tm).astype(jnp.int32)
    return pos, tile_expert, n_used, row_token


def kernel(x, c, ctx, c_ctx, w_ada, b_ada, norm1_g, norm2_g, w_in, s5_lam_re, s5_lam_im, s5_b_re, s5_b_im, s5_c_re, s5_c_im, s5_log_step, s5_d, s5_w_glu, fourier_w, mix_norm_s5_g, mix_norm_f_g, w_out, moe_w_group, moe_b_group, moe_w_router, moe_b_router, moe_w_gate, moe_w_up, moe_w_down, final_norm_g):
    b, l, d = x.shape
    lc = ctx.shape[1]
    depth = w_ada.shape[0]
    assert depth == 1 and l == FFT_R * FFT_R and lc % S5_CHUNK == 0 and b % 8 == 0
    ds5 = s5_d.shape[-1]
    df = w_in.shape[-1] - ds5
    n_exp = moe_w_gate.shape[1]
    row = lambda a: a.reshape(1, -1)

    cond = jnp.concatenate([c, c_ctx[None, :], jnp.zeros((7, d), _F32)], axis=0)
    mod = _adaln(cond, w_ada[0], b_ada[0])
    sh1, sc1, g1, sh2, sc2, g2 = [mod[:b, i * d:(i + 1) * d].reshape(b, 1, d) for i in range(6)]
    csh1 = mod[b:b + 1, 0:d]
    csc1 = mod[b:b + 1, d:2 * d]

    w_in_bf = w_in[0].astype(_BF)
    m1 = jnp.asarray(_fft_stage1_matrix()).astype(_BF)
    m2 = jnp.asarray(_fft_stage2_matrices()).astype(_BF)
    perm = jnp.asarray(_chunk_row_perm())
    zs, gc = _inproj(x, sh1, sc1, row(norm1_g[0]), w_in_bf, m1)
    u_ctx = _inproj_ctx(ctx, csh1, csc1, row(norm1_g[0]), w_in_bf[:, :ds5], perm)
    u_lat = _s5_pack(zs, perm)

    fg = df // FOURIER_GROUPS
    cc = np.arange(fg)
    ang = 2.0 * np.pi * ((cc[:, None] * cc[None, :]) % fg) / fg
    scale = 1.0 / math.sqrt(l * fg)
    cw = jnp.einsum("cm,gmd->gcd", jnp.asarray(np.cos(ang) * scale, _F32), fourier_w[0], precision=_HI)
    sw = jnp.einsum("cm,gmd->gcd", jnp.asarray(np.sin(ang) * scale, _F32), fourier_w[0], precision=_HI)
    fn = _fft2(gc, m2, cw.astype(_BF), sw.astype(_BF), row(mix_norm_f_g[0]))

    ws, wy, at = _s5_params(s5_lam_re[0], s5_lam_im[0], s5_b_re[0], s5_b_im[0],
                            s5_c_re[0], s5_c_im[0], s5_log_step[0])
    y_g = _s5(u_ctx, u_lat, ws, wy, at, b)

    n_rt = N_EXPERT_GROUPS * (1 + EXPERTS_PER_GROUP)
    w_rt = jnp.concatenate([moe_w_router[0].reshape(d, -1), moe_w_group[0]], axis=-1)
    w_rt = jnp.pad(w_rt, ((0, 0), (0, ROUTE_LANES - n_rt)))
    w_rt_hi = w_rt.astype(_BF)
    w_rt = jnp.stack([w_rt_hi, (w_rt - w_rt_hi.astype(_F32)).astype(_BF)])
    b_rt = jnp.pad(jnp.concatenate([moe_b_router[0].reshape(-1), moe_b_group[0]]), (0, ROUTE_LANES - n_rt))
    tm = MOE_TM
    nb = b // MOE_PARTS
    n_tok = nb * l
    nr = 2 * n_tok + n_exp * tm
    take_rows = lambda a, idx: a.at[idx].get(mode="promise_in_bounds")
    wglu_bf, wo_bf = s5_w_glu[0].astype(_BF), w_out[0].astype(_BF)
    parts = []
    for part in range(MOE_PARTS):
        x1, tmod, rt, rtt, cnt = _post(x, zs, y_g, fn, row(s5_d[0]), wglu_bf, row(mix_norm_s5_g[0]), wo_bf,
                                       g1, row(norm2_g[0]), sh2, sc2, w_rt, row(b_rt), part * nb, nb)
        rec = rtt.transpose(1, 0, 2).reshape(ROUTE_ROWS, n_tok)
        eid = rec[0:2].astype(jnp.int32)
        rank = rec[4:6].astype(jnp.int32)
        tok_ids = jnp.broadcast_to(jnp.arange(n_tok, dtype=jnp.int32), (2, n_tok))
        pos, tile_expert, n_used, row_token = _route_plan(eid, rank, tok_ids,
                                                          cnt[:, 0].astype(jnp.int32), tm, nr)
        xs = take_rows(tmod.reshape(n_tok, d), row_token)
        ys = _moe(tile_expert, n_used, xs, moe_w_gate[0], moe_w_up[0], moe_w_down[0])
        y0 = take_rows(ys, pos[0]).reshape(nb, l, d)
        y1 = take_rows(ys, pos[1]).reshape(nb, l, d)
        parts.append((x1, y0, y1, rt))
    return _final(parts, g2, row(final_norm_g))
```

```python
import functools
import math

import numpy as np
import jax
import jax.numpy as jnp
from jax import lax
from jax.experimental import pallas as pl
from jax.experimental.pallas import tpu as pltpu

EPS = 1e-6
S5_GROUP = 16
S5_STATE = 64
S5_CHUNK = 16
S5_SCAN_GROUPS = 4
FOURIER_GROUPS = 4
N_EXPERT_GROUPS = 4
EXPERTS_PER_GROUP = 8
FFT_R = 64
FFT_BLK = 8
MOE_TM = 512
MOE_PARTS = 1
ROUTE_LANES = 128
ROUTE_ROWS = 8
VMEM_LIMIT = 56 * 1024 * 1024

_HI = lax.Precision.HIGHEST
_BF = jnp.bfloat16
_F32 = jnp.float32


def _cparams(sem):
    return pltpu.CompilerParams(dimension_semantics=sem, vmem_limit_bytes=VMEM_LIMIT)


def _dot(a, b):
    return jnp.dot(a, b, preferred_element_type=_F32)


def _rms(x, g):
    return x * lax.rsqrt(jnp.mean(x * x, axis=-1, keepdims=True) + EPS) * g


def _adaln_body(c_ref, w_ref, b_ref, o_ref):
    c = c_ref[...]
    a = c * jax.nn.sigmoid(c)
    o_ref[...] = jnp.dot(a, w_ref[...], preferred_element_type=_F32, precision=_HI) + b_ref[...]


def _adaln(cond, w, b):
    m, d = cond.shape
    n = w.shape[1]
    tn = 768
    return pl.pallas_call(
        _adaln_body,
        grid=(n // tn,),
        in_specs=[pl.BlockSpec((m, d), lambda j: (0, 0)),
                  pl.BlockSpec((d, tn), lambda j: (0, j)),
                  pl.BlockSpec((1, tn), lambda j: (0, j))],
        out_specs=pl.BlockSpec((m, tn), lambda j: (0, j)),
        out_shape=jax.ShapeDtypeStruct((m, n), _F32),
        compiler_params=_cparams(("arbitrary",)),
        name="adaln",
    )(cond, w, b.reshape(1, n))


def _inproj_body(x_ref, sh_ref, sc_ref, g_ref, w_ref, m1_ref, zs_ref, gc_ref):
    r, blk, d = x_ref.shape
    x = x_ref[...].reshape(r * blk, d)
    h = _rms(x, g_ref[...]) * (1.0 + sc_ref[...]) + sh_ref[...]
    z = _dot(h.astype(_BF), w_ref[...])
    ds5 = zs_ref.shape[-1]
    zs_ref[...] = z[:, :ds5].reshape(r, blk, ds5)
    v = z[:, ds5:].astype(_BF)
    g1 = _dot(m1_ref[...], v)
    gc_ref[...] = g1.astype(_BF).reshape(r, 2 * blk, v.shape[-1])


def _inproj(x, sh, sc, g, w_bf, m1):
    b, l, d = x.shape
    r, blk = FFT_R, FFT_BLK
    nj = r // blk
    dmix = w_bf.shape[1]
    ds5 = dmix // 2
    df = dmix - ds5
    x4 = x.reshape(b, r, r, d)
    zs, gc = pl.pallas_call(
        _inproj_body,
        grid=(b, nj),
        in_specs=[pl.BlockSpec((None, r, blk, d), lambda i, j: (i, 0, j, 0)),
                  pl.BlockSpec((None, 1, d), lambda i, j: (i, 0, 0)),
                  pl.BlockSpec((None, 1, d), lambda i, j: (i, 0, 0)),
                  pl.BlockSpec((1, d), lambda i, j: (0, 0)),
                  pl.BlockSpec((d, dmix), lambda i, j: (0, 0)),
                  pl.BlockSpec(m1.shape, lambda i, j: (0, 0))],
        out_specs=[pl.BlockSpec((None, r, blk, ds5), lambda i, j: (i, 0, j, 0)),
                   pl.BlockSpec((None, r, 2 * blk, df), lambda i, j: (i, 0, j, 0))],
        out_shape=[jax.ShapeDtypeStruct((b, r, r, ds5), _F32),
                   jax.ShapeDtypeStruct((b, r, 2 * r, df), _BF)],
        compiler_params=_cparams(("parallel", "arbitrary")),
        name="inproj",
    )(x4, sh, sc, g, w_bf, m1)
    return zs.reshape(b, l, ds5), gc


def _seg_transpose(arrs):
    n = len(arrs)
    seg = lax.broadcasted_iota(jnp.int32, arrs[0].shape, 1) // S5_GROUP
    d = n // 2
    while d >= 1:
        keep = (seg & d) == 0
        new = list(arrs)
        for i in range(n):
            if i & d == 0:
                a, b = arrs[i], arrs[i + d]
                new[i] = jnp.where(keep, a, pltpu.roll(b, S5_GROUP * d, 1))
                new[i + d] = jnp.where(keep, pltpu.roll(a, 128 - S5_GROUP * d, 1), b)
        arrs = new
        d //= 2
    return arrs


@functools.lru_cache(maxsize=None)
def _chunk_row_perm():
    t = S5_CHUNK
    p = np.zeros((t * t, t * t), np.float32)
    for c in range(t):
        for tl in range(t):
            p[tl * t + c, c * t + tl] = 1.0
    return p.astype(jnp.bfloat16)


def _pack_chunks(z_bf, p_ref, u_ref):
    t = S5_CHUNK
    nsub = z_bf.shape[0] // (t * t)
    pieces = [_dot(p_ref[...], z_bf[s * t * t:(s + 1) * t * t, :]) for s in range(nsub)]
    a = []
    for tl in range(t):
        rows = [pc[tl * t:(tl + 1) * t, :] for pc in pieces]
        a.append(rows[0] if nsub == 1 else jnp.concatenate(rows, axis=0))
    lanes = 128
    gpb = lanes // S5_GROUP
    for j in range(z_bf.shape[1] // lanes):
        for hi in range(t // gpb):
            outs = _seg_transpose([a[gpb * hi + tlo][:, lanes * j:lanes * (j + 1)] for tlo in range(gpb)])
            for glo in range(gpb):
                u_ref[gpb * j + glo, :, lanes * hi:lanes * (hi + 1)] = outs[glo].astype(u_ref.dtype)


def _inproj_ctx_body(x_ref, sh_ref, sc_ref, g_ref, w_ref, p_ref, u_ref):
    h = _rms(x_ref[...], g_ref[...]) * (1.0 + sc_ref[...]) + sh_ref[...]
    z = _dot(h.astype(_BF), w_ref[...])
    _pack_chunks(z.astype(_BF), p_ref, u_ref)


def _inproj_ctx(ctx, sh, sc, g, w_s5_bf, perm):
    b, lc, d = ctx.shape
    ds5 = w_s5_bf.shape[1]
    n_grp = ds5 // S5_GROUP
    kk = S5_CHUNK * S5_GROUP
    return pl.pallas_call(
        _inproj_ctx_body,
        grid=(b,),
        in_specs=[pl.BlockSpec((None, lc, d), lambda i: (i, 0, 0)),
                  pl.BlockSpec((1, d), lambda i: (0, 0)),
                  pl.BlockSpec((1, d), lambda i: (0, 0)),
                  pl.BlockSpec((1, d), lambda i: (0, 0)),
                  pl.BlockSpec((d, ds5), lambda i: (0, 0)),
                  pl.BlockSpec(perm.shape, lambda i: (0, 0))],
        out_specs=pl.BlockSpec((n_grp, lc // S5_CHUNK, kk), lambda i: (0, 0, i)),
        out_shape=jax.ShapeDtypeStruct((n_grp, lc // S5_CHUNK, b * kk), _BF),
        compiler_params=_cparams(("arbitrary",)),
        name="inproj_ctx",
    )(ctx, sh, sc, g, w_s5_bf, perm)


def _s5_pack_body(z_ref, p_ref, u_ref):
    _pack_chunks(z_ref[...].astype(_BF), p_ref, u_ref)


def _s5_pack(zs, perm, tok=2048):
    b, l, ds5 = zs.shape
    n_grp = ds5 // S5_GROUP
    kk = S5_CHUNK * S5_GROUP
    return pl.pallas_call(
        _s5_pack_body,
        grid=(b, l // tok),
        in_specs=[pl.BlockSpec((None, tok, ds5), lambda i, j: (i, j, 0)),
                  pl.BlockSpec(perm.shape, lambda i, j: (0, 0))],
        out_specs=pl.BlockSpec((n_grp, tok // S5_CHUNK, kk), lambda i, j: (0, j, i)),
        out_shape=jax.ShapeDtypeStruct((n_grp, l // S5_CHUNK, b * kk), _BF),
        compiler_params=_cparams(("parallel", "arbitrary")),
        name="s5_pack",
    )(zs, perm)


@functools.lru_cache(maxsize=None)
def _fft_stage1_matrix():
    r, blk = FFT_R, FFT_BLK
    k1 = np.arange(r)[:, None]
    l1 = np.arange(r)[None, :]
    ang = 2.0 * np.pi * ((k1 * l1) % r) / r
    f = np.stack([np.cos(ang), -np.sin(ang)], axis=1)
    m = np.einsum("kal,pq->kaplq", f, np.eye(blk))
    return m.reshape(r * 2 * blk, r * blk).astype(np.float32)


@functools.lru_cache(maxsize=None)
def _fft_stage2_matrices():
    r, blk = FFT_R, FFT_BLK
    n = r * r
    nt = r // blk
    l2 = np.arange(r)
    k2 = np.arange(r)
    out = np.zeros((nt, 2, r, blk, blk, r // blk, 2, blk), np.float32)
    for i in range(nt):
        for kl in range(blk):
            k1 = blk * i + kl
            ang = 2.0 * np.pi * (((k2[:, None] * l2[None, :] * r) + l2[None, :] * k1) % n) / n
            tr, ti = np.cos(ang), -np.sin(ang)
            tr = tr.reshape(r, r // blk, blk)
            ti = ti.reshape(r, r // blk, blk)
            out[i, 0, :, kl, kl, :, 0, :] = tr
            out[i, 0, :, kl, kl, :, 1, :] = -ti
            out[i, 1, :, kl, kl, :, 0, :] = ti
            out[i, 1, :, kl, kl, :, 1, :] = tr
    return out.reshape(nt, 2 * r * blk, blk * 2 * r)


def _fft2_body(gc_ref, m2_ref, cw_ref, sw_ref, g_ref, o_ref):
    kb, rr, df = gc_ref.shape
    gc = gc_ref[...].reshape(kb * rr, df)
    x = _dot(m2_ref[...], gc)
    half = x.shape[0] // 2
    xr = x[:half].astype(_BF)
    xi = x[half:].astype(_BF)
    ng = cw_ref.shape[0]
    fg = df // ng
    parts = []
    for g in range(ng):
        sl = slice(g * fg, (g + 1) * fg)
        parts.append(_dot(xr[:, sl], cw_ref[g]) + _dot(xi[:, sl], sw_ref[g]))
    f = jnp.concatenate(parts, axis=-1)
    fn = _rms(f, g_ref[...])
    o_ref[...] = fn.reshape(o_ref.shape)


def _fft2(gc, m2, cw, sw, gf):
    b, r, rr, df = gc.shape
    blk = FFT_BLK
    nt = r // blk
    out = pl.pallas_call(
        _fft2_body,
        grid=(nt, b),
        in_specs=[pl.BlockSpec((None, blk, rr, df), lambda j, i: (i, j, 0, 0)),
                  pl.BlockSpec((None,) + m2.shape[1:], lambda j, i: (j, 0, 0)),
                  pl.BlockSpec(cw.shape, lambda j, i: (0, 0, 0)),
                  pl.BlockSpec(sw.shape, lambda j, i: (0, 0, 0)),
                  pl.BlockSpec((1, df), lambda j, i: (0, 0))],
        out_specs=pl.BlockSpec((None, r, blk, df), lambda j, i: (i, 0, j, 0)),
        out_shape=jax.ShapeDtypeStruct((b, r, r, df), _F32),
        compiler_params=_cparams(("parallel", "arbitrary")),
        name="fft2",
    )(gc, m2, cw, sw, gf)
    return out.reshape(b, r * r, df)


def _s5_body(uc_ref, ul_ref, ws_ref, wy_ref, at_ref, y_ref,
             s_re, s_im, ha_re, hb_re, ha_im, hb_im, *, nb):
    p = S5_STATE
    gs, n_ctx, _ = uc_ref.shape
    n_lat = ul_ref.shape[1]
    kk = ws_ref.shape[1]
    for gi in range(gs):
        ws = ws_ref[gi]
        for b in range(nb):
            sl = slice(b * kk, (b + 1) * kk)
            sc = _dot(uc_ref[gi, :, sl], ws)
            sl_ = _dot(ul_ref[gi, :, sl], ws)
            s_re[gi, pl.ds(b, n_ctx, stride=nb), :] = sc[:, :2 * p]
            s_im[gi, pl.ds(b, n_ctx, stride=nb), :] = sc[:, 2 * p:]
            s_re[gi, pl.ds(n_ctx * nb + b, n_lat, stride=nb), :] = sl_[:, :2 * p]
            s_im[gi, pl.ds(n_ctx * nb + b, n_lat, stride=nb), :] = sl_[:, 2 * p:]

    a_re = [at_ref[gi, 0:1, :] for gi in range(gs)]
    a_im = [at_ref[gi, 1:2, :] for gi in range(gs)]
    is_f = lax.broadcasted_iota(jnp.int32, (nb, 2 * p), 1) < p
    n_all = n_ctx + n_lat

    def load(gi, cf, cr):
        rf = pl.multiple_of(cf * nb, nb)
        rr = pl.multiple_of(cr * nb, nb)
        return (jnp.where(is_f, s_re[gi, pl.ds(rf, nb), :], s_re[gi, pl.ds(rr, nb), :]),
                jnp.where(is_f, s_im[gi, pl.ds(rf, nb), :], s_im[gi, pl.ds(rr, nb), :]))

    def advance(gi, h_re, h_im, x_re, x_im):
        return (a_re[gi] * h_re - a_im[gi] * h_im + x_re, a_re[gi] * h_im + a_im[gi] * h_re + x_im)

    def ctx_step(step, carry):
        out = []
        for gi in range(gs):
            x_re, x_im = load(gi, step, n_ctx - 1 - step)
            out.append(advance(gi, *carry[gi], x_re, x_im))
        return tuple(out)

    def lat_step(step, carry):
        rf = pl.multiple_of(step * nb, nb)
        rr = pl.multiple_of((n_lat - 1 - step) * nb, nb)
        out = []
        for gi in range(gs):
            h_re, h_im = carry[gi]
            ha_re[gi, pl.ds(rf, nb), :] = h_re
            hb_re[gi, pl.ds(rr, nb), :] = h_re
            ha_im[gi, pl.ds(rf, nb), :] = h_im
            hb_im[gi, pl.ds(rr, nb), :] = h_im
            x_re, x_im = load(gi, n_ctx + step, n_all - 1 - step)
            out.append(advance(gi, h_re, h_im, x_re, x_im))
        return tuple(out)

    zero = jnp.zeros((nb, 2 * p), _F32)
    carry = lax.fori_loop(0, n_ctx, ctx_step, tuple((zero, zero) for _ in range(gs)), unroll=2)
    lax.fori_loop(0, n_lat, lat_step, carry, unroll=2)

    for gi in range(gs):
        wy = wy_ref[gi]
        for b in range(nb):
            rows = pl.ds(b, n_lat, stride=nb)
            hin = jnp.concatenate([ha_re[gi, rows, :], hb_re[gi, rows, :],
                                   ha_im[gi, rows, :], hb_im[gi, rows, :]], axis=-1)
            lhs = jnp.concatenate([ul_ref[gi, :, b * kk:(b + 1) * kk], hin.astype(_BF)], axis=-1)
            y_ref[gi, :, b * kk:(b + 1) * kk] = _dot(lhs, wy).astype(y_ref.dtype)


def _s5(u_ctx, u_lat, ws, wy, at, nb):
    ng, n_ctx, w = u_ctx.shape
    n_lat = u_lat.shape[1]
    gs = S5_SCAN_GROUPS
    body = functools.partial(_s5_body, nb=nb)
    hs = pltpu.VMEM((gs, n_lat * nb, 2 * S5_STATE), _F32)
    ss = pltpu.VMEM((gs, (n_ctx + n_lat) * nb, 2 * S5_STATE), _F32)
    grp = lambda a: pl.BlockSpec((gs,) + a.shape[1:], lambda g: (g, 0, 0))
    return pl.pallas_call(
        body,
        grid=(ng // gs,),
        in_specs=[grp(u_ctx), grp(u_lat), grp(ws), grp(wy), grp(at)],
        out_specs=pl.BlockSpec((gs, n_lat, w), lambda g: (g, 0, 0)),
        out_shape=jax.ShapeDtypeStruct((ng, n_lat, w), _BF),
        scratch_shapes=[ss, ss, hs, hs, hs, hs],
        compiler_params=_cparams(("parallel",)),
        name="s5_scan",
    )(u_ctx, u_lat, ws, wy, at)


def _s5_param_body(lr_ref, li_ref, ls_ref, btr_ref, bti_ref, cr_ref, ci_ref, ws_ref, wy_ref, at_ref):
    t, hh, p = S5_CHUNK, S5_GROUP, S5_STATE
    kk = t * hh
    lr, li = lr_ref[...], li_ref[...]
    dt = jnp.exp(ls_ref[...])
    ldt, idt = lr * dt, li * dt
    mag = jnp.exp(ldt)
    ab_re, ab_im = mag * jnp.cos(idt), mag * jnp.sin(idt)
    den = lr * lr + li * li
    nr, ni = ab_re - 1.0, ab_im
    q_re = (nr * lr + ni * li) / den
    q_im = (ni * lr - nr * li) / den
    kf = lax.broadcasted_iota(jnp.int32, (2 * t, p), 0).astype(_F32)

    ap_re, ap_im, bb_re, bb_im = [], [], [], []
    for d in range(2):
        pm = jnp.exp(kf * ldt[d:d + 1, :])
        ap_re.append(pm * jnp.cos(kf * idt[d:d + 1, :]))
        ap_im.append(pm * jnp.sin(kf * idt[d:d + 1, :]))
        bb_re.append(q_re[d:d + 1, :] * btr_ref[d] - q_im[d:d + 1, :] * bti_ref[d])
        bb_im.append(q_re[d:d + 1, :] * bti_ref[d] + q_im[d:d + 1, :] * btr_ref[d])

    def rep(a, ks):
        return jnp.concatenate([jnp.broadcast_to(a[k:k + 1, :], (hh, p)) for k in ks], axis=0)

    def tile(a, n):
        return jnp.concatenate([a] * n, axis=0)

    def cmul(xr, xi, yr, yi):
        return xr * yr - xi * yi, xr * yi + xi * yr

    ks_f = list(range(t + 1))
    ks_r = list(range(t, -1, -1))
    caf_re, caf_im = cmul(tile(cr_ref[0], t + 1), tile(ci_ref[0], t + 1), rep(ap_re[0], ks_f), rep(ap_im[0], ks_f))
    car_re, car_im = cmul(tile(cr_ref[1], t + 1), tile(ci_ref[1], t + 1), rep(ap_re[1], ks_r), rep(ap_im[1], ks_r))

    def lag_blocks(bre, bim, ca_re, ca_im):
        lhs = jnp.concatenate([bre, bim], axis=1)
        rhs = jnp.concatenate([ca_re, -ca_im], axis=1)
        return lax.dot_general(lhs, rhs, (((1,), (1,)), ((), ())), precision=_HI,
                               preferred_element_type=_F32)

    w_f = lag_blocks(bb_re[0], bb_im[0], caf_re[:kk], caf_im[:kk])
    w_r = lag_blocks(bb_re[1], bb_im[1], car_re[hh:], car_im[hh:])
    lane = lax.broadcasted_iota(jnp.int32, (hh, kk), 1)
    for tlp in range(t):
        sf = hh * tlp
        sr = hh * (t - 1 - tlp)
        a = w_f if sf == 0 else jnp.where(lane >= sf, pltpu.roll(w_f, sf, 1), 0.0)
        b = w_r if sr == 0 else jnp.where(lane < kk - sr, pltpu.roll(w_r, kk - sr, 1), 0.0)
        wy_ref[hh * tlp:hh * (tlp + 1), :] = (a + b).astype(wy_ref.dtype)

    zeros = jnp.zeros((p, kk), _F32)
    blocks = [caf_re[hh:].T, zeros, zeros, car_re[:kk].T, (-caf_im[hh:]).T, zeros, zeros, (-car_im[:kk]).T]
    for i, blk in enumerate(blocks):
        wy_ref[kk + p * i:kk + p * (i + 1), :] = blk.astype(wy_ref.dtype)

    pf = list(range(t - 1, -1, -1))
    pr = list(range(t))
    f_re, f_im = cmul(rep(ap_re[0], pf), rep(ap_im[0], pf), tile(bb_re[0], t), tile(bb_im[0], t))
    r_re, r_im = cmul(rep(ap_re[1], pr), rep(ap_im[1], pr), tile(bb_re[1], t), tile(bb_im[1], t))
    ws_ref[...] = jnp.concatenate([f_re, r_re, f_im, r_im], axis=1).astype(ws_ref.dtype)

    a_re = jnp.concatenate([ap_re[0][t:t + 1, :], ap_re[1][t:t + 1, :]], axis=1)
    a_im = jnp.concatenate([ap_im[0][t:t + 1, :], ap_im[1][t:t + 1, :]], axis=1)
    row = lax.broadcasted_iota(jnp.int32, at_ref.shape, 0)
    at_ref[...] = jnp.where(row == 0, a_re, jnp.where(row == 1, a_im, 0.0))


def _s5_params(lam_re, lam_im, b_re, b_im, c_re, c_im, log_step):
    _, ng, p = lam_re.shape
    hh = b_re.shape[-1]
    kk = S5_CHUNK * hh
    gd = lambda a: jnp.swapaxes(a.astype(_F32), 0, 1)
    args = (gd(lam_re), gd(lam_im), gd(log_step)[..., None],
            jnp.swapaxes(gd(b_re), 2, 3), jnp.swapaxes(gd(b_im), 2, 3), gd(c_re), gd(c_im))
    spec = lambda a: pl.BlockSpec((None,) + a.shape[1:], lambda g: (g,) + (0,) * (a.ndim - 1))
    return pl.pallas_call(
        _s5_param_body,
        grid=(ng,),
        in_specs=[spec(a) for a in args],
        out_specs=[pl.BlockSpec((None, kk, 4 * p), lambda g: (g, 0, 0)),
                   pl.BlockSpec((None, kk + 8 * p, kk), lambda g: (g, 0, 0)),
                   pl.BlockSpec((None, 8, 2 * p), lambda g: (g, 0, 0))],
        out_shape=[jax.ShapeDtypeStruct((ng, kk, 4 * p), _BF),
                   jax.ShapeDtypeStruct((ng, kk + 8 * p, kk), _BF),
                   jax.ShapeDtypeStruct((ng, 8, 2 * p), _F32)],
        compiler_params=_cparams(("parallel",)),
        name="s5_params",
    )(*args)


def _post_body(x_ref, u_ref, yg_ref, fn_ref, d_ref, wglu_ref, gs_ref, wo_ref, g1_ref,
               n2_ref, sh_ref, sc_ref, wr_ref, br_ref, tri_ref, x1_ref, t_ref, rt_ref, rtt_ref, cnt_ref,
               y_scr, cnt_scr):
    ds5 = u_ref.shape[-1]
    first = (pl.program_id(0) == 0) & (pl.program_id(1) == 0)

    @pl.when(first)
    def _():
        cnt_scr[...] = jnp.zeros_like(cnt_scr)

    t_chunk = S5_CHUNK
    n_chunk = yg_ref.shape[1]
    lanes = 128
    gpb = lanes // S5_GROUP
    for j in range(ds5 // lanes):
        for hi in range(t_chunk // gpb):
            outs = _seg_transpose([yg_ref[gpb * j + glo, :, lanes * hi:lanes * (hi + 1)].astype(_F32)
                                   for glo in range(gpb)])
            for tlo in range(gpb):
                y_scr[j, pl.ds(gpb * hi + tlo, n_chunk, stride=t_chunk), :] = outs[tlo]
    y_s5 = jnp.concatenate([y_scr[j] for j in range(ds5 // lanes)], axis=-1)

    y = y_s5 + d_ref[...] * u_ref[...]
    y = jax.nn.gelu(y, approximate=True)
    y = y * jax.nn.sigmoid(_dot(y.astype(_BF), wglu_ref[...]))
    yn = _rms(y, gs_ref[...]).astype(_BF)
    mix = _dot(yn, wo_ref[0:ds5, :]) + _dot(fn_ref[...].astype(_BF), wo_ref[ds5:, :])
    x1 = x_ref[...] + g1_ref[...] * mix
    x1_ref[...] = x1
    t = _rms(x1, n2_ref[...]) * (1.0 + sc_ref[...]) + sh_ref[...]
    t_hi = t.astype(_BF)
    t_ref[...] = t_hi

    ng, epg = N_EXPERT_GROUPS, EXPERTS_PER_GROUP
    t_lo = (t - t_hi.astype(_F32)).astype(_BF)
    logits = (_dot(t_hi, wr_ref[0]) + _dot(t_lo, wr_ref[0]) + _dot(t_hi, wr_ref[1])) + br_ref[...]
    tm = logits.shape[0]
    lt = logits.T
    neg = jnp.float32(-jnp.inf)
    big = jnp.int32(1 << 20)
    row8 = lax.broadcasted_iota(jnp.int32, (epg, tm), 0)
    amax = lambda v: jnp.max(v, axis=0, keepdims=True)
    first = lambda v, m: jnp.min(jnp.where(v == m, row8, big), axis=0, keepdims=True)
    gt = lt[ng * epg:ng * epg + epg, :]
    gl = jnp.where(row8 < ng, gt, neg)
    gmax = amax(gl)
    gidx = first(gl, gmax)
    gw = 1.0 / jnp.sum(jnp.where(row8 < ng, jnp.exp(gt - gmax), 0.0), axis=0, keepdims=True)
    el = lt[0:epg, :]
    for g in range(1, ng):
        el = jnp.where(gidx == g, lt[g * epg:(g + 1) * epg, :], el)
    v0 = amax(el)
    i0 = first(el, v0)
    el1 = jnp.where(row8 == i0, neg, el)
    v1 = amax(el1)
    i1 = first(el1, v1)
    p0 = 1.0 / (1.0 + jnp.exp(v1 - v0))
    w0 = gw * p0
    w1 = gw * (1.0 - p0)
    e0 = gidx * epg + i0
    e1 = gidx * epg + i1

    rowe = lax.broadcasted_iota(jnp.int32, (ng * epg, tm), 0)
    oh0 = rowe == e0
    oh1 = rowe == e1
    oh = (oh0 | oh1).astype(_F32)
    prefix = _dot(oh.astype(_BF), tri_ref[...]) + cnt_scr[...]
    r0 = jnp.sum(jnp.where(oh0, prefix, 0.0), axis=0, keepdims=True)
    r1 = jnp.sum(jnp.where(oh1, prefix, 0.0), axis=0, keepdims=True)
    cnt = cnt_scr[...] + jnp.sum(oh, axis=1, keepdims=True)
    cnt_scr[...] = cnt
    cnt_ref[...] = cnt

    rec = jnp.concatenate([e0.astype(_F32), e1.astype(_F32), w0, w1, r0, r1,
                           jnp.zeros((ROUTE_ROWS - 6, tm), _F32)], axis=0)
    rtt_ref[...] = rec
    rt_ref[...] = jnp.concatenate([rec, jnp.zeros((ROUTE_LANES - ROUTE_ROWS, tm), _F32)], axis=0).T


@functools.lru_cache(maxsize=None)
def _strict_upper_ones(n):
    return np.triu(np.ones((n, n), np.float32), 1).astype(jnp.bfloat16)


def _post(x, u, y_g, fn, s5_d, wglu_bf, gs, wo_bf, g1, n2, sh2, sc2, wr, br, b0, nb, tm=512):
    b, l, d = x.shape
    ds5 = u.shape[-1]
    n_grp, _, w = y_g.shape
    kk = w // b
    n_exp = N_EXPERT_GROUPS * EXPERTS_PER_GROUP
    tri = jnp.asarray(_strict_upper_ones(tm))
    tok = lambda w: pl.BlockSpec((None, tm, w), lambda i, j: (i + b0, j, 0))
    otok = lambda w: pl.BlockSpec((None, tm, w), lambda i, j: (i, j, 0))
    per_b = pl.BlockSpec((None, 1, d), lambda i, j: (i + b0, 0, 0))
    full = lambda a: pl.BlockSpec(a.shape, lambda i, j: (0,) * a.ndim)
    ygs = pl.BlockSpec((n_grp, tm // S5_CHUNK, kk), lambda i, j: (0, j, i + b0))
    return pl.pallas_call(
        _post_body,
        grid=(nb, l // tm),
        in_specs=[tok(d), tok(ds5), ygs, tok(fn.shape[-1]), full(s5_d), full(wglu_bf),
                  full(gs), full(wo_bf), per_b, full(n2), per_b, per_b, full(wr), full(br), full(tri)],
        out_specs=[otok(d), otok(d), otok(ROUTE_LANES),
                   pl.BlockSpec((None, ROUTE_ROWS, tm), lambda i, j: (i, 0, j)),
                   pl.BlockSpec((n_exp, 1), lambda i, j: (0, 0))],
        out_shape=[jax.ShapeDtypeStruct((nb, l, d), _F32),
                   jax.ShapeDtypeStruct((nb, l, d), _BF),
                   jax.ShapeDtypeStruct((nb, l, ROUTE_LANES), _F32),
                   jax.ShapeDtypeStruct((nb, ROUTE_ROWS, l), _F32),
                   jax.ShapeDtypeStruct((n_exp, 1), _F32)],
        scratch_shapes=[pltpu.VMEM((ds5 // 128, tm, 128), _F32),
                        pltpu.VMEM((n_exp, 1), _F32)],
        compiler_params=_cparams(("arbitrary", "arbitrary")),
        name="post_mixer",
    )(x, u, y_g, fn, s5_d, wglu_bf, gs, wo_bf, g1, n2, sh2, sc2, wr, br, tri)


def _moe_body(te_ref, nt_ref, xs_ref, wg_ref, wu_ref, wd_ref, o_ref, wgu_bf, wd_bf):
    i = pl.program_id(0)
    de = wg_ref.shape[-1]
    used = i < nt_ref[0]
    new_expert = (i == 0) | (te_ref[i] != te_ref[jnp.maximum(i - 1, 0)])

    @pl.when(used & new_expert)
    def _():
        wgu_bf[:, 0:de] = wg_ref[...].astype(_BF)
        wgu_bf[:, de:2 * de] = wu_ref[...].astype(_BF)
        wd_bf[...] = wd_ref[...].astype(_BF)

    @pl.when(used)
    def _():
        h = _dot(xs_ref[...], wgu_bf[...])
        hg, hu = h[:, 0:de], h[:, de:2 * de]
        a = hg * jax.nn.sigmoid(hg) * hu
        o_ref[...] = _dot(a.astype(_BF), wd_bf[...]).astype(o_ref.dtype)

    @pl.when(jnp.logical_not(used))
    def _():
        o_ref[...] = jnp.zeros_like(o_ref)


def _moe(tile_expert, n_tiles_used, xs, wg, wu, wd):
    nr, d = xs.shape
    tm = MOE_TM
    de = wg.shape[-1]
    grid_spec = pltpu.PrefetchScalarGridSpec(
        num_scalar_prefetch=2,
        grid=(nr // tm,),
        in_specs=[pl.BlockSpec((tm, d), lambda i, te, nt: (i, 0)),
                  pl.BlockSpec((None, d, de), lambda i, te, nt: (te[i], 0, 0)),
                  pl.BlockSpec((None, d, de), lambda i, te, nt: (te[i], 0, 0)),
                  pl.BlockSpec((None, de, d), lambda i, te, nt: (te[i], 0, 0))],
        out_specs=pl.BlockSpec((tm, d), lambda i, te, nt: (i, 0)),
        scratch_shapes=[pltpu.VMEM((d, 2 * de), _BF), pltpu.VMEM((de, d), _BF)],
    )
    return pl.pallas_call(
        _moe_body,
        grid_spec=grid_spec,
        out_shape=jax.ShapeDtypeStruct((nr, d), _BF),
        compiler_params=_cparams(("arbitrary",)),
        name="moe_experts",
    )(tile_expert, n_tiles_used, xs, wg, wu, wd)


def _final_body(g2_ref, gf_ref, *refs, n_part, nb):
    o_ref = refs[-1]
    i = pl.program_id(0)
    for part in range(n_part):
        x1_ref, y0_ref, y1_ref, rt_ref = refs[4 * part:4 * part + 4]

        @pl.when((i >= part * nb) & (i < (part + 1) * nb))
        def _():
            w0 = rt_ref[:, 2:3]
            w1 = rt_ref[:, 3:4]
            m = w0 * y0_ref[...].astype(_F32) + w1 * y1_ref[...].astype(_F32)
            o_ref[...] = _rms(x1_ref[...] + g2_ref[...] * m, gf_ref[...])


def _final(parts, g2, gf, tm=512):
    n_part = len(parts)
    nb, l, d = parts[0][0].shape
    nj = l // tm
    in_specs = [pl.BlockSpec((None, 1, d), lambda i, j: (i, 0, 0)),
                pl.BlockSpec((1, d), lambda i, j: (0, 0))]
    args = [g2, gf]
    for part, arrs in enumerate(parts):
        def imap(i, j, part=part):
            own = (i >= part * nb) & (i < (part + 1) * nb)
            return (jnp.clip(i - part * nb, 0, nb - 1), jnp.where(own, j, jnp.where(i < part * nb, 0, nj - 1)), 0)
        for a in arrs:
            in_specs.append(pl.BlockSpec((None, tm, a.shape[-1]), imap))
            args.append(a)
    return pl.pallas_call(
        functools.partial(_final_body, n_part=n_part, nb=nb),
        grid=(n_part * nb, nj),
        in_specs=in_specs,
        out_specs=pl.BlockSpec((None, tm, d), lambda i, j: (i, j, 0)),
        out_shape=jax.ShapeDtypeStruct((n_part * nb, l, d), _F32),
        compiler_params=_cparams(("parallel", "arbitrary")),
        name="final_norm",
    )(*args)


def _route_plan(eid, rank, tok_ids, counts, tm, nr):
    n_experts = counts.shape[0]
    n_pairs = eid.size
    assert nr == n_pairs + n_experts * tm
    padded = ((counts + tm - 1) // tm) * tm
    pad_end = jnp.cumsum(padded)
    pad_start = pad_end - padded
    pos = rank
    for e in range(n_experts):
        pos = pos + jnp.where(eid == e, pad_start[e], 0)
    tile_start = jnp.arange(nr // tm, dtype=jnp.int32) * tm
    tile_expert = jnp.sum(tile_start[:, None] >= pad_end[None, :], axis=-1)
    tile_expert = jnp.minimum(tile_expert, n_experts - 1).astype(jnp.int32)
    stride = 1 << 17
    experts = jnp.arange(n_experts, dtype=jnp.int32)[:, None]
    q = jnp.arange(tm, dtype=jnp.int32)[None, :]
    pad_key = jnp.where(q < (padded - counts)[:, None], experts * stride + counts[:, None] + q, 1 << 30)
    pad_tok = (experts * tm + q) % (n_pairs // 2)
    keys = jnp.concatenate([(eid * stride + rank).reshape(-1), pad_key.reshape(-1)])
    toks = jnp.concatenate([tok_ids.reshape(-1), pad_tok.reshape(-1)])
    _, row_token = lax.sort_key_val(keys, toks)
    n_used = (pad_end[-1:] // tm).astype(jnp.int32)
    return pos, tile_expert, n_used, row_token


def kernel(x, c, ctx, c_ctx, w_ada, b_ada, norm1_g, norm2_g, w_in, s5_lam_re, s5_lam_im, s5_b_re, s5_b_im, s5_c_re, s5_c_im, s5_log_step, s5_d, s5_w_glu, fourier_w, mix_norm_s5_g, mix_norm_f_g, w_out, moe_w_group, moe_b_group, moe_w_router, moe_b_router, moe_w_gate, moe_w_up, moe_w_down, final_norm_g):
    b, l, d = x.shape
    lc = ctx.shape[1]
    depth = w_ada.shape[0]
    assert depth == 1 and l == FFT_R * FFT_R and lc % S5_CHUNK == 0 and b % 8 == 0
    ds5 = s5_d.shape[-1]
    df = w_in.shape[-1] - ds5
    n_exp = moe_w_gate.shape[1]
    row = lambda a: a.reshape(1, -1)

    cond = jnp.concatenate([c, c_ctx[None, :], jnp.zeros((7, d), _F32)], axis=0)
    mod = _adaln(cond, w_ada[0], b_ada[0])
    sh1, sc1, g1, sh2, sc2, g2 = [mod[:b, i * d:(i + 1) * d].reshape(b, 1, d) for i in range(6)]
    csh1 = mod[b:b + 1, 0:d]
    csc1 = mod[b:b + 1, d:2 * d]

    w_in_bf = w_in[0].astype(_BF)
    m1 = jnp.asarray(_fft_stage1_matrix()).astype(_BF)
    m2 = jnp.asarray(_fft_stage2_matrices()).astype(_BF)
    perm = jnp.asarray(_chunk_row_perm())
    zs, gc = _inproj(x, sh1, sc1, row(norm1_g[0]), w_in_bf, m1)
    u_ctx = _inproj_ctx(ctx, csh1, csc1, row(norm1_g[0]), w_in_bf[:, :ds5], perm)
    u_lat = _s5_pack(zs, perm)

    fg = df // FOURIER_GROUPS
    cc = np.arange(fg)
    ang = 2.0 * np.pi * ((cc[:, None] * cc[None, :]) % fg) / fg
    scale = 1.0 / math.sqrt(l * fg)
    cw = jnp.einsum("cm,gmd->gcd", jnp.asarray(np.cos(ang) * scale, _F32), fourier_w[0], precision=_HI)
    sw = jnp.einsum("cm,gmd->gcd", jnp.asarray(np.sin(ang) * scale, _F32), fourier_w[0], precision=_HI)
    fn = _fft2(gc, m2, cw.astype(_BF), sw.astype(_BF), row(mix_norm_f_g[0]))

    ws, wy, at = _s5_params(s5_lam_re[0], s5_lam_im[0], s5_b_re[0], s5_b_im[0],
                            s5_c_re[0], s5_c_im[0], s5_log_step[0])
    y_g = _s5(u_ctx, u_lat, ws, wy, at, b)

    n_rt = N_EXPERT_GROUPS * (1 + EXPERTS_PER_GROUP)
    w_rt = jnp.concatenate([moe_w_router[0].reshape(d, -1), moe_w_group[0]], axis=-1)
    w_rt = jnp.pad(w_rt, ((0, 0), (0, ROUTE_LANES - n_rt)))
    w_rt_hi = w_rt.astype(_BF)
    w_rt = jnp.stack([w_rt_hi, (w_rt - w_rt_hi.astype(_F32)).astype(_BF)])
    b_rt = jnp.pad(jnp.concatenate([moe_b_router[0].reshape(-1), moe_b_group[0]]), (0, ROUTE_LANES - n_rt))
    tm = MOE_TM
    nb = b // MOE_PARTS
    n_tok = nb * l
    nr = 2 * n_tok + n_exp * tm
    take_rows = lambda a, idx: a.at[idx].get(mode="promise_in_bounds")
    wglu_bf, wo_bf = s5_w_glu[0].astype(_BF), w_out[0].astype(_BF)
    parts = []
    for part in range(MOE_PARTS):
        x1, tmod, rt, rtt, cnt = _post(x, zs, y_g, fn, row(s5_d[0]), wglu_bf, row(mix_norm_s5_g[0]), wo_bf,
                                       g1, row(norm2_g[0]), sh2, sc2, w_rt, row(b_rt), part * nb, nb)
        rec = rtt.transpose(1, 0, 2).reshape(ROUTE_ROWS, n_tok)
        eid = rec[0:2].astype(jnp.int32)
        rank = rec[4:6].astype(jnp.int32)
        tok_ids = jnp.broadcast_to(jnp.arange(n_tok, dtype=jnp.int32), (2, n_tok))
        pos, tile_expert, n_used, row_token = _route_plan(eid, rank, tok_ids,
                                                          cnt[:, 0].astype(jnp.int32), tm, nr)
        xs = take_rows(tmod.reshape(n_tok, d), row_token)
        ys = _moe(tile_expert, n_used, xs, moe_w_gate[0], moe_w_up[0], moe_w_down[0])
        y0 = take_rows(ys, pos[0]).reshape(nb, l, d)
        y1 = take_rows(ys, pos[1]).reshape(nb, l, d)
        parts.append((x1, y0, y1, rt))
    return _final(parts, g2, row(final_norm_g))
```

```python
import functools
import math

import numpy as np
import jax
import jax.numpy as jnp
from jax import lax
from jax.experimental import pallas as pl
from jax.experimental.pallas import tpu as pltpu

EPS = 1e-6
S5_GROUP = 16
S5_STATE = 64
S5_CHUNK = 16
S5_SCAN_GROUPS = 4
FOURIER_GROUPS = 4
N_EXPERT_GROUPS = 4
EXPERTS_PER_GROUP = 8
FFT_R = 64
FFT_BLK = 8
MOE_TM = 512
MOE_PARTS = 1
ROUTE_LANES = 128
ROUTE_ROWS = 8
VMEM_LIMIT = 56 * 1024 * 1024

_HI = lax.Precision.HIGHEST
_BF = jnp.bfloat16
_F32 = jnp.float32


def _cparams(sem):
    return pltpu.CompilerParams(dimension_semantics=sem, vmem_limit_bytes=VMEM_LIMIT)


def _dot(a, b):
    return jnp.dot(a, b, preferred_element_type=_F32)


def _rms(x, g):
    return x * lax.rsqrt(jnp.mean(x * x, axis=-1, keepdims=True) + EPS) * g


def _adaln_body(c_ref, w_ref, b_ref, o_ref):
    c = c_ref[...]
    a = c * jax.nn.sigmoid(c)
    o_ref[...] = jnp.dot(a, w_ref[...], preferred_element_type=_F32, precision=_HI) + b_ref[...]


def _adaln(cond, w, b):
    m, d = cond.shape
    n = w.shape[1]
    tn = 768
    return pl.pallas_call(
        _adaln_body,
        grid=(n // tn,),
        in_specs=[pl.BlockSpec((m, d), lambda j: (0, 0)),
                  pl.BlockSpec((d, tn), lambda j: (0, j)),
                  pl.BlockSpec((1, tn), lambda j: (0, j))],
        out_specs=pl.BlockSpec((m, tn), lambda j: (0, j)),
        out_shape=jax.ShapeDtypeStruct((m, n), _F32),
        compiler_params=_cparams(("arbitrary",)),
        name="adaln",
    )(cond, w, b.reshape(1, n))


def _inproj_body(x_ref, sh_ref, sc_ref, g_ref, w_ref, m1_ref, zs_ref, gc_ref):
    r, blk, d = x_ref.shape
    x = x_ref[...].reshape(r * blk, d)
    h = _rms(x, g_ref[...]) * (1.0 + sc_ref[...]) + sh_ref[...]
    z = _dot(h.astype(_BF), w_ref[...])
    ds5 = zs_ref.shape[-1]
    zs_ref[...] = z[:, :ds5].reshape(r, blk, ds5)
    v = z[:, ds5:].astype(_BF)
    g1 = _dot(m1_ref[...], v)
    gc_ref[...] = g1.astype(_BF).reshape(r, 2 * blk, v.shape[-1])


def _inproj(x, sh, sc, g, w_bf, m1):
    b, l, d = x.shape
    r, blk = FFT_R, FFT_BLK
    nj = r // blk
    dmix = w_bf.shape[1]
    ds5 = dmix // 2
    df = dmix - ds5
    x4 = x.reshape(b, r, r, d)
    zs, gc = pl.pallas_call(
        _inproj_body,
        grid=(b, nj),
        in_specs=[pl.BlockSpec((None, r, blk, d), lambda i, j: (i, 0, j, 0)),
                  pl.BlockSpec((None, 1, d), lambda i, j: (i, 0, 0)),
                  pl.BlockSpec((None, 1, d), lambda i, j: (i, 0, 0)),
                  pl.BlockSpec((1, d), lambda i, j: (0, 0)),
                  pl.BlockSpec((d, dmix), lambda i, j: (0, 0)),
                  pl.BlockSpec(m1.shape, lambda i, j: (0, 0))],
        out_specs=[pl.BlockSpec((None, r, blk, ds5), lambda i, j: (i, 0, j, 0)),
                   pl.BlockSpec((None, r, 2 * blk, df), lambda i, j: (i, 0, j, 0))],
        out_shape=[jax.ShapeDtypeStruct((b, r, r, ds5), _F32),
                   jax.ShapeDtypeStruct((b, r, 2 * r, df), _BF)],
        compiler_params=_cparams(("parallel", "arbitrary")),
        name="inproj",
    )(x4, sh, sc, g, w_bf, m1)
    return zs.reshape(b, l, ds5), gc


def _seg_transpose(arrs):
    n = len(arrs)
    seg = lax.broadcasted_iota(jnp.int32, arrs[0].shape, 1) // S5_GROUP
    d = n // 2
    while d >= 1:
        keep = (seg & d) == 0
        new = list(arrs)
        for i in range(n):
            if i & d == 0:
                a, b = arrs[i], arrs[i + d]
                new[i] = jnp.where(keep, a, pltpu.roll(b, S5_GROUP * d, 1))
                new[i + d] = jnp.where(keep, pltpu.roll(a, 128 - S5_GROUP * d, 1), b)
        arrs = new
        d //= 2
    return arrs


@functools.lru_cache(maxsize=None)
def _chunk_row_perm():
    t = S5_CHUNK
    p = np.zeros((t * t, t * t), np.float32)
    for c in range(t):
        for tl in range(t):
            p[tl * t + c, c * t + tl] = 1.0
    return p.astype(jnp.bfloat16)


def _pack_chunks(z_bf, p_ref, u_ref):
    t = S5_CHUNK
    nsub = z_bf.shape[0] // (t * t)
    pieces = [_dot(p_ref[...], z_bf[s * t * t:(s + 1) * t * t, :]) for s in range(nsub)]
    a = []
    for tl in range(t):
        rows = [pc[tl * t:(tl + 1) * t, :] for pc in pieces]
        a.append(rows[0] if nsub == 1 else jnp.concatenate(rows, axis=0))
    lanes = 128
    gpb = lanes // S5_GROUP
    for j in range(z_bf.shape[1] // lanes):
        for hi in range(t // gpb):
            outs = _seg_transpose([a[gpb * hi + tlo][:, lanes * j:lanes * (j + 1)] for tlo in range(gpb)])
            for glo in range(gpb):
                u_ref[gpb * j + glo, :, lanes * hi:lanes * (hi + 1)] = outs[glo].astype(u_ref.dtype)


def _inproj_ctx_body(x_ref, sh_ref, sc_ref, g_ref, w_ref, p_ref, u_ref):
    h = _rms(x_ref[...], g_ref[...]) * (1.0 + sc_ref[...]) + sh_ref[...]
    z = _dot(h.astype(_BF), w_ref[...])
    _pack_chunks(z.astype(_BF), p_ref, u_ref)


def _inproj_ctx(ctx, sh, sc, g, w_s5_bf, perm):
    b, lc, d = ctx.shape
    ds5 = w_s5_bf.shape[1]
    n_grp = ds5 // S5_GROUP
    kk = S5_CHUNK * S5_GROUP
    return pl.pallas_call(
        _inproj_ctx_body,
        grid=(b,),
        in_specs=[pl.BlockSpec((None, lc, d), lambda i: (i, 0, 0)),
                  pl.BlockSpec((1, d), lambda i: (0, 0)),
                  pl.BlockSpec((1, d), lambda i: (0, 0)),
                  pl.BlockSpec((1, d), lambda i: (0, 0)),
                  pl.BlockSpec((d, ds5), lambda i: (0, 0)),
                  pl.BlockSpec(perm.shape, lambda i: (0, 0))],
        out_specs=pl.BlockSpec((n_grp, lc // S5_CHUNK, kk), lambda i: (0, 0, i)),
        out_shape=jax.ShapeDtypeStruct((n_grp, lc // S5_CHUNK, b * kk), _BF),
        compiler_params=_cparams(("arbitrary",)),
        name="inproj_ctx",
    )(ctx, sh, sc, g, w_s5_bf, perm)


def _s5_pack_body(z_ref, p_ref, u_ref):
    _pack_chunks(z_ref[...].astype(_BF), p_ref, u_ref)


def _s5_pack(zs, perm, tok=2048):
    b, l, ds5 = zs.shape
    n_grp = ds5 // S5_GROUP
    kk = S5_CHUNK * S5_GROUP
    return pl.pallas_call(
        _s5_pack_body,
        grid=(b, l // tok),
        in_specs=[pl.BlockSpec((None, tok, ds5), lambda i, j: (i, j, 0)),
                  pl.BlockSpec(perm.shape, lambda i, j: (0, 0))],
        out_specs=pl.BlockSpec((n_grp, tok // S5_CHUNK, kk), lambda i, j: (0, j, i)),
        out_shape=jax.ShapeDtypeStruct((n_grp, l // S5_CHUNK, b * kk), _BF),
        compiler_params=_cparams(("parallel", "arbitrary")),
        name="s5_pack",
    )(zs, perm)


@functools.lru_cache(maxsize=None)
def _fft_stage1_matrix():
    r, blk = FFT_R, FFT_BLK
    k1 = np.arange(r)[:, None]
    l1 = np.arange(r)[None, :]
    ang = 2.0 * np.pi * ((k1 * l1) % r) / r
    f = np.stack([np.cos(ang), -np.sin(ang)], axis=1)
    m = np.einsum("kal,pq->kaplq", f, np.eye(blk))
    return m.reshape(r * 2 * blk, r * blk).astype(np.float32)


@functools.lru_cache(maxsize=None)
def _fft_stage2_matrices():
    r, blk = FFT_R, FFT_BLK
    n = r * r
    nt = r // blk
    l2 = np.arange(r)
    k2 = np.arange(r)
    out = np.zeros((nt, 2, r, blk, blk, r // blk, 2, blk), np.float32)
    for i in range(nt):
        for kl in range(blk):
            k1 = blk * i + kl
            ang = 2.0 * np.pi * (((k2[:, None] * l2[None, :] * r) + l2[None, :] * k1) % n) / n
            tr, ti = np.cos(ang), -np.sin(ang)
            tr = tr.reshape(r, r // blk, blk)
            ti = ti.reshape(r, r // blk, blk)
            out[i, 0, :, kl, kl, :, 0, :] = tr
            out[i, 0, :, kl, kl, :, 1, :] = -ti
            out[i, 1, :, kl, kl, :, 0, :] = ti
            out[i, 1, :, kl, kl, :, 1, :] = tr
    return out.reshape(nt, 2 * r * blk, blk * 2 * r)


def _fft2_body(gc_ref, m2_ref, cw_ref, sw_ref, g_ref, o_ref):
    kb, rr, df = gc_ref.shape
    gc = gc_ref[...].reshape(kb * rr, df)
    x = _dot(m2_ref[...], gc)
    half = x.shape[0] // 2
    xr = x[:half].astype(_BF)
    xi = x[half:].astype(_BF)
    ng = cw_ref.shape[0]
    fg = df // ng
    parts = []
    for g in range(ng):
        sl = slice(g * fg, (g + 1) * fg)
        parts.append(_dot(xr[:, sl], cw_ref[g]) + _dot(xi[:, sl], sw_ref[g]))
    f = jnp.concatenate(parts, axis=-1)
    fn = _rms(f, g_ref[...])
    o_ref[...] = fn.reshape(o_ref.shape)


def _fft2(gc, m2, cw, sw, gf):
    b, r, rr, df = gc.shape
    blk = FFT_BLK
    nt = r // blk
    out = pl.pallas_call(
        _fft2_body,
        grid=(nt, b),
        in_specs=[pl.BlockSpec((None, blk, rr, df), lambda j, i: (i, j, 0, 0)),
                  pl.BlockSpec((None,) + m2.shape[1:], lambda j, i: (j, 0, 0)),
                  pl.BlockSpec(cw.shape, lambda j, i: (0, 0, 0)),
                  pl.BlockSpec(sw.shape, lambda j, i: (0, 0, 0)),
                  pl.BlockSpec((1, df), lambda j, i: (0, 0))],
        out_specs=pl.BlockSpec((None, r, blk, df), lambda j, i: (i, 0, j, 0)),
        out_shape=jax.ShapeDtypeStruct((b, r, r, df), _F32),
        compiler_params=_cparams(("parallel", "arbitrary")),
        name="fft2",
    )(gc, m2, cw, sw, gf)
    return out.reshape(b, r * r, df)


def _s5_body(uc_ref, ul_ref, ws_ref, wy_ref, at_ref, y_ref,
             s_re, s_im, ha_re, hb_re, ha_im, hb_im, *, nb):
    p = S5_STATE
    gs, n_ctx, _ = uc_ref.shape
    n_lat = ul_ref.shape[1]
    kk = ws_ref.shape[1]
    for gi in range(gs):
        ws = ws_ref[gi]
        for b in range(nb):
            sl = slice(b * kk, (b + 1) * kk)
            sc = _dot(uc_ref[gi, :, sl], ws)
            sl_ = _dot(ul_ref[gi, :, sl], ws)
            s_re[gi, pl.ds(b, n_ctx, stride=nb), :] = sc[:, :2 * p]
            s_im[gi, pl.ds(b, n_ctx, stride=nb), :] = sc[:, 2 * p:]
            s_re[gi, pl.ds(n_ctx * nb + b, n_lat, stride=nb), :] = sl_[:, :2 * p]
            s_im[gi, pl.ds(n_ctx * nb + b, n_lat, stride=nb), :] = sl_[:, 2 * p:]

    a_re = [at_ref[gi, 0:1, :] for gi in range(gs)]
    a_im = [at_ref[gi, 1:2, :] for gi in range(gs)]
    is_f = lax.broadcasted_iota(jnp.int32, (nb, 2 * p), 1) < p
    n_all = n_ctx + n_lat

    def load(gi, cf, cr):
        rf = pl.multiple_of(cf * nb, nb)
        rr = pl.multiple_of(cr * nb, nb)
        return (jnp.where(is_f, s_re[gi, pl.ds(rf, nb), :], s_re[gi, pl.ds(rr, nb), :]),
                jnp.where(is_f, s_im[gi, pl.ds(rf, nb), :], s_im[gi, pl.ds(rr, nb), :]))

    def advance(gi, h_re, h_im, x_re, x_im):
        return (a_re[gi] * h_re - a_im[gi] * h_im + x_re, a_re[gi] * h_im + a_im[gi] * h_re + x_im)

    def ctx_step(step, carry):
        out = []
        for gi in range(gs):
            x_re, x_im = load(gi, step, n_ctx - 1 - step)
            out.append(advance(gi, *carry[gi], x_re, x_im))
        return tuple(out)

    def lat_step(step, carry):
        rf = pl.multiple_of(step * nb, nb)
        rr = pl.multiple_of((n_lat - 1 - step) * nb, nb)
        out = []
        for gi in range(gs):
            h_re, h_im = carry[gi]
            ha_re[gi, pl.ds(rf, nb), :] = h_re
            hb_re[gi, pl.ds(rr, nb), :] = h_re
            ha_im[gi, pl.ds(rf, nb), :] = h_im
            hb_im[gi, pl.ds(rr, nb), :] = h_im
            x_re, x_im = load(gi, n_ctx + step, n_all - 1 - step)
            out.append(advance(gi, h_re, h_im, x_re, x_im))
        return tuple(out)

    zero = jnp.zeros((nb, 2 * p), _F32)
    carry = lax.fori_loop(0, n_ctx, ctx_step, tuple((zero, zero) for _ in range(gs)), unroll=2)
    lax.fori_loop(0, n_lat, lat_step, carry, unroll=2)

    for gi in range(gs):
        wy = wy_ref[gi]
        for b in range(nb):
            rows = pl.ds(b, n_lat, stride=nb)
            hin = jnp.concatenate([ha_re[gi, rows, :], hb_re[gi, rows, :],
                                   ha_im[gi, rows, :], hb_im[gi, rows, :]], axis=-1)
            lhs = jnp.concatenate([ul_ref[gi, :, b * kk:(b + 1) * kk], hin.astype(_BF)], axis=-1)
            y_ref[gi, :, b * kk:(b + 1) * kk] = _dot(lhs, wy).astype(y_ref.dtype)


def _s5(u_ctx, u_lat, ws, wy, at, nb):
    ng, n_ctx, w = u_ctx.shape
    n_lat = u_lat.shape[1]
    gs = S5_SCAN_GROUPS
    body = functools.partial(_s5_body, nb=nb)
    hs = pltpu.VMEM((gs, n_lat * nb, 2 * S5_STATE), _F32)
    ss = pltpu.VMEM((gs, (n_ctx + n_lat) * nb, 2 * S5_STATE), _F32)
    grp = lambda a: pl.BlockSpec((gs,) + a.shape[1:], lambda g: (g, 0, 0))
    return pl.pallas_call(
        body,
        grid=(ng // gs,),
        in_specs=[grp(u_ctx), grp(u_lat), grp(ws), grp(wy), grp(at)],
        out_specs=pl.BlockSpec((gs, n_lat, w), lambda g: (g, 0, 0)),
        out_shape=jax.ShapeDtypeStruct((ng, n_lat, w), _BF),
        scratch_shapes=[ss, ss, hs, hs, hs, hs],
        compiler_params=_cparams(("parallel",)),
        name="s5_scan",
    )(u_ctx, u_lat, ws, wy, at)


def _s5_param_body(lr_ref, li_ref, ls_ref, btr_ref, bti_ref, cr_ref, ci_ref, ws_ref, wy_ref, at_ref):
    t, hh, p = S5_CHUNK, S5_GROUP, S5_STATE
    kk = t * hh
    lr, li = lr_ref[...], li_ref[...]
    dt = jnp.exp(ls_ref[...])
    ldt, idt = lr * dt, li * dt
    mag = jnp.exp(ldt)
    ab_re, ab_im = mag * jnp.cos(idt), mag * jnp.sin(idt)
    den = lr * lr + li * li
    nr, ni = ab_re - 1.0, ab_im
    q_re = (nr * lr + ni * li) / den
    q_im = (ni * lr - nr * li) / den
    kf = lax.broadcasted_iota(jnp.int32, (2 * t, p), 0).astype(_F32)

    ap_re, ap_im, bb_re, bb_im = [], [], [], []
    for d in range(2):
        pm = jnp.exp(kf * ldt[d:d + 1, :])
        ap_re.append(pm * jnp.cos(kf * idt[d:d + 1, :]))
        ap_im.append(pm * jnp.sin(kf * idt[d:d + 1, :]))
        bb_re.append(q_re[d:d + 1, :] * btr_ref[d] - q_im[d:d + 1, :] * bti_ref[d])
        bb_im.append(q_re[d:d + 1, :] * bti_ref[d] + q_im[d:d + 1, :] * btr_ref[d])

    def rep(a, ks):
        return jnp.concatenate([jnp.broadcast_to(a[k:k + 1, :], (hh, p)) for k in ks], axis=0)

    def tile(a, n):
        return jnp.concatenate([a] * n, axis=0)

    def cmul(xr, xi, yr, yi):
        return xr * yr - xi * yi, xr * yi + xi * yr

    ks_f = list(range(t + 1))
    ks_r = list(range(t, -1, -1))
    caf_re, caf_im = cmul(tile(cr_ref[0], t + 1), tile(ci_ref[0], t + 1), rep(ap_re[0], ks_f), rep(ap_im[0], ks_f))
    car_re, car_im = cmul(tile(cr_ref[1], t + 1), tile(ci_ref[1], t + 1), rep(ap_re[1], ks_r), rep(ap_im[1], ks_r))

    def lag_blocks(bre, bim, ca_re, ca_im):
        lhs = jnp.concatenate([bre, bim], axis=1)
        rhs = jnp.concatenate([ca_re, -ca_im], axis=1)
        return lax.dot_general(lhs, rhs, (((1,), (1,)), ((), ())), precision=_HI,
                               preferred_element_type=_F32)

    w_f = lag_blocks(bb_re[0], bb_im[0], caf_re[:kk], caf_im[:kk])
    w_r = lag_blocks(bb_re[1], bb_im[1], car_re[hh:], car_im[hh:])
    lane = lax.broadcasted_iota(jnp.int32, (hh, kk), 1)
    for tlp in range(t):
        sf = hh * tlp
        sr = hh * (t - 1 - tlp)
        a = w_f if sf == 0 else jnp.where(lane >= sf, pltpu.roll(w_f, sf, 1), 0.0)
        b = w_r if sr == 0 else jnp.where(lane < kk - sr, pltpu.roll(w_r, kk - sr, 1), 0.0)
        wy_ref[hh * tlp:hh * (tlp + 1), :] = (a + b).astype(wy_ref.dtype)

    zeros = jnp.zeros((p, kk), _F32)
    blocks = [caf_re[hh:].T, zeros, zeros, car_re[:kk].T, (-caf_im[hh:]).T, zeros, zeros, (-car_im[:kk]).T]
    for i, blk in enumerate(blocks):
        wy_ref[kk + p * i:kk + p * (i + 1), :] = blk.astype(wy_ref.dtype)

    pf = list(range(t - 1, -1, -1))
    pr = list(range(t))
    f_re, f_im = cmul(rep(ap_re[0], pf), rep(ap_im[0], pf), tile(bb_re[0], t), tile(bb_im[0], t))
    r_re, r_im = cmul(rep(ap_re[1], pr), rep(ap_im[1], pr), tile(bb_re[1], t), tile(bb_im[1], t))
    ws_ref[...] = jnp.concatenate([f_re, r_re, f_im, r_im], axis=1).astype(ws_ref.dtype)

    a_re = jnp.concatenate([ap_re[0][t:t + 1, :], ap_re[1][t:t + 1, :]], axis=1)
    a_im = jnp.concatenate([ap_im[0][t:t + 1, :], ap_im[1][t:t + 1, :]], axis=1)
    row = lax.broadcasted_iota(jnp.int32, at_ref.shape, 0)
    at_ref[...] = jnp.where(row == 0, a_re, jnp.where(row == 1, a_im, 0.0))


def _s5_params(lam_re, lam_im, b_re, b_im, c_re, c_im, log_step):
    _, ng, p = lam_re.shape
    hh = b_re.shape[-1]
    kk = S5_CHUNK * hh
    gd = lambda a: jnp.swapaxes(a.astype(_F32), 0, 1)
    args = (gd(lam_re), gd(lam_im), gd(log_step)[..., None],
            jnp.swapaxes(gd(b_re), 2, 3), jnp.swapaxes(gd(b_im), 2, 3), gd(c_re), gd(c_im))
    spec = lambda a: pl.BlockSpec((None,) + a.shape[1:], lambda g: (g,) + (0,) * (a.ndim - 1))
    return pl.pallas_call(
        _s5_param_body,
        grid=(ng,),
        in_specs=[spec(a) for a in args],
        out_specs=[pl.BlockSpec((None, kk, 4 * p), lambda g: (g, 0, 0)),
                   pl.BlockSpec((None, kk + 8 * p, kk), lambda g: (g, 0, 0)),
                   pl.BlockSpec((None, 8, 2 * p), lambda g: (g, 0, 0))],
        out_shape=[jax.ShapeDtypeStruct((ng, kk, 4 * p), _BF),
                   jax.ShapeDtypeStruct((ng, kk + 8 * p, kk), _BF),
                   jax.ShapeDtypeStruct((ng, 8, 2 * p), _F32)],
        compiler_params=_cparams(("parallel",)),
        name="s5_params",
    )(*args)


def _post_body(x_ref, u_ref, yg_ref, fn_ref, d_ref, wglu_ref, gs_ref, wo_ref, g1_ref,
               n2_ref, sh_ref, sc_ref, wr_ref, br_ref, tri_ref, x1_ref, t_ref, rt_ref, rtt_ref, cnt_ref,
               y_scr, cnt_scr):
    ds5 = u_ref.shape[-1]
    first = (pl.program_id(0) == 0) & (pl.program_id(1) == 0)

    @pl.when(first)
    def _():
        cnt_scr[...] = jnp.zeros_like(cnt_scr)

    t_chunk = S5_CHUNK
    n_chunk = yg_ref.shape[1]
    lanes = 128
    gpb = lanes // S5_GROUP
    for j in range(ds5 // lanes):
        for hi in range(t_chunk // gpb):
            outs = _seg_transpose([yg_ref[gpb * j + glo, :, lanes * hi:lanes * (hi + 1)].astype(_F32)
                                   for glo in range(gpb)])
            for tlo in range(gpb):
                y_scr[j, pl.ds(gpb * hi + tlo, n_chunk, stride=t_chunk), :] = outs[tlo]
    y_s5 = jnp.concatenate([y_scr[j] for j in range(ds5 // lanes)], axis=-1)

    y = y_s5 + d_ref[...] * u_ref[...]
    y = jax.nn.gelu(y, approximate=True)
    y = y * jax.nn.sigmoid(_dot(y.astype(_BF), wglu_ref[...]))
    yn = _rms(y, gs_ref[...]).astype(_BF)
    mix = _dot(yn, wo_ref[0:ds5, :]) + _dot(fn_ref[...].astype(_BF), wo_ref[ds5:, :])
    x1 = x_ref[...] + g1_ref[...] * mix
    x1_ref[...] = x1
    t = _rms(x1, n2_ref[...]) * (1.0 + sc_ref[...]) + sh_ref[...]
    t_hi = t.astype(_BF)
    t_ref[...] = t_hi

    ng, epg = N_EXPERT_GROUPS, EXPERTS_PER_GROUP
    t_lo = (t - t_hi.astype(_F32)).astype(_BF)
    logits = (_dot(t_hi, wr_ref[0]) + _dot(t_lo, wr_ref[0]) + _dot(t_hi, wr_ref[1])) + br_ref[...]
    tm = logits.shape[0]
    lt = logits.T
    neg = jnp.float32(-jnp.inf)
    big = jnp.int32(1 << 20)
    row8 = lax.broadcasted_iota(jnp.int32, (epg, tm), 0)
    amax = lambda v: jnp.max(v, axis=0, keepdims=True)
    first = lambda v, m: jnp.min(jnp.where(v == m, row8, big), axis=0, keepdims=True)
    gt = lt[ng * epg:ng * epg + epg, :]
    gl = jnp.where(row8 < ng, gt, neg)
    gmax = amax(gl)
    gidx = first(gl, gmax)
    gw = 1.0 / jnp.sum(jnp.where(row8 < ng, jnp.exp(gt - gmax), 0.0), axis=0, keepdims=True)
    el = lt[0:epg, :]
    for g in range(1, ng):
        el = jnp.where(gidx == g, lt[g * epg:(g + 1) * epg, :], el)
    v0 = amax(el)
    i0 = first(el, v0)
    el1 = jnp.where(row8 == i0, neg, el)
    v1 = amax(el1)
    i1 = first(el1, v1)
    p0 = 1.0 / (1.0 + jnp.exp(v1 - v0))
    w0 = gw * p0
    w1 = gw * (1.0 - p0)
    e0 = gidx * epg + i0
    e1 = gidx * epg + i1

    rowe = lax.broadcasted_iota(jnp.int32, (ng * epg, tm), 0)
    oh0 = rowe == e0
    oh1 = rowe == e1
    oh = (oh0 | oh1).astype(_F32)
    prefix = _dot(oh.astype(_BF), tri_ref[...]) + cnt_scr[...]
    r0 = jnp.sum(jnp.where(oh0, prefix, 0.0), axis=0, keepdims=True)
    r1 = jnp.sum(jnp.where(oh1, prefix, 0.0), axis=0, keepdims=True)
    cnt = cnt_scr[...] + jnp.sum(oh, axis=1, keepdims=True)
    cnt_scr[...] = cnt
    cnt_ref[...] = cnt

    rec = jnp.concatenate([e0.astype(_F32), e1.astype(_F32), w0, w1, r0, r1,
                           jnp.zeros((ROUTE_ROWS - 6, tm), _F32)], axis=0)
    rtt_ref[...] = rec
    rt_ref[...] = jnp.concatenate([rec, jnp.zeros((ROUTE_LANES - ROUTE_ROWS, tm), _F32)], axis=0).T


@functools.lru_cache(maxsize=None)
def _strict_upper_ones(n):
    return np.triu(np.ones((n, n), np.float32), 1).astype(jnp.bfloat16)


def _post(x, u, y_g, fn, s5_d, wglu_bf, gs, wo_bf, g1, n2, sh2, sc2, wr, br, b0, nb, tm=512):
    b, l, d = x.shape
    ds5 = u.shape[-1]
    n_grp, _, w = y_g.shape
    kk = w // b
    n_exp = N_EXPERT_GROUPS * EXPERTS_PER_GROUP
    tri = jnp.asarray(_strict_upper_ones(tm))
    tok = lambda w: pl.BlockSpec((None, tm, w), lambda i, j: (i + b0, j, 0))
    otok = lambda w: pl.BlockSpec((None, tm, w), lambda i, j: (i, j, 0))
    per_b = pl.BlockSpec((None, 1, d), lambda i, j: (i + b0, 0, 0))
    full = lambda a: pl.BlockSpec(a.shape, lambda i, j: (0,) * a.ndim)
    ygs = pl.BlockSpec((n_grp, tm // S5_CHUNK, kk), lambda i, j: (0, j, i + b0))
    return pl.pallas_call(
        _post_body,
        grid=(nb, l // tm),
        in_specs=[tok(d), tok(ds5), ygs, tok(fn.shape[-1]), full(s5_d), full(wglu_bf),
                  full(gs), full(wo_bf), per_b, full(n2), per_b, per_b, full(wr), full(br), full(tri)],
        out_specs=[otok(d), otok(d), otok(ROUTE_LANES),
                   pl.BlockSpec((None, ROUTE_ROWS, tm), lambda i, j: (i, 0, j)),
                   pl.BlockSpec((n_exp, 1), lambda i, j: (0, 0))],
        out_shape=[jax.ShapeDtypeStruct((nb, l, d), _F32),
                   jax.ShapeDtypeStruct((nb, l, d), _BF),
                   jax.ShapeDtypeStruct((nb, l, ROUTE_LANES), _F32),
                   jax.ShapeDtypeStruct((nb, ROUTE_ROWS, l), _F32),
                   jax.ShapeDtypeStruct((n_exp, 1), _F32)],
        scratch_shapes=[pltpu.VMEM((ds5 // 128, tm, 128), _F32),
                        pltpu.VMEM((n_exp, 1), _F32)],
        compiler_params=_cparams(("arbitrary", "arbitrary")),
        name="post_mixer",
    )(x, u, y_g, fn, s5_d, wglu_bf, gs, wo_bf, g1, n2, sh2, sc2, wr, br, tri)


def _moe_body(te_ref, nt_ref, xs_ref, wg_ref, wu_ref, wd_ref, o_ref, wgu_bf, wd_bf):
    i = pl.program_id(0)
    de = wg_ref.shape[-1]
    used = i < nt_ref[0]
    new_expert = (i == 0) | (te_ref[i] != te_ref[jnp.maximum(i - 1, 0)])

    @pl.when(used & new_expert)
    def _():
        wgu_bf[:, 0:de] = wg_ref[...].astype(_BF)
        wgu_bf[:, de:2 * de] = wu_ref[...].astype(_BF)
        wd_bf[...] = wd_ref[...].astype(_BF)

    @pl.when(used)
    def _():
        h = _dot(xs_ref[...], wgu_bf[...])
        hg, hu = h[:, 0:de], h[:, de:2 * de]
        a = hg * jax.nn.sigmoid(hg) * hu
        o_ref[...] = _dot(a.astype(_BF), wd_bf[...]).astype(o_ref.dtype)

    @pl.when(jnp.logical_not(used))
    def _():
        o_ref[...] = jnp.zeros_like(o_ref)


def _moe(tile_expert, n_tiles_used, xs, wg, wu, wd):
    nr, d = xs.shape
    tm = MOE_TM
    de = wg.shape[-1]
    grid_spec = pltpu.PrefetchScalarGridSpec(
        num_scalar_prefetch=2,
        grid=(nr // tm,),
        in_specs=[pl.BlockSpec((tm, d), lambda i, te, nt: (i, 0)),
                  pl.BlockSpec((None, d, de), lambda i, te, nt: (te[i], 0, 0)),
                  pl.BlockSpec((None, d, de), lambda i, te, nt: (te[i], 0, 0)),
                  pl.BlockSpec((None, de, d), lambda i, te, nt: (te[i], 0, 0))],
        out_specs=pl.BlockSpec((tm, d), lambda i, te, nt: (i, 0)),
        scratch_shapes=[pltpu.VMEM((d, 2 * de), _BF), pltpu.VMEM((de, d), _BF)],
    )
    return pl.pallas_call(
        _moe_body,
        grid_spec=grid_spec,
        out_shape=jax.ShapeDtypeStruct((nr, d), _BF),
        compiler_params=_cparams(("arbitrary",)),
        name="moe_experts",
    )(tile_expert, n_tiles_used, xs, wg, wu, wd)


def _final_body(g2_ref, gf_ref, *refs, n_part, nb):
    o_ref = refs[-1]
    i = pl.program_id(0)
    for part in range(n_part):
        x1_ref, y0_ref, y1_ref, rt_ref = refs[4 * part:4 * part + 4]

        @pl.when((i >= part * nb) & (i < (part + 1) * nb))
        def _():
            w0 = rt_ref[:, 2:3]
            w1 = rt_ref[:, 3:4]
            m = w0 * y0_ref[...].astype(_F32) + w1 * y1_ref[...].astype(_F32)
            o_ref[...] = _rms(x1_ref[...] + g2_ref[...] * m, gf_ref[...])


def _final(parts, g2, gf, tm=512):
    n_part = len(parts)
    nb, l, d = parts[0][0].shape
    nj = l // tm
    in_specs = [pl.BlockSpec((None, 1, d), lambda i, j: (i, 0, 0)),
                pl.BlockSpec((1, d), lambda i, j: (0, 0))]
    args = [g2, gf]
    for part, arrs in enumerate(parts):
        def imap(i, j, part=part):
            own = (i >= part * nb) & (i < (part + 1) * nb)
            return (jnp.clip(i - part * nb, 0, nb - 1), jnp.where(own, j, jnp.where(i < part * nb, 0, nj - 1)), 0)
        for a in arrs:
            in_specs.append(pl.BlockSpec((None, tm, a.shape[-1]), imap))
            args.append(a)
    return pl.pallas_call(
        functools.partial(_final_body, n_part=n_part, nb=nb),
        grid=(n_part * nb, nj),
        in_specs=in_specs,
        out_specs=pl.BlockSpec((None, tm, d), lambda i, j: (i, j, 0)),
        out_shape=jax.ShapeDtypeStruct((n_part * nb, l, d), _F32),
        compiler_params=_cparams(("parallel", "arbitrary")),
        name="final_norm",
    )(*args)


def _route_plan(eid, rank, tok_ids, counts, tm, nr):
    n_experts = counts.shape[0]
    n_pairs = eid.size
    padded = ((counts + tm - 1) // tm) * tm
    pad_end = jnp.cumsum(padded)
    pad_start = pad_end - padded
    raw_start = jnp.cumsum(counts) - counts
    pos = rank
    for e in range(n_experts):
        pos = pos + jnp.where(eid == e, pad_start[e], 0)
    tile_start = jnp.arange(nr // tm, dtype=jnp.int32) * tm
    tile_expert = jnp.sum(tile_start[:, None] >= pad_end[None, :], axis=-1)
    tile_expert = jnp.minimum(tile_expert, n_experts - 1).astype(jnp.int32)
    _, sorted_tok = lax.sort(((eid * n_pairs + rank).reshape(-1), tok_ids.reshape(-1)),
                             num_keys=1, is_stable=False)
    sel = tile_expert[:, None] == jnp.arange(n_experts, dtype=jnp.int32)[None, :]
    per_tile = lambda v: jnp.repeat(jnp.sum(jnp.where(sel, v[None, :], 0), axis=-1), tm)
    off = jnp.arange(nr, dtype=jnp.int32) - per_tile(pad_start)
    valid = off < per_tile(counts)
    j = jnp.clip(per_tile(raw_start) + off, 0, n_pairs - 1)
    filler = jnp.arange(nr, dtype=jnp.int32) % (n_pairs // 2)
    row_token = jnp.where(valid, sorted_tok.at[j].get(mode="promise_in_bounds"), filler)
    n_used = (pad_end[-1:] // tm).astype(jnp.int32)
    return pos, tile_expert, n_used, row_token


def kernel(x, c, ctx, c_ctx, w_ada, b_ada, norm1_g, norm2_g, w_in, s5_lam_re, s5_lam_im, s5_b_re, s5_b_im, s5_c_re, s5_c_im, s5_log_step, s5_d, s5_w_glu, fourier_w, mix_norm_s5_g, mix_norm_f_g, w_out, moe_w_group, moe_b_group, moe_w_router, moe_b_router, moe_w_gate, moe_w_up, moe_w_down, final_norm_g):
    b, l, d = x.shape
    lc = ctx.shape[1]
    depth = w_ada.shape[0]
    assert depth == 1 and l == FFT_R * FFT_R and lc % S5_CHUNK == 0 and b % 8 == 0
    ds5 = s5_d.shape[-1]
    df = w_in.shape[-1] - ds5
    n_exp = moe_w_gate.shape[1]
    row = lambda a: a.reshape(1, -1)

    cond = jnp.concatenate([c, c_ctx[None, :], jnp.zeros((7, d), _F32)], axis=0)
    mod = _adaln(cond, w_ada[0], b_ada[0])
    sh1, sc1, g1, sh2, sc2, g2 = [mod[:b, i * d:(i + 1) * d].reshape(b, 1, d) for i in range(6)]
    csh1 = mod[b:b + 1, 0:d]
    csc1 = mod[b:b + 1, d:2 * d]

    w_in_bf = w_in[0].astype(_BF)
    m1 = jnp.asarray(_fft_stage1_matrix()).astype(_BF)
    m2 = jnp.asarray(_fft_stage2_matrices()).astype(_BF)
    perm = jnp.asarray(_chunk_row_perm())
    zs, gc = _inproj(x, sh1, sc1, row(norm1_g[0]), w_in_bf, m1)
    u_ctx = _inproj_ctx(ctx, csh1, csc1, row(norm1_g[0]), w_in_bf[:, :ds5], perm)
    u_lat = _s5_pack(zs, perm)

    fg = df // FOURIER_GROUPS
    cc = np.arange(fg)
    ang = 2.0 * np.pi * ((cc[:, None] * cc[None, :]) % fg) / fg
    scale = 1.0 / math.sqrt(l * fg)
    cw = jnp.einsum("cm,gmd->gcd", jnp.asarray(np.cos(ang) * scale, _F32), fourier_w[0], precision=_HI)
    sw = jnp.einsum("cm,gmd->gcd", jnp.asarray(np.sin(ang) * scale, _F32), fourier_w[0], precision=_HI)
    fn = _fft2(gc, m2, cw.astype(_BF), sw.astype(_BF), row(mix_norm_f_g[0]))

    ws, wy, at = _s5_params(s5_lam_re[0], s5_lam_im[0], s5_b_re[0], s5_b_im[0],
                            s5_c_re[0], s5_c_im[0], s5_log_step[0])
    y_g = _s5(u_ctx, u_lat, ws, wy, at, b)

    n_rt = N_EXPERT_GROUPS * (1 + EXPERTS_PER_GROUP)
    w_rt = jnp.concatenate([moe_w_router[0].reshape(d, -1), moe_w_group[0]], axis=-1)
    w_rt = jnp.pad(w_rt, ((0, 0), (0, ROUTE_LANES - n_rt)))
    w_rt_hi = w_rt.astype(_BF)
    w_rt = jnp.stack([w_rt_hi, (w_rt - w_rt_hi.astype(_F32)).astype(_BF)])
    b_rt = jnp.pad(jnp.concatenate([moe_b_router[0].reshape(-1), moe_b_group[0]]), (0, ROUTE_LANES - n_rt))
    tm = MOE_TM
    nb = b // MOE_PARTS
    n_tok = nb * l
    nr = 2 * n_tok + n_exp * tm
    take_rows = lambda a, idx: a.at[idx].get(mode="promise_in_bounds")
    wglu_bf, wo_bf = s5_w_glu[0].astype(_BF), w_out[0].astype(_BF)
    parts = []
    for part in range(MOE_PARTS):
        x1, tmod, rt, rtt, cnt = _post(x, zs, y_g, fn, row(s5_d[0]), wglu_bf, row(mix_norm_s5_g[0]), wo_bf,
                                       g1, row(norm2_g[0]), sh2, sc2, w_rt, row(b_rt), part * nb, nb)
        rec = rtt.transpose(1, 0, 2).reshape(ROUTE_ROWS, n_tok)
        eid = rec[0:2].astype(jnp.int32)
        rank = rec[4:6].astype(jnp.int32)
        tok_ids = jnp.broadcast_to(jnp.arange(n_tok, dtype=jnp.int32), (2, n_tok))
        pos, tile_expert, n_used, row_token = _route_plan(eid, rank, tok_ids,
                                                          cnt[:, 0].astype(jnp.int32), tm, nr)
        xs = take_rows(tmod.reshape(n_tok, d), row_token)
        ys = _moe(tile_expert, n_used, xs, moe_w_gate[0], moe_w_up[0], moe_w_down[0])
        y0 = take_rows(ys, pos[0]).reshape(nb, l, d)
        y1 = take_rows(ys, pos[1]).reshape(nb, l, d)
        parts.append((x1, y0, y1, rt))
    return _final(parts, g2, row(final_norm_g))
```

```python
import functools
import math

import numpy as np
import jax
import jax.numpy as jnp
from jax import lax
from jax.experimental import pallas as pl
from jax.experimental.pallas import tpu as pltpu

EPS = 1e-6
S5_GROUP = 16
S5_STATE = 64
S5_CHUNK = 16
S5_SCAN_GROUPS = 4
FOURIER_GROUPS = 4
N_EXPERT_GROUPS = 4
EXPERTS_PER_GROUP = 8
FFT_R = 64
FFT_BLK = 8
MOE_TM = 512
MOE_PARTS = 1
ROUTE_LANES = 128
ROUTE_ROWS = 8
VMEM_LIMIT = 56 * 1024 * 1024

_HI = lax.Precision.HIGHEST
_BF = jnp.bfloat16
_F32 = jnp.float32


def _cparams(sem):
    return pltpu.CompilerParams(dimension_semantics=sem, vmem_limit_bytes=VMEM_LIMIT)


def _dot(a, b):
    return jnp.dot(a, b, preferred_element_type=_F32)


def _rms(x, g):
    return x * lax.rsqrt(jnp.mean(x * x, axis=-1, keepdims=True) + EPS) * g


def _adaln_body(c_ref, w_ref, b_ref, o_ref):
    c = c_ref[...]
    a = c * jax.nn.sigmoid(c)
    o_ref[...] = jnp.dot(a, w_ref[...], preferred_element_type=_F32, precision=_HI) + b_ref[...]


def _adaln(cond, w, b):
    m, d = cond.shape
    n = w.shape[1]
    tn = 768
    return pl.pallas_call(
        _adaln_body,
        grid=(n // tn,),
        in_specs=[pl.BlockSpec((m, d), lambda j: (0, 0)),
                  pl.BlockSpec((d, tn), lambda j: (0, j)),
                  pl.BlockSpec((1, tn), lambda j: (0, j))],
        out_specs=pl.BlockSpec((m, tn), lambda j: (0, j)),
        out_shape=jax.ShapeDtypeStruct((m, n), _F32),
        compiler_params=_cparams(("arbitrary",)),
        name="adaln",
    )(cond, w, b.reshape(1, n))


def _inproj_body(x_ref, sh_ref, sc_ref, g_ref, w_ref, m1_ref, zs_ref, gc_ref):
    r, blk, d = x_ref.shape
    x = x_ref[...].reshape(r * blk, d)
    h = _rms(x, g_ref[...]) * (1.0 + sc_ref[...]) + sh_ref[...]
    z = _dot(h.astype(_BF), w_ref[...])
    ds5 = zs_ref.shape[-1]
    zs_ref[...] = z[:, :ds5].reshape(r, blk, ds5)
    v = z[:, ds5:].astype(_BF)
    g1 = _dot(m1_ref[...], v)
    gc_ref[...] = g1.astype(_BF).reshape(r, 2 * blk, v.shape[-1])


def _inproj(x, sh, sc, g, w_bf, m1):
    b, l, d = x.shape
    r, blk = FFT_R, FFT_BLK
    nj = r // blk
    dmix = w_bf.shape[1]
    ds5 = dmix // 2
    df = dmix - ds5
    x4 = x.reshape(b, r, r, d)
    zs, gc = pl.pallas_call(
        _inproj_body,
        grid=(b, nj),
        in_specs=[pl.BlockSpec((None, r, blk, d), lambda i, j: (i, 0, j, 0)),
                  pl.BlockSpec((None, 1, d), lambda i, j: (i, 0, 0)),
                  pl.BlockSpec((None, 1, d), lambda i, j: (i, 0, 0)),
                  pl.BlockSpec((1, d), lambda i, j: (0, 0)),
                  pl.BlockSpec((d, dmix), lambda i, j: (0, 0)),
                  pl.BlockSpec(m1.shape, lambda i, j: (0, 0))],
        out_specs=[pl.BlockSpec((None, r, blk, ds5), lambda i, j: (i, 0, j, 0)),
                   pl.BlockSpec((None, r, 2 * blk, df), lambda i, j: (i, 0, j, 0))],
        out_shape=[jax.ShapeDtypeStruct((b, r, r, ds5), _F32),
                   jax.ShapeDtypeStruct((b, r, 2 * r, df), _BF)],
        compiler_params=_cparams(("parallel", "arbitrary")),
        name="inproj",
    )(x4, sh, sc, g, w_bf, m1)
    return zs.reshape(b, l, ds5), gc


def _seg_transpose(arrs):
    n = len(arrs)
    seg = lax.broadcasted_iota(jnp.int32, arrs[0].shape, 1) // S5_GROUP
    d = n // 2
    while d >= 1:
        keep = (seg & d) == 0
        new = list(arrs)
        for i in range(n):
            if i & d == 0:
                a, b = arrs[i], arrs[i + d]
                new[i] = jnp.where(keep, a, pltpu.roll(b, S5_GROUP * d, 1))
                new[i + d] = jnp.where(keep, pltpu.roll(a, 128 - S5_GROUP * d, 1), b)
        arrs = new
        d //= 2
    return arrs


@functools.lru_cache(maxsize=None)
def _chunk_row_perm():
    t = S5_CHUNK
    p = np.zeros((t * t, t * t), np.float32)
    for c in range(t):
        for tl in range(t):
            p[tl * t + c, c * t + tl] = 1.0
    return p.astype(jnp.bfloat16)


def _pack_chunks(z_bf, p_ref, u_ref):
    t = S5_CHUNK
    nsub = z_bf.shape[0] // (t * t)
    pieces = [_dot(p_ref[...], z_bf[s * t * t:(s + 1) * t * t, :]) for s in range(nsub)]
    a = []
    for tl in range(t):
        rows = [pc[tl * t:(tl + 1) * t, :] for pc in pieces]
        a.append(rows[0] if nsub == 1 else jnp.concatenate(rows, axis=0))
    lanes = 128
    gpb = lanes // S5_GROUP
    for j in range(z_bf.shape[1] // lanes):
        for hi in range(t // gpb):
            outs = _seg_transpose([a[gpb * hi + tlo][:, lanes * j:lanes * (j + 1)] for tlo in range(gpb)])
            for glo in range(gpb):
                u_ref[gpb * j + glo, :, lanes * hi:lanes * (hi + 1)] = outs[glo].astype(u_ref.dtype)


def _inproj_ctx_body(x_ref, sh_ref, sc_ref, g_ref, w_ref, p_ref, u_ref):
    h = _rms(x_ref[...], g_ref[...]) * (1.0 + sc_ref[...]) + sh_ref[...]
    z = _dot(h.astype(_BF), w_ref[...])
    _pack_chunks(z.astype(_BF), p_ref, u_ref)


def _inproj_ctx(ctx, sh, sc, g, w_s5_bf, perm):
    b, lc, d = ctx.shape
    ds5 = w_s5_bf.shape[1]
    n_grp = ds5 // S5_GROUP
    kk = S5_CHUNK * S5_GROUP
    return pl.pallas_call(
        _inproj_ctx_body,
        grid=(b,),
        in_specs=[pl.BlockSpec((None, lc, d), lambda i: (i, 0, 0)),
                  pl.BlockSpec((1, d), lambda i: (0, 0)),
                  pl.BlockSpec((1, d), lambda i: (0, 0)),
                  pl.BlockSpec((1, d), lambda i: (0, 0)),
                  pl.BlockSpec((d, ds5), lambda i: (0, 0)),
                  pl.BlockSpec(perm.shape, lambda i: (0, 0))],
        out_specs=pl.BlockSpec((n_grp, lc // S5_CHUNK, kk), lambda i: (0, 0, i)),
        out_shape=jax.ShapeDtypeStruct((n_grp, lc // S5_CHUNK, b * kk), _BF),
        compiler_params=_cparams(("arbitrary",)),
        name="inproj_ctx",
    )(ctx, sh, sc, g, w_s5_bf, perm)


def _s5_pack_body(z_ref, p_ref, u_ref):
    _pack_chunks(z_ref[...].astype(_BF), p_ref, u_ref)


def _s5_pack(zs, perm, tok=2048):
    b, l, ds5 = zs.shape
    n_grp = ds5 // S5_GROUP
    kk = S5_CHUNK * S5_GROUP
    return pl.pallas_call(
        _s5_pack_body,
        grid=(b, l // tok),
        in_specs=[pl.BlockSpec((None, tok, ds5), lambda i, j: (i, j, 0)),
                  pl.BlockSpec(perm.shape, lambda i, j: (0, 0))],
        out_specs=pl.BlockSpec((n_grp, tok // S5_CHUNK, kk), lambda i, j: (0, j, i)),
        out_shape=jax.ShapeDtypeStruct((n_grp, l // S5_CHUNK, b * kk), _BF),
        compiler_params=_cparams(("parallel", "arbitrary")),
        name="s5_pack",
    )(zs, perm)


@functools.lru_cache(maxsize=None)
def _fft_stage1_matrix():
    r, blk = FFT_R, FFT_BLK
    k1 = np.arange(r)[:, None]
    l1 = np.arange(r)[None, :]
    ang = 2.0 * np.pi * ((k1 * l1) % r) / r
    f = np.stack([np.cos(ang), -np.sin(ang)], axis=1)
    m = np.einsum("kal,pq->kaplq", f, np.eye(blk))
    return m.reshape(r * 2 * blk, r * blk).astype(np.float32)


@functools.lru_cache(maxsize=None)
def _fft_stage2_matrices():
    r, blk = FFT_R, FFT_BLK
    n = r * r
    nt = r // blk
    l2 = np.arange(r)
    k2 = np.arange(r)
    out = np.zeros((nt, 2, r, blk, blk, r // blk, 2, blk), np.float32)
    for i in range(nt):
        for kl in range(blk):
            k1 = blk * i + kl
            ang = 2.0 * np.pi * (((k2[:, None] * l2[None, :] * r) + l2[None, :] * k1) % n) / n
            tr, ti = np.cos(ang), -np.sin(ang)
            tr = tr.reshape(r, r // blk, blk)
            ti = ti.reshape(r, r // blk, blk)
            out[i, 0, :, kl, kl, :, 0, :] = tr
            out[i, 0, :, kl, kl, :, 1, :] = -ti
            out[i, 1, :, kl, kl, :, 0, :] = ti
            out[i, 1, :, kl, kl, :, 1, :] = tr
    return out.reshape(nt, 2 * r * blk, blk * 2 * r)


def _fft2_body(gc_ref, m2_ref, cw_ref, sw_ref, g_ref, o_ref):
    kb, rr, df = gc_ref.shape
    gc = gc_ref[...].reshape(kb * rr, df)
    x = _dot(m2_ref[...], gc)
    half = x.shape[0] // 2
    xr = x[:half].astype(_BF)
    xi = x[half:].astype(_BF)
    ng = cw_ref.shape[0]
    fg = df // ng
    parts = []
    for g in range(ng):
        sl = slice(g * fg, (g + 1) * fg)
        parts.append(_dot(xr[:, sl], cw_ref[g]) + _dot(xi[:, sl], sw_ref[g]))
    f = jnp.concatenate(parts, axis=-1)
    fn = _rms(f, g_ref[...])
    o_ref[...] = fn.reshape(o_ref.shape)


def _fft2(gc, m2, cw, sw, gf):
    b, r, rr, df = gc.shape
    blk = FFT_BLK
    nt = r // blk
    out = pl.pallas_call(
        _fft2_body,
        grid=(nt, b),
        in_specs=[pl.BlockSpec((None, blk, rr, df), lambda j, i: (i, j, 0, 0)),
                  pl.BlockSpec((None,) + m2.shape[1:], lambda j, i: (j, 0, 0)),
                  pl.BlockSpec(cw.shape, lambda j, i: (0, 0, 0)),
                  pl.BlockSpec(sw.shape, lambda j, i: (0, 0, 0)),
                  pl.BlockSpec((1, df), lambda j, i: (0, 0))],
        out_specs=pl.BlockSpec((None, r, blk, df), lambda j, i: (i, 0, j, 0)),
        out_shape=jax.ShapeDtypeStruct((b, r, r, df), _F32),
        compiler_params=_cparams(("parallel", "arbitrary")),
        name="fft2",
    )(gc, m2, cw, sw, gf)
    return out.reshape(b, r * r, df)


def _s5_body(uc_ref, ul_ref, ws_ref, wy_ref, at_ref, y_ref,
             s_re, s_im, ha_re, hb_re, ha_im, hb_im, *, nb):
    p = S5_STATE
    gs, n_ctx, _ = uc_ref.shape
    n_lat = ul_ref.shape[1]
    kk = ws_ref.shape[1]
    for gi in range(gs):
        ws = ws_ref[gi]
        for b in range(nb):
            sl = slice(b * kk, (b + 1) * kk)
            sc = _dot(uc_ref[gi, :, sl], ws)
            sl_ = _dot(ul_ref[gi, :, sl], ws)
            s_re[gi, pl.ds(b, n_ctx, stride=nb), :] = sc[:, :2 * p]
            s_im[gi, pl.ds(b, n_ctx, stride=nb), :] = sc[:, 2 * p:]
            s_re[gi, pl.ds(n_ctx * nb + b, n_lat, stride=nb), :] = sl_[:, :2 * p]
            s_im[gi, pl.ds(n_ctx * nb + b, n_lat, stride=nb), :] = sl_[:, 2 * p:]

    a_re = [at_ref[gi, 0:1, :] for gi in range(gs)]
    a_im = [at_ref[gi, 1:2, :] for gi in range(gs)]
    is_f = lax.broadcasted_iota(jnp.int32, (nb, 2 * p), 1) < p
    n_all = n_ctx + n_lat

    def load(gi, cf, cr):
        rf = pl.multiple_of(cf * nb, nb)
        rr = pl.multiple_of(cr * nb, nb)
        return (jnp.where(is_f, s_re[gi, pl.ds(rf, nb), :], s_re[gi, pl.ds(rr, nb), :]),
                jnp.where(is_f, s_im[gi, pl.ds(rf, nb), :], s_im[gi, pl.ds(rr, nb), :]))

    def advance(gi, h_re, h_im, x_re, x_im):
        return (a_re[gi] * h_re - a_im[gi] * h_im + x_re, a_re[gi] * h_im + a_im[gi] * h_re + x_im)

    def ctx_step(step, carry):
        out = []
        for gi in range(gs):
            x_re, x_im = load(gi, step, n_ctx - 1 - step)
            out.append(advance(gi, *carry[gi], x_re, x_im))
        return tuple(out)

    def lat_step(step, carry):
        rf = pl.multiple_of(step * nb, nb)
        rr = pl.multiple_of((n_lat - 1 - step) * nb, nb)
        out = []
        for gi in range(gs):
            h_re, h_im = carry[gi]
            ha_re[gi, pl.ds(rf, nb), :] = h_re
            hb_re[gi, pl.ds(rr, nb), :] = h_re
            ha_im[gi, pl.ds(rf, nb), :] = h_im
            hb_im[gi, pl.ds(rr, nb), :] = h_im
            x_re, x_im = load(gi, n_ctx + step, n_all - 1 - step)
            out.append(advance(gi, h_re, h_im, x_re, x_im))
        return tuple(out)

    zero = jnp.zeros((nb, 2 * p), _F32)
    carry = lax.fori_loop(0, n_ctx, ctx_step, tuple((zero, zero) for _ in range(gs)), unroll=2)
    lax.fori_loop(0, n_lat, lat_step, carry, unroll=2)

    for gi in range(gs):
        wy = wy_ref[gi]
        for b in range(nb):
            rows = pl.ds(b, n_lat, stride=nb)
            hin = jnp.concatenate([ha_re[gi, rows, :], hb_re[gi, rows, :],
                                   ha_im[gi, rows, :], hb_im[gi, rows, :]], axis=-1)
            lhs = jnp.concatenate([ul_ref[gi, :, b * kk:(b + 1) * kk], hin.astype(_BF)], axis=-1)
            y_ref[gi, :, b * kk:(b + 1) * kk] = _dot(lhs, wy).astype(y_ref.dtype)


def _s5(u_ctx, u_lat, ws, wy, at, nb):
    ng, n_ctx, w = u_ctx.shape
    n_lat = u_lat.shape[1]
    gs = S5_SCAN_GROUPS
    body = functools.partial(_s5_body, nb=nb)
    hs = pltpu.VMEM((gs, n_lat * nb, 2 * S5_STATE), _F32)
    ss = pltpu.VMEM((gs, (n_ctx + n_lat) * nb, 2 * S5_STATE), _F32)
    grp = lambda a: pl.BlockSpec((gs,) + a.shape[1:], lambda g: (g, 0, 0))
    return pl.pallas_call(
        body,
        grid=(ng // gs,),
        in_specs=[grp(u_ctx), grp(u_lat), grp(ws), grp(wy), grp(at)],
        out_specs=pl.BlockSpec((gs, n_lat, w), lambda g: (g, 0, 0)),
        out_shape=jax.ShapeDtypeStruct((ng, n_lat, w), _BF),
        scratch_shapes=[ss, ss, hs, hs, hs, hs],
        compiler_params=_cparams(("parallel",)),
        name="s5_scan",
    )(u_ctx, u_lat, ws, wy, at)


def _s5_param_body(lr_ref, li_ref, ls_ref, btr_ref, bti_ref, cr_ref, ci_ref, ws_ref, wy_ref, at_ref):
    t, hh, p = S5_CHUNK, S5_GROUP, S5_STATE
    kk = t * hh
    lr, li = lr_ref[...], li_ref[...]
    dt = jnp.exp(ls_ref[...])
    ldt, idt = lr * dt, li * dt
    mag = jnp.exp(ldt)
    ab_re, ab_im = mag * jnp.cos(idt), mag * jnp.sin(idt)
    den = lr * lr + li * li
    nr, ni = ab_re - 1.0, ab_im
    q_re = (nr * lr + ni * li) / den
    q_im = (ni * lr - nr * li) / den
    kf = lax.broadcasted_iota(jnp.int32, (2 * t, p), 0).astype(_F32)

    ap_re, ap_im, bb_re, bb_im = [], [], [], []
    for d in range(2):
        pm = jnp.exp(kf * ldt[d:d + 1, :])
        ap_re.append(pm * jnp.cos(kf * idt[d:d + 1, :]))
        ap_im.append(pm * jnp.sin(kf * idt[d:d + 1, :]))
        bb_re.append(q_re[d:d + 1, :] * btr_ref[d] - q_im[d:d + 1, :] * bti_ref[d])
        bb_im.append(q_re[d:d + 1, :] * bti_ref[d] + q_im[d:d + 1, :] * btr_ref[d])

    def rep(a, ks):
        return jnp.concatenate([jnp.broadcast_to(a[k:k + 1, :], (hh, p)) for k in ks], axis=0)

    def tile(a, n):
        return jnp.concatenate([a] * n, axis=0)

    def cmul(xr, xi, yr, yi):
        return xr * yr - xi * yi, xr * yi + xi * yr

    ks_f = list(range(t + 1))
    ks_r = list(range(t, -1, -1))
    caf_re, caf_im = cmul(tile(cr_ref[0], t + 1), tile(ci_ref[0], t + 1), rep(ap_re[0], ks_f), rep(ap_im[0], ks_f))
    car_re, car_im = cmul(tile(cr_ref[1], t + 1), tile(ci_ref[1], t + 1), rep(ap_re[1], ks_r), rep(ap_im[1], ks_r))

    def lag_blocks(bre, bim, ca_re, ca_im):
        lhs = jnp.concatenate([bre, bim], axis=1)
        rhs = jnp.concatenate([ca_re, -ca_im], axis=1)
        return lax.dot_general(lhs, rhs, (((1,), (1,)), ((), ())), precision=_HI,
                               preferred_element_type=_F32)

    w_f = lag_blocks(bb_re[0], bb_im[0], caf_re[:kk], caf_im[:kk])
    w_r = lag_blocks(bb_re[1], bb_im[1], car_re[hh:], car_im[hh:])
    lane = lax.broadcasted_iota(jnp.int32, (hh, kk), 1)
    for tlp in range(t):
        sf = hh * tlp
        sr = hh * (t - 1 - tlp)
        a = w_f if sf == 0 else jnp.where(lane >= sf, pltpu.roll(w_f, sf, 1), 0.0)
        b = w_r if sr == 0 else jnp.where(lane < kk - sr, pltpu.roll(w_r, kk - sr, 1), 0.0)
        wy_ref[hh * tlp:hh * (tlp + 1), :] = (a + b).astype(wy_ref.dtype)

    zeros = jnp.zeros((p, kk), _F32)
    blocks = [caf_re[hh:].T, zeros, zeros, car_re[:kk].T, (-caf_im[hh:]).T, zeros, zeros, (-car_im[:kk]).T]
    for i, blk in enumerate(blocks):
        wy_ref[kk + p * i:kk + p * (i + 1), :] = blk.astype(wy_ref.dtype)

    pf = list(range(t - 1, -1, -1))
    pr = list(range(t))
    f_re, f_im = cmul(rep(ap_re[0], pf), rep(ap_im[0], pf), tile(bb_re[0], t), tile(bb_im[0], t))
    r_re, r_im = cmul(rep(ap_re[1], pr), rep(ap_im[1], pr), tile(bb_re[1], t), tile(bb_im[1], t))
    ws_ref[...] = jnp.concatenate([f_re, r_re, f_im, r_im], axis=1).astype(ws_ref.dtype)

    a_re = jnp.concatenate([ap_re[0][t:t + 1, :], ap_re[1][t:t + 1, :]], axis=1)
    a_im = jnp.concatenate([ap_im[0][t:t + 1, :], ap_im[1][t:t + 1, :]], axis=1)
    row = lax.broadcasted_iota(jnp.int32, at_ref.shape, 0)
    at_ref[...] = jnp.where(row == 0, a_re, jnp.where(row == 1, a_im, 0.0))


def _s5_params(lam_re, lam_im, b_re, b_im, c_re, c_im, log_step):
    _, ng, p = lam_re.shape
    hh = b_re.shape[-1]
    kk = S5_CHUNK * hh
    gd = lambda a: jnp.swapaxes(a.astype(_F32), 0, 1)
    args = (gd(lam_re), gd(lam_im), gd(log_step)[..., None],
            jnp.swapaxes(gd(b_re), 2, 3), jnp.swapaxes(gd(b_im), 2, 3), gd(c_re), gd(c_im))
    spec = lambda a: pl.BlockSpec((None,) + a.shape[1:], lambda g: (g,) + (0,) * (a.ndim - 1))
    return pl.pallas_call(
        _s5_param_body,
        grid=(ng,),
        in_specs=[spec(a) for a in args],
        out_specs=[pl.BlockSpec((None, kk, 4 * p), lambda g: (g, 0, 0)),
                   pl.BlockSpec((None, kk + 8 * p, kk), lambda g: (g, 0, 0)),
                   pl.BlockSpec((None, 8, 2 * p), lambda g: (g, 0, 0))],
        out_shape=[jax.ShapeDtypeStruct((ng, kk, 4 * p), _BF),
                   jax.ShapeDtypeStruct((ng, kk + 8 * p, kk), _BF),
                   jax.ShapeDtypeStruct((ng, 8, 2 * p), _F32)],
        compiler_params=_cparams(("parallel",)),
        name="s5_params",
    )(*args)


def _post_body(x_ref, u_ref, yg_ref, fn_ref, d_ref, wglu_ref, gs_ref, wo_ref, g1_ref,
               n2_ref, sh_ref, sc_ref, wr_ref, br_ref, tri_ref, x1_ref, t_ref, rt_ref, rtt_ref, cnt_ref,
               y_scr, cnt_scr):
    ds5 = u_ref.shape[-1]
    first = (pl.program_id(0) == 0) & (pl.program_id(1) == 0)

    @pl.when(first)
    def _():
        cnt_scr[...] = jnp.zeros_like(cnt_scr)

    t_chunk = S5_CHUNK
    n_chunk = yg_ref.shape[1]
    lanes = 128
    gpb = lanes // S5_GROUP
    for j in range(ds5 // lanes):
        for hi in range(t_chunk // gpb):
            outs = _seg_transpose([yg_ref[gpb * j + glo, :, lanes * hi:lanes * (hi + 1)].astype(_F32)
                                   for glo in range(gpb)])
            for tlo in range(gpb):
                y_scr[j, pl.ds(gpb * hi + tlo, n_chunk, stride=t_chunk), :] = outs[tlo]
    y_s5 = jnp.concatenate([y_scr[j] for j in range(ds5 // lanes)], axis=-1)

    y = y_s5 + d_ref[...] * u_ref[...]
    y = jax.nn.gelu(y, approximate=True)
    y = y * jax.nn.sigmoid(_dot(y.astype(_BF), wglu_ref[...]))
    yn = _rms(y, gs_ref[...]).astype(_BF)
    mix = _dot(yn, wo_ref[0:ds5, :]) + _dot(fn_ref[...].astype(_BF), wo_ref[ds5:, :])
    x1 = x_ref[...] + g1_ref[...] * mix
    x1_ref[...] = x1
    t = _rms(x1, n2_ref[...]) * (1.0 + sc_ref[...]) + sh_ref[...]
    t_hi = t.astype(_BF)
    t_ref[...] = t_hi

    ng, epg = N_EXPERT_GROUPS, EXPERTS_PER_GROUP
    t_lo = (t - t_hi.astype(_F32)).astype(_BF)
    logits = (_dot(t_hi, wr_ref[0]) + _dot(t_lo, wr_ref[0]) + _dot(t_hi, wr_ref[1])) + br_ref[...]
    tm = logits.shape[0]
    lt = logits.T
    neg = jnp.float32(-jnp.inf)
    big = jnp.int32(1 << 20)
    row8 = lax.broadcasted_iota(jnp.int32, (epg, tm), 0)
    amax = lambda v: jnp.max(v, axis=0, keepdims=True)
    first = lambda v, m: jnp.min(jnp.where(v == m, row8, big), axis=0, keepdims=True)
    gt = lt[ng * epg:ng * epg + epg, :]
    gl = jnp.where(row8 < ng, gt, neg)
    gmax = amax(gl)
    gidx = first(gl, gmax)
    gw = 1.0 / jnp.sum(jnp.where(row8 < ng, jnp.exp(gt - gmax), 0.0), axis=0, keepdims=True)
    el = lt[0:epg, :]
    for g in range(1, ng):
        el = jnp.where(gidx == g, lt[g * epg:(g + 1) * epg, :], el)
    v0 = amax(el)
    i0 = first(el, v0)
    el1 = jnp.where(row8 == i0, neg, el)
    v1 = amax(el1)
    i1 = first(el1, v1)
    p0 = 1.0 / (1.0 + jnp.exp(v1 - v0))
    w0 = gw * p0
    w1 = gw * (1.0 - p0)
    e0 = gidx * epg + i0
    e1 = gidx * epg + i1

    rowe = lax.broadcasted_iota(jnp.int32, (ng * epg, tm), 0)
    oh0 = rowe == e0
    oh1 = rowe == e1
    oh = (oh0 | oh1).astype(_F32)
    prefix = _dot(oh.astype(_BF), tri_ref[...]) + cnt_scr[...]
    r0 = jnp.sum(jnp.where(oh0, prefix, 0.0), axis=0, keepdims=True)
    r1 = jnp.sum(jnp.where(oh1, prefix, 0.0), axis=0, keepdims=True)
    cnt = cnt_scr[...] + jnp.sum(oh, axis=1, keepdims=True)
    cnt_scr[...] = cnt
    cnt_ref[...] = cnt

    rec = jnp.concatenate([e0.astype(_F32), e1.astype(_F32), w0, w1, r0, r1,
                           jnp.zeros((ROUTE_ROWS - 6, tm), _F32)], axis=0)
    rtt_ref[...] = rec
    rt_ref[...] = jnp.concatenate([rec, jnp.zeros((ROUTE_LANES - ROUTE_ROWS, tm), _F32)], axis=0).T


@functools.lru_cache(maxsize=None)
def _strict_upper_ones(n):
    return np.triu(np.ones((n, n), np.float32), 1).astype(jnp.bfloat16)


def _post(x, u, y_g, fn, s5_d, wglu_bf, gs, wo_bf, g1, n2, sh2, sc2, wr, br, b0, nb, tm=512):
    b, l, d = x.shape
    ds5 = u.shape[-1]
    n_grp, _, w = y_g.shape
    kk = w // b
    n_exp = N_EXPERT_GROUPS * EXPERTS_PER_GROUP
    tri = jnp.asarray(_strict_upper_ones(tm))
    tok = lambda w: pl.BlockSpec((None, tm, w), lambda i, j: (i + b0, j, 0))
    otok = lambda w: pl.BlockSpec((None, tm, w), lambda i, j: (i, j, 0))
    per_b = pl.BlockSpec((None, 1, d), lambda i, j: (i + b0, 0, 0))
    full = lambda a: pl.BlockSpec(a.shape, lambda i, j: (0,) * a.ndim)
    ygs = pl.BlockSpec((n_grp, tm // S5_CHUNK, kk), lambda i, j: (0, j, i + b0))
    return pl.pallas_call(
        _post_body,
        grid=(nb, l // tm),
        in_specs=[tok(d), tok(ds5), ygs, tok(fn.shape[-1]), full(s5_d), full(wglu_bf),
                  full(gs), full(wo_bf), per_b, full(n2), per_b, per_b, full(wr), full(br), full(tri)],
        out_specs=[otok(d), otok(d), otok(ROUTE_LANES),
                   pl.BlockSpec((None, ROUTE_ROWS, tm), lambda i, j: (i, 0, j)),
                   pl.BlockSpec((n_exp, 1), lambda i, j: (0, 0))],
        out_shape=[jax.ShapeDtypeStruct((nb, l, d), _F32),
                   jax.ShapeDtypeStruct((nb, l, d), _BF),
                   jax.ShapeDtypeStruct((nb, l, ROUTE_LANES), _F32),
                   jax.ShapeDtypeStruct((nb, ROUTE_ROWS, l), _F32),
                   jax.ShapeDtypeStruct((n_exp, 1), _F32)],
        scratch_shapes=[pltpu.VMEM((ds5 // 128, tm, 128), _F32),
                        pltpu.VMEM((n_exp, 1), _F32)],
        compiler_params=_cparams(("arbitrary", "arbitrary")),
        name="post_mixer",
    )(x, u, y_g, fn, s5_d, wglu_bf, gs, wo_bf, g1, n2, sh2, sc2, wr, br, tri)


def _moe_body(te_ref, nt_ref, xs_ref, wg_ref, wu_ref, wd_ref, o_ref, wgu_bf, wd_bf):
    i = pl.program_id(0)
    de = wg_ref.shape[-1]
    used = i < nt_ref[0]
    new_expert = (i == 0) | (te_ref[i] != te_ref[jnp.maximum(i - 1, 0)])

    @pl.when(used & new_expert)
    def _():
        wgu_bf[:, 0:de] = wg_ref[...].astype(_BF)
        wgu_bf[:, de:2 * de] = wu_ref[...].astype(_BF)
        wd_bf[...] = wd_ref[...].astype(_BF)

    @pl.when(used)
    def _():
        h = _dot(xs_ref[...], wgu_bf[...])
        hg, hu = h[:, 0:de], h[:, de:2 * de]
        a = hg * jax.nn.sigmoid(hg) * hu
        o_ref[...] = _dot(a.astype(_BF), wd_bf[...]).astype(o_ref.dtype)

    @pl.when(jnp.logical_not(used))
    def _():
        o_ref[...] = jnp.zeros_like(o_ref)


def _moe(tile_expert, n_tiles_used, xs, wg, wu, wd):
    nr, d = xs.shape
    tm = MOE_TM
    de = wg.shape[-1]
    grid_spec = pltpu.PrefetchScalarGridSpec(
        num_scalar_prefetch=2,
        grid=(nr // tm,),
        in_specs=[pl.BlockSpec((tm, d), lambda i, te, nt: (i, 0)),
                  pl.BlockSpec((None, d, de), lambda i, te, nt: (te[i], 0, 0)),
                  pl.BlockSpec((None, d, de), lambda i, te, nt: (te[i], 0, 0)),
                  pl.BlockSpec((None, de, d), lambda i, te, nt: (te[i], 0, 0))],
        out_specs=pl.BlockSpec((tm, d), lambda i, te, nt: (i, 0)),
        scratch_shapes=[pltpu.VMEM((d, 2 * de), _BF), pltpu.VMEM((de, d), _BF)],
    )
    return pl.pallas_call(
        _moe_body,
        grid_spec=grid_spec,
        out_shape=jax.ShapeDtypeStruct((nr, d), _BF),
        compiler_params=_cparams(("arbitrary",)),
        name="moe_experts",
    )(tile_expert, n_tiles_used, xs, wg, wu, wd)


def _final_body(g2_ref, gf_ref, *refs, n_part, nb):
    o_ref = refs[-1]
    i = pl.program_id(0)
    for part in range(n_part):
        x1_ref, y0_ref, y1_ref, rt_ref = refs[4 * part:4 * part + 4]

        @pl.when((i >= part * nb) & (i < (part + 1) * nb))
        def _():
            w0 = rt_ref[:, 2:3]
            w1 = rt_ref[:, 3:4]
            m = w0 * y0_ref[...].astype(_F32) + w1 * y1_ref[...].astype(_F32)
            o_ref[...] = _rms(x1_ref[...] + g2_ref[...] * m, gf_ref[...])


def _final(parts, g2, gf, tm=512):
    n_part = len(parts)
    nb, l, d = parts[0][0].shape
    nj = l // tm
    in_specs = [pl.BlockSpec((None, 1, d), lambda i, j: (i, 0, 0)),
                pl.BlockSpec((1, d), lambda i, j: (0, 0))]
    args = [g2, gf]
    for part, arrs in enumerate(parts):
        def imap(i, j, part=part):
            own = (i >= part * nb) & (i < (part + 1) * nb)
            return (jnp.clip(i - part * nb, 0, nb - 1), jnp.where(own, j, jnp.where(i < part * nb, 0, nj - 1)), 0)
        for a in arrs:
            in_specs.append(pl.BlockSpec((None, tm, a.shape[-1]), imap))
            args.append(a)
    return pl.pallas_call(
        functools.partial(_final_body, n_part=n_part, nb=nb),
        grid=(n_part * nb, nj),
        in_specs=in_specs,
        out_specs=pl.BlockSpec((None, tm, d), lambda i, j: (i, j, 0)),
        out_shape=jax.ShapeDtypeStruct((n_part * nb, l, d), _F32),
        compiler_params=_cparams(("parallel", "arbitrary")),
        name="final_norm",
    )(*args)


def _pair_rows_body(ps_ref, eid_ref, rank_ref, pos_ref):
    eid = eid_ref[...]
    pos = rank_ref[...]
    for e in range(ps_ref.shape[0]):
        pos = pos + jnp.where(eid == e, ps_ref[e], 0)
    pos_ref[...] = pos


def _pair_rows(pad_start, eid, rank):
    shape = eid.shape
    rows = 16
    e2, r2 = eid.reshape(rows, -1), rank.reshape(rows, -1)
    full = pl.BlockSpec(e2.shape, lambda i, ps: (0, 0))
    grid_spec = pltpu.PrefetchScalarGridSpec(num_scalar_prefetch=1, grid=(1,), in_specs=[full, full],
                                             out_specs=full)
    pos = pl.pallas_call(
        _pair_rows_body,
        grid_spec=grid_spec,
        out_shape=jax.ShapeDtypeStruct(e2.shape, jnp.int32),
        compiler_params=_cparams(("arbitrary",)),
        name="pair_rows",
    )(pad_start, e2, r2)
    return pos.reshape(shape)


def _route_plan(eid, rank, tok_ids, counts, tm, nr):
    n_experts = counts.shape[0]
    n_pairs = eid.size
    padded = ((counts + tm - 1) // tm) * tm
    pad_end = jnp.cumsum(padded)
    pad_start = pad_end - padded
    raw_start = jnp.cumsum(counts) - counts
    pos = _pair_rows(pad_start.astype(jnp.int32), eid, rank)
    tile_start =jnp.arange(nr // tm, dtype=jnp.int32) * tm
    tile_expert = jnp.sum(tile_start[:, None] >= pad_end[None, :], axis=-1)
    tile_expert = jnp.minimum(tile_expert, n_experts - 1).astype(jnp.int32)
    _, sorted_tok = lax.sort(((eid * n_pairs + rank).reshape(-1), tok_ids.reshape(-1)),
                             num_keys=1, is_stable=False)
    sel = tile_expert[:, None] == jnp.arange(n_experts, dtype=jnp.int32)[None, :]
    per_tile = lambda v: jnp.repeat(jnp.sum(jnp.where(sel, v[None, :], 0), axis=-1), tm)
    off = jnp.arange(nr, dtype=jnp.int32) - per_tile(pad_start)
    valid = off < per_tile(counts)
    j = jnp.clip(per_tile(raw_start) + off, 0, n_pairs - 1)
    filler = jnp.arange(nr, dtype=jnp.int32) % (n_pairs // 2)
    row_token = jnp.where(valid, sorted_tok.at[j].get(mode="promise_in_bounds"), filler)
    n_used = (pad_end[-1:] // tm).astype(jnp.int32)
    return pos, tile_expert, n_used, row_token


def kernel(x, c, ctx, c_ctx, w_ada, b_ada, norm1_g, norm2_g, w_in, s5_lam_re, s5_lam_im, s5_b_re, s5_b_im, s5_c_re, s5_c_im, s5_log_step, s5_d, s5_w_glu, fourier_w, mix_norm_s5_g, mix_norm_f_g, w_out, moe_w_group, moe_b_group, moe_w_router, moe_b_router, moe_w_gate, moe_w_up, moe_w_down, final_norm_g):
    b, l, d = x.shape
    lc = ctx.shape[1]
    depth = w_ada.shape[0]
    assert depth == 1 and l == FFT_R * FFT_R and lc % S5_CHUNK == 0 and b % 8 == 0
    ds5 = s5_d.shape[-1]
    df = w_in.shape[-1] - ds5
    n_exp = moe_w_gate.shape[1]
    row = lambda a: a.reshape(1, -1)

    cond = jnp.concatenate([c, c_ctx[None, :], jnp.zeros((7, d), _F32)], axis=0)
    mod = _adaln(cond, w_ada[0], b_ada[0])
    sh1, sc1, g1, sh2, sc2, g2 = [mod[:b, i * d:(i + 1) * d].reshape(b, 1, d) for i in range(6)]
    csh1 = mod[b:b + 1, 0:d]
    csc1 = mod[b:b + 1, d:2 * d]

    w_in_bf = w_in[0].astype(_BF)
    m1 = jnp.asarray(_fft_stage1_matrix()).astype(_BF)
    m2 = jnp.asarray(_fft_stage2_matrices()).astype(_BF)
    perm = jnp.asarray(_chunk_row_perm())
    zs, gc = _inproj(x, sh1, sc1, row(norm1_g[0]), w_in_bf, m1)
    u_ctx = _inproj_ctx(ctx, csh1, csc1, row(norm1_g[0]), w_in_bf[:, :ds5], perm)
    u_lat = _s5_pack(zs, perm)

    fg = df // FOURIER_GROUPS
    cc = np.arange(fg)
    ang = 2.0 * np.pi * ((cc[:, None] * cc[None, :]) % fg) / fg
    scale = 1.0 / math.sqrt(l * fg)
    cw = jnp.einsum("cm,gmd->gcd", jnp.asarray(np.cos(ang) * scale, _F32), fourier_w[0], precision=_HI)
    sw = jnp.einsum("cm,gmd->gcd", jnp.asarray(np.sin(ang) * scale, _F32), fourier_w[0], precision=_HI)
    fn = _fft2(gc, m2, cw.astype(_BF), sw.astype(_BF), row(mix_norm_f_g[0]))

    ws, wy, at = _s5_params(s5_lam_re[0], s5_lam_im[0], s5_b_re[0], s5_b_im[0],
                            s5_c_re[0], s5_c_im[0], s5_log_step[0])
    y_g = _s5(u_ctx, u_lat, ws, wy, at, b)

    n_rt = N_EXPERT_GROUPS * (1 + EXPERTS_PER_GROUP)
    w_rt = jnp.concatenate([moe_w_router[0].reshape(d, -1), moe_w_group[0]], axis=-1)
    w_rt = jnp.pad(w_rt, ((0, 0), (0, ROUTE_LANES - n_rt)))
    w_rt_hi = w_rt.astype(_BF)
    w_rt = jnp.stack([w_rt_hi, (w_rt - w_rt_hi.astype(_F32)).astype(_BF)])
    b_rt = jnp.pad(jnp.concatenate([moe_b_router[0].reshape(-1), moe_b_group[0]]), (0, ROUTE_LANES - n_rt))
    tm = MOE_TM
    nb = b // MOE_PARTS
    n_tok = nb * l
    nr = 2 * n_tok + n_exp * tm
    take_rows = lambda a, idx: a.at[idx].get(mode="promise_in_bounds")
    wglu_bf, wo_bf = s5_w_glu[0].astype(_BF), w_out[0].astype(_BF)
    parts = []
    for part in range(MOE_PARTS):
        x1, tmod, rt, rtt, cnt = _post(x, zs, y_g, fn, row(s5_d[0]), wglu_bf, row(mix_norm_s5_g[0]), wo_bf,
                                       g1, row(norm2_g[0]), sh2, sc2, w_rt, row(b_rt), part * nb, nb)
        rec = rtt.transpose(1, 0, 2).reshape(ROUTE_ROWS, n_tok)
        eid = rec[0:2].astype(jnp.int32)
        rank = rec[4:6].astype(jnp.int32)
        tok_ids = jnp.broadcast_to(jnp.arange(n_tok, dtype=jnp.int32), (2, n_tok))
        pos, tile_expert, n_used, row_token = _route_plan(eid, rank, tok_ids,
                                                          cnt[:, 0].astype(jnp.int32), tm, nr)
        xs = take_rows(tmod.reshape(n_tok, d), row_token)
        ys = _moe(tile_expert, n_used, xs, moe_w_gate[0], moe_w_up[0], moe_w_down[0])
        y0 = take_rows(ys, pos[0]).reshape(nb, l, d)
        y1 = take_rows(ys, pos[1]).reshape(nb, l, d)
        parts.append((x1, y0, y1, rt))
    return _final(parts, g2, row(final_norm_g))
```

```python
import functools
import math

import numpy as np
import jax
import jax.numpy as jnp
from jax import lax
from jax.experimental import pallas as pl
from jax.experimental.pallas import tpu as pltpu

EPS = 1e-6
S5_GROUP = 16
S5_STATE = 64
S5_CHUNK = 16
S5_SCAN_GROUPS = 4
FOURIER_GROUPS = 4
N_EXPERT_GROUPS = 4
EXPERTS_PER_GROUP = 8
FFT_R = 64
FFT_BLK = 8
MOE_TM = 512
MOE_PARTS = 1
ROUTE_LANES = 128
ROUTE_ROWS = 8
VMEM_LIMIT = 56 * 1024 * 1024

_HI = lax.Precision.HIGHEST
_BF = jnp.bfloat16
_F32 = jnp.float32


def _cparams(sem):
    return pltpu.CompilerParams(dimension_semantics=sem, vmem_limit_bytes=VMEM_LIMIT)


def _dot(a, b):
    return jnp.dot(a, b, preferred_element_type=_F32)


def _rms(x, g):
    return x * lax.rsqrt(jnp.mean(x * x, axis=-1, keepdims=True) + EPS) * g


def _adaln_body(c_ref, w_ref, b_ref, o_ref):
    c = c_ref[...]
    a = c * jax.nn.sigmoid(c)
    o_ref[...] = jnp.dot(a, w_ref[...], preferred_element_type=_F32, precision=_HI) + b_ref[...]


def _adaln(cond, w, b):
    m, d = cond.shape
    n = w.shape[1]
    tn = 768
    return pl.pallas_call(
        _adaln_body,
        grid=(n // tn,),
        in_specs=[pl.BlockSpec((m, d), lambda j: (0, 0)),
                  pl.BlockSpec((d, tn), lambda j: (0, j)),
                  pl.BlockSpec((1, tn), lambda j: (0, j))],
        out_specs=pl.BlockSpec((m, tn), lambda j: (0, j)),
        out_shape=jax.ShapeDtypeStruct((m, n), _F32),
        compiler_params=_cparams(("arbitrary",)),
        name="adaln",
    )(cond, w, b.reshape(1, n))


def _inproj_body(x_ref, sh_ref, sc_ref, g_ref, w_ref, m1_ref, zs_ref, gc_ref):
    r, blk, d = x_ref.shape
    x = x_ref[...].reshape(r * blk, d)
    h = _rms(x, g_ref[...]) * (1.0 + sc_ref[...]) + sh_ref[...]
    z = _dot(h.astype(_BF), w_ref[...])
    ds5 = zs_ref.shape[-1]
    zs_ref[...] = z[:, :ds5].reshape(r, blk, ds5)
    v = z[:, ds5:].astype(_BF)
    g1 = _dot(m1_ref[...], v)
    gc_ref[...] = g1.astype(_BF).reshape(r, 2 * blk, v.shape[-1])


def _inproj(x, sh, sc, g, w_bf, m1):
    b, l, d = x.shape
    r, blk = FFT_R, FFT_BLK
    nj = r // blk
    dmix = w_bf.shape[1]
    ds5 = dmix // 2
    df = dmix - ds5
    x4 = x.reshape(b, r, r, d)
    zs, gc = pl.pallas_call(
        _inproj_body,
        grid=(b, nj),
        in_specs=[pl.BlockSpec((None, r, blk, d), lambda i, j: (i, 0, j, 0)),
                  pl.BlockSpec((None, 1, d), lambda i, j: (i, 0, 0)),
                  pl.BlockSpec((None, 1, d), lambda i, j: (i, 0, 0)),
                  pl.BlockSpec((1, d), lambda i, j: (0, 0)),
                  pl.BlockSpec((d, dmix), lambda i, j: (0, 0)),
                  pl.BlockSpec(m1.shape, lambda i, j: (0, 0))],
        out_specs=[pl.BlockSpec((None, r, blk, ds5), lambda i, j: (i, 0, j, 0)),
                   pl.BlockSpec((None, r, 2 * blk, df), lambda i, j: (i, 0, j, 0))],
        out_shape=[jax.ShapeDtypeStruct((b, r, r, ds5), _F32),
                   jax.ShapeDtypeStruct((b, r, 2 * r, df), _BF)],
        compiler_params=_cparams(("parallel", "arbitrary")),
        name="inproj",
    )(x4, sh, sc, g, w_bf, m1)
    return zs.reshape(b, l, ds5), gc


def _seg_transpose(arrs):
    n = len(arrs)
    seg = lax.broadcasted_iota(jnp.int32, arrs[0].shape, 1) // S5_GROUP
    d = n // 2
    while d >= 1:
        keep = (seg & d) == 0
        new = list(arrs)
        for i in range(n):
            if i & d == 0:
                a, b = arrs[i], arrs[i + d]
                new[i] = jnp.where(keep, a, pltpu.roll(b, S5_GROUP * d, 1))
                new[i + d] = jnp.where(keep, pltpu.roll(a, 128 - S5_GROUP * d, 1), b)
        arrs = new
        d //= 2
    return arrs


@functools.lru_cache(maxsize=None)
def _chunk_row_perm():
    t = S5_CHUNK
    p = np.zeros((t * t, t * t), np.float32)
    for c in range(t):
        for tl in range(t):
            p[tl * t + c, c * t + tl] = 1.0
    return p.astype(jnp.bfloat16)


def _pack_chunks(z_bf, p_ref, u_ref):
    t = S5_CHUNK
    nsub = z_bf.shape[0] // (t * t)
    pieces = [_dot(p_ref[...], z_bf[s * t * t:(s + 1) * t * t, :]) for s in range(nsub)]
    a = []
    for tl in range(t):
        rows = [pc[tl * t:(tl + 1) * t, :] for pc in pieces]
        a.append(rows[0] if nsub == 1 else jnp.concatenate(rows, axis=0))
    lanes = 128
    gpb = lanes // S5_GROUP
    for j in range(z_bf.shape[1] // lanes):
        for hi in range(t // gpb):
            outs = _seg_transpose([a[gpb * hi + tlo][:, lanes * j:lanes * (j + 1)] for tlo in range(gpb)])
            for glo in range(gpb):
                u_ref[gpb * j + glo, :, lanes * hi:lanes * (hi + 1)] = outs[glo].astype(u_ref.dtype)


def _inproj_ctx_body(x_ref, sh_ref, sc_ref, g_ref, w_ref, p_ref, u_ref):
    h = _rms(x_ref[...], g_ref[...]) * (1.0 + sc_ref[...]) + sh_ref[...]
    z = _dot(h.astype(_BF), w_ref[...])
    _pack_chunks(z.astype(_BF), p_ref, u_ref)


def _inproj_ctx(ctx, sh, sc, g, w_s5_bf, perm):
    b, lc, d = ctx.shape
    ds5 = w_s5_bf.shape[1]
    n_grp = ds5 // S5_GROUP
    kk = S5_CHUNK * S5_GROUP
    return pl.pallas_call(
        _inproj_ctx_body,
        grid=(b,),
        in_specs=[pl.BlockSpec((None, lc, d), lambda i: (i, 0, 0)),
                  pl.BlockSpec((1, d), lambda i: (0, 0)),
                  pl.BlockSpec((1, d), lambda i: (0, 0)),
                  pl.BlockSpec((1, d), lambda i: (0, 0)),
                  pl.BlockSpec((d, ds5), lambda i: (0, 0)),
                  pl.BlockSpec(perm.shape, lambda i: (0, 0))],
        out_specs=pl.BlockSpec((n_grp, lc // S5_CHUNK, kk), lambda i: (0, 0, i)),
        out_shape=jax.ShapeDtypeStruct((n_grp, lc // S5_CHUNK, b * kk), _BF),
        compiler_params=_cparams(("arbitrary",)),
        name="inproj_ctx",
    )(ctx, sh, sc, g, w_s5_bf, perm)


def _s5_pack_body(z_ref, p_ref, u_ref):
    _pack_chunks(z_ref[...].astype(_BF), p_ref, u_ref)


def _s5_pack(zs, perm, tok=2048):
    b, l, ds5 = zs.shape
    n_grp = ds5 // S5_GROUP
    kk = S5_CHUNK * S5_GROUP
    return pl.pallas_call(
        _s5_pack_body,
        grid=(b, l // tok),
        in_specs=[pl.BlockSpec((None, tok, ds5), lambda i, j: (i, j, 0)),
                  pl.BlockSpec(perm.shape, lambda i, j: (0, 0))],
        out_specs=pl.BlockSpec((n_grp, tok // S5_CHUNK, kk), lambda i, j: (0, j, i)),
        out_shape=jax.ShapeDtypeStruct((n_grp, l // S5_CHUNK, b * kk), _BF),
        compiler_params=_cparams(("parallel", "arbitrary")),
        name="s5_pack",
    )(zs, perm)


@functools.lru_cache(maxsize=None)
def _fft_stage1_matrix():
    r, blk = FFT_R, FFT_BLK
    k1 = np.arange(r)[:, None]
    l1 = np.arange(r)[None, :]
    ang = 2.0 * np.pi * ((k1 * l1) % r) / r
    f = np.stack([np.cos(ang), -np.sin(ang)], axis=1)
    m = np.einsum("kal,pq->kaplq", f, np.eye(blk))
    return m.reshape(r * 2 * blk, r * blk).astype(np.float32)


@functools.lru_cache(maxsize=None)
def _fft_stage2_matrices():
    r, blk = FFT_R, FFT_BLK
    n = r * r
    nt = r // blk
    l2 = np.arange(r)
    k2 = np.arange(r)
    out = np.zeros((nt, 2, r, blk, blk, r // blk, 2, blk), np.float32)
    for i in range(nt):
        for kl in range(blk):
            k1 = blk * i + kl
            ang = 2.0 * np.pi * (((k2[:, None] * l2[None, :] * r) + l2[None, :] * k1) % n) / n
            tr, ti = np.cos(ang), -np.sin(ang)
            tr = tr.reshape(r, r // blk, blk)
            ti = ti.reshape(r, r // blk, blk)
            out[i, 0, :, kl, kl, :, 0, :] = tr
            out[i, 0, :, kl, kl, :, 1, :] = -ti
            out[i, 1, :, kl, kl, :, 0, :] = ti
            out[i, 1, :, kl, kl, :, 1, :] = tr
    return out.reshape(nt, 2 * r * blk, blk * 2 * r)


def _fft2_body(gc_ref, m2_ref, cw_ref, sw_ref, g_ref, o_ref):
    kb, rr, df = gc_ref.shape
    gc = gc_ref[...].reshape(kb * rr, df)
    x = _dot(m2_ref[...], gc)
    half = x.shape[0] // 2
    xr = x[:half].astype(_BF)
    xi = x[half:].astype(_BF)
    ng = cw_ref.shape[0]
    fg = df // ng
    parts = []
    for g in range(ng):
        sl = slice(g * fg, (g + 1) * fg)
        parts.append(_dot(xr[:, sl], cw_ref[g]) + _dot(xi[:, sl], sw_ref[g]))
    f = jnp.concatenate(parts, axis=-1)
    fn = _rms(f, g_ref[...])
    o_ref[...] = fn.reshape(o_ref.shape)


def _fft2(gc, m2, cw, sw, gf):
    b, r, rr, df = gc.shape
    blk = FFT_BLK
    nt = r // blk
    out = pl.pallas_call(
        _fft2_body,
        grid=(nt, b),
        in_specs=[pl.BlockSpec((None, blk, rr, df), lambda j, i: (i, j, 0, 0)),
                  pl.BlockSpec((None,) + m2.shape[1:], lambda j, i: (j, 0, 0)),
                  pl.BlockSpec(cw.shape, lambda j, i: (0, 0, 0)),
                  pl.BlockSpec(sw.shape, lambda j, i: (0, 0, 0)),
                  pl.BlockSpec((1, df), lambda j, i: (0, 0))],
        out_specs=pl.BlockSpec((None, r, blk, df), lambda j, i: (i, 0, j, 0)),
        out_shape=jax.ShapeDtypeStruct((b, r, r, df), _F32),
        compiler_params=_cparams(("parallel", "arbitrary")),
        name="fft2",
    )(gc, m2, cw, sw, gf)
    return out.reshape(b, r * r, df)


def _s5_body(uc_ref, ul_ref, ws_ref, wy_ref, at_ref, y_ref,
             s_re, s_im, ha_re, hb_re, ha_im, hb_im, *, nb):
    p = S5_STATE
    gs, n_ctx, _ = uc_ref.shape
    n_lat = ul_ref.shape[1]
    kk = ws_ref.shape[1]
    for gi in range(gs):
        ws = ws_ref[gi]
        for b in range(nb):
            sl = slice(b * kk, (b + 1) * kk)
            sc = _dot(uc_ref[gi, :, sl], ws)
            sl_ = _dot(ul_ref[gi, :, sl], ws)
            s_re[gi, pl.ds(b, n_ctx, stride=nb), :] = sc[:, :2 * p]
            s_im[gi, pl.ds(b, n_ctx, stride=nb), :] = sc[:, 2 * p:]
            s_re[gi, pl.ds(n_ctx * nb + b, n_lat, stride=nb), :] = sl_[:, :2 * p]
            s_im[gi, pl.ds(n_ctx * nb + b, n_lat, stride=nb), :] = sl_[:, 2 * p:]

    a_re = [at_ref[gi, 0:1, :] for gi in range(gs)]
    a_im = [at_ref[gi, 1:2, :] for gi in range(gs)]
    is_f = lax.broadcasted_iota(jnp.int32, (nb, 2 * p), 1) < p
    n_all = n_ctx + n_lat

    def load(gi, cf, cr):
        rf = pl.multiple_of(cf * nb, nb)
        rr = pl.multiple_of(cr * nb, nb)
        return (jnp.where(is_f, s_re[gi, pl.ds(rf, nb), :], s_re[gi, pl.ds(rr, nb), :]),
                jnp.where(is_f, s_im[gi, pl.ds(rf, nb), :], s_im[gi, pl.ds(rr, nb), :]))

    def advance(gi, h_re, h_im, x_re, x_im):
        return (a_re[gi] * h_re - a_im[gi] * h_im + x_re, a_re[gi] * h_im + a_im[gi] * h_re + x_im)

    def ctx_step(step, carry):
        out = []
        for gi in range(gs):
            x_re, x_im = load(gi, step, n_ctx - 1 - step)
            out.append(advance(gi, *carry[gi], x_re, x_im))
        return tuple(out)

    def lat_step(step, carry):
        rf = pl.multiple_of(step * nb, nb)
        rr = pl.multiple_of((n_lat - 1 - step) * nb, nb)
        out = []
        for gi in range(gs):
            h_re, h_im = carry[gi]
            ha_re[gi, pl.ds(rf, nb), :] = h_re
            hb_re[gi, pl.ds(rr, nb), :] = h_re
            ha_im[gi, pl.ds(rf, nb), :] = h_im
            hb_im[gi, pl.ds(rr, nb), :] = h_im
            x_re, x_im = load(gi, n_ctx + step, n_all - 1 - step)
            out.append(advance(gi, h_re, h_im, x_re, x_im))
        return tuple(out)

    zero = jnp.zeros((nb, 2 * p), _F32)
    carry = lax.fori_loop(0, n_ctx, ctx_step, tuple((zero, zero) for _ in range(gs)), unroll=4)
    lax.fori_loop(0, n_lat, lat_step, carry, unroll=4)

    for gi in range(gs):
        wy = wy_ref[gi]
        for b in range(nb):
            rows = pl.ds(b, n_lat, stride=nb)
            hin = jnp.concatenate([ha_re[gi, rows, :], hb_re[gi, rows, :],
                                   ha_im[gi, rows, :], hb_im[gi, rows, :]], axis=-1)
            lhs = jnp.concatenate([ul_ref[gi, :, b * kk:(b + 1) * kk], hin.astype(_BF)], axis=-1)
            y_ref[gi, :, b * kk:(b + 1) * kk] = _dot(lhs, wy).astype(y_ref.dtype)


def _s5(u_ctx, u_lat, ws, wy, at, nb):
    ng, n_ctx, w = u_ctx.shape
    n_lat = u_lat.shape[1]
    gs = S5_SCAN_GROUPS
    body = functools.partial(_s5_body, nb=nb)
    hs = pltpu.VMEM((gs, n_lat * nb, 2 * S5_STATE), _F32)
    ss = pltpu.VMEM((gs, (n_ctx + n_lat) * nb, 2 * S5_STATE), _F32)
    grp = lambda a: pl.BlockSpec((gs,) + a.shape[1:], lambda g: (g, 0, 0))
    return pl.pallas_call(
        body,
        grid=(ng // gs,),
        in_specs=[grp(u_ctx), grp(u_lat), grp(ws), grp(wy), grp(at)],
        out_specs=pl.BlockSpec((gs, n_lat, w), lambda g: (g, 0, 0)),
        out_shape=jax.ShapeDtypeStruct((ng, n_lat, w), _BF),
        scratch_shapes=[ss, ss, hs, hs, hs, hs],
        compiler_params=_cparams(("parallel",)),
        name="s5_scan",
    )(u_ctx, u_lat, ws, wy, at)


def _s5_param_body(lr_ref, li_ref, ls_ref, btr_ref, bti_ref, cr_ref, ci_ref, ws_ref, wy_ref, at_ref):
    t, hh, p = S5_CHUNK, S5_GROUP, S5_STATE
    kk = t * hh
    lr, li = lr_ref[...], li_ref[...]
    dt = jnp.exp(ls_ref[...])
    ldt, idt = lr * dt, li * dt
    mag = jnp.exp(ldt)
    ab_re, ab_im = mag * jnp.cos(idt), mag * jnp.sin(idt)
    den = lr * lr + li * li
    nr, ni = ab_re - 1.0, ab_im
    q_re = (nr * lr + ni * li) / den
    q_im = (ni * lr - nr * li) / den
    kf = lax.broadcasted_iota(jnp.int32, (2 * t, p), 0).astype(_F32)

    ap_re, ap_im, bb_re, bb_im = [], [], [], []
    for d in range(2):
        pm = jnp.exp(kf * ldt[d:d + 1, :])
        ap_re.append(pm * jnp.cos(kf * idt[d:d + 1, :]))
        ap_im.append(pm * jnp.sin(kf * idt[d:d + 1, :]))
        bb_re.append(q_re[d:d + 1, :] * btr_ref[d] - q_im[d:d + 1, :] * bti_ref[d])
        bb_im.append(q_re[d:d + 1, :] * bti_ref[d] + q_im[d:d + 1, :] * btr_ref[d])

    def rep(a, ks):
        return jnp.concatenate([jnp.broadcast_to(a[k:k + 1, :], (hh, p)) for k in ks], axis=0)

    def tile(a, n):
        return jnp.concatenate([a] * n, axis=0)

    def cmul(xr, xi, yr, yi):
        return xr * yr - xi * yi, xr * yi + xi * yr

    ks_f = list(range(t + 1))
    ks_r = list(range(t, -1, -1))
    caf_re, caf_im = cmul(tile(cr_ref[0], t + 1), tile(ci_ref[0], t + 1), rep(ap_re[0], ks_f), rep(ap_im[0], ks_f))
    car_re, car_im = cmul(tile(cr_ref[1], t + 1), tile(ci_ref[1], t + 1), rep(ap_re[1], ks_r), rep(ap_im[1], ks_r))

    def lag_blocks(bre, bim, ca_re, ca_im):
        lhs = jnp.concatenate([bre, bim], axis=1)
        rhs = jnp.concatenate([ca_re, -ca_im], axis=1)
        return lax.dot_general(lhs, rhs, (((1,), (1,)), ((), ())), precision=_HI,
                               preferred_element_type=_F32)

    w_f = lag_blocks(bb_re[0], bb_im[0], caf_re[:kk], caf_im[:kk])
    w_r = lag_blocks(bb_re[1], bb_im[1], car_re[hh:], car_im[hh:])
    lane = lax.broadcasted_iota(jnp.int32, (hh, kk), 1)
    for tlp in range(t):
        sf = hh * tlp
        sr = hh * (t - 1 - tlp)
        a = w_f if sf == 0 else jnp.where(lane >= sf, pltpu.roll(w_f, sf, 1), 0.0)
        b = w_r if sr == 0 else jnp.where(lane < kk - sr, pltpu.roll(w_r, kk - sr, 1), 0.0)
        wy_ref[hh * tlp:hh * (tlp + 1), :] = (a + b).astype(wy_ref.dtype)

    zeros = jnp.zeros((p, kk), _F32)
    blocks = [caf_re[hh:].T, zeros, zeros, car_re[:kk].T, (-caf_im[hh:]).T, zeros, zeros, (-car_im[:kk]).T]
    for i, blk in enumerate(blocks):
        wy_ref[kk + p * i:kk + p * (i + 1), :] = blk.astype(wy_ref.dtype)

    pf = list(range(t - 1, -1, -1))
    pr = list(range(t))
    f_re, f_im = cmul(rep(ap_re[0], pf), rep(ap_im[0], pf), tile(bb_re[0], t), tile(bb_im[0], t))
    r_re, r_im = cmul(rep(ap_re[1], pr), rep(ap_im[1], pr), tile(bb_re[1], t), tile(bb_im[1], t))
    ws_ref[...] = jnp.concatenate([f_re, r_re, f_im, r_im], axis=1).astype(ws_ref.dtype)

    a_re = jnp.concatenate([ap_re[0][t:t + 1, :], ap_re[1][t:t + 1, :]], axis=1)
    a_im = jnp.concatenate([ap_im[0][t:t + 1, :], ap_im[1][t:t + 1, :]], axis=1)
    row = lax.broadcasted_iota(jnp.int32, at_ref.shape, 0)
    at_ref[...] = jnp.where(row == 0, a_re, jnp.where(row == 1, a_im, 0.0))


def _s5_params(lam_re, lam_im, b_re, b_im, c_re, c_im, log_step):
    _, ng, p = lam_re.shape
    hh = b_re.shape[-1]
    kk = S5_CHUNK * hh
    gd = lambda a: jnp.swapaxes(a.astype(_F32), 0, 1)
    args = (gd(lam_re), gd(lam_im), gd(log_step)[..., None],
            jnp.swapaxes(gd(b_re), 2, 3), jnp.swapaxes(gd(b_im), 2, 3), gd(c_re), gd(c_im))
    spec = lambda a: pl.BlockSpec((None,) + a.shape[1:], lambda g: (g,) + (0,) * (a.ndim - 1))
    return pl.pallas_call(
        _s5_param_body,
        grid=(ng,),
        in_specs=[spec(a) for a in args],
        out_specs=[pl.BlockSpec((None, kk, 4 * p), lambda g: (g, 0, 0)),
                   pl.BlockSpec((None, kk + 8 * p, kk), lambda g: (g, 0, 0)),
                   pl.BlockSpec((None, 8, 2 * p), lambda g: (g, 0, 0))],
        out_shape=[jax.ShapeDtypeStruct((ng, kk, 4 * p), _BF),
                   jax.ShapeDtypeStruct((ng, kk + 8 * p, kk), _BF),
                   jax.ShapeDtypeStruct((ng, 8, 2 * p), _F32)],
        compiler_params=_cparams(("parallel",)),
        name="s5_params",
    )(*args)


def _post_body(x_ref, u_ref, yg_ref, fn_ref, d_ref, wglu_ref, gs_ref, wo_ref, g1_ref,
               n2_ref, sh_ref, sc_ref, wr_ref, br_ref, tri_ref, x1_ref, t_ref, rt_ref, rtt_ref, cnt_ref,
               y_scr, cnt_scr):
    ds5 = u_ref.shape[-1]
    first = (pl.program_id(0) == 0) & (pl.program_id(1) == 0)

    @pl.when(first)
    def _():
        cnt_scr[...] = jnp.zeros_like(cnt_scr)

    t_chunk = S5_CHUNK
    n_chunk = yg_ref.shape[1]
    lanes = 128
    gpb = lanes // S5_GROUP
    for j in range(ds5 // lanes):
        for hi in range(t_chunk // gpb):
            outs = _seg_transpose([yg_ref[gpb * j + glo, :, lanes * hi:lanes * (hi + 1)].astype(_F32)
                                   for glo in range(gpb)])
            for tlo in range(gpb):
                y_scr[j, pl.ds(gpb * hi + tlo, n_chunk, stride=t_chunk), :] = outs[tlo]
    y_s5 = jnp.concatenate([y_scr[j] for j in range(ds5 // lanes)], axis=-1)

    y = y_s5 + d_ref[...] * u_ref[...]
    y = jax.nn.gelu(y, approximate=True)
    y = y * jax.nn.sigmoid(_dot(y.astype(_BF), wglu_ref[...]))
    yn = _rms(y, gs_ref[...]).astype(_BF)
    mix = _dot(yn, wo_ref[0:ds5, :]) + _dot(fn_ref[...].astype(_BF), wo_ref[ds5:, :])
    x1 = x_ref[...] + g1_ref[...] * mix
    x1_ref[...] = x1
    t = _rms(x1, n2_ref[...]) * (1.0 + sc_ref[...]) + sh_ref[...]
    t_hi = t.astype(_BF)
    t_ref[...] = t_hi

    ng, epg = N_EXPERT_GROUPS, EXPERTS_PER_GROUP
    t_lo = (t - t_hi.astype(_F32)).astype(_BF)
    logits = (_dot(t_hi, wr_ref[0]) + _dot(t_lo, wr_ref[0]) + _dot(t_hi, wr_ref[1])) + br_ref[...]
    tm = logits.shape[0]
    lt = logits.T
    neg = jnp.float32(-jnp.inf)
    big = jnp.int32(1 << 20)
    row8 = lax.broadcasted_iota(jnp.int32, (epg, tm), 0)
    amax = lambda v: jnp.max(v, axis=0, keepdims=True)
    first = lambda v, m: jnp.min(jnp.where(v == m, row8, big), axis=0, keepdims=True)
    gt = lt[ng * epg:ng * epg + epg, :]
    gl = jnp.where(row8 < ng, gt, neg)
    gmax = amax(gl)
    gidx = first(gl, gmax)
    gw = 1.0 / jnp.sum(jnp.where(row8 < ng, jnp.exp(gt - gmax), 0.0), axis=0, keepdims=True)
    el = lt[0:epg, :]
    for g in range(1, ng):
        el = jnp.where(gidx == g, lt[g * epg:(g + 1) * epg, :], el)
    v0 = amax(el)
    i0 = first(el, v0)
    el1 = jnp.where(row8 == i0, neg, el)
    v1 = amax(el1)
    i1 = first(el1, v1)
    p0 = 1.0 / (1.0 + jnp.exp(v1 - v0))
    w0 = gw * p0
    w1 = gw * (1.0 - p0)
    e0 = gidx * epg + i0
    e1 = gidx * epg + i1

    rowe = lax.broadcasted_iota(jnp.int32, (ng * epg, tm), 0)
    oh0 = rowe == e0
    oh1 = rowe == e1
    oh = (oh0 | oh1).astype(_F32)
    prefix = _dot(oh.astype(_BF), tri_ref[...]) + cnt_scr[...]
    r0 = jnp.sum(jnp.where(oh0, prefix, 0.0), axis=0, keepdims=True)
    r1 = jnp.sum(jnp.where(oh1, prefix, 0.0), axis=0, keepdims=True)
    cnt = cnt_scr[...] + jnp.sum(oh, axis=1, keepdims=True)
    cnt_scr[...] = cnt
    cnt_ref[...] = cnt

    rec = jnp.concatenate([e0.astype(_F32), e1.astype(_F32), w0, w1, r0, r1,
                           jnp.zeros((ROUTE_ROWS - 6, tm), _F32)], axis=0)
    rtt_ref[...] = rec
    rt_ref[...] = jnp.concatenate([rec, jnp.zeros((ROUTE_LANES - ROUTE_ROWS, tm), _F32)], axis=0).T


@functools.lru_cache(maxsize=None)
def _strict_upper_ones(n):
    return np.triu(np.ones((n, n), np.float32), 1).astype(jnp.bfloat16)


def _post(x, u, y_g, fn, s5_d, wglu_bf, gs, wo_bf, g1, n2, sh2, sc2, wr, br, b0, nb, tm=512):
    b, l, d = x.shape
    ds5 = u.shape[-1]
    n_grp, _, w = y_g.shape
    kk = w // b
    n_exp = N_EXPERT_GROUPS * EXPERTS_PER_GROUP
    tri = jnp.asarray(_strict_upper_ones(tm))
    tok = lambda w: pl.BlockSpec((None, tm, w), lambda i, j: (i + b0, j, 0))
    otok = lambda w: pl.BlockSpec((None, tm, w), lambda i, j: (i, j, 0))
    per_b = pl.BlockSpec((None, 1, d), lambda i, j: (i + b0, 0, 0))
    full = lambda a: pl.BlockSpec(a.shape, lambda i, j: (0,) * a.ndim)
    ygs = pl.BlockSpec((n_grp, tm // S5_CHUNK, kk), lambda i, j: (0, j, i + b0))
    return pl.pallas_call(
        _post_body,
        grid=(nb, l // tm),
        in_specs=[tok(d), tok(ds5), ygs, tok(fn.shape[-1]), full(s5_d), full(wglu_bf),
                  full(gs), full(wo_bf), per_b, full(n2), per_b, per_b, full(wr), full(br), full(tri)],
        out_specs=[otok(d), otok(d), otok(ROUTE_LANES),
                   pl.BlockSpec((None, ROUTE_ROWS, tm), lambda i, j: (i, 0, j)),
                   pl.BlockSpec((n_exp, 1), lambda i, j: (0, 0))],
        out_shape=[jax.ShapeDtypeStruct((nb, l, d), _F32),
                   jax.ShapeDtypeStruct((nb, l, d), _BF),
                   jax.ShapeDtypeStruct((nb, l, ROUTE_LANES), _F32),
                   jax.ShapeDtypeStruct((nb, ROUTE_ROWS, l), _F32),
                   jax.ShapeDtypeStruct((n_exp, 1), _F32)],
        scratch_shapes=[pltpu.VMEM((ds5 // 128, tm, 128), _F32),
                        pltpu.VMEM((n_exp, 1), _F32)],
        compiler_params=_cparams(("arbitrary", "arbitrary")),
        name="post_mixer",
    )(x, u, y_g, fn, s5_d, wglu_bf, gs, wo_bf, g1, n2, sh2, sc2, wr, br, tri)


def _moe_body(te_ref, nt_ref, xs_ref, wg_ref, wu_ref, wd_ref, o_ref, wgu_bf, wd_bf):
    i = pl.program_id(0)
    de = wg_ref.shape[-1]
    used = i < nt_ref[0]
    new_expert = (i == 0) | (te_ref[i] != te_ref[jnp.maximum(i - 1, 0)])

    @pl.when(used & new_expert)
    def _():
        wgu_bf[:, 0:de] = wg_ref[...].astype(_BF)
        wgu_bf[:, de:2 * de] = wu_ref[...].astype(_BF)
        wd_bf[...] = wd_ref[...].astype(_BF)

    @pl.when(used)
    def _():
        h = _dot(xs_ref[...], wgu_bf[...])
        hg, hu = h[:, 0:de], h[:, de:2 * de]
        a = hg * jax.nn.sigmoid(hg) * hu
        o_ref[...] = _dot(a.astype(_BF), wd_bf[...]).astype(o_ref.dtype)

    @pl.when(jnp.logical_not(used))
    def _():
        o_ref[...] = jnp.zeros_like(o_ref)


def _moe(tile_expert, n_tiles_used, xs, wg, wu, wd):
    nr, d = xs.shape
    tm = MOE_TM
    de = wg.shape[-1]
    grid_spec = pltpu.PrefetchScalarGridSpec(
        num_scalar_prefetch=2,
        grid=(nr // tm,),
        in_specs=[pl.BlockSpec((tm, d), lambda i, te, nt: (i, 0)),
                  pl.BlockSpec((None, d, de), lambda i, te, nt: (te[i], 0, 0)),
                  pl.BlockSpec((None, d, de), lambda i, te, nt: (te[i], 0, 0)),
                  pl.BlockSpec((None, de, d), lambda i, te, nt: (te[i], 0, 0))],
        out_specs=pl.BlockSpec((tm, d), lambda i, te, nt: (i, 0)),
        scratch_shapes=[pltpu.VMEM((d, 2 * de), _BF), pltpu.VMEM((de, d), _BF)],
    )
    return pl.pallas_call(
        _moe_body,
        grid_spec=grid_spec,
        out_shape=jax.ShapeDtypeStruct((nr, d), _BF),
        compiler_params=_cparams(("arbitrary",)),
        name="moe_experts",
    )(tile_expert, n_tiles_used, xs, wg, wu, wd)


def _final_body(g2_ref, gf_ref, *refs, n_part, nb):
    o_ref = refs[-1]
    i = pl.program_id(0)
    for part in range(n_part):
        x1_ref, y0_ref, y1_ref, rt_ref = refs[4 * part:4 * part + 4]

        @pl.when((i >= part * nb) & (i < (part + 1) * nb))
        def _():
            w0 = rt_ref[:, 2:3]
            w1 = rt_ref[:, 3:4]
            m = w0 * y0_ref[...].astype(_F32) + w1 * y1_ref[...].astype(_F32)
            o_ref[...] = _rms(x1_ref[...] + g2_ref[...] * m, gf_ref[...])


def _final(parts, g2, gf, tm=512):
    n_part = len(parts)
    nb, l, d = parts[0][0].shape
    nj = l // tm
    in_specs = [pl.BlockSpec((None, 1, d), lambda i, j: (i, 0, 0)),
                pl.BlockSpec((1, d), lambda i, j: (0, 0))]
    args = [g2, gf]
    for part, arrs in enumerate(parts):
        def imap(i, j, part=part):
            own = (i >= part * nb) & (i < (part + 1) * nb)
            return (jnp.clip(i - part * nb, 0, nb - 1), jnp.where(own, j, jnp.where(i < part * nb, 0, nj - 1)), 0)
        for a in arrs:
            in_specs.append(pl.BlockSpec((None, tm, a.shape[-1]), imap))
            args.append(a)
    return pl.pallas_call(
        functools.partial(_final_body, n_part=n_part, nb=nb),
        grid=(n_part * nb, nj),
        in_specs=in_specs,
        out_specs=pl.BlockSpec((None, tm, d), lambda i, j: (i, j, 0)),
        out_shape=jax.ShapeDtypeStruct((n_part * nb, l, d), _F32),
        compiler_params=_cparams(("parallel", "arbitrary")),
        name="final_norm",
    )(*args)


def _pair_rows_body(ps_ref, eid_ref, rank_ref, pos_ref):
    eid = eid_ref[...]
    pos = rank_ref[...]
    for e in range(ps_ref.shape[0]):
        pos = pos + jnp.where(eid == e, ps_ref[e], 0)
    pos_ref[...] = pos


def _pair_rows(pad_start, eid, rank):
    shape = eid.shape
    rows = 16
    e2, r2 = eid.reshape(rows, -1), rank.reshape(rows, -1)
    full = pl.BlockSpec(e2.shape, lambda i, ps: (0, 0))
    grid_spec = pltpu.PrefetchScalarGridSpec(num_scalar_prefetch=1, grid=(1,), in_specs=[full, full],
                                             out_specs=full)
    pos = pl.pallas_call(
        _pair_rows_body,
        grid_spec=grid_spec,
        out_shape=jax.ShapeDtypeStruct(e2.shape, jnp.int32),
        compiler_params=_cparams(("arbitrary",)),
        name="pair_rows",
    )(pad_start, e2, r2)
    return pos.reshape(shape)


def _route_plan(eid, rank, tok_ids, counts, tm, nr):
    n_experts = counts.shape[0]
    n_pairs = eid.size
    padded = ((counts + tm - 1) // tm) * tm
    pad_end = jnp.cumsum(padded)
    pad_start = pad_end - padded
    raw_start = jnp.cumsum(counts) - counts
    pos = _pair_rows(pad_start.astype(jnp.int32), eid, rank)
    tile_start =jnp.arange(nr // tm, dtype=jnp.int32) * tm
    tile_expert = jnp.sum(tile_start[:, None] >= pad_end[None, :], axis=-1)
    tile_expert = jnp.minimum(tile_expert, n_experts - 1).astype(jnp.int32)
    _, sorted_tok = lax.sort(((eid * n_pairs + rank).reshape(-1), tok_ids.reshape(-1)),
                             num_keys=1, is_stable=False)
    sel = tile_expert[:, None] == jnp.arange(n_experts, dtype=jnp.int32)[None, :]
    per_tile = lambda v: jnp.repeat(jnp.sum(jnp.where(sel, v[None, :], 0), axis=-1), tm)
    off = jnp.arange(nr, dtype=jnp.int32) - per_tile(pad_start)
    valid = off < per_tile(counts)
    j = jnp.clip(per_tile(raw_start) + off, 0, n_pairs - 1)
    filler = jnp.arange(nr, dtype=jnp.int32) % (n_pairs // 2)
    row_token = jnp.where(valid, sorted_tok.at[j].get(mode="promise_in_bounds"), filler)
    n_used = (pad_end[-1:] // tm).astype(jnp.int32)
    return pos, tile_expert, n_used, row_token


def kernel(x, c, ctx, c_ctx, w_ada, b_ada, norm1_g, norm2_g, w_in, s5_lam_re, s5_lam_im, s5_b_re, s5_b_im, s5_c_re, s5_c_im, s5_log_step, s5_d, s5_w_glu, fourier_w, mix_norm_s5_g, mix_norm_f_g, w_out, moe_w_group, moe_b_group, moe_w_router, moe_b_router, moe_w_gate, moe_w_up, moe_w_down, final_norm_g):
    b, l, d = x.shape
    lc = ctx.shape[1]
    depth = w_ada.shape[0]
    assert depth == 1 and l == FFT_R * FFT_R and lc % S5_CHUNK == 0 and b % 8 == 0
    ds5 = s5_d.shape[-1]
    df = w_in.shape[-1] - ds5
    n_exp = moe_w_gate.shape[1]
    row = lambda a: a.reshape(1, -1)

    cond = jnp.concatenate([c, c_ctx[None, :], jnp.zeros((7, d), _F32)], axis=0)
    mod = _adaln(cond, w_ada[0], b_ada[0])
    sh1, sc1, g1, sh2, sc2, g2 = [mod[:b, i * d:(i + 1) * d].reshape(b, 1, d) for i in range(6)]
    csh1 = mod[b:b + 1, 0:d]
    csc1 = mod[b:b + 1, d:2 * d]

    w_in_bf = w_in[0].astype(_BF)
    m1 = jnp.asarray(_fft_stage1_matrix()).astype(_BF)
    m2 = jnp.asarray(_fft_stage2_matrices()).astype(_BF)
    perm = jnp.asarray(_chunk_row_perm())
    zs, gc = _inproj(x, sh1, sc1, row(norm1_g[0]), w_in_bf, m1)
    u_ctx = _inproj_ctx(ctx, csh1, csc1, row(norm1_g[0]), w_in_bf[:, :ds5], perm)
    u_lat = _s5_pack(zs, perm)

    fg = df // FOURIER_GROUPS
    cc = np.arange(fg)
    ang = 2.0 * np.pi * ((cc[:, None] * cc[None, :]) % fg) / fg
    scale = 1.0 / math.sqrt(l * fg)
    cw = jnp.einsum("cm,gmd->gcd", jnp.asarray(np.cos(ang) * scale, _F32), fourier_w[0], precision=_HI)
    sw = jnp.einsum("cm,gmd->gcd", jnp.asarray(np.sin(ang) * scale, _F32), fourier_w[0], precision=_HI)
    fn = _fft2(gc, m2, cw.astype(_BF), sw.astype(_BF), row(mix_norm_f_g[0]))

    ws, wy, at = _s5_params(s5_lam_re[0], s5_lam_im[0], s5_b_re[0], s5_b_im[0],
                            s5_c_re[0], s5_c_im[0], s5_log_step[0])
    y_g = _s5(u_ctx, u_lat, ws, wy, at, b)

    n_rt = N_EXPERT_GROUPS * (1 + EXPERTS_PER_GROUP)
    w_rt = jnp.concatenate([moe_w_router[0].reshape(d, -1), moe_w_group[0]], axis=-1)
    w_rt = jnp.pad(w_rt, ((0, 0), (0, ROUTE_LANES - n_rt)))
    w_rt_hi = w_rt.astype(_BF)
    w_rt = jnp.stack([w_rt_hi, (w_rt - w_rt_hi.astype(_F32)).astype(_BF)])
    b_rt = jnp.pad(jnp.concatenate([moe_b_router[0].reshape(-1), moe_b_group[0]]), (0, ROUTE_LANES - n_rt))
    tm = MOE_TM
    nb = b // MOE_PARTS
    n_tok = nb * l
    nr = 2 * n_tok + n_exp * tm
    take_rows = lambda a, idx: a.at[idx].get(mode="promise_in_bounds")
    wglu_bf, wo_bf = s5_w_glu[0].astype(_BF), w_out[0].astype(_BF)
    parts = []
    for part in range(MOE_PARTS):
        x1, tmod, rt, rtt, cnt = _post(x, zs, y_g, fn, row(s5_d[0]), wglu_bf, row(mix_norm_s5_g[0]), wo_bf,
                                       g1, row(norm2_g[0]), sh2, sc2, w_rt, row(b_rt), part * nb, nb)
        rec = rtt.transpose(1, 0, 2).reshape(ROUTE_ROWS, n_tok)
        eid = rec[0:2].astype(jnp.int32)
        rank = rec[4:6].astype(jnp.int32)
        tok_ids = jnp.broadcast_to(jnp.arange(n_tok, dtype=jnp.int32), (2, n_tok))
        pos, tile_expert, n_used, row_token = _route_plan(eid, rank, tok_ids,
                                                          cnt[:, 0].astype(jnp.int32), tm, nr)
        xs = take_rows(tmod.reshape(n_tok, d), row_token)
        ys = _moe(tile_expert, n_used, xs, moe_w_gate[0], moe_w_up[0], moe_w_down[0])
        y0 = take_rows(ys, pos[0]).reshape(nb, l, d)
        y1 = take_rows(ys, pos[1]).reshape(nb, l, d)
        parts.append((x1, y0, y1, rt))
    return _final(parts, g2, row(final_norm_g))
```
